```python
import math
import jax
import jax.numpy as jnp
from jax import lax
import numpy as np

D_MODEL = 1024
BATCH = 4
SEQ = 8192
DEPTH = 2

N_BRANCH = 4
BRANCH_WIDTH = 256
MLA_HEADS = 4
MLA_Q_LORA = 384
MLA_KV_LORA = 128
MLA_NOPE = 64
MLA_ROPE = 32
MLA_V = 64
FNET_GROUPS = 4
FNET_GROUP_DIM = 64
DIFF_HEADS = 4
DIFF_HEAD_DIM = 32
HGRN_HEADS = 4
HGRN_KEY_DIM = 64
HGRN_VAL_DIM = 64
HGRN_CHUNK = 64
D_FF = -(-8 * D_MODEL // (3 * 256)) * 256

ROPE_THETA = 10000.0
Q_BLOCK = 128
EPS = 1e-6

IN_WIDTHS = (
    MLA_Q_LORA,
    MLA_KV_LORA,
    MLA_ROPE,
    FNET_GROUPS * FNET_GROUP_DIM,
    2 * DIFF_HEADS * DIFF_HEAD_DIM,
    2 * DIFF_HEADS * DIFF_HEAD_DIM,
    DIFF_HEADS * 2 * DIFF_HEAD_DIM,
    HGRN_HEADS * HGRN_KEY_DIM,
    HGRN_HEADS * HGRN_VAL_DIM,
    HGRN_HEADS * HGRN_KEY_DIM,
    HGRN_HEADS * HGRN_KEY_DIM,
    HGRN_HEADS * HGRN_VAL_DIM,
    N_BRANCH * D_MODEL,
)
IN_TOTAL = sum(IN_WIDTHS)

kernel_name = 'hybrid_mla_fnet_diffattn_hgrn2_encoder'


def rmsnorm(x, g):
    xf = x.astype(jnp.float32)
    y = xf * lax.rsqrt(jnp.mean(xf * xf, axis=-1, keepdims=True) + EPS)
    return (y * g.astype(jnp.float32)).astype(x.dtype)


def split_cols(z, widths):
    offsets = [int(o) for o in np.cumsum(widths)[:-1]]
    return jnp.split(z, offsets, axis=-1)


def rope_tables(seq_len, dim):
    inv_freq = 1.0 / (ROPE_THETA ** (jnp.arange(0, dim, 2, dtype=jnp.float32) / dim))
    ang = jnp.arange(seq_len, dtype=jnp.float32)[:, None] * inv_freq[None, :]
    return jnp.cos(ang), jnp.sin(ang)


def apply_rope(t, cos, sin):
    tf = t.astype(jnp.float32)
    t1, t2 = jnp.split(tf, 2, axis=-1)
    c = cos[None, :, None, :]
    s = sin[None, :, None, :]
    return jnp.concatenate([t1 * c - t2 * s, t1 * s + t2 * c], axis=-1).astype(t.dtype)


def to_query_blocks(t):
    b, s, h, d = t.shape
    return t.reshape(b, s // Q_BLOCK, Q_BLOCK, h, d).transpose(1, 0, 2, 3, 4)


def from_query_blocks(t):
    nb, b, qb, h, d = t.shape
    return t.transpose(1, 0, 2, 3, 4).reshape(b, nb * qb, h, d)


def block_softmax_attention(q, k, v, scale):
    def one_block(qb):
        s = jnp.einsum('bqhd,bkhd->bhqk', qb, k).astype(jnp.float32) * scale
        p = jax.nn.softmax(s, axis=-1)
        return jnp.einsum('bhqk,bkhd->bqhd', p.astype(v.dtype), v)
    return from_query_blocks(lax.map(one_block, to_query_blocks(q)))


def block_diff_attention(q1, q2, k1, k2, v, lam, scale):
    def one_block(qs):
        qb1, qb2 = qs
        s1 = jnp.einsum('bqhd,bkhd->bhqk', qb1, k1).astype(jnp.float32) * scale
        s2 = jnp.einsum('bqhd,bkhd->bhqk', qb2, k2).astype(jnp.float32) * scale
        p = jax.nn.softmax(s1, axis=-1) - lam * jax.nn.softmax(s2, axis=-1)
        return jnp.einsum('bhqk,bkhd->bqhd', p.astype(v.dtype), v)
    return from_query_blocks(lax.map(one_block, (to_query_blocks(q1), to_query_blocks(q2))))


def mla_mixer(c_q, c_kv, k_rope, g_qa, w_uq, g_kva, w_ukv, g_qn, g_kn, cos, sin):
    b, s, _ = c_q.shape
    qk_dim = MLA_NOPE + MLA_ROPE
    q = (rmsnorm(c_q, g_qa) @ w_uq).reshape(b, s, MLA_HEADS, qk_dim)
    kv = (rmsnorm(c_kv, g_kva) @ w_ukv).reshape(b, s, MLA_HEADS, MLA_NOPE + MLA_V)
    k_nope, v = jnp.split(kv, [MLA_NOPE], axis=-1)
    k_pe = jnp.broadcast_to(k_rope[:, :, None, :], (b, s, MLA_HEADS, MLA_ROPE))
    k = jnp.concatenate([k_nope, k_pe], axis=-1)
    q = rmsnorm(q, g_qn)
    k = rmsnorm(k, g_kn)
    q = jnp.concatenate([q[..., :MLA_NOPE], apply_rope(q[..., MLA_NOPE:], cos, sin)], axis=-1)
    k = jnp.concatenate([k[..., :MLA_NOPE], apply_rope(k[..., MLA_NOPE:], cos, sin)], axis=-1)
    o = block_softmax_attention(q, k, v, qk_dim ** -0.5)
    return o.reshape(b, s, MLA_HEADS * MLA_V)


def fnet_mixer(u):
    b, s, _ = u.shape
    uf = u.astype(jnp.float32).reshape(b, s, FNET_GROUPS, FNET_GROUP_DIM)
    y = jnp.fft.fft2(uf, axes=(1, 3), norm='ortho').real
    return y.reshape(b, s, FNET_GROUPS * FNET_GROUP_DIM).astype(u.dtype)


def diff_mixer(q_in, k_in, v_in, g_qn, g_kn, lq1, lk1, lq2, lk2, g_sub, layer_idx, cos, sin):
    b, s, _ = q_in.shape
    d = DIFF_HEAD_DIM
    q = apply_rope(rmsnorm(q_in.reshape(b, s, 2 * DIFF_HEADS, d), g_qn), cos, sin)
    k = apply_rope(rmsnorm(k_in.reshape(b, s, 2 * DIFF_HEADS, d), g_kn), cos, sin)
    q = q.reshape(b, s, DIFF_HEADS, 2, d)
    k = k.reshape(b, s, DIFF_HEADS, 2, d)
    v = v_in.reshape(b, s, DIFF_HEADS, 2 * d)
    lambda_init = 0.8 - 0.6 * math.exp(-0.3 * layer_idx)
    lam = (jnp.exp(jnp.sum(lq1.astype(jnp.float32) * lk1.astype(jnp.float32)))
           - jnp.exp(jnp.sum(lq2.astype(jnp.float32) * lk2.astype(jnp.float32)))
           + lambda_init)
    o = block_diff_attention(q[:, :, :, 0], q[:, :, :, 1], k[:, :, :, 0], k[:, :, :, 1], v, lam, d ** -0.5)
    o = rmsnorm(o, g_sub) * (1.0 - lambda_init)
    return o.reshape(b, s, DIFF_HEADS * 2 * d)


def hgrn2_chunk_scan(q, k, v, log_f):
    b, s, h, dk = q.shape
    dv = v.shape[-1]
    c = HGRN_CHUNK
    n_chunks = s // c

    def chunks(t):
        return t.reshape(b, n_chunks, c, h, t.shape[-1]).transpose(1, 0, 3, 2, 4)

    lower_tri = jnp.tril(jnp.ones((c, c), dtype=bool))[:, :, None]

    def step(state, inp):
        qc, kc, vc, lc = inp
        cum = jnp.cumsum(lc, axis=2)
        inter = jnp.einsum('bhtk,bhkv->bhtv', qc * jnp.exp(cum), state)
        rel = cum[:, :, :, None, :] - cum[:, :, None, :, :]
        decay = jnp.exp(jnp.where(lower_tri, rel, -jnp.inf))
        scores = jnp.einsum('bhtk,bhtsk,bhsk->bhts', qc, decay, kc)
        intra = jnp.einsum('bhts,bhsv->bhtv', scores, vc)
        last = cum[:, :, -1:, :]
        new_state = (jnp.exp(last[:, :, 0, :])[..., None] * state
                     + jnp.einsum('bhsk,bhsv->bhkv', kc * jnp.exp(last - cum), vc))
        return new_state, inter + intra

    init = jnp.zeros((b, h, dk, dv), jnp.float32)
    _, o = lax.scan(step, init, (chunks(q), chunks(k), chunks(v), chunks(log_f)))
    return o.transpose(1, 0, 3, 2, 4).reshape(b, s, h, dv)


def hgrn2_mixer(q_in, i_in, f_fwd_in, f_bwd_in, g_in, lb_fwd, lb_bwd, g_on):
    b, s, _ = q_in.shape
    h, dk, dv = HGRN_HEADS, HGRN_KEY_DIM, HGRN_VAL_DIM
    q = q_in.astype(jnp.float32).reshape(b, s, h, dk)
    v = i_in.astype(jnp.float32).reshape(b, s, h, dv)

    def forget(z, lb):
        lbh = lb.reshape(h, dk)
        f = lbh + (1.0 - lbh) * jax.nn.sigmoid(z.astype(jnp.float32).reshape(b, s, h, dk))
        return 1.0 - f, jnp.log(f)

    k_f, lf_f = forget(f_fwd_in, lb_fwd)
    k_b, lf_b = forget(f_bwd_in, lb_bwd)
    o_fwd = hgrn2_chunk_scan(q, k_f, v, lf_f)
    o_bwd = jnp.flip(hgrn2_chunk_scan(jnp.flip(q, 1), jnp.flip(k_b, 1), jnp.flip(v, 1), jnp.flip(lf_b, 1)), 1)
    o = rmsnorm(o_fwd + o_bwd, g_on) * jax.nn.sigmoid(g_in.astype(jnp.float32).reshape(b, s, h, dv))
    return o.reshape(b, s, h * dv).astype(q_in.dtype)


def setup_inputs(seed: int = 0) -> dict:
    key = jax.random.key(seed)
    ks = jax.random.split(key, 24)
    L = DEPTH
    qk_m = MLA_NOPE + MLA_ROPE

    def nrm(k, shape, scale):
        return jax.random.normal(k, shape, jnp.float32) * scale

    def gain(k, shape):
        return 1.0 + 0.02 * jax.random.normal(k, shape, jnp.float32)

    return {
        'x': nrm(ks[0], (BATCH, SEQ, D_MODEL), 1.0),
        'ln_mix': gain(ks[1], (L, D_MODEL)),
        'w_in': nrm(ks[2], (L, D_MODEL, IN_TOTAL), D_MODEL ** -0.5),
        'mla_g_qa': gain(ks[3], (L, MLA_Q_LORA)),
        'mla_w_uq': nrm(ks[4], (L, MLA_Q_LORA, MLA_HEADS * qk_m), MLA_Q_LORA ** -0.5),
        'mla_g_kva': gain(ks[5], (L, MLA_KV_LORA)),
        'mla_w_ukv': nrm(ks[6], (L, MLA_KV_LORA, MLA_HEADS * (MLA_NOPE + MLA_V)), MLA_KV_LORA ** -0.5),
        'mla_g_qn': gain(ks[7], (L, qk_m)),
        'mla_g_kn': gain(ks[8], (L, qk_m)),
        'diff_g_qn': gain(ks[9], (L, DIFF_HEAD_DIM)),
        'diff_g_kn': gain(ks[10], (L, DIFF_HEAD_DIM)),
        'diff_lq1': nrm(ks[11], (L, DIFF_HEAD_DIM), 0.1),
        'diff_lk1': nrm(ks[12], (L, DIFF_HEAD_DIM), 0.1),
        'diff_lq2': nrm(ks[13], (L, DIFF_HEAD_DIM), 0.1),
        'diff_lk2': nrm(ks[14], (L, DIFF_HEAD_DIM), 0.1),
        'diff_g_sub': gain(ks[15], (L, 2 * DIFF_HEAD_DIM)),
        'hgrn_lb_logits': nrm(ks[16], (2, L, HGRN_HEADS * HGRN_KEY_DIM), 0.5),
        'hgrn_g_on': gain(ks[17], (L, HGRN_VAL_DIM)),
        'w_branch': nrm(ks[18], (L, N_BRANCH, BRANCH_WIDTH, D_MODEL), BRANCH_WIDTH ** -0.5),
        'w_out': nrm(ks[19], (L, D_MODEL, D_MODEL), D_MODEL ** -0.5),
        'ln_ffn': gain(ks[20], (L, D_MODEL)),
        'w_gate_up': nrm(ks[21], (L, D_MODEL, 2 * D_FF), D_MODEL ** -0.5),
        'w_down': nrm(ks[22], (L, D_FF, D_MODEL), D_FF ** -0.5),
    }


def reference(x, ln_mix, w_in, mla_g_qa, mla_w_uq, mla_g_kva, mla_w_ukv, mla_g_qn, mla_g_kn,
              diff_g_qn, diff_g_kn, diff_lq1, diff_lk1, diff_lq2, diff_lk2, diff_g_sub,
              hgrn_lb_logits, hgrn_g_on, w_branch, w_out, ln_ffn, w_gate_up, w_down):
    seq_len = x.shape[1]
    cos_m, sin_m = rope_tables(seq_len, MLA_ROPE)
    cos_d, sin_d = rope_tables(seq_len, DIFF_HEAD_DIM)
    lb_p = jax.nn.softmax(hgrn_lb_logits.astype(jnp.float32), axis=1)
    lower_bounds = jnp.cumsum(lb_p, axis=1) - lb_p[:, :1]

    for l in range(DEPTH):
        h = rmsnorm(x, ln_mix[l])
        z = h @ w_in[l]
        (c_q, c_kv, k_rope, fnet_in, dq, dk, dv,
         hq, hi, hf_fwd, hf_bwd, hg, gate_logits) = split_cols(z, IN_WIDTHS)

        o_mla = mla_mixer(c_q, c_kv, k_rope, mla_g_qa[l], mla_w_uq[l], mla_g_kva[l], mla_w_ukv[l],
                          mla_g_qn[l], mla_g_kn[l], cos_m, sin_m)
        o_fnet = fnet_mixer(fnet_in)
        o_diff = diff_mixer(dq, dk, dv, diff_g_qn[l], diff_g_kn[l], diff_lq1[l], diff_lk1[l],
                            diff_lq2[l], diff_lk2[l], diff_g_sub[l], l, cos_d, sin_d)
        o_hgrn = hgrn2_mixer(hq, hi, hf_fwd, hf_bwd, hg, lower_bounds[0, l], lower_bounds[1, l],
                             hgrn_g_on[l])

        merged = jnp.zeros_like(x)
        for n, o_branch in enumerate((o_mla, o_fnet, o_diff, o_hgrn)):
            gate = jax.nn.sigmoid(gate_logits[..., n * D_MODEL:(n + 1) * D_MODEL].astype(jnp.float32))
            merged = merged + gate.astype(x.dtype) * (o_branch @ w_branch[l, n])
        x = x + merged @ w_out[l]

        h2 = rmsnorm(x, ln_ffn[l])
        g_part, u_part = jnp.split(h2 @ w_gate_up[l], 2, axis=-1)
        x = x + (jax.nn.silu(g_part) * u_part) @ w_down[l]
    return x
```

```python
import functools
import math

import numpy as np
import jax
import jax.numpy as jnp
from jax import lax
from jax.experimental import pallas as pl
from jax.experimental.pallas import tpu as pltpu

f32 = jnp.float32
bf16 = jnp.bfloat16

D_MODEL = 1024
DEPTH = 2
MLA_HEADS = 4
MLA_Q_LORA = 384
MLA_KV_LORA = 128
MLA_NOPE = 64
MLA_ROPE = 32
MLA_V = 64
MLA_QK = MLA_NOPE + MLA_ROPE
DIFF_HEADS = 4
DIFF_HEAD_DIM = 32
HGRN_HEADS = 4
HGRN_DIM = 64
BRANCH = 256
D_FF = 2816
FF_CHUNK = 256
ROPE_THETA = 10000.0
EPS = 1e-6
LOG2E = 1.4426950408889634
LANE = 128
FNET_N2 = 64
FNET_K1_BLOCK = 8
VMEM_LIMIT = 56 * 1024 * 1024

_SEG = {}
_off = 0
for _name, _w in (("cq", 384), ("ckv", 128), ("krope", 128), ("fnet", 256), ("dq", 256), ("dk", 256),
                  ("dv", 512), ("hq", 256), ("hi", 256), ("hff", 256), ("hfb", 256), ("hg", 256)):
    _SEG[_name] = (_off, _off + _w)
    _off += _w
W_CAT = _off


def _cparams(sem):
    return pltpu.CompilerParams(dimension_semantics=sem, vmem_limit_bytes=VMEM_LIMIT)


def _const_spec(shape):
    nd = len(shape)
    return pl.BlockSpec(shape, lambda *_: (0,) * nd, pipeline_mode=pl.Buffered(1))


def _rms(x, g):
    return x * lax.rsqrt(jnp.mean(x * x, axis=-1, keepdims=True) + EPS) * g


def _sigmoid(x):
    return 1.0 / (1.0 + jnp.exp(-x))


def _rope(t, tab_ref):
    return t * tab_ref[0] + pltpu.roll(t, LANE - 16, 1) * tab_ref[1] + pltpu.roll(t, 16, 1) * tab_ref[2]


def _in_proj_kernel(x_ref, ln_ref, wcat_ref, gqa_ref, wuq_ref, gkva_ref, wukv_ref, gqn_ref, gkn_ref,
                    ropem_ref, gdq_ref, gdk_ref, g32_ref, roped_ref, lb_ref,
                    qm_ref, km_ref, vm_ref, fn_ref, qd_ref, kd_ref, vd_ref, hq_ref, hv_ref, hk_ref,
                    hlf_ref, hg_ref):
    h = _rms(x_ref[0], ln_ref[...]).astype(bf16)

    def seg(name):
        a, b = _SEG[name]
        return jnp.dot(h, wcat_ref[:, a:b], preferred_element_type=f32)

    ones_lane = (lax.broadcasted_iota(jnp.int32, (1, LANE), 1) == MLA_V).astype(f32)

    cqn = _rms(seg("cq"), gqa_ref[...]).astype(bf16)
    q = jnp.dot(cqn, wuq_ref[...], preferred_element_type=f32)
    for hh in range(MLA_HEADS):
        sl = slice(hh * LANE, (hh + 1) * LANE)
        qh = q[:, sl]
        ms = jnp.sum(qh * qh, axis=-1, keepdims=True) * (1.0 / MLA_QK)
        qm_ref[0, :, sl] = _rope(qh * lax.rsqrt(ms + EPS) * gqn_ref[...], ropem_ref).astype(bf16)

    ckvn = _rms(seg("ckv"), gkva_ref[...]).astype(bf16)
    kv = jnp.dot(ckvn, wukv_ref[...], preferred_element_type=f32)
    krope = seg("krope")
    for hh in range(MLA_HEADS):
        sl = slice(hh * LANE, (hh + 1) * LANE)
        kh = kv[:, sl] + krope
        ms = jnp.sum(kh * kh, axis=-1, keepdims=True) * (1.0 / MLA_QK)
        km_ref[0, :, sl] = _rope(kh * lax.rsqrt(ms + EPS) * gkn_ref[...], ropem_ref).astype(bf16)
        vsl = slice((MLA_HEADS + hh) * LANE, (MLA_HEADS + hh + 1) * LANE)
        vm_ref[0, :, sl] = (kv[:, vsl] + ones_lane).astype(bf16)

    fn_ref[0] = seg("fnet").astype(bf16)

    for name, g_ref, o_ref in (("dq", gdq_ref, qd_ref), ("dk", gdk_ref, kd_ref)):
        t = seg(name)
        ms = jnp.dot((t * t).astype(bf16), g32_ref[...], preferred_element_type=f32)
        t = t * lax.rsqrt(ms + EPS) * g_ref[...]
        for c in range(BRANCH // LANE):
            sl = slice(c * LANE, (c + 1) * LANE)
            o_ref[0, :, sl] = _rope(t[:, sl], roped_ref).astype(bf16)
    dv = seg("dv")
    for hh in range(DIFF_HEADS):
        sl = slice(hh * LANE, (hh + 1) * LANE)
        vd_ref[0, :, sl] = (dv[:, sl] + ones_lane).astype(bf16)

    hq_ref[0] = seg("hq").astype(bf16)
    hv_ref[0] = seg("hi").astype(bf16)
    hg_ref[0] = seg("hg").astype(bf16)
    for d, name in enumerate(("hff", "hfb")):
        lb = lb_ref[d:d + 1, :]
        f = lb + (1.0 - lb) * _sigmoid(seg(name))
        hk_ref[0, :, d * BRANCH:(d + 1) * BRANCH] = (1.0 - f).astype(bf16)
        lf = jnp.log(f)
        hi = lf.astype(bf16)
        hlf_ref[0, :, 2 * d * BRANCH:(2 * d + 1) * BRANCH] = hi
        hlf_ref[0, :, (2 * d + 1) * BRANCH:(2 * d + 2) * BRANCH] = (lf - hi.astype(f32)).astype(bf16)


def _in_proj(x, p, rope_m, rope_d, tm):
    B, S, _ = x.shape
    row = lambda w: pl.BlockSpec((1, tm, w), lambda b, i: (b, i, 0))
    tab = pl.BlockSpec((3, tm, LANE), lambda b, i: (0, i, 0))
    out_w = (512, 512, 512, 256, 256, 256, 512, 256, 256, 512, 1024, 256)
    consts = (p["ln_mix"], p["wcat"], p["gqa"], p["wuq"], p["gkva"], p["wukv"], p["gqn"], p["gkn"])
    consts2 = (p["gdq"], p["gdk"], p["g32"])
    in_specs = ([row(D_MODEL)] + [_const_spec(c.shape) for c in consts] + [tab]
                + [_const_spec(c.shape) for c in consts2] + [tab, _const_spec(p["lb"].shape)])
    return pl.pallas_call(
        _in_proj_kernel,
        grid=(B, S // tm),
        in_specs=in_specs,
        out_specs=[row(w) for w in out_w],
        out_shape=[jax.ShapeDtypeStruct((B, S, w), bf16) for w in out_w],
        compiler_params=_cparams(("parallel", "parallel")),
        name="in_proj",
    )(x, *consts, rope_m, *consts2, rope_d, p["lb"])


def _online_softmax_step(s, v, m_ref, acc_ref, idx):
    m_prev = m_ref[idx]
    m_new = jnp.maximum(m_prev, jnp.max(s, axis=1, keepdims=True))
    alpha = jnp.exp2(m_prev - m_new)
    p = jnp.exp2((s - m_new[:, :1]).astype(bf16))
    acc_ref[idx] = alpha * acc_ref[idx] + jnp.dot(p, v, preferred_element_type=f32)
    m_ref[idx] = m_new


_NT = (((1,), (1,)), ((), ()))
_TN = (((0,), (0,)), ((), ()))


def _mla_attn_kernel(q_ref, k_ref, v_ref, o_ref, m_ref, acc_ref):
    j = pl.program_id(2)

    @pl.when(j == 0)
    def _():
        m_ref[...] = jnp.full(m_ref.shape, -jnp.inf, f32)
        acc_ref[...] = jnp.zeros(acc_ref.shape, f32)

    for hh in range(MLA_HEADS):
        sl = slice(hh * LANE, (hh + 1) * LANE)
        s = lax.dot_general(q_ref[0, :, sl], k_ref[0, :, sl], _NT, preferred_element_type=f32)
        _online_softmax_step(s, v_ref[0, :, sl], m_ref, acc_ref, hh)

    @pl.when(j == pl.num_programs(2) - 1)
    def _():
        for hh in range(MLA_HEADS):
            acc = acc_ref[hh]
            o_ref[0, :, hh * LANE:(hh + 1) * LANE] = (acc / acc[:, MLA_V:MLA_V + 1]).astype(bf16)


def _diff_attn_kernel(lam_ref, q_ref, k_ref, v_ref, gsub_ref, o_ref, m_ref, acc_ref):
    j = pl.program_id(2)

    @pl.when(j == 0)
    def _():
        m_ref[...] = jnp.full(m_ref.shape, -jnp.inf, f32)
        acc_ref[...] = jnp.zeros(acc_ref.shape, f32)

    q = q_ref[0]
    k = k_ref[0]
    group = lax.broadcasted_iota(jnp.int32, (1, BRANCH), 1) // DIFF_HEAD_DIM
    for g in range(2 * DIFF_HEADS):
        qg = jnp.where(group == g, q, jnp.zeros_like(q))
        s = lax.dot_general(qg, k, _NT, preferred_element_type=f32)
        hh = g // 2
        _online_softmax_step(s, v_ref[0, :, hh * LANE:(hh + 1) * LANE], m_ref, acc_ref, g)

    @pl.when(j == pl.num_programs(2) - 1)
    def _():
        lam = lam_ref[0]
        vmask = lax.broadcasted_iota(jnp.int32, (1, LANE), 1) < 2 * DIFF_HEAD_DIM
        for hh in range(DIFF_HEADS):
            a1 = acc_ref[2 * hh]
            a2 = acc_ref[2 * hh + 1]
            o = a1 / a1[:, 64:65] - lam * (a2 / a2[:, 64:65])
            o = jnp.where(vmask, o, 0.0)
            ms = jnp.sum(o * o, axis=-1, keepdims=True) * (1.0 / (2 * DIFF_HEAD_DIM))
            o_ref[0, :, hh * LANE:(hh + 1) * LANE] = (o * lax.rsqrt(ms + EPS) * gsub_ref[...]).astype(bf16)


def _mla_attn(q, k, v, tq, tk):
    B, S, W = q.shape
    return pl.pallas_call(
        _mla_attn_kernel,
        grid=(B, S // tq, S // tk),
        in_specs=[pl.BlockSpec((1, tq, W), lambda b, i, j: (b, i, 0)),
                  pl.BlockSpec((1, tk, W), lambda b, i, j: (b, j, 0)),
                  pl.BlockSpec((1, tk, W), lambda b, i, j: (b, j, 0))],
        out_specs=pl.BlockSpec((1, tq, W), lambda b, i, j: (b, i, 0)),
        out_shape=jax.ShapeDtypeStruct((B, S, W), bf16),
        scratch_shapes=[pltpu.VMEM((MLA_HEADS, tq, LANE), f32), pltpu.VMEM((MLA_HEADS, tq, LANE), f32)],
        compiler_params=_cparams(("parallel", "parallel", "arbitrary")),
        name="mla_attn",
    )(q, k, v)


def _diff_attn(lam, q, k, v, gsub, tq, tk):
    B, S, W = v.shape
    return pl.pallas_call(
        _diff_attn_kernel,
        grid=(B, S // tq, S // tk),
        in_specs=[pl.BlockSpec(memory_space=pltpu.SMEM),
                  pl.BlockSpec((1, tq, BRANCH), lambda b, i, j: (b, i, 0)),
                  pl.BlockSpec((1, tk, BRANCH), lambda b, i, j: (b, j, 0)),
                  pl.BlockSpec((1, tk, W), lambda b, i, j: (b, j, 0)),
                  pl.BlockSpec((1, LANE), lambda b, i, j: (0, 0))],
        out_specs=pl.BlockSpec((1, tq, W), lambda b, i, j: (b, i, 0)),
        out_shape=jax.ShapeDtypeStruct((B, S, W), bf16),
        scratch_shapes=[pltpu.VMEM((2 * DIFF_HEADS, tq, LANE), f32), pltpu.VMEM((2 * DIFF_HEADS, tq, LANE), f32)],
        compiler_params=_cparams(("parallel", "parallel", "arbitrary")),
        name="diff_attn",
    )(lam, q, k, v, gsub)


def _fnet1_kernel(x_ref, w_ref, a_ref):
    a_ref[0] = jnp.dot(w_ref[...], x_ref[0], preferred_element_type=f32).astype(bf16)


def _fnet2_kernel(a_ref, t_ref, cs_ref, o_ref):
    for jj in range(FNET_K1_BLOCK):
        a = jnp.concatenate([a_ref[0, 0, jj], a_ref[0, 1, jj]], axis=0)
        z = jnp.dot(t_ref[jj], a, preferred_element_type=f32)
        zc = jnp.concatenate([z[:FNET_N2], z[FNET_N2:]], axis=1).astype(bf16)
        o_ref[0, :, jj, :] = jnp.dot(zc, cs_ref[...], preferred_element_type=f32).astype(bf16)


def _fnet_consts(S):
    n1 = S // FNET_N2
    a = np.arange(n1)
    ang1 = 2.0 * np.pi * np.outer(a, a) / n1
    w1 = np.concatenate([np.cos(ang1), -np.sin(ang1)], axis=0)
    n2 = np.arange(FNET_N2)
    phi = 2.0 * np.pi * (n2[None, None, :] * a[:, None, None] / S + n2[None, None, :] * n2[None, :, None] / FNET_N2)
    tr, ti = np.cos(phi), -np.sin(phi)
    t = np.concatenate([np.concatenate([tr, -ti], axis=2), np.concatenate([ti, tr], axis=2)], axis=1)
    c = np.arange(BRANCH)
    same = (c[:, None] // FNET_N2) == (c[None, :] // FNET_N2)
    angc = 2.0 * np.pi * np.outer(c % FNET_N2, c % FNET_N2) / FNET_N2
    norm = 1.0 / math.sqrt(S * FNET_N2)
    cs = np.concatenate([np.where(same, np.cos(angc), 0.0), np.where(same, np.sin(angc), 0.0)], axis=0) * norm
    return (jnp.asarray(w1, f32).astype(bf16), jnp.asarray(t, f32).astype(bf16), jnp.asarray(cs, f32).astype(bf16))


def _fnet(u, consts):
    B, S, W = u.shape
    w1, t, cs = consts
    n1 = S // FNET_N2
    cols = FNET_N2 * W
    tn = min(cols, 4096)
    a = pl.pallas_call(
        _fnet1_kernel,
        grid=(B, cols // tn),
        in_specs=[pl.BlockSpec((1, n1, tn), lambda b, i: (b, 0, i)), _const_spec(w1.shape)],
        out_specs=pl.BlockSpec((1, 2 * n1, tn), lambda b, i: (b, 0, i)),
        out_shape=jax.ShapeDtypeStruct((B, 2 * n1, cols), bf16),
        compiler_params=_cparams(("parallel", "parallel")),
        name="fnet1",
    )(u.reshape(B, n1, cols), w1)
    kb = FNET_K1_BLOCK
    y = pl.pallas_call(
        _fnet2_kernel,
        grid=(B, n1 // kb),
        in_specs=[pl.BlockSpec((1, 2, kb, FNET_N2, W), lambda b, i: (b, 0, i, 0, 0)),
                  pl.BlockSpec((kb, 2 * FNET_N2, 2 * FNET_N2), lambda b, i: (i, 0, 0)),
                  _const_spec(cs.shape)],
        out_specs=pl.BlockSpec((1, FNET_N2, kb, W), lambda b, i: (b, 0, i, 0)),
        out_shape=jax.ShapeDtypeStruct((B, FNET_N2, n1, W), bf16),
        compiler_params=_cparams(("parallel", "parallel")),
        name="fnet2",
    )(a.reshape(B, 2, n1, FNET_N2, W), t, cs)
    return y.reshape(B, S, W)


def _hgrn_consts(C):
    L = int(math.log2(C))
    t = np.arange(C)
    m = np.zeros((2, (L + 2) * C, C), np.float32)
    lvl = np.full((2, C, C), -1, np.int32)
    for d in (0, 1):
        for li in range(L):
            n = C >> li
            half = n // 2
            blk = t // n
            mid = blk * n + half
            upper = (t % n) >= half
            for r in range(C):
                if d == 0:
                    if upper[r]:
                        m[d, li * C + r, mid[r]:r + 1] = 1.0
                    else:
                        m[d, li * C + r, r + 1:mid[r]] = 1.0
                else:
                    if upper[r]:
                        m[d, li * C + r, mid[r]:r] = 1.0
                    else:
                        m[d, li * C + r, r:mid[r]] = 1.0
            same = blk[:, None] == blk[None, :]
            if d == 0:
                msk = same & upper[:, None] & (~upper)[None, :]
            else:
                msk = same & (~upper)[:, None] & upper[None, :]
            lvl[d][msk] = li
        for r in range(C):
            if d == 0:
                m[d, L * C + r, :r + 1] = 1.0
                m[d, (L + 1) * C + r, r + 1:] = 1.0
            else:
                m[d, L * C + r, r:] = 1.0
                m[d, (L + 1) * C + r, :r] = 1.0
    c = np.arange(BRANCH)
    headsum = ((c[:, None] // HGRN_DIM) == (c[None, :] // HGRN_DIM)).astype(np.float32)
    return jnp.asarray(m, f32).astype(bf16), jnp.asarray(lvl), jnp.asarray(headsum, f32).astype(bf16)


def _hgrn_kernel(q_ref, v_ref, k_ref, lf_ref, m_ref, lvl_ref, hs_ref, o_ref, st_ref, *, C, L):
    d = pl.program_id(1)

    @pl.when(pl.program_id(2) == 0)
    def _():
        st_ref[...] = jnp.zeros(st_ref.shape, f32)

    q = q_ref[0].astype(f32)
    k = k_ref[0].astype(f32)
    v = v_ref[0]
    e2 = jnp.dot(m_ref[0], lf_ref[0], preferred_element_type=f32)
    e = e2[:, :BRANCH] + e2[:, BRANCH:]
    head = lax.broadcasted_iota(jnp.int32, (1, BRANCH), 1) // HGRN_DIM
    lvl = lvl_ref[0]

    scores = [jnp.zeros((C, C), f32) for _ in range(HGRN_HEADS)]
    for li in range(L):
        ex = jnp.exp(e[li * C:(li + 1) * C])
        qt = (q * ex).astype(bf16)
        kt = (k * ex).astype(bf16)
        for hh in range(HGRN_HEADS):
            s = lax.dot_general(jnp.where(head == hh, qt, jnp.zeros_like(qt)), kt, _NT, preferred_element_type=f32)
            scores[hh] = jnp.where(lvl == li, s, scores[hh])
    o = jnp.zeros((C, BRANCH), f32)
    for hh in range(HGRN_HEADS):
        vh = jnp.where(head == hh, v, jnp.zeros_like(v))
        o = o + jnp.dot(scores[hh].astype(bf16), vh, preferred_element_type=f32)
    o = o + jnp.dot((q * k).astype(bf16), hs_ref[...], preferred_element_type=f32) * v.astype(f32)
    ein = jnp.exp(e[L * C:(L + 1) * C])
    st = st_ref[...]
    o = o + lax.dot_general((q * ein).astype(bf16), st.astype(bf16), _NT, preferred_element_type=f32)
    o_ref[0, 0] = o.astype(bf16)
    kout = (k * jnp.exp(e[(L + 1) * C:(L + 2) * C])).astype(bf16)
    upd = lax.dot_general(v, kout, _TN, preferred_element_type=f32)
    total = jnp.where(d == 0, ein[C - 1:C], ein[0:1])
    r = lax.broadcasted_iota(jnp.int32, (BRANCH, BRANCH), 0) // HGRN_DIM
    cidx = lax.broadcasted_iota(jnp.int32, (BRANCH, BRANCH), 1) // HGRN_DIM
    st_ref[...] = st * total + jnp.where(r == cidx, upd, 0.0)


def _hgrn(hq, hv, hk, hlf, consts, C):
    B, S, W = hq.shape
    m, lvl, hs = consts
    L = int(math.log2(C))
    nb = S // C
    blk = lambda b, d, i: i + d * (nb - 1 - 2 * i)
    return pl.pallas_call(
        functools.partial(_hgrn_kernel, C=C, L=L),
        grid=(B, 2, nb),
        in_specs=[pl.BlockSpec((1, C, W), lambda b, d, i: (b, blk(b, d, i), 0)),
                  pl.BlockSpec((1, C, W), lambda b, d, i: (b, blk(b, d, i), 0)),
                  pl.BlockSpec((1, C, W), lambda b, d, i: (b, blk(b, d, i), d)),
                  pl.BlockSpec((1, C, 2 * W), lambda b, d, i: (b, blk(b, d, i), d)),
                  pl.BlockSpec((1, (L + 2) * C, C), lambda b, d, i: (d, 0, 0)),
                  pl.BlockSpec((1, C, C), lambda b, d, i: (d, 0, 0)),
                  _const_spec(hs.shape)],
        out_specs=pl.BlockSpec((1, 1, C, W), lambda b, d, i: (d, b, blk(b, d, i), 0)),
        out_shape=jax.ShapeDtypeStruct((2, B, S, W), bf16),
        scratch_shapes=[pltpu.VMEM((BRANCH, BRANCH), f32)],
        compiler_params=_cparams(("parallel", "arbitrary", "arbitrary")),
        name="hgrn",
    )(hq, hv, hk, hlf, m, lvl, hs)


def _merge_kernel(x_ref, om_ref, of_ref, od_ref, oh_ref, hg_ref, ln_ref, wg_ref, wbm_ref, wbf_ref, wbd_ref,
                  wbh_ref, wo_ref, gon_ref, g64_ref, o_ref):
    x = x_ref[0]
    h = _rms(x, ln_ref[...]).astype(bf16)
    oh = oh_ref[0, 0].astype(f32) + oh_ref[1, 0].astype(f32)
    ms = jnp.dot((oh * oh).astype(bf16), g64_ref[...], preferred_element_type=f32)
    oh = (oh * lax.rsqrt(ms + EPS) * gon_ref[...] * _sigmoid(hg_ref[0].astype(f32))).astype(bf16)
    merged = None
    for n, (o_n, w_ref) in enumerate(((om_ref[0], wbm_ref), (of_ref[0], wbf_ref), (od_ref[0], wbd_ref), (oh, wbh_ref))):
        gate = _sigmoid(jnp.dot(h, wg_ref[:, n * D_MODEL:(n + 1) * D_MODEL], preferred_element_type=f32))
        y = gate * jnp.dot(o_n, w_ref[...], preferred_element_type=f32)
        merged = y if merged is None else merged + y
    o_ref[0] = x + jnp.dot(merged.astype(bf16), wo_ref[...], preferred_element_type=f32)


def _merge(x, om, of, od, oh, hg, p, tm):
    B, S, _ = x.shape
    row = lambda w: pl.BlockSpec((1, tm, w), lambda b, i: (b, i, 0))
    consts = (p["ln_mix"], p["wgate"], p["wbm"], p["wbf"], p["wbd"], p["wbh"], p["wout"], p["gon"], p["g64"])
    return pl.pallas_call(
        _merge_kernel,
        grid=(B, S // tm),
        in_specs=[row(D_MODEL), row(512), row(BRANCH), row(512),
                  pl.BlockSpec((2, 1, tm, BRANCH), lambda b, i: (0, b, i, 0)), row(BRANCH)]
                 + [_const_spec(c.shape) for c in consts],
        out_specs=row(D_MODEL),
        out_shape=jax.ShapeDtypeStruct((B, S, D_MODEL), f32),
        compiler_params=_cparams(("parallel", "parallel")),
        name="merge",
    )(x, om, of, od, oh, hg, *consts)


def _ffn_kernel(x_ref, ln_ref, wgu_ref, wd_ref, o_ref):
    x = x_ref[0]
    h = _rms(x, ln_ref[...]).astype(bf16)

    def chunk(c, acc):
        gu = jnp.dot(h, wgu_ref[c], preferred_element_type=f32)
        g = gu[:, :FF_CHUNK]
        a = (g * _sigmoid(g) * gu[:, FF_CHUNK:]).astype(bf16)
        return acc + jnp.dot(a, wd_ref[c], preferred_element_type=f32)

    o_ref[0] = lax.fori_loop(0, D_FF // FF_CHUNK, chunk, x)


def _ffn(x, p, tm):
    B, S, _ = x.shape
    row = pl.BlockSpec((1, tm, D_MODEL), lambda b, i: (b, i, 0))
    consts = (p["ln_ffn"], p["wgu"], p["wd"])
    return pl.pallas_call(
        _ffn_kernel,
        grid=(B, S // tm),
        in_specs=[row] + [_const_spec(c.shape) for c in consts],
        out_specs=row,
        out_shape=jax.ShapeDtypeStruct((B, S, D_MODEL), f32),
        compiler_params=_cparams(("parallel", "parallel")),
        name="ffn",
    )(x, *consts)


def _pad_heads(w, heads, width):
    lead = w.shape[:-1]
    w = w.reshape(lead + (heads, width))
    w = jnp.pad(w, [(0, 0)] * len(lead) + [(0, 0), (0, LANE - width)])
    return w.reshape(lead + (heads * LANE,))


def _rope_tables(S):
    def cs(dim):
        inv = 1.0 / (ROPE_THETA ** (jnp.arange(0, dim, 2, dtype=f32) / dim))
        ang = jnp.arange(S, dtype=f32)[:, None] * inv[None, :]
        return jnp.cos(ang), jnp.sin(ang)

    cm, sm = cs(MLA_ROPE)
    one, zero = jnp.ones((S, MLA_NOPE), f32), jnp.zeros((S, MLA_NOPE), f32)
    z16, z32, o32 = jnp.zeros((S, 16), f32), jnp.zeros((S, 32), f32), jnp.ones((S, 32), f32)
    rope_m = jnp.stack([jnp.concatenate([one, cm, cm, o32], 1),
                        jnp.concatenate([zero, -sm, z16, z32], 1),
                        jnp.concatenate([zero, z16, sm, z32], 1)])
    cd, sd = cs(DIFF_HEAD_DIM)
    rope_d = jnp.stack([jnp.tile(jnp.concatenate([cd, cd], 1), (1, 4)),
                        jnp.tile(jnp.concatenate([-sd, z16], 1), (1, 4)),
                        jnp.tile(jnp.concatenate([z16, sd], 1), (1, 4))])
    return rope_m, rope_d


def _block_mean(width, group):
    c = np.arange(width)
    return jnp.asarray(((c[:, None] // group) == (c[None, :] // group)) / group, f32).astype(bf16)


def _layer_params(l, a, lower_bounds):
    w = a["w_in"][l]
    offs = np.cumsum([0, 384, 128, 32, 256, 256, 256, 256, 256, 256, 256, 256, 256, 4096])
    col = lambda i: w[:, offs[i]:offs[i + 1]]
    wcat = jnp.concatenate([col(0), col(1), jnp.pad(col(2), ((0, 0), (MLA_NOPE, LANE - MLA_QK))), col(3), col(4),
                            col(5), _pad_heads(col(6), DIFF_HEADS, 2 * DIFF_HEAD_DIM), col(7), col(8), col(9),
                            col(10), col(11)], axis=1).astype(bf16)
    wukv = a["mla_w_ukv"][l].reshape(MLA_KV_LORA, MLA_HEADS, MLA_NOPE + MLA_V)
    wukv = jnp.concatenate([_pad_heads(wukv[:, :, :MLA_NOPE].reshape(MLA_KV_LORA, -1), MLA_HEADS, MLA_NOPE),
                            _pad_heads(wukv[:, :, MLA_NOPE:].reshape(MLA_KV_LORA, -1), MLA_HEADS, MLA_V)], axis=1)
    lam_init = 0.8 - 0.6 * math.exp(-0.3 * l)
    lam = (jnp.exp(jnp.sum(a["diff_lq1"][l] * a["diff_lk1"][l])) - jnp.exp(jnp.sum(a["diff_lq2"][l] * a["diff_lk2"][l]))
           + lam_init)
    wb = a["w_branch"][l]
    wgu = a["w_gate_up"][l]
    nc = D_FF // FF_CHUNK
    wgu = jnp.concatenate([wgu[:, :D_FF].reshape(D_MODEL, nc, FF_CHUNK), wgu[:, D_FF:].reshape(D_MODEL, nc, FF_CHUNK)],
                          axis=2).transpose(1, 0, 2)
    row = lambda v: v.reshape(1, -1).astype(f32)
    return {
        "ln_mix": row(a["ln_mix"][l]),
        "wcat": wcat,
        "gqa": row(a["mla_g_qa"][l]),
        "wuq": _pad_heads(a["mla_w_uq"][l], MLA_HEADS, MLA_QK).astype(bf16),
        "gkva": row(a["mla_g_kva"][l]),
        "wukv": wukv.astype(bf16),
        "gqn": row(jnp.pad(a["mla_g_qn"][l], (0, LANE - MLA_QK))) * (MLA_QK ** -0.5 * LOG2E),
        "gkn": row(jnp.pad(a["mla_g_kn"][l], (0, LANE - MLA_QK))),
        "gdq": row(jnp.tile(a["diff_g_qn"][l], 2 * DIFF_HEADS)) * (DIFF_HEAD_DIM ** -0.5 * LOG2E),
        "gdk": row(jnp.tile(a["diff_g_kn"][l], 2 * DIFF_HEADS)),
        "g32": _block_mean(BRANCH, DIFF_HEAD_DIM),
        "lb": lower_bounds[:, l].astype(f32),
        "lam": lam.reshape(1).astype(f32),
        "gsub": row(jnp.pad(a["diff_g_sub"][l], (0, LANE - 2 * DIFF_HEAD_DIM))) * (1.0 - lam_init),
        "wgate": col(12).astype(bf16),
        "wbm": _pad_heads(wb[0].T, MLA_HEADS, MLA_V).T.astype(bf16),
        "wbf": wb[1].astype(bf16),
        "wbd": _pad_heads(wb[2].T, DIFF_HEADS, 2 * DIFF_HEAD_DIM).T.astype(bf16),
        "wbh": wb[3].astype(bf16),
        "wout": a["w_out"][l].astype(bf16),
        "gon": row(jnp.tile(a["hgrn_g_on"][l], HGRN_HEADS)),
        "g64": _block_mean(BRANCH, HGRN_DIM),
        "ln_ffn": row(a["ln_ffn"][l]),
        "wgu": wgu.astype(bf16),
        "wd": a["w_down"][l].reshape(nc, FF_CHUNK, D_MODEL).astype(bf16),
    }


def _tiles(S):
    return {"tm_in": min(256, S), "tq": min(512, S), "tk": min(512, S), "hgrn_chunk": min(256, S),
            "tm_merge": min(256, S), "tm_ffn": min(512, S)}


def kernel(x, ln_mix, w_in, mla_g_qa, mla_w_uq, mla_g_kva, mla_w_ukv, mla_g_qn, mla_g_kn, diff_g_qn, diff_g_kn,
           diff_lq1, diff_lk1, diff_lq2, diff_lk2, diff_g_sub, hgrn_lb_logits, hgrn_g_on, w_branch, w_out, ln_ffn,
           w_gate_up, w_down):
    a = dict(ln_mix=ln_mix, w_in=w_in, mla_g_qa=mla_g_qa, mla_w_uq=mla_w_uq, mla_g_kva=mla_g_kva,
             mla_w_ukv=mla_w_ukv, mla_g_qn=mla_g_qn, mla_g_kn=mla_g_kn, diff_g_qn=diff_g_qn, diff_g_kn=diff_g_kn,
             diff_lq1=diff_lq1, diff_lk1=diff_lk1, diff_lq2=diff_lq2, diff_lk2=diff_lk2, diff_g_sub=diff_g_sub,
             hgrn_g_on=hgrn_g_on, w_branch=w_branch, w_out=w_out, ln_ffn=ln_ffn, w_gate_up=w_gate_up, w_down=w_down)
    S = x.shape[1]
    t = _tiles(S)
    rope_m, rope_d = _rope_tables(S)
    lb_p = jax.nn.softmax(hgrn_lb_logits.astype(f32), axis=1)
    lower_bounds = jnp.cumsum(lb_p, axis=1) - lb_p[:, :1]
    fnet_consts = _fnet_consts(S)
    hgrn_consts = _hgrn_consts(t["hgrn_chunk"])
    for l in range(DEPTH):
        p = _layer_params(l, a, lower_bounds)
        qm, km, vm, fn, qd, kd, vd, hq, hv, hk, hlf, hg = _in_proj(x, p, rope_m, rope_d, t["tm_in"])
        om = _mla_attn(qm, km, vm, t["tq"], t["tk"])
        od = _diff_attn(p["lam"], qd, kd, vd, p["gsub"], t["tq"], t["tk"])
        of = _fnet(fn, fnet_consts)
        oh = _hgrn(hq, hv, hk, hlf, hgrn_consts, t["hgrn_chunk"])
        x = _merge(x, om, of, od, oh, hg, p, t["tm_merge"])
        x = _ffn(x, p, t["tm_ffn"])
    return x
```

```python
import functools
import math

import numpy as np
import jax
import jax.numpy as jnp
from jax import lax
from jax.experimental import pallas as pl
from jax.experimental.pallas import tpu as pltpu

f32 = jnp.float32
bf16 = jnp.bfloat16

D_MODEL = 1024
DEPTH = 2
MLA_HEADS = 4
MLA_Q_LORA = 384
MLA_KV_LORA = 128
MLA_NOPE = 64
MLA_ROPE = 32
MLA_V = 64
MLA_QK = MLA_NOPE + MLA_ROPE
DIFF_HEADS = 4
DIFF_HEAD_DIM = 32
HGRN_HEADS = 4
HGRN_DIM = 64
BRANCH = 256
D_FF = 2816
FF_CHUNK = 256
ROPE_THETA = 10000.0
EPS = 1e-6
LOG2E = 1.4426950408889634
LANE = 128
FNET_N2 = 64
FNET_K1_BLOCK = 8
VMEM_LIMIT = 56 * 1024 * 1024
SCORE_BOUND_LOG2 = 60.0

_SEG = {}
_off = 0
for _name, _w in (("cq", 384), ("ckv", 128), ("krope", 128), ("fnet", 256), ("dq", 256), ("dk", 256),
                  ("dv", 512), ("hq", 256), ("hi", 256), ("hff", 256), ("hfb", 256), ("hg", 256)):
    _SEG[_name] = (_off, _off + _w)
    _off += _w
W_CAT = _off


def _cparams(sem):
    return pltpu.CompilerParams(dimension_semantics=sem, vmem_limit_bytes=VMEM_LIMIT)


def _const_spec(shape):
    nd = len(shape)
    return pl.BlockSpec(shape, lambda *_: (0,) * nd, pipeline_mode=pl.Buffered(1))


def _rms(x, g):
    return x * lax.rsqrt(jnp.mean(x * x, axis=-1, keepdims=True) + EPS) * g


def _sigmoid(x):
    return 1.0 / (1.0 + jnp.exp(-x))


def _rope(t, tab_ref):
    return t * tab_ref[0] + pltpu.roll(t, LANE - 16, 1) * tab_ref[1] + pltpu.roll(t, 16, 1) * tab_ref[2]


def _in_proj_kernel(x_ref, ln_ref, wcat_ref, gqa_ref, wuq_ref, gkva_ref, wukv_ref, gqn_ref, gkn_ref,
                    ropem_ref, gdq_ref, gdk_ref, g32_ref, roped_ref, lb_ref,
                    qm_ref, km_ref, vm_ref, fn_ref, qd_ref, kd_ref, vd_ref, hq_ref, hv_ref, hk_ref,
                    hlf_ref, hg_ref):
    h = _rms(x_ref[0], ln_ref[...]).astype(bf16)

    def seg(name):
        a, b = _SEG[name]
        return jnp.dot(h, wcat_ref[:, a:b], preferred_element_type=f32)

    ones_lane = (lax.broadcasted_iota(jnp.int32, (1, LANE), 1) == MLA_V).astype(f32)

    cqn = _rms(seg("cq"), gqa_ref[...]).astype(bf16)
    q = jnp.dot(cqn, wuq_ref[...], preferred_element_type=f32)
    for hh in range(MLA_HEADS):
        sl = slice(hh * LANE, (hh + 1) * LANE)
        qh = q[:, sl]
        ms = jnp.sum(qh * qh, axis=-1, keepdims=True) * (1.0 / MLA_QK)
        qm_ref[0, :, sl] = _rope(qh * lax.rsqrt(ms + EPS) * gqn_ref[...], ropem_ref).astype(bf16)

    ckvn = _rms(seg("ckv"), gkva_ref[...]).astype(bf16)
    kv = jnp.dot(ckvn, wukv_ref[...], preferred_element_type=f32)
    krope = seg("krope")
    for hh in range(MLA_HEADS):
        sl = slice(hh * LANE, (hh + 1) * LANE)
        kh = kv[:, sl] + krope
        ms = jnp.sum(kh * kh, axis=-1, keepdims=True) * (1.0 / MLA_QK)
        km_ref[0, :, sl] = _rope(kh * lax.rsqrt(ms + EPS) * gkn_ref[...], ropem_ref).astype(bf16)
        vsl = slice((MLA_HEADS + hh) * LANE, (MLA_HEADS + hh + 1) * LANE)
        vm_ref[0, :, sl] = (kv[:, vsl] + ones_lane).astype(bf16)

    fn_ref[0] = seg("fnet").astype(bf16)

    for name, g_ref, o_ref in (("dq", gdq_ref, qd_ref), ("dk", gdk_ref, kd_ref)):
        t = seg(name)
        ms = jnp.dot((t * t).astype(bf16), g32_ref[...], preferred_element_type=f32)
        t = t * lax.rsqrt(ms + EPS) * g_ref[...]
        for c in range(BRANCH // LANE):
            sl = slice(c * LANE, (c + 1) * LANE)
            o_ref[0, :, sl] = _rope(t[:, sl], roped_ref).astype(bf16)
    dv = seg("dv")
    for hh in range(DIFF_HEADS):
        sl = slice(hh * LANE, (hh + 1) * LANE)
        vd_ref[0, :, sl] = (dv[:, sl] + ones_lane).astype(bf16)

    hq_ref[0] = seg("hq").astype(bf16)
    hv_ref[0] = seg("hi").astype(bf16)
    hg_ref[0] = seg("hg").astype(bf16)
    for d, name in enumerate(("hff", "hfb")):
        lb = lb_ref[d:d + 1, :]
        f = lb + (1.0 - lb) * _sigmoid(seg(name))
        hk_ref[0, :, d * BRANCH:(d + 1) * BRANCH] = (1.0 - f).astype(bf16)
        lf = jnp.log(f)
        hi = lf.astype(bf16)
        hlf_ref[0, :, 2 * d * BRANCH:(2 * d + 1) * BRANCH] = hi
        hlf_ref[0, :, (2 * d + 1) * BRANCH:(2 * d + 2) * BRANCH] = (lf - hi.astype(f32)).astype(bf16)


def _in_proj(x, p, rope_m, rope_d, tm):
    B, S, _ = x.shape
    row = lambda w: pl.BlockSpec((1, tm, w), lambda b, i: (b, i, 0))
    tab = pl.BlockSpec((3, tm, LANE), lambda b, i: (0, i, 0))
    out_w = (512, 512, 512, 256, 256, 256, 512, 256, 256, 512, 1024, 256)
    consts = (p["ln_mix"], p["wcat"], p["gqa"], p["wuq"], p["gkva"], p["wukv"], p["gqn"], p["gkn"])
    consts2 = (p["gdq"], p["gdk"], p["g32"])
    in_specs = ([row(D_MODEL)] + [_const_spec(c.shape) for c in consts] + [tab]
                + [_const_spec(c.shape) for c in consts2] + [tab, _const_spec(p["lb"].shape)])
    return pl.pallas_call(
        _in_proj_kernel,
        grid=(B, S // tm),
        in_specs=in_specs,
        out_specs=[row(w) for w in out_w],
        out_shape=[jax.ShapeDtypeStruct((B, S, w), bf16) for w in out_w],
        compiler_params=_cparams(("parallel", "parallel")),
        name="in_proj",
    )(x, *consts, rope_m, *consts2, rope_d, p["lb"])


def _online_softmax_step(s, v, m_ref, acc_ref, idx):
    m_prev = m_ref[idx]
    m_new = jnp.maximum(m_prev, jnp.max(s, axis=1, keepdims=True))
    alpha = jnp.exp2(m_prev - m_new)
    p = jnp.exp2((s - m_new[:, :1]).astype(bf16))
    acc_ref[idx] = alpha * acc_ref[idx] + jnp.dot(p, v, preferred_element_type=f32)
    m_ref[idx] = m_new


_NT = (((1,), (1,)), ((), ()))
_TN = (((0,), (0,)), ((), ()))


def _mla_attn_kernel(q_ref, k_ref, v_ref, o_ref, m_ref, acc_ref):
    j = pl.program_id(2)

    @pl.when(j == 0)
    def _():
        m_ref[...] = jnp.full(m_ref.shape, -jnp.inf, f32)
        acc_ref[...] = jnp.zeros(acc_ref.shape, f32)

    for hh in range(MLA_HEADS):
        sl = slice(hh * LANE, (hh + 1) * LANE)
        s = lax.dot_general(q_ref[0, :, sl], k_ref[0, :, sl], _NT, preferred_element_type=f32)
        _online_softmax_step(s, v_ref[0, :, sl], m_ref, acc_ref, hh)

    @pl.when(j == pl.num_programs(2) - 1)
    def _():
        for hh in range(MLA_HEADS):
            acc = acc_ref[hh]
            o_ref[0, :, hh * LANE:(hh + 1) * LANE] = (acc / acc[:, MLA_V:MLA_V + 1]).astype(bf16)


def _diff_finalize(lam_ref, gsub_ref, acc_ref, o_ref):
    lam = lam_ref[0]
    vmask = lax.broadcasted_iota(jnp.int32, (1, LANE), 1) < 2 * DIFF_HEAD_DIM
    for hh in range(DIFF_HEADS):
        a1 = acc_ref[2 * hh]
        a2 = acc_ref[2 * hh + 1]
        o = a1 / a1[:, 64:65] - lam * (a2 / a2[:, 64:65])
        o = jnp.where(vmask, o, 0.0)
        ms = jnp.sum(o * o, axis=-1, keepdims=True) * (1.0 / (2 * DIFF_HEAD_DIM))
        o_ref[0, :, hh * LANE:(hh + 1) * LANE] = (o * lax.rsqrt(ms + EPS) * gsub_ref[...]).astype(bf16)


def _diff_attn_kernel(lam_ref, q_ref, k_ref, v_ref, gsub_ref, o_ref, m_ref, acc_ref):
    j = pl.program_id(2)

    @pl.when(j == 0)
    def _():
        m_ref[...] = jnp.full(m_ref.shape, -jnp.inf, f32)
        acc_ref[...] = jnp.zeros(acc_ref.shape, f32)

    q = q_ref[0]
    k = k_ref[0]
    group = lax.broadcasted_iota(jnp.int32, (1, BRANCH), 1) // DIFF_HEAD_DIM
    for g in range(2 * DIFF_HEADS):
        qg = jnp.where(group == g, q, jnp.zeros_like(q))
        s = lax.dot_general(qg, k, _NT, preferred_element_type=f32)
        hh = g // 2
        _online_softmax_step(s, v_ref[0, :, hh * LANE:(hh + 1) * LANE], m_ref, acc_ref, g)

    @pl.when(j == pl.num_programs(2) - 1)
    def _():
        _diff_finalize(lam_ref, gsub_ref, acc_ref, o_ref)


def _mla_attn_bounded_kernel(q_ref, k_ref, v_ref, o_ref, acc_ref, *, tk):
    acc_ref[...] = jnp.zeros(acc_ref.shape, f32)

    def kv_block(j, carry):
        rows = pl.ds(pl.multiple_of(j * tk, tk), tk)
        for hh in range(MLA_HEADS):
            sl = slice(hh * LANE, (hh + 1) * LANE)
            s = lax.dot_general(q_ref[0, :, sl], k_ref[0, rows, sl], _NT, preferred_element_type=f32)
            acc_ref[hh] += jnp.dot(jnp.exp2(s).astype(bf16), v_ref[0, rows, sl], preferred_element_type=f32)
        return carry

    lax.fori_loop(0, k_ref.shape[1] // tk, kv_block, 0)
    for hh in range(MLA_HEADS):
        acc = acc_ref[hh]
        o_ref[0, :, hh * LANE:(hh + 1) * LANE] = (acc / acc[:, MLA_V:MLA_V + 1]).astype(bf16)


def _diff_attn_bounded_kernel(lam_ref, q_ref, k_ref, v_ref, gsub_ref, o_ref, acc_ref, *, tk):
    acc_ref[...] = jnp.zeros(acc_ref.shape, f32)
    q = q_ref[0]
    group = lax.broadcasted_iota(jnp.int32, (1, BRANCH), 1) // DIFF_HEAD_DIM
    qs = [jnp.where(group == g, q, jnp.zeros_like(q)) for g in range(2 * DIFF_HEADS)]

    def kv_block(j, carry):
        rows = pl.ds(pl.multiple_of(j * tk, tk), tk)
        k = k_ref[0, rows, :]
        for g in range(2 * DIFF_HEADS):
            s = lax.dot_general(qs[g], k, _NT, preferred_element_type=f32)
            hh = g // 2
            acc_ref[g] += jnp.dot(jnp.exp2(s).astype(bf16), v_ref[0, rows, hh * LANE:(hh + 1) * LANE],
                                  preferred_element_type=f32)
        return carry

    lax.fori_loop(0, k_ref.shape[1] // tk, kv_block, 0)
    _diff_finalize(lam_ref, gsub_ref, acc_ref, o_ref)


def _attn_bounded_call(kernel_fn, n_acc, name, args, in_specs, S, W, B, tq):
    return pl.pallas_call(
        kernel_fn,
        grid=(B, S // tq),
        in_specs=in_specs,
        out_specs=pl.BlockSpec((1, tq, W), lambda b, i: (b, i, 0)),
        out_shape=jax.ShapeDtypeStruct((B, S, W), bf16),
        scratch_shapes=[pltpu.VMEM((n_acc, tq, LANE), f32)],
        compiler_params=_cparams(("parallel", "parallel")),
        name=name,
    )(*args)


def _mla_attn_bounded(q, k, v, tq, tk):
    B, S, W = q.shape
    full = pl.BlockSpec((1, S, W), lambda b, i: (b, 0, 0))
    in_specs = [pl.BlockSpec((1, tq, W), lambda b, i: (b, i, 0)), full, full]
    return _attn_bounded_call(functools.partial(_mla_attn_bounded_kernel, tk=tk), MLA_HEADS, "mla_attn_bounded",
                              (q, k, v), in_specs, S, W, B, tq)


def _diff_attn_bounded(lam, q, k, v, gsub, tq, tk):
    B, S, W = v.shape
    in_specs = [pl.BlockSpec(memory_space=pltpu.SMEM),
                pl.BlockSpec((1, tq, BRANCH), lambda b, i: (b, i, 0)),
                pl.BlockSpec((1, S, BRANCH), lambda b, i: (b, 0, 0)),
                pl.BlockSpec((1, S, W), lambda b, i: (b, 0, 0)),
                pl.BlockSpec((1, LANE), lambda b, i: (0, 0))]
    return _attn_bounded_call(functools.partial(_diff_attn_bounded_kernel, tk=tk), 2 * DIFF_HEADS,
                              "diff_attn_bounded", (lam, q, k, v, gsub), in_specs, S, W, B, tq)


def _mla_attn(q, k, v, tq, tk):
    B, S, W = q.shape
    return pl.pallas_call(
        _mla_attn_kernel,
        grid=(B, S // tq, S // tk),
        in_specs=[pl.BlockSpec((1, tq, W), lambda b, i, j: (b, i, 0)),
                  pl.BlockSpec((1, tk, W), lambda b, i, j: (b, j, 0)),
                  pl.BlockSpec((1, tk, W), lambda b, i, j: (b, j, 0))],
        out_specs=pl.BlockSpec((1, tq, W), lambda b, i, j: (b, i, 0)),
        out_shape=jax.ShapeDtypeStruct((B, S, W), bf16),
        scratch_shapes=[pltpu.VMEM((MLA_HEADS, tq, LANE), f32), pltpu.VMEM((MLA_HEADS, tq, LANE), f32)],
        compiler_params=_cparams(("parallel", "parallel", "arbitrary")),
        name="mla_attn",
    )(q, k, v)


def _diff_attn(lam, q, k, v, gsub, tq, tk):
    B, S, W = v.shape
    return pl.pallas_call(
        _diff_attn_kernel,
        grid=(B, S // tq, S // tk),
        in_specs=[pl.BlockSpec(memory_space=pltpu.SMEM),
                  pl.BlockSpec((1, tq, BRANCH), lambda b, i, j: (b, i, 0)),
                  pl.BlockSpec((1, tk, BRANCH), lambda b, i, j: (b, j, 0)),
                  pl.BlockSpec((1, tk, W), lambda b, i, j: (b, j, 0)),
                  pl.BlockSpec((1, LANE), lambda b, i, j: (0, 0))],
        out_specs=pl.BlockSpec((1, tq, W), lambda b, i, j: (b, i, 0)),
        out_shape=jax.ShapeDtypeStruct((B, S, W), bf16),
        scratch_shapes=[pltpu.VMEM((2 * DIFF_HEADS, tq, LANE), f32), pltpu.VMEM((2 * DIFF_HEADS, tq, LANE), f32)],
        compiler_params=_cparams(("parallel", "parallel", "arbitrary")),
        name="diff_attn",
    )(lam, q, k, v, gsub)


def _fnet1_kernel(x_ref, w_ref, a_ref):
    a_ref[0] = jnp.dot(w_ref[...], x_ref[0], preferred_element_type=f32).astype(bf16)


def _fnet2_kernel(a_ref, t_ref, cs_ref, o_ref):
    for jj in range(FNET_K1_BLOCK):
        a = jnp.concatenate([a_ref[0, 0, jj], a_ref[0, 1, jj]], axis=0)
        z = jnp.dot(t_ref[jj], a, preferred_element_type=f32)
        zc = jnp.concatenate([z[:FNET_N2], z[FNET_N2:]], axis=1).astype(bf16)
        o_ref[0, :, jj, :] = jnp.dot(zc, cs_ref[...], preferred_element_type=f32).astype(bf16)


def _fnet_consts(S):
    n1 = S // FNET_N2
    a = np.arange(n1)
    ang1 = 2.0 * np.pi * np.outer(a, a) / n1
    w1 = np.concatenate([np.cos(ang1), -np.sin(ang1)], axis=0)
    n2 = np.arange(FNET_N2)
    phi = 2.0 * np.pi * (n2[None, None, :] * a[:, None, None] / S + n2[None, None, :] * n2[None, :, None] / FNET_N2)
    tr, ti = np.cos(phi), -np.sin(phi)
    t = np.concatenate([np.concatenate([tr, -ti], axis=2), np.concatenate([ti, tr], axis=2)], axis=1)
    c = np.arange(BRANCH)
    same = (c[:, None] // FNET_N2) == (c[None, :] // FNET_N2)
    angc = 2.0 * np.pi * np.outer(c % FNET_N2, c % FNET_N2) / FNET_N2
    norm = 1.0 / math.sqrt(S * FNET_N2)
    cs = np.concatenate([np.where(same, np.cos(angc), 0.0), np.where(same, np.sin(angc), 0.0)], axis=0) * norm
    return (jnp.asarray(w1, f32).astype(bf16), jnp.asarray(t, f32).astype(bf16), jnp.asarray(cs, f32).astype(bf16))


def _fnet(u, consts):
    B, S, W = u.shape
    w1, t, cs = consts
    n1 = S // FNET_N2
    cols = FNET_N2 * W
    tn = min(cols, 4096)
    a = pl.pallas_call(
        _fnet1_kernel,
        grid=(B, cols // tn),
        in_specs=[pl.BlockSpec((1, n1, tn), lambda b, i: (b, 0, i)), _const_spec(w1.shape)],
        out_specs=pl.BlockSpec((1, 2 * n1, tn), lambda b, i: (b, 0, i)),
        out_shape=jax.ShapeDtypeStruct((B, 2 * n1, cols), bf16),
        compiler_params=_cparams(("parallel", "parallel")),
        name="fnet1",
    )(u.reshape(B, n1, cols), w1)
    kb = FNET_K1_BLOCK
    y = pl.pallas_call(
        _fnet2_kernel,
        grid=(B, n1 // kb),
        in_specs=[pl.BlockSpec((1, 2, kb, FNET_N2, W), lambda b, i: (b, 0, i, 0, 0)),
                  pl.BlockSpec((kb, 2 * FNET_N2, 2 * FNET_N2), lambda b, i: (i, 0, 0)),
                  _const_spec(cs.shape)],
        out_specs=pl.BlockSpec((1, FNET_N2, kb, W), lambda b, i: (b, 0, i, 0)),
        out_shape=jax.ShapeDtypeStruct((B, FNET_N2, n1, W), bf16),
        compiler_params=_cparams(("parallel", "parallel")),
        name="fnet2",
    )(a.reshape(B, 2, n1, FNET_N2, W), t, cs)
    return y.reshape(B, S, W)


def _hgrn_consts(C):
    L = int(math.log2(C))
    t = np.arange(C)
    m = np.zeros((2, (L + 2) * C, C), np.float32)
    lvl = np.full((2, C, C), -1, np.int32)
    for d in (0, 1):
        for li in range(L):
            n = C >> li
            half = n // 2
            blk = t // n
            mid = blk * n + half
            upper = (t % n) >= half
            for r in range(C):
                if d == 0:
                    if upper[r]:
                        m[d, li * C + r, mid[r]:r + 1] = 1.0
                    else:
                        m[d, li * C + r, r + 1:mid[r]] = 1.0
                else:
                    if upper[r]:
                        m[d, li * C + r, mid[r]:r] = 1.0
                    else:
                        m[d, li * C + r, r:mid[r]] = 1.0
            same = blk[:, None] == blk[None, :]
            if d == 0:
                msk = same & upper[:, None] & (~upper)[None, :]
            else:
                msk = same & (~upper)[:, None] & upper[None, :]
            lvl[d][msk] = li
        for r in range(C):
            if d == 0:
                m[d, L * C + r, :r + 1] = 1.0
                m[d, (L + 1) * C + r, r + 1:] = 1.0
            else:
                m[d, L * C + r, r:] = 1.0
                m[d, (L + 1) * C + r, :r] = 1.0
    c = np.arange(BRANCH)
    headsum = ((c[:, None] // HGRN_DIM) == (c[None, :] // HGRN_DIM)).astype(np.float32)
    return jnp.asarray(m, f32).astype(bf16), jnp.asarray(lvl), jnp.asarray(headsum, f32).astype(bf16)


def _hgrn_kernel(q_ref, v_ref, k_ref, lf_ref, m_ref, lvl_ref, hs_ref, o_ref, st_ref, *, C, L):
    d = pl.program_id(1)

    @pl.when(pl.program_id(2) == 0)
    def _():
        st_ref[...] = jnp.zeros(st_ref.shape, f32)

    q = q_ref[0].astype(f32)
    k = k_ref[0].astype(f32)
    v = v_ref[0]
    e2 = jnp.dot(m_ref[0], lf_ref[0], preferred_element_type=f32)
    e = e2[:, :BRANCH] + e2[:, BRANCH:]
    head = lax.broadcasted_iota(jnp.int32, (1, BRANCH), 1) // HGRN_DIM
    lvl = lvl_ref[0]

    scores = [jnp.zeros((C, C), f32) for _ in range(HGRN_HEADS)]
    for li in range(L):
        ex = jnp.exp(e[li * C:(li + 1) * C])
        qt = (q * ex).astype(bf16)
        kt = (k * ex).astype(bf16)
        for hh in range(HGRN_HEADS):
            s = lax.dot_general(jnp.where(head == hh, qt, jnp.zeros_like(qt)), kt, _NT, preferred_element_type=f32)
            scores[hh] = jnp.where(lvl == li, s, scores[hh])
    o = jnp.zeros((C, BRANCH), f32)
    for hh in range(HGRN_HEADS):
        vh = jnp.where(head == hh, v, jnp.zeros_like(v))
        o = o + jnp.dot(scores[hh].astype(bf16), vh, preferred_element_type=f32)
    o = o + jnp.dot((q * k).astype(bf16), hs_ref[...], preferred_element_type=f32) * v.astype(f32)
    ein = jnp.exp(e[L * C:(L + 1) * C])
    st = st_ref[...]
    o = o + lax.dot_general((q * ein).astype(bf16), st.astype(bf16), _NT, preferred_element_type=f32)
    o_ref[0, 0] = o.astype(bf16)
    kout = (k * jnp.exp(e[(L + 1) * C:(L + 2) * C])).astype(bf16)
    upd = lax.dot_general(v, kout, _TN, preferred_element_type=f32)
    total = jnp.where(d == 0, ein[C - 1:C], ein[0:1])
    r = lax.broadcasted_iota(jnp.int32, (BRANCH, BRANCH), 0) // HGRN_DIM
    cidx = lax.broadcasted_iota(jnp.int32, (BRANCH, BRANCH), 1) // HGRN_DIM
    st_ref[...] = st * total + jnp.where(r == cidx, upd, 0.0)


def _hgrn(hq, hv, hk, hlf, consts, C):
    B, S, W = hq.shape
    m, lvl, hs = consts
    L = int(math.log2(C))
    nb = S // C
    blk = lambda b, d, i: i + d * (nb - 1 - 2 * i)
    return pl.pallas_call(
        functools.partial(_hgrn_kernel, C=C, L=L),
        grid=(B, 2, nb),
        in_specs=[pl.BlockSpec((1, C, W), lambda b, d, i: (b, blk(b, d, i), 0)),
                  pl.BlockSpec((1, C, W), lambda b, d, i: (b, blk(b, d, i), 0)),
                  pl.BlockSpec((1, C, W), lambda b, d, i: (b, blk(b, d, i), d)),
                  pl.BlockSpec((1, C, 2 * W), lambda b, d, i: (b, blk(b, d, i), d)),
                  pl.BlockSpec((1, (L + 2) * C, C), lambda b, d, i: (d, 0, 0)),
                  pl.BlockSpec((1, C, C), lambda b, d, i: (d, 0, 0)),
                  _const_spec(hs.shape)],
        out_specs=pl.BlockSpec((1, 1, C, W), lambda b, d, i: (d, b, blk(b, d, i), 0)),
        out_shape=jax.ShapeDtypeStruct((2, B, S, W), bf16),
        scratch_shapes=[pltpu.VMEM((BRANCH, BRANCH), f32)],
        compiler_params=_cparams(("parallel", "arbitrary", "arbitrary")),
        name="hgrn",
    )(hq, hv, hk, hlf, m, lvl, hs)


def _merge_kernel(x_ref, om_ref, of_ref, od_ref, oh_ref, hg_ref, ln_ref, wg_ref, wbm_ref, wbf_ref, wbd_ref,
                  wbh_ref, wo_ref, gon_ref, g64_ref, o_ref):
    x = x_ref[0]
    h = _rms(x, ln_ref[...]).astype(bf16)
    oh = oh_ref[0, 0].astype(f32) + oh_ref[1, 0].astype(f32)
    ms = jnp.dot((oh * oh).astype(bf16), g64_ref[...], preferred_element_type=f32)
    oh = (oh * lax.rsqrt(ms + EPS) * gon_ref[...] * _sigmoid(hg_ref[0].astype(f32))).astype(bf16)
    merged = None
    for n, (o_n, w_ref) in enumerate(((om_ref[0], wbm_ref), (of_ref[0], wbf_ref), (od_ref[0], wbd_ref), (oh, wbh_ref))):
        gate = _sigmoid(jnp.dot(h, wg_ref[:, n * D_MODEL:(n + 1) * D_MODEL], preferred_element_type=f32))
        y = gate * jnp.dot(o_n, w_ref[...], preferred_element_type=f32)
        merged = y if merged is None else merged + y
    o_ref[0] = x + jnp.dot(merged.astype(bf16), wo_ref[...], preferred_element_type=f32)


def _merge(x, om, of, od, oh, hg, p, tm):
    B, S, _ = x.shape
    row = lambda w: pl.BlockSpec((1, tm, w), lambda b, i: (b, i, 0))
    consts = (p["ln_mix"], p["wgate"], p["wbm"], p["wbf"], p["wbd"], p["wbh"], p["wout"], p["gon"], p["g64"])
    return pl.pallas_call(
        _merge_kernel,
        grid=(B, S // tm),
        in_specs=[row(D_MODEL), row(512), row(BRANCH), row(512),
                  pl.BlockSpec((2, 1, tm, BRANCH), lambda b, i: (0, b, i, 0)), row(BRANCH)]
                 + [_const_spec(c.shape) for c in consts],
        out_specs=row(D_MODEL),
        out_shape=jax.ShapeDtypeStruct((B, S, D_MODEL), f32),
        compiler_params=_cparams(("parallel", "parallel")),
        name="merge",
    )(x, om, of, od, oh, hg, *consts)


def _ffn_kernel(x_ref, ln_ref, wgu_ref, wd_ref, o_ref):
    x = x_ref[0]
    h = _rms(x, ln_ref[...]).astype(bf16)

    def chunk(c, acc):
        gu = jnp.dot(h, wgu_ref[c], preferred_element_type=f32)
        g = gu[:, :FF_CHUNK]
        a = (g * _sigmoid(g) * gu[:, FF_CHUNK:]).astype(bf16)
        return acc + jnp.dot(a, wd_ref[c], preferred_element_type=f32)

    o_ref[0] = lax.fori_loop(0, D_FF // FF_CHUNK, chunk, x)


def _ffn(x, p, tm):
    B, S, _ = x.shape
    row = pl.BlockSpec((1, tm, D_MODEL), lambda b, i: (b, i, 0))
    consts = (p["ln_ffn"], p["wgu"], p["wd"])
    return pl.pallas_call(
        _ffn_kernel,
        grid=(B, S // tm),
        in_specs=[row] + [_const_spec(c.shape) for c in consts],
        out_specs=row,
        out_shape=jax.ShapeDtypeStruct((B, S, D_MODEL), f32),
        compiler_params=_cparams(("parallel", "parallel")),
        name="ffn",
    )(x, *consts)


def _pad_heads(w, heads, width):
    lead = w.shape[:-1]
    w = w.reshape(lead + (heads, width))
    w = jnp.pad(w, [(0, 0)] * len(lead) + [(0, 0), (0, LANE - width)])
    return w.reshape(lead + (heads * LANE,))


def _rope_tables(S):
    def cs(dim):
        inv = 1.0 / (ROPE_THETA ** (jnp.arange(0, dim, 2, dtype=f32) / dim))
        ang = jnp.arange(S, dtype=f32)[:, None] * inv[None, :]
        return jnp.cos(ang), jnp.sin(ang)

    cm, sm = cs(MLA_ROPE)
    one, zero = jnp.ones((S, MLA_NOPE), f32), jnp.zeros((S, MLA_NOPE), f32)
    z16, z32, o32 = jnp.zeros((S, 16), f32), jnp.zeros((S, 32), f32), jnp.ones((S, 32), f32)
    rope_m = jnp.stack([jnp.concatenate([one, cm, cm, o32], 1),
                        jnp.concatenate([zero, -sm, z16, z32], 1),
                        jnp.concatenate([zero, z16, sm, z32], 1)])
    cd, sd = cs(DIFF_HEAD_DIM)
    rope_d = jnp.stack([jnp.tile(jnp.concatenate([cd, cd], 1), (1, 4)),
                        jnp.tile(jnp.concatenate([-sd, z16], 1), (1, 4)),
                        jnp.tile(jnp.concatenate([z16, sd], 1), (1, 4))])
    return rope_m, rope_d


def _block_mean(width, group):
    c = np.arange(width)
    return jnp.asarray(((c[:, None] // group) == (c[None, :] // group)) / group, f32).astype(bf16)


def _layer_params(l, a, lower_bounds):
    w = a["w_in"][l]
    offs = np.cumsum([0, 384, 128, 32, 256, 256, 256, 256, 256, 256, 256, 256, 256, 4096])
    col = lambda i: w[:, offs[i]:offs[i + 1]]
    wcat = jnp.concatenate([col(0), col(1), jnp.pad(col(2), ((0, 0), (MLA_NOPE, LANE - MLA_QK))), col(3), col(4),
                            col(5), _pad_heads(col(6), DIFF_HEADS, 2 * DIFF_HEAD_DIM), col(7), col(8), col(9),
                            col(10), col(11)], axis=1).astype(bf16)
    wukv = a["mla_w_ukv"][l].reshape(MLA_KV_LORA, MLA_HEADS, MLA_NOPE + MLA_V)
    wukv = jnp.concatenate([_pad_heads(wukv[:, :, :MLA_NOPE].reshape(MLA_KV_LORA, -1), MLA_HEADS, MLA_NOPE),
                            _pad_heads(wukv[:, :, MLA_NOPE:].reshape(MLA_KV_LORA, -1), MLA_HEADS, MLA_V)], axis=1)
    lam_init = 0.8 - 0.6 * math.exp(-0.3 * l)
    lam = (jnp.exp(jnp.sum(a["diff_lq1"][l] * a["diff_lk1"][l])) - jnp.exp(jnp.sum(a["diff_lq2"][l] * a["diff_lk2"][l]))
           + lam_init)
    wb = a["w_branch"][l]
    wgu = a["w_gate_up"][l]
    nc = D_FF // FF_CHUNK
    wgu = jnp.concatenate([wgu[:, :D_FF].reshape(D_MODEL, nc, FF_CHUNK), wgu[:, D_FF:].reshape(D_MODEL, nc, FF_CHUNK)],
                          axis=2).transpose(1, 0, 2)
    row = lambda v: v.reshape(1, -1).astype(f32)
    return {
        "ln_mix": row(a["ln_mix"][l]),
        "wcat": wcat,
        "gqa": row(a["mla_g_qa"][l]),
        "wuq": _pad_heads(a["mla_w_uq"][l], MLA_HEADS, MLA_QK).astype(bf16),
        "gkva": row(a["mla_g_kva"][l]),
        "wukv": wukv.astype(bf16),
        "gqn": row(jnp.pad(a["mla_g_qn"][l], (0, LANE - MLA_QK))) * (MLA_QK ** -0.5 * LOG2E),
        "gkn": row(jnp.pad(a["mla_g_kn"][l], (0, LANE - MLA_QK))),
        "gdq": row(jnp.tile(a["diff_g_qn"][l], 2 * DIFF_HEADS)) * (DIFF_HEAD_DIM ** -0.5 * LOG2E),
        "gdk": row(jnp.tile(a["diff_g_kn"][l], 2 * DIFF_HEADS)),
        "g32": _block_mean(BRANCH, DIFF_HEAD_DIM),
        "lb": lower_bounds[:, l].astype(f32),
        "lam": lam.reshape(1).astype(f32),
        "mla_bound": MLA_QK ** 0.5 * LOG2E * jnp.max(jnp.abs(a["mla_g_qn"][l])) * jnp.max(jnp.abs(a["mla_g_kn"][l])),
        "diff_bound": (DIFF_HEAD_DIM ** 0.5 * LOG2E * jnp.max(jnp.abs(a["diff_g_qn"][l]))
                       * jnp.max(jnp.abs(a["diff_g_kn"][l]))),
        "gsub": row(jnp.pad(a["diff_g_sub"][l], (0, LANE - 2 * DIFF_HEAD_DIM))) * (1.0 - lam_init),
        "wgate": col(12).astype(bf16),
        "wbm": _pad_heads(wb[0].T, MLA_HEADS, MLA_V).T.astype(bf16),
        "wbf": wb[1].astype(bf16),
        "wbd": _pad_heads(wb[2].T, DIFF_HEADS, 2 * DIFF_HEAD_DIM).T.astype(bf16),
        "wbh": wb[3].astype(bf16),
        "wout": a["w_out"][l].astype(bf16),
        "gon": row(jnp.tile(a["hgrn_g_on"][l], HGRN_HEADS)),
        "g64": _block_mean(BRANCH, HGRN_DIM),
        "ln_ffn": row(a["ln_ffn"][l]),
        "wgu": wgu.astype(bf16),
        "wd": a["w_down"][l].reshape(nc, FF_CHUNK, D_MODEL).astype(bf16),
    }


def _tiles(S):
    return {"tm_in": min(256, S), "tq": min(512, S), "tk": min(512, S), "hgrn_chunk": min(256, S),
            "tm_merge": min(256, S), "tm_ffn": min(512, S)}


def kernel(x, ln_mix, w_in, mla_g_qa, mla_w_uq, mla_g_kva, mla_w_ukv, mla_g_qn, mla_g_kn, diff_g_qn, diff_g_kn,
           diff_lq1, diff_lk1, diff_lq2, diff_lk2, diff_g_sub, hgrn_lb_logits, hgrn_g_on, w_branch, w_out, ln_ffn,
           w_gate_up, w_down):
    a = dict(ln_mix=ln_mix, w_in=w_in, mla_g_qa=mla_g_qa, mla_w_uq=mla_w_uq, mla_g_kva=mla_g_kva,
             mla_w_ukv=mla_w_ukv, mla_g_qn=mla_g_qn, mla_g_kn=mla_g_kn, diff_g_qn=diff_g_qn, diff_g_kn=diff_g_kn,
             diff_lq1=diff_lq1, diff_lk1=diff_lk1, diff_lq2=diff_lq2, diff_lk2=diff_lk2, diff_g_sub=diff_g_sub,
             hgrn_g_on=hgrn_g_on, w_branch=w_branch, w_out=w_out, ln_ffn=ln_ffn, w_gate_up=w_gate_up, w_down=w_down)
    S = x.shape[1]
    t = _tiles(S)
    rope_m, rope_d = _rope_tables(S)
    lb_p = jax.nn.softmax(hgrn_lb_logits.astype(f32), axis=1)
    lower_bounds = jnp.cumsum(lb_p, axis=1) - lb_p[:, :1]
    fnet_consts = _fnet_consts(S)
    hgrn_consts = _hgrn_consts(t["hgrn_chunk"])
    for l in range(DEPTH):
        p = _layer_params(l, a, lower_bounds)
        qm, km, vm, fn, qd, kd, vd, hq, hv, hk, hlf, hg = _in_proj(x, p, rope_m, rope_d, t["tm_in"])
        om = lax.cond(p["mla_bound"] <= SCORE_BOUND_LOG2,
                      lambda: _mla_attn_bounded(qm, km, vm, t["tq"], t["tk"]),
                      lambda: _mla_attn(qm, km, vm, t["tq"], t["tk"]))
        od = lax.cond(p["diff_bound"] <= SCORE_BOUND_LOG2,
                      lambda: _diff_attn_bounded(p["lam"], qd, kd, vd, p["gsub"], t["tq"], t["tk"]),
                      lambda: _diff_attn(p["lam"], qd, kd, vd, p["gsub"], t["tq"], t["tk"]))
        of = _fnet(fn, fnet_consts)
        oh = _hgrn(hq, hv, hk, hlf, hgrn_consts, t["hgrn_chunk"])
        x = _merge(x, om, of, od, oh, hg, p, t["tm_merge"])
        x = _ffn(x, p, t["tm_ffn"])
    return x
```

```python
import functools
import math

import numpy as np
import jax
import jax.numpy as jnp
from jax import lax
from jax.experimental import pallas as pl
from jax.experimental.pallas import tpu as pltpu

f32 = jnp.float32
bf16 = jnp.bfloat16

D_MODEL = 1024
DEPTH = 2
MLA_HEADS = 4
MLA_Q_LORA = 384
MLA_KV_LORA = 128
MLA_NOPE = 64
MLA_ROPE = 32
MLA_V = 64
MLA_QK = MLA_NOPE + MLA_ROPE
DIFF_HEADS = 4
DIFF_HEAD_DIM = 32
HGRN_HEADS = 4
HGRN_DIM = 64
BRANCH = 256
D_FF = 2816
FF_CHUNK = 256
ROPE_THETA = 10000.0
EPS = 1e-6
LOG2E = 1.4426950408889634
LANE = 128
FNET_N2 = 64
FNET_K1_BLOCK = 8
VMEM_LIMIT = 56 * 1024 * 1024
SCORE_BOUND_LOG2 = 60.0

_SEG = {}
_off = 0
for _name, _w in (("cq", 384), ("ckv", 128), ("krope", 128), ("fnet", 256), ("dq", 256), ("dk", 256),
                  ("dv", 512), ("hq", 256), ("hi", 256), ("hff", 256), ("hfb", 256), ("hg", 256)):
    _SEG[_name] = (_off, _off + _w)
    _off += _w
W_CAT = _off


def _cparams(sem):
    return pltpu.CompilerParams(dimension_semantics=sem, vmem_limit_bytes=VMEM_LIMIT)


def _const_spec(shape):
    nd = len(shape)
    return pl.BlockSpec(shape, lambda *_: (0,) * nd, pipeline_mode=pl.Buffered(1))


def _rms(x, g):
    return x * lax.rsqrt(jnp.mean(x * x, axis=-1, keepdims=True) + EPS) * g


def _sigmoid(x):
    return 1.0 / (1.0 + jnp.exp(-x))


def _rope(t, tab_ref):
    return t * tab_ref[0] + pltpu.roll(t, LANE - 16, 1) * tab_ref[1] + pltpu.roll(t, 16, 1) * tab_ref[2]


def _in_proj_kernel(x_ref, ln_ref, wcat_ref, gqa_ref, wuq_ref, gkva_ref, wukv_ref, gqn_ref, gkn_ref,
                    ropem_ref, gdq_ref, gdk_ref, g32_ref, roped_ref, lb_ref,
                    qm_ref, km_ref, vm_ref, fn_ref, qd_ref, kd_ref, vd_ref, hq_ref, hv_ref, hk_ref,
                    hlf_ref, hg_ref):
    h = _rms(x_ref[0], ln_ref[...]).astype(bf16)

    def seg(name):
        a, b = _SEG[name]
        return jnp.dot(h, wcat_ref[:, a:b], preferred_element_type=f32)

    ones_lane = (lax.broadcasted_iota(jnp.int32, (1, LANE), 1) == MLA_V).astype(f32)

    cqn = _rms(seg("cq"), gqa_ref[...]).astype(bf16)
    q = jnp.dot(cqn, wuq_ref[...], preferred_element_type=f32)
    for hh in range(MLA_HEADS):
        sl = slice(hh * LANE, (hh + 1) * LANE)
        qh = q[:, sl]
        ms = jnp.sum(qh * qh, axis=-1, keepdims=True) * (1.0 / MLA_QK)
        qm_ref[0, :, sl] = _rope(qh * lax.rsqrt(ms + EPS) * gqn_ref[...], ropem_ref).astype(bf16)

    ckvn = _rms(seg("ckv"), gkva_ref[...]).astype(bf16)
    kv = jnp.dot(ckvn, wukv_ref[...], preferred_element_type=f32)
    krope = seg("krope")
    for hh in range(MLA_HEADS):
        sl = slice(hh * LANE, (hh + 1) * LANE)
        kh = kv[:, sl] + krope
        ms = jnp.sum(kh * kh, axis=-1, keepdims=True) * (1.0 / MLA_QK)
        km_ref[0, :, sl] = _rope(kh * lax.rsqrt(ms + EPS) * gkn_ref[...], ropem_ref).astype(bf16)
        vsl = slice((MLA_HEADS + hh) * LANE, (MLA_HEADS + hh + 1) * LANE)
        vm_ref[0, :, sl] = (kv[:, vsl] + ones_lane).astype(bf16)

    fn_ref[0] = seg("fnet").astype(bf16)

    for name, g_ref, o_ref in (("dq", gdq_ref, qd_ref), ("dk", gdk_ref, kd_ref)):
        t = seg(name)
        ms = jnp.dot((t * t).astype(bf16), g32_ref[...], preferred_element_type=f32)
        t = t * lax.rsqrt(ms + EPS) * g_ref[...]
        for c in range(BRANCH // LANE):
            sl = slice(c * LANE, (c + 1) * LANE)
            o_ref[0, :, sl] = _rope(t[:, sl], roped_ref).astype(bf16)
    dv = seg("dv")
    for hh in range(DIFF_HEADS):
        sl = slice(hh * LANE, (hh + 1) * LANE)
        vd_ref[0, :, sl] = (dv[:, sl] + ones_lane).astype(bf16)

    hq_ref[0] = seg("hq").astype(bf16)
    hv_ref[0] = seg("hi").astype(bf16)
    hg_ref[0] = seg("hg").astype(bf16)
    for d, name in enumerate(("hff", "hfb")):
        lb = lb_ref[d:d + 1, :]
        f = lb + (1.0 - lb) * _sigmoid(seg(name))
        hk_ref[0, :, d * BRANCH:(d + 1) * BRANCH] = (1.0 - f).astype(bf16)
        lf = jnp.log(f)
        hi = lf.astype(bf16)
        hlf_ref[0, :, 2 * d * BRANCH:(2 * d + 1) * BRANCH] = hi
        hlf_ref[0, :, (2 * d + 1) * BRANCH:(2 * d + 2) * BRANCH] = (lf - hi.astype(f32)).astype(bf16)


def _in_proj(x, p, rope_m, rope_d, tm):
    B, S, _ = x.shape
    row = lambda w: pl.BlockSpec((1, tm, w), lambda b, i: (b, i, 0))
    tab = pl.BlockSpec((3, tm, LANE), lambda b, i: (0, i, 0))
    out_w = (512, 512, 512, 256, 256, 256, 512, 256, 256, 512, 1024, 256)
    consts = (p["ln_mix"], p["wcat"], p["gqa"], p["wuq"], p["gkva"], p["wukv"], p["gqn"], p["gkn"])
    consts2 = (p["gdq"], p["gdk"], p["g32"])
    in_specs = ([row(D_MODEL)] + [_const_spec(c.shape) for c in consts] + [tab]
                + [_const_spec(c.shape) for c in consts2] + [tab, _const_spec(p["lb"].shape)])
    return pl.pallas_call(
        _in_proj_kernel,
        grid=(B, S // tm),
        in_specs=in_specs,
        out_specs=[row(w) for w in out_w],
        out_shape=[jax.ShapeDtypeStruct((B, S, w), bf16) for w in out_w],
        compiler_params=_cparams(("parallel", "parallel")),
        name="in_proj",
    )(x, *consts, rope_m, *consts2, rope_d, p["lb"])


def _online_softmax_step(s, v, m_ref, acc_ref, idx):
    m_prev = m_ref[idx]
    m_new = jnp.maximum(m_prev, jnp.max(s, axis=1, keepdims=True))
    alpha = jnp.exp2(m_prev - m_new)
    p = jnp.exp2((s - m_new[:, :1]).astype(bf16))
    acc_ref[idx] = alpha * acc_ref[idx] + jnp.dot(p, v, preferred_element_type=f32)
    m_ref[idx] = m_new


_NT = (((1,), (1,)), ((), ()))
_TN = (((0,), (0,)), ((), ()))


def _mla_attn_kernel(q_ref, k_ref, v_ref, o_ref, m_ref, acc_ref):
    j = pl.program_id(2)

    @pl.when(j == 0)
    def _():
        m_ref[...] = jnp.full(m_ref.shape, -jnp.inf, f32)
        acc_ref[...] = jnp.zeros(acc_ref.shape, f32)

    for hh in range(MLA_HEADS):
        sl = slice(hh * LANE, (hh + 1) * LANE)
        s = lax.dot_general(q_ref[0, :, sl], k_ref[0, :, sl], _NT, preferred_element_type=f32)
        _online_softmax_step(s, v_ref[0, :, sl], m_ref, acc_ref, hh)

    @pl.when(j == pl.num_programs(2) - 1)
    def _():
        for hh in range(MLA_HEADS):
            acc = acc_ref[hh]
            o_ref[0, :, hh * LANE:(hh + 1) * LANE] = (acc / acc[:, MLA_V:MLA_V + 1]).astype(bf16)


def _diff_finalize(lam_ref, gsub_ref, acc_ref, o_ref):
    lam = lam_ref[0]
    vmask = lax.broadcasted_iota(jnp.int32, (1, LANE), 1) < 2 * DIFF_HEAD_DIM
    for hh in range(DIFF_HEADS):
        a1 = acc_ref[2 * hh]
        a2 = acc_ref[2 * hh + 1]
        o = a1 / a1[:, 64:65] - lam * (a2 / a2[:, 64:65])
        o = jnp.where(vmask, o, 0.0)
        ms = jnp.sum(o * o, axis=-1, keepdims=True) * (1.0 / (2 * DIFF_HEAD_DIM))
        o_ref[0, :, hh * LANE:(hh + 1) * LANE] = (o * lax.rsqrt(ms + EPS) * gsub_ref[...]).astype(bf16)


def _diff_attn_kernel(lam_ref, q_ref, k_ref, v_ref, gsub_ref, o_ref, m_ref, acc_ref):
    j = pl.program_id(2)

    @pl.when(j == 0)
    def _():
        m_ref[...] = jnp.full(m_ref.shape, -jnp.inf, f32)
        acc_ref[...] = jnp.zeros(acc_ref.shape, f32)

    q = q_ref[0]
    k = k_ref[0]
    group = lax.broadcasted_iota(jnp.int32, (1, BRANCH), 1) // DIFF_HEAD_DIM
    for g in range(2 * DIFF_HEADS):
        qg = jnp.where(group == g, q, jnp.zeros_like(q))
        s = lax.dot_general(qg, k, _NT, preferred_element_type=f32)
        hh = g // 2
        _online_softmax_step(s, v_ref[0, :, hh * LANE:(hh + 1) * LANE], m_ref, acc_ref, g)

    @pl.when(j == pl.num_programs(2) - 1)
    def _():
        _diff_finalize(lam_ref, gsub_ref, acc_ref, o_ref)


def _mla_attn_bounded_kernel(q_ref, k_ref, v_ref, o_ref, acc_ref, *, tk):
    acc_ref[...] = jnp.zeros(acc_ref.shape, f32)

    def kv_block(j, carry):
        rows = pl.ds(pl.multiple_of(j * tk, tk), tk)
        for hh in range(MLA_HEADS):
            sl = slice(hh * LANE, (hh + 1) * LANE)
            s = lax.dot_general(q_ref[0, :, sl], k_ref[0, rows, sl], _NT, preferred_element_type=f32)
            acc_ref[hh] += jnp.dot(jnp.exp2(s).astype(bf16), v_ref[0, rows, sl], preferred_element_type=f32)
        return carry

    lax.fori_loop(0, k_ref.shape[1] // tk, kv_block, 0)
    for hh in range(MLA_HEADS):
        acc = acc_ref[hh]
        o_ref[0, :, hh * LANE:(hh + 1) * LANE] = (acc / acc[:, MLA_V:MLA_V + 1]).astype(bf16)


def _diff_attn_bounded_kernel(lam_ref, q_ref, k_ref, v_ref, gsub_ref, o_ref, acc_ref, *, tk):
    acc_ref[...] = jnp.zeros(acc_ref.shape, f32)
    q = q_ref[0]
    group = lax.broadcasted_iota(jnp.int32, (1, BRANCH), 1) // DIFF_HEAD_DIM
    qs = [jnp.where(group == g, q, jnp.zeros_like(q)) for g in range(2 * DIFF_HEADS)]

    def kv_block(j, carry):
        rows = pl.ds(pl.multiple_of(j * tk, tk), tk)
        k = k_ref[0, rows, :]
        for g in range(2 * DIFF_HEADS):
            s = lax.dot_general(qs[g], k, _NT, preferred_element_type=f32)
            hh = g // 2
            acc_ref[g] += jnp.dot(jnp.exp2(s).astype(bf16), v_ref[0, rows, hh * LANE:(hh + 1) * LANE],
                                  preferred_element_type=f32)
        return carry

    lax.fori_loop(0, k_ref.shape[1] // tk, kv_block, 0)
    _diff_finalize(lam_ref, gsub_ref, acc_ref, o_ref)


def _attn_bounded_call(kernel_fn, n_acc, name, args, in_specs, S, W, B, tq):
    return pl.pallas_call(
        kernel_fn,
        grid=(B, S // tq),
        in_specs=in_specs,
        out_specs=pl.BlockSpec((1, tq, W), lambda b, i: (b, i, 0)),
        out_shape=jax.ShapeDtypeStruct((B, S, W), bf16),
        scratch_shapes=[pltpu.VMEM((n_acc, tq, LANE), f32)],
        compiler_params=_cparams(("parallel", "parallel")),
        name=name,
    )(*args)


def _mla_attn_bounded(q, k, v, tq, tk):
    B, S, W = q.shape
    full = pl.BlockSpec((1, S, W), lambda b, i: (b, 0, 0))
    in_specs = [pl.BlockSpec((1, tq, W), lambda b, i: (b, i, 0)), full, full]
    return _attn_bounded_call(functools.partial(_mla_attn_bounded_kernel, tk=tk), MLA_HEADS, "mla_attn_bounded",
                              (q, k, v), in_specs, S, W, B, tq)


def _diff_attn_bounded(lam, q, k, v, gsub, tq, tk):
    B, S, W = v.shape
    in_specs = [pl.BlockSpec(memory_space=pltpu.SMEM),
                pl.BlockSpec((1, tq, BRANCH), lambda b, i: (b, i, 0)),
                pl.BlockSpec((1, S, BRANCH), lambda b, i: (b, 0, 0)),
                pl.BlockSpec((1, S, W), lambda b, i: (b, 0, 0)),
                pl.BlockSpec((1, LANE), lambda b, i: (0, 0))]
    return _attn_bounded_call(functools.partial(_diff_attn_bounded_kernel, tk=tk), 2 * DIFF_HEADS,
                              "diff_attn_bounded", (lam, q, k, v, gsub), in_specs, S, W, B, tq)


def _mla_attn(q, k, v, tq, tk):
    B, S, W = q.shape
    return pl.pallas_call(
        _mla_attn_kernel,
        grid=(B, S // tq, S // tk),
        in_specs=[pl.BlockSpec((1, tq, W), lambda b, i, j: (b, i, 0)),
                  pl.BlockSpec((1, tk, W), lambda b, i, j: (b, j, 0)),
                  pl.BlockSpec((1, tk, W), lambda b, i, j: (b, j, 0))],
        out_specs=pl.BlockSpec((1, tq, W), lambda b, i, j: (b, i, 0)),
        out_shape=jax.ShapeDtypeStruct((B, S, W), bf16),
        scratch_shapes=[pltpu.VMEM((MLA_HEADS, tq, LANE), f32), pltpu.VMEM((MLA_HEADS, tq, LANE), f32)],
        compiler_params=_cparams(("parallel", "parallel", "arbitrary")),
        name="mla_attn",
    )(q, k, v)


def _diff_attn(lam, q, k, v, gsub, tq, tk):
    B, S, W = v.shape
    return pl.pallas_call(
        _diff_attn_kernel,
        grid=(B, S // tq, S // tk),
        in_specs=[pl.BlockSpec(memory_space=pltpu.SMEM),
                  pl.BlockSpec((1, tq, BRANCH), lambda b, i, j: (b, i, 0)),
                  pl.BlockSpec((1, tk, BRANCH), lambda b, i, j: (b, j, 0)),
                  pl.BlockSpec((1, tk, W), lambda b, i, j: (b, j, 0)),
                  pl.BlockSpec((1, LANE), lambda b, i, j: (0, 0))],
        out_specs=pl.BlockSpec((1, tq, W), lambda b, i, j: (b, i, 0)),
        out_shape=jax.ShapeDtypeStruct((B, S, W), bf16),
        scratch_shapes=[pltpu.VMEM((2 * DIFF_HEADS, tq, LANE), f32), pltpu.VMEM((2 * DIFF_HEADS, tq, LANE), f32)],
        compiler_params=_cparams(("parallel", "parallel", "arbitrary")),
        name="diff_attn",
    )(lam, q, k, v, gsub)


def _fnet1_kernel(x_ref, w_ref, a_ref):
    a_ref[0] = jnp.dot(w_ref[...], x_ref[0], preferred_element_type=f32).astype(bf16)


def _fnet2_kernel(a_ref, t_ref, cs_ref, o_ref):
    for jj in range(FNET_K1_BLOCK):
        a = jnp.concatenate([a_ref[0, 0, jj], a_ref[0, 1, jj]], axis=0)
        z = jnp.dot(t_ref[jj], a, preferred_element_type=f32)
        zc = jnp.concatenate([z[:FNET_N2], z[FNET_N2:]], axis=1).astype(bf16)
        o_ref[0, :, jj, :] = jnp.dot(zc, cs_ref[...], preferred_element_type=f32).astype(bf16)


def _fnet_consts(S):
    n1 = S // FNET_N2
    a = np.arange(n1)
    ang1 = 2.0 * np.pi * np.outer(a, a) / n1
    w1 = np.concatenate([np.cos(ang1), -np.sin(ang1)], axis=0)
    n2 = np.arange(FNET_N2)
    phi = 2.0 * np.pi * (n2[None, None, :] * a[:, None, None] / S + n2[None, None, :] * n2[None, :, None] / FNET_N2)
    tr, ti = np.cos(phi), -np.sin(phi)
    t = np.concatenate([np.concatenate([tr, -ti], axis=2), np.concatenate([ti, tr], axis=2)], axis=1)
    c = np.arange(BRANCH)
    same = (c[:, None] // FNET_N2) == (c[None, :] // FNET_N2)
    angc = 2.0 * np.pi * np.outer(c % FNET_N2, c % FNET_N2) / FNET_N2
    norm = 1.0 / math.sqrt(S * FNET_N2)
    cs = np.concatenate([np.where(same, np.cos(angc), 0.0), np.where(same, np.sin(angc), 0.0)], axis=0) * norm
    return (jnp.asarray(w1, f32).astype(bf16), jnp.asarray(t, f32).astype(bf16), jnp.asarray(cs, f32).astype(bf16))


def _fnet(u, consts):
    B, S, W = u.shape
    w1, t, cs = consts
    n1 = S // FNET_N2
    cols = FNET_N2 * W
    tn = min(cols, 4096)
    a = pl.pallas_call(
        _fnet1_kernel,
        grid=(B, cols // tn),
        in_specs=[pl.BlockSpec((1, n1, tn), lambda b, i: (b, 0, i)), _const_spec(w1.shape)],
        out_specs=pl.BlockSpec((1, 2 * n1, tn), lambda b, i: (b, 0, i)),
        out_shape=jax.ShapeDtypeStruct((B, 2 * n1, cols), bf16),
        compiler_params=_cparams(("parallel", "parallel")),
        name="fnet1",
    )(u.reshape(B, n1, cols), w1)
    kb = FNET_K1_BLOCK
    y = pl.pallas_call(
        _fnet2_kernel,
        grid=(B, n1 // kb),
        in_specs=[pl.BlockSpec((1, 2, kb, FNET_N2, W), lambda b, i: (b, 0, i, 0, 0)),
                  pl.BlockSpec((kb, 2 * FNET_N2, 2 * FNET_N2), lambda b, i: (i, 0, 0)),
                  _const_spec(cs.shape)],
        out_specs=pl.BlockSpec((1, FNET_N2, kb, W), lambda b, i: (b, 0, i, 0)),
        out_shape=jax.ShapeDtypeStruct((B, FNET_N2, n1, W), bf16),
        compiler_params=_cparams(("parallel", "parallel")),
        name="fnet2",
    )(a.reshape(B, 2, n1, FNET_N2, W), t, cs)
    return y.reshape(B, S, W)


def _hgrn_consts(C):
    L = int(math.log2(C))
    t = np.arange(C)
    m = np.zeros((2, (L + 2) * C, C), np.float32)
    lvl = np.full((2, C, C), -1, np.int32)
    for d in (0, 1):
        for li in range(L):
            n = C >> li
            half = n // 2
            blk = t // n
            mid = blk * n + half
            upper = (t % n) >= half
            for r in range(C):
                if d == 0:
                    if upper[r]:
                        m[d, li * C + r, mid[r]:r + 1] = 1.0
                    else:
                        m[d, li * C + r, r + 1:mid[r]] = 1.0
                else:
                    if upper[r]:
                        m[d, li * C + r, mid[r]:r] = 1.0
                    else:
                        m[d, li * C + r, r:mid[r]] = 1.0
            same = blk[:, None] == blk[None, :]
            if d == 0:
                msk = same & upper[:, None] & (~upper)[None, :]
            else:
                msk = same & (~upper)[:, None] & upper[None, :]
            lvl[d][msk] = li
        for r in range(C):
            if d == 0:
                m[d, L * C + r, :r + 1] = 1.0
                m[d, (L + 1) * C + r, r + 1:] = 1.0
            else:
                m[d, L * C + r, r:] = 1.0
                m[d, (L + 1) * C + r, :r] = 1.0
    c = np.arange(BRANCH)
    headsum = ((c[:, None] // HGRN_DIM) == (c[None, :] // HGRN_DIM)).astype(np.float32)
    return jnp.asarray(m, f32).astype(bf16), jnp.asarray(lvl), jnp.asarray(headsum, f32).astype(bf16)


def _hgrn_kernel(q_ref, v_ref, k_ref, lf_ref, m_ref, lvl_ref, hs_ref, o_ref, st_ref, *, C, L):
    d = pl.program_id(1)

    @pl.when(pl.program_id(2) == 0)
    def _():
        st_ref[...] = jnp.zeros(st_ref.shape, f32)

    q = q_ref[0].astype(f32)
    k = k_ref[0].astype(f32)
    v = v_ref[0]
    e2 = jnp.dot(m_ref[0], lf_ref[0], preferred_element_type=f32)
    e = e2[:, :BRANCH] + e2[:, BRANCH:]
    head = lax.broadcasted_iota(jnp.int32, (1, BRANCH), 1) // HGRN_DIM
    lvl = lvl_ref[0]

    scores = [jnp.zeros((C, C), f32) for _ in range(HGRN_HEADS)]
    for li in range(L):
        ex = jnp.exp(e[li * C:(li + 1) * C])
        qt = (q * ex).astype(bf16)
        kt = (k * ex).astype(bf16)
        for hh in range(HGRN_HEADS):
            s = lax.dot_general(jnp.where(head == hh, qt, jnp.zeros_like(qt)), kt, _NT, preferred_element_type=f32)
            scores[hh] = jnp.where(lvl == li, s, scores[hh])
    o = jnp.zeros((C, BRANCH), f32)
    for hh in range(HGRN_HEADS):
        vh = jnp.where(head == hh, v, jnp.zeros_like(v))
        o = o + jnp.dot(scores[hh].astype(bf16), vh, preferred_element_type=f32)
    o = o + jnp.dot((q * k).astype(bf16), hs_ref[...], preferred_element_type=f32) * v.astype(f32)
    ein = jnp.exp(e[L * C:(L + 1) * C])
    st = st_ref[...]
    o = o + lax.dot_general((q * ein).astype(bf16), st.astype(bf16), _NT, preferred_element_type=f32)
    o_ref[0, 0] = o.astype(bf16)
    kout = (k * jnp.exp(e[(L + 1) * C:(L + 2) * C])).astype(bf16)
    upd = lax.dot_general(v, kout, _TN, preferred_element_type=f32)
    total = jnp.where(d == 0, ein[C - 1:C], ein[0:1])
    r = lax.broadcasted_iota(jnp.int32, (BRANCH, BRANCH), 0) // HGRN_DIM
    cidx = lax.broadcasted_iota(jnp.int32, (BRANCH, BRANCH), 1) // HGRN_DIM
    st_ref[...] = st * total + jnp.where(r == cidx, upd, 0.0)


def _hgrn(hq, hv, hk, hlf, consts, C):
    B, S, W = hq.shape
    m, lvl, hs = consts
    L = int(math.log2(C))
    nb = S // C
    blk = lambda b, d, i: i + d * (nb - 1 - 2 * i)
    return pl.pallas_call(
        functools.partial(_hgrn_kernel, C=C, L=L),
        grid=(B, 2, nb),
        in_specs=[pl.BlockSpec((1, C, W), lambda b, d, i: (b, blk(b, d, i), 0)),
                  pl.BlockSpec((1, C, W), lambda b, d, i: (b, blk(b, d, i), 0)),
                  pl.BlockSpec((1, C, W), lambda b, d, i: (b, blk(b, d, i), d)),
                  pl.BlockSpec((1, C, 2 * W), lambda b, d, i: (b, blk(b, d, i), d)),
                  pl.BlockSpec((1, (L + 2) * C, C), lambda b, d, i: (d, 0, 0)),
                  pl.BlockSpec((1, C, C), lambda b, d, i: (d, 0, 0)),
                  _const_spec(hs.shape)],
        out_specs=pl.BlockSpec((1, 1, C, W), lambda b, d, i: (d, b, blk(b, d, i), 0)),
        out_shape=jax.ShapeDtypeStruct((2, B, S, W), bf16),
        scratch_shapes=[pltpu.VMEM((BRANCH, BRANCH), f32)],
        compiler_params=_cparams(("parallel", "arbitrary", "arbitrary")),
        name="hgrn",
    )(hq, hv, hk, hlf, m, lvl, hs)


def _merge_kernel(x_ref, om_ref, of_ref, od_ref, oh_ref, hg_ref, ln_ref, wg_ref, wbm_ref, wbf_ref, wbd_ref,
                  wbh_ref, wo_ref, gon_ref, g64_ref, o_ref):
    x = x_ref[0]
    h = _rms(x, ln_ref[...]).astype(bf16)
    oh = oh_ref[0, 0].astype(f32) + oh_ref[1, 0].astype(f32)
    ms = jnp.dot((oh * oh).astype(bf16), g64_ref[...], preferred_element_type=f32)
    oh = (oh * lax.rsqrt(ms + EPS) * gon_ref[...] * _sigmoid(hg_ref[0].astype(f32))).astype(bf16)
    merged = None
    for n, (o_n, w_ref) in enumerate(((om_ref[0], wbm_ref), (of_ref[0], wbf_ref), (od_ref[0], wbd_ref), (oh, wbh_ref))):
        gate = _sigmoid(jnp.dot(h, wg_ref[:, n * D_MODEL:(n + 1) * D_MODEL], preferred_element_type=f32))
        y = gate * jnp.dot(o_n, w_ref[...], preferred_element_type=f32)
        merged = y if merged is None else merged + y
    o_ref[0] = x + jnp.dot(merged.astype(bf16), wo_ref[...], preferred_element_type=f32)


def _merge(x, om, of, od, oh, hg, p, tm):
    B, S, _ = x.shape
    row = lambda w: pl.BlockSpec((1, tm, w), lambda b, i: (b, i, 0))
    consts = (p["ln_mix"], p["wgate"], p["wbm"], p["wbf"], p["wbd"], p["wbh"], p["wout"], p["gon"], p["g64"])
    return pl.pallas_call(
        _merge_kernel,
        grid=(B, S // tm),
        in_specs=[row(D_MODEL), row(512), row(BRANCH), row(512),
                  pl.BlockSpec((2, 1, tm, BRANCH), lambda b, i: (0, b, i, 0)), row(BRANCH)]
                 + [_const_spec(c.shape) for c in consts],
        out_specs=row(D_MODEL),
        out_shape=jax.ShapeDtypeStruct((B, S, D_MODEL), f32),
        compiler_params=_cparams(("parallel", "parallel")),
        name="merge",
    )(x, om, of, od, oh, hg, *consts)


def _ffn_kernel(x_ref, ln_ref, wgu_ref, wd_ref, o_ref):
    x = x_ref[0]
    h = _rms(x, ln_ref[...]).astype(bf16)

    def chunk(c, acc):
        gu = jnp.dot(h, wgu_ref[c], preferred_element_type=f32)
        g = gu[:, :FF_CHUNK]
        a = (g * _sigmoid(g) * gu[:, FF_CHUNK:]).astype(bf16)
        return acc + jnp.dot(a, wd_ref[c], preferred_element_type=f32)

    o_ref[0] = lax.fori_loop(0, D_FF // FF_CHUNK, chunk, x, unroll=True)


def _ffn(x, p, tm):
    B, S, _ = x.shape
    row = pl.BlockSpec((1, tm, D_MODEL), lambda b, i: (b, i, 0))
    consts = (p["ln_ffn"], p["wgu"], p["wd"])
    return pl.pallas_call(
        _ffn_kernel,
        grid=(B, S // tm),
        in_specs=[row] + [_const_spec(c.shape) for c in consts],
        out_specs=row,
        out_shape=jax.ShapeDtypeStruct((B, S, D_MODEL), f32),
        compiler_params=_cparams(("parallel", "parallel")),
        name="ffn",
    )(x, *consts)


def _pad_heads(w, heads, width):
    lead = w.shape[:-1]
    w = w.reshape(lead + (heads, width))
    w = jnp.pad(w, [(0, 0)] * len(lead) + [(0, 0), (0, LANE - width)])
    return w.reshape(lead + (heads * LANE,))


def _rope_tables(S):
    def cs(dim):
        inv = 1.0 / (ROPE_THETA ** (jnp.arange(0, dim, 2, dtype=f32) / dim))
        ang = jnp.arange(S, dtype=f32)[:, None] * inv[None, :]
        return jnp.cos(ang), jnp.sin(ang)

    cm, sm = cs(MLA_ROPE)
    one, zero = jnp.ones((S, MLA_NOPE), f32), jnp.zeros((S, MLA_NOPE), f32)
    z16, z32, o32 = jnp.zeros((S, 16), f32), jnp.zeros((S, 32), f32), jnp.ones((S, 32), f32)
    rope_m = jnp.stack([jnp.concatenate([one, cm, cm, o32], 1),
                        jnp.concatenate([zero, -sm, z16, z32], 1),
                        jnp.concatenate([zero, z16, sm, z32], 1)])
    cd, sd = cs(DIFF_HEAD_DIM)
    rope_d = jnp.stack([jnp.tile(jnp.concatenate([cd, cd], 1), (1, 4)),
                        jnp.tile(jnp.concatenate([-sd, z16], 1), (1, 4)),
                        jnp.tile(jnp.concatenate([z16, sd], 1), (1, 4))])
    return rope_m, rope_d


def _block_mean(width, group):
    c = np.arange(width)
    return jnp.asarray(((c[:, None] // group) == (c[None, :] // group)) / group, f32).astype(bf16)


def _layer_params(l, a, lower_bounds):
    w = a["w_in"][l]
    offs = np.cumsum([0, 384, 128, 32, 256, 256, 256, 256, 256, 256, 256, 256, 256, 4096])
    col = lambda i: w[:, offs[i]:offs[i + 1]]
    wcat = jnp.concatenate([col(0), col(1), jnp.pad(col(2), ((0, 0), (MLA_NOPE, LANE - MLA_QK))), col(3), col(4),
                            col(5), _pad_heads(col(6), DIFF_HEADS, 2 * DIFF_HEAD_DIM), col(7), col(8), col(9),
                            col(10), col(11)], axis=1).astype(bf16)
    wukv = a["mla_w_ukv"][l].reshape(MLA_KV_LORA, MLA_HEADS, MLA_NOPE + MLA_V)
    wukv = jnp.concatenate([_pad_heads(wukv[:, :, :MLA_NOPE].reshape(MLA_KV_LORA, -1), MLA_HEADS, MLA_NOPE),
                            _pad_heads(wukv[:, :, MLA_NOPE:].reshape(MLA_KV_LORA, -1), MLA_HEADS, MLA_V)], axis=1)
    lam_init = 0.8 - 0.6 * math.exp(-0.3 * l)
    lam = (jnp.exp(jnp.sum(a["diff_lq1"][l] * a["diff_lk1"][l])) - jnp.exp(jnp.sum(a["diff_lq2"][l] * a["diff_lk2"][l]))
           + lam_init)
    wb = a["w_branch"][l]
    wgu = a["w_gate_up"][l]
    nc = D_FF // FF_CHUNK
    wgu = jnp.concatenate([wgu[:, :D_FF].reshape(D_MODEL, nc, FF_CHUNK), wgu[:, D_FF:].reshape(D_MODEL, nc, FF_CHUNK)],
                          axis=2).transpose(1, 0, 2)
    row = lambda v: v.reshape(1, -1).astype(f32)
    return {
        "ln_mix": row(a["ln_mix"][l]),
        "wcat": wcat,
        "gqa": row(a["mla_g_qa"][l]),
        "wuq": _pad_heads(a["mla_w_uq"][l], MLA_HEADS, MLA_QK).astype(bf16),
        "gkva": row(a["mla_g_kva"][l]),
        "wukv": wukv.astype(bf16),
        "gqn": row(jnp.pad(a["mla_g_qn"][l], (0, LANE - MLA_QK))) * (MLA_QK ** -0.5 * LOG2E),
        "gkn": row(jnp.pad(a["mla_g_kn"][l], (0, LANE - MLA_QK))),
        "gdq": row(jnp.tile(a["diff_g_qn"][l], 2 * DIFF_HEADS)) * (DIFF_HEAD_DIM ** -0.5 * LOG2E),
        "gdk": row(jnp.tile(a["diff_g_kn"][l], 2 * DIFF_HEADS)),
        "g32": _block_mean(BRANCH, DIFF_HEAD_DIM),
        "lb": lower_bounds[:, l].astype(f32),
        "lam": lam.reshape(1).astype(f32),
        "mla_bound": MLA_QK ** 0.5 * LOG2E * jnp.max(jnp.abs(a["mla_g_qn"][l])) * jnp.max(jnp.abs(a["mla_g_kn"][l])),
        "diff_bound": (DIFF_HEAD_DIM ** 0.5 * LOG2E * jnp.max(jnp.abs(a["diff_g_qn"][l]))
                       * jnp.max(jnp.abs(a["diff_g_kn"][l]))),
        "gsub": row(jnp.pad(a["diff_g_sub"][l], (0, LANE - 2 * DIFF_HEAD_DIM))) * (1.0 - lam_init),
        "wgate": col(12).astype(bf16),
        "wbm": _pad_heads(wb[0].T, MLA_HEADS, MLA_V).T.astype(bf16),
        "wbf": wb[1].astype(bf16),
        "wbd": _pad_heads(wb[2].T, DIFF_HEADS, 2 * DIFF_HEAD_DIM).T.astype(bf16),
        "wbh": wb[3].astype(bf16),
        "wout": a["w_out"][l].astype(bf16),
        "gon": row(jnp.tile(a["hgrn_g_on"][l], HGRN_HEADS)),
        "g64": _block_mean(BRANCH, HGRN_DIM),
        "ln_ffn": row(a["ln_ffn"][l]),
        "wgu": wgu.astype(bf16),
        "wd": a["w_down"][l].reshape(nc, FF_CHUNK, D_MODEL).astype(bf16),
    }


def _tiles(S):
    return {"tm_in": min(256, S), "tq": min(512, S), "tk": min(512, S), "tqb": min(512, S), "tkb": min(1024, S),
            "hgrn_chunk": min(256, S),
            "tm_merge": min(256, S), "tm_ffn": min(512, S)}


def kernel(x, ln_mix, w_in, mla_g_qa, mla_w_uq, mla_g_kva, mla_w_ukv, mla_g_qn, mla_g_kn, diff_g_qn, diff_g_kn,
           diff_lq1, diff_lk1, diff_lq2, diff_lk2, diff_g_sub, hgrn_lb_logits, hgrn_g_on, w_branch, w_out, ln_ffn,
           w_gate_up, w_down):
    a = dict(ln_mix=ln_mix, w_in=w_in, mla_g_qa=mla_g_qa, mla_w_uq=mla_w_uq, mla_g_kva=mla_g_kva,
             mla_w_ukv=mla_w_ukv, mla_g_qn=mla_g_qn, mla_g_kn=mla_g_kn, diff_g_qn=diff_g_qn, diff_g_kn=diff_g_kn,
             diff_lq1=diff_lq1, diff_lk1=diff_lk1, diff_lq2=diff_lq2, diff_lk2=diff_lk2, diff_g_sub=diff_g_sub,
             hgrn_g_on=hgrn_g_on, w_branch=w_branch, w_out=w_out, ln_ffn=ln_ffn, w_gate_up=w_gate_up, w_down=w_down)
    S = x.shape[1]
    t = _tiles(S)
    rope_m, rope_d = _rope_tables(S)
    lb_p = jax.nn.softmax(hgrn_lb_logits.astype(f32), axis=1)
    lower_bounds = jnp.cumsum(lb_p, axis=1) - lb_p[:, :1]
    fnet_consts = _fnet_consts(S)
    hgrn_consts = _hgrn_consts(t["hgrn_chunk"])
    for l in range(DEPTH):
        p = _layer_params(l, a, lower_bounds)
        qm, km, vm, fn, qd, kd, vd, hq, hv, hk, hlf, hg = _in_proj(x, p, rope_m, rope_d, t["tm_in"])
        om = lax.cond(p["mla_bound"] <= SCORE_BOUND_LOG2,
                      lambda: _mla_attn_bounded(qm, km, vm, t["tqb"], t["tkb"]),
                      lambda: _mla_attn(qm, km, vm, t["tq"], t["tk"]))
        od = lax.cond(p["diff_bound"] <= SCORE_BOUND_LOG2,
                      lambda: _diff_attn_bounded(p["lam"], qd, kd, vd, p["gsub"], t["tqb"], t["tkb"]),
                      lambda: _diff_attn(p["lam"], qd, kd, vd, p["gsub"], t["tq"], t["tk"]))
        of = _fnet(fn, fnet_consts)
        oh = _hgrn(hq, hv, hk, hlf, hgrn_consts, t["hgrn_chunk"])
        x = _merge(x, om, of, od, oh, hg, p, t["tm_merge"])
        x = _ffn(x, p, t["tm_ffn"])
    return x
```

```python
import functools
import math

import numpy as np
import jax
import jax.numpy as jnp
from jax import lax
from jax.experimental import pallas as pl
from jax.experimental.pallas import tpu as pltpu

f32 = jnp.float32
bf16 = jnp.bfloat16

D_MODEL = 1024
DEPTH = 2
MLA_HEADS = 4
MLA_Q_LORA = 384
MLA_KV_LORA = 128
MLA_NOPE = 64
MLA_ROPE = 32
MLA_V = 64
MLA_QK = MLA_NOPE + MLA_ROPE
DIFF_HEADS = 4
DIFF_HEAD_DIM = 32
HGRN_HEADS = 4
HGRN_DIM = 64
BRANCH = 256
D_FF = 2816
FF_CHUNK = 256
ROPE_THETA = 10000.0
EPS = 1e-6
LOG2E = 1.4426950408889634
LANE = 128
FNET_N2 = 64
FNET_K1_BLOCK = 8
VMEM_LIMIT = 56 * 1024 * 1024
SCORE_BOUND_LOG2 = 60.0

_SEG = {}
_off = 0
for _name, _w in (("cq", 384), ("ckv", 128), ("krope", 128), ("fnet", 256), ("dq", 256), ("dk", 256),
                  ("dv", 512), ("hq", 256), ("hi", 256), ("hff", 256), ("hfb", 256), ("hg", 256)):
    _SEG[_name] = (_off, _off + _w)
    _off += _w
W_CAT = _off


def _cparams(sem):
    return pltpu.CompilerParams(dimension_semantics=sem, vmem_limit_bytes=VMEM_LIMIT)


def _const_spec(shape):
    nd = len(shape)
    return pl.BlockSpec(shape, lambda *_: (0,) * nd, pipeline_mode=pl.Buffered(1))


def _rms(x, g):
    return x * lax.rsqrt(jnp.mean(x * x, axis=-1, keepdims=True) + EPS) * g


def _sigmoid(x):
    return 1.0 / (1.0 + jnp.exp(-x))


def _rope(t, tab_ref):
    return t * tab_ref[0] + pltpu.roll(t, LANE - 16, 1) * tab_ref[1] + pltpu.roll(t, 16, 1) * tab_ref[2]


def _in_proj_kernel(x_ref, ln_ref, wcat_ref, gqa_ref, wuq_ref, gkva_ref, wukv_ref, gqn_ref, gkn_ref,
                    ropem_ref, gdq_ref, gdk_ref, g32_ref, roped_ref, lb_ref,
                    qm_ref, km_ref, vm_ref, fn_ref, qd_ref, kd_ref, vd_ref, hq_ref, hv_ref, hk_ref,
                    hlf_ref, hg_ref):
    h = _rms(x_ref[0], ln_ref[...]).astype(bf16)

    def seg(name):
        a, b = _SEG[name]
        return jnp.dot(h, wcat_ref[:, a:b], preferred_element_type=f32)

    ones_lane = (lax.broadcasted_iota(jnp.int32, (1, LANE), 1) == MLA_V).astype(f32)

    cqn = _rms(seg("cq"), gqa_ref[...]).astype(bf16)
    q = jnp.dot(cqn, wuq_ref[...], preferred_element_type=f32)
    for hh in range(MLA_HEADS):
        sl = slice(hh * LANE, (hh + 1) * LANE)
        qh = q[:, sl]
        ms = jnp.sum(qh * qh, axis=-1, keepdims=True) * (1.0 / MLA_QK)
        qm_ref[0, :, sl] = _rope(qh * lax.rsqrt(ms + EPS) * gqn_ref[...], ropem_ref).astype(bf16)

    ckvn = _rms(seg("ckv"), gkva_ref[...]).astype(bf16)
    kv = jnp.dot(ckvn, wukv_ref[...], preferred_element_type=f32)
    krope = seg("krope")
    for hh in range(MLA_HEADS):
        sl = slice(hh * LANE, (hh + 1) * LANE)
        kh = kv[:, sl] + krope
        ms = jnp.sum(kh * kh, axis=-1, keepdims=True) * (1.0 / MLA_QK)
        km_ref[0, :, sl] = _rope(kh * lax.rsqrt(ms + EPS) * gkn_ref[...], ropem_ref).astype(bf16)
        vsl = slice((MLA_HEADS + hh) * LANE, (MLA_HEADS + hh + 1) * LANE)
        vm_ref[0, :, sl] = (kv[:, vsl] + ones_lane).astype(bf16)

    fn_ref[0] = seg("fnet").astype(bf16)

    for name, g_ref, o_ref in (("dq", gdq_ref, qd_ref), ("dk", gdk_ref, kd_ref)):
        t = seg(name)
        ms = jnp.dot((t * t).astype(bf16), g32_ref[...], preferred_element_type=f32)
        t = t * lax.rsqrt(ms + EPS) * g_ref[...]
        for c in range(BRANCH // LANE):
            sl = slice(c * LANE, (c + 1) * LANE)
            o_ref[0, :, sl] = _rope(t[:, sl], roped_ref).astype(bf16)
    dv = seg("dv")
    for hh in range(DIFF_HEADS):
        sl = slice(hh * LANE, (hh + 1) * LANE)
        vd_ref[0, :, sl] = (dv[:, sl] + ones_lane).astype(bf16)

    hq_ref[0] = seg("hq").astype(bf16)
    hv_ref[0] = seg("hi").astype(bf16)
    hg_ref[0] = seg("hg").astype(bf16)
    for d, name in enumerate(("hff", "hfb")):
        lb = lb_ref[d:d + 1, :]
        f = lb + (1.0 - lb) * _sigmoid(seg(name))
        hk_ref[0, :, d * BRANCH:(d + 1) * BRANCH] = (1.0 - f).astype(bf16)
        lf = jnp.log(f)
        hi = lf.astype(bf16)
        hlf_ref[0, :, 2 * d * BRANCH:(2 * d + 1) * BRANCH] = hi
        hlf_ref[0, :, (2 * d + 1) * BRANCH:(2 * d + 2) * BRANCH] = (lf - hi.astype(f32)).astype(bf16)


def _in_proj(x, p, rope_m, rope_d, tm):
    B, S, _ = x.shape
    row = lambda w: pl.BlockSpec((1, tm, w), lambda b, i: (b, i, 0))
    tab = pl.BlockSpec((3, tm, LANE), lambda b, i: (0, i, 0))
    out_w = (512, 512, 512, 256, 256, 256, 512, 256, 256, 512, 1024, 256)
    consts = (p["ln_mix"], p["wcat"], p["gqa"], p["wuq"], p["gkva"], p["wukv"], p["gqn"], p["gkn"])
    consts2 = (p["gdq"], p["gdk"], p["g32"])
    in_specs = ([row(D_MODEL)] + [_const_spec(c.shape) for c in consts] + [tab]
                + [_const_spec(c.shape) for c in consts2] + [tab, _const_spec(p["lb"].shape)])
    return pl.pallas_call(
        _in_proj_kernel,
        grid=(B, S // tm),
        in_specs=in_specs,
        out_specs=[row(w) for w in out_w],
        out_shape=[jax.ShapeDtypeStruct((B, S, w), bf16) for w in out_w],
        compiler_params=_cparams(("parallel", "parallel")),
        name="in_proj",
    )(x, *consts, rope_m, *consts2, rope_d, p["lb"])


def _online_softmax_step(s, v, m_ref, acc_ref, idx):
    m_prev = m_ref[idx]
    m_new = jnp.maximum(m_prev, jnp.max(s, axis=1, keepdims=True))
    alpha = jnp.exp2(m_prev - m_new)
    p = jnp.exp2((s - m_new[:, :1]).astype(bf16))
    acc_ref[idx] = alpha * acc_ref[idx] + jnp.dot(p, v, preferred_element_type=f32)
    m_ref[idx] = m_new


_NT = (((1,), (1,)), ((), ()))
_TN = (((0,), (0,)), ((), ()))


def _mla_attn_kernel(q_ref, k_ref, v_ref, o_ref, m_ref, acc_ref):
    j = pl.program_id(2)

    @pl.when(j == 0)
    def _():
        m_ref[...] = jnp.full(m_ref.shape, -jnp.inf, f32)
        acc_ref[...] = jnp.zeros(acc_ref.shape, f32)

    for hh in range(MLA_HEADS):
        sl = slice(hh * LANE, (hh + 1) * LANE)
        s = lax.dot_general(q_ref[0, :, sl], k_ref[0, :, sl], _NT, preferred_element_type=f32)
        _online_softmax_step(s, v_ref[0, :, sl], m_ref, acc_ref, hh)

    @pl.when(j == pl.num_programs(2) - 1)
    def _():
        for hh in range(MLA_HEADS):
            acc = acc_ref[hh]
            o_ref[0, :, hh * LANE:(hh + 1) * LANE] = (acc / acc[:, MLA_V:MLA_V + 1]).astype(bf16)


def _diff_finalize(lam_ref, gsub_ref, acc_ref, o_ref):
    lam = lam_ref[0]
    vmask = lax.broadcasted_iota(jnp.int32, (1, LANE), 1) < 2 * DIFF_HEAD_DIM
    for hh in range(DIFF_HEADS):
        a1 = acc_ref[2 * hh]
        a2 = acc_ref[2 * hh + 1]
        o = a1 / a1[:, 64:65] - lam * (a2 / a2[:, 64:65])
        o = jnp.where(vmask, o, 0.0)
        ms = jnp.sum(o * o, axis=-1, keepdims=True) * (1.0 / (2 * DIFF_HEAD_DIM))
        o_ref[0, :, hh * LANE:(hh + 1) * LANE] = (o * lax.rsqrt(ms + EPS) * gsub_ref[...]).astype(bf16)


def _diff_attn_kernel(lam_ref, q_ref, k_ref, v_ref, gsub_ref, o_ref, m_ref, acc_ref):
    j = pl.program_id(2)

    @pl.when(j == 0)
    def _():
        m_ref[...] = jnp.full(m_ref.shape, -jnp.inf, f32)
        acc_ref[...] = jnp.zeros(acc_ref.shape, f32)

    q = q_ref[0]
    k = k_ref[0]
    group = lax.broadcasted_iota(jnp.int32, (1, BRANCH), 1) // DIFF_HEAD_DIM
    for g in range(2 * DIFF_HEADS):
        qg = jnp.where(group == g, q, jnp.zeros_like(q))
        s = lax.dot_general(qg, k, _NT, preferred_element_type=f32)
        hh = g // 2
        _online_softmax_step(s, v_ref[0, :, hh * LANE:(hh + 1) * LANE], m_ref, acc_ref, g)

    @pl.when(j == pl.num_programs(2) - 1)
    def _():
        _diff_finalize(lam_ref, gsub_ref, acc_ref, o_ref)


def _mla_attn_bounded_kernel(q_ref, k_ref, v_ref, o_ref, acc_ref, *, tk):
    acc_ref[...] = jnp.zeros(acc_ref.shape, f32)

    def kv_block(j, carry):
        rows = pl.ds(pl.multiple_of(j * tk, tk), tk)
        for hh in range(MLA_HEADS):
            sl = slice(hh * LANE, (hh + 1) * LANE)
            s = lax.dot_general(q_ref[0, :, sl], k_ref[0, rows, sl], _NT, preferred_element_type=f32)
            acc_ref[hh] += jnp.dot(jnp.exp2(s).astype(bf16), v_ref[0, rows, sl], preferred_element_type=f32)
        return carry

    lax.fori_loop(0, k_ref.shape[1] // tk, kv_block, 0)
    for hh in range(MLA_HEADS):
        acc = acc_ref[hh]
        o_ref[0, :, hh * LANE:(hh + 1) * LANE] = (acc / acc[:, MLA_V:MLA_V + 1]).astype(bf16)


def _diff_attn_bounded_kernel(lam_ref, q_ref, k_ref, v_ref, gsub_ref, o_ref, acc_ref, *, tk):
    acc_ref[...] = jnp.zeros(acc_ref.shape, f32)
    q = q_ref[0]
    group = lax.broadcasted_iota(jnp.int32, (1, BRANCH), 1) // DIFF_HEAD_DIM
    qs = [jnp.where(group == g, q, jnp.zeros_like(q)) for g in range(2 * DIFF_HEADS)]

    def kv_block(j, carry):
        rows = pl.ds(pl.multiple_of(j * tk, tk), tk)
        k = k_ref[0, rows, :]
        for g in range(2 * DIFF_HEADS):
            s = lax.dot_general(qs[g], k, _NT, preferred_element_type=f32)
            hh = g // 2
            acc_ref[g] += jnp.dot(jnp.exp2(s).astype(bf16), v_ref[0, rows, hh * LANE:(hh + 1) * LANE],
                                  preferred_element_type=f32)
        return carry

    lax.fori_loop(0, k_ref.shape[1] // tk, kv_block, 0)
    _diff_finalize(lam_ref, gsub_ref, acc_ref, o_ref)


def _attn_bounded_call(kernel_fn, n_acc, name, args, in_specs, S, W, B, tq):
    return pl.pallas_call(
        kernel_fn,
        grid=(B, S // tq),
        in_specs=in_specs,
        out_specs=pl.BlockSpec((1, tq, W), lambda b, i: (b, i, 0)),
        out_shape=jax.ShapeDtypeStruct((B, S, W), bf16),
        scratch_shapes=[pltpu.VMEM((n_acc, tq, LANE), f32)],
        compiler_params=_cparams(("parallel", "parallel")),
        name=name,
    )(*args)


def _mla_attn_bounded(q, k, v, tq, tk):
    B, S, W = q.shape
    full = pl.BlockSpec((1, S, W), lambda b, i: (b, 0, 0))
    in_specs = [pl.BlockSpec((1, tq, W), lambda b, i: (b, i, 0)), full, full]
    return _attn_bounded_call(functools.partial(_mla_attn_bounded_kernel, tk=tk), MLA_HEADS, "mla_attn_bounded",
                              (q, k, v), in_specs, S, W, B, tq)


def _diff_attn_bounded(lam, q, k, v, gsub, tq, tk):
    B, S, W = v.shape
    in_specs = [pl.BlockSpec(memory_space=pltpu.SMEM),
                pl.BlockSpec((1, tq, BRANCH), lambda b, i: (b, i, 0)),
                pl.BlockSpec((1, S, BRANCH), lambda b, i: (b, 0, 0)),
                pl.BlockSpec((1, S, W), lambda b, i: (b, 0, 0)),
                pl.BlockSpec((1, LANE), lambda b, i: (0, 0))]
    return _attn_bounded_call(functools.partial(_diff_attn_bounded_kernel, tk=tk), 2 * DIFF_HEADS,
                              "diff_attn_bounded", (lam, q, k, v, gsub), in_specs, S, W, B, tq)


def _mla_attn(q, k, v, tq, tk):
    B, S, W = q.shape
    return pl.pallas_call(
        _mla_attn_kernel,
        grid=(B, S // tq, S // tk),
        in_specs=[pl.BlockSpec((1, tq, W), lambda b, i, j: (b, i, 0)),
                  pl.BlockSpec((1, tk, W), lambda b, i, j: (b, j, 0)),
                  pl.BlockSpec((1, tk, W), lambda b, i, j: (b, j, 0))],
        out_specs=pl.BlockSpec((1, tq, W), lambda b, i, j: (b, i, 0)),
        out_shape=jax.ShapeDtypeStruct((B, S, W), bf16),
        scratch_shapes=[pltpu.VMEM((MLA_HEADS, tq, LANE), f32), pltpu.VMEM((MLA_HEADS, tq, LANE), f32)],
        compiler_params=_cparams(("parallel", "parallel", "arbitrary")),
        name="mla_attn",
    )(q, k, v)


def _diff_attn(lam, q, k, v, gsub, tq, tk):
    B, S, W = v.shape
    return pl.pallas_call(
        _diff_attn_kernel,
        grid=(B, S // tq, S // tk),
        in_specs=[pl.BlockSpec(memory_space=pltpu.SMEM),
                  pl.BlockSpec((1, tq, BRANCH), lambda b, i, j: (b, i, 0)),
                  pl.BlockSpec((1, tk, BRANCH), lambda b, i, j: (b, j, 0)),
                  pl.BlockSpec((1, tk, W), lambda b, i, j: (b, j, 0)),
                  pl.BlockSpec((1, LANE), lambda b, i, j: (0, 0))],
        out_specs=pl.BlockSpec((1, tq, W), lambda b, i, j: (b, i, 0)),
        out_shape=jax.ShapeDtypeStruct((B, S, W), bf16),
        scratch_shapes=[pltpu.VMEM((2 * DIFF_HEADS, tq, LANE), f32), pltpu.VMEM((2 * DIFF_HEADS, tq, LANE), f32)],
        compiler_params=_cparams(("parallel", "parallel", "arbitrary")),
        name="diff_attn",
    )(lam, q, k, v, gsub)


def _fnet1_kernel(x_ref, w_ref, a_ref):
    a_ref[0] = jnp.dot(w_ref[...], x_ref[0], preferred_element_type=f32).astype(bf16)


def _fnet2_kernel(a_ref, t_ref, cs_ref, o_ref):
    for jj in range(FNET_K1_BLOCK):
        a = jnp.concatenate([a_ref[0, 0, jj], a_ref[0, 1, jj]], axis=0)
        z = jnp.dot(t_ref[jj], a, preferred_element_type=f32)
        zc = jnp.concatenate([z[:FNET_N2], z[FNET_N2:]], axis=1).astype(bf16)
        o_ref[0, :, jj, :] = jnp.dot(zc, cs_ref[...], preferred_element_type=f32).astype(bf16)


def _fnet_consts(S):
    n1 = S // FNET_N2
    a = np.arange(n1)
    ang1 = 2.0 * np.pi * np.outer(a, a) / n1
    w1 = np.concatenate([np.cos(ang1), -np.sin(ang1)], axis=0)
    n2 = np.arange(FNET_N2)
    phi = 2.0 * np.pi * (n2[None, None, :] * a[:, None, None] / S + n2[None, None, :] * n2[None, :, None] / FNET_N2)
    tr, ti = np.cos(phi), -np.sin(phi)
    t = np.concatenate([np.concatenate([tr, -ti], axis=2), np.concatenate([ti, tr], axis=2)], axis=1)
    c = np.arange(BRANCH)
    same = (c[:, None] // FNET_N2) == (c[None, :] // FNET_N2)
    angc = 2.0 * np.pi * np.outer(c % FNET_N2, c % FNET_N2) / FNET_N2
    norm = 1.0 / math.sqrt(S * FNET_N2)
    cs = np.concatenate([np.where(same, np.cos(angc), 0.0), np.where(same, np.sin(angc), 0.0)], axis=0) * norm
    return (jnp.asarray(w1, f32).astype(bf16), jnp.asarray(t, f32).astype(bf16), jnp.asarray(cs, f32).astype(bf16))


def _fnet(u, consts):
    B, S, W = u.shape
    w1, t, cs = consts
    n1 = S // FNET_N2
    cols = FNET_N2 * W
    tn = min(cols, 4096)
    a = pl.pallas_call(
        _fnet1_kernel,
        grid=(B, cols // tn),
        in_specs=[pl.BlockSpec((1, n1, tn), lambda b, i: (b, 0, i)), _const_spec(w1.shape)],
        out_specs=pl.BlockSpec((1, 2 * n1, tn), lambda b, i: (b, 0, i)),
        out_shape=jax.ShapeDtypeStruct((B, 2 * n1, cols), bf16),
        compiler_params=_cparams(("parallel", "parallel")),
        name="fnet1",
    )(u.reshape(B, n1, cols), w1)
    kb = FNET_K1_BLOCK
    y = pl.pallas_call(
        _fnet2_kernel,
        grid=(B, n1 // kb),
        in_specs=[pl.BlockSpec((1, 2, kb, FNET_N2, W), lambda b, i: (b, 0, i, 0, 0)),
                  pl.BlockSpec((kb, 2 * FNET_N2, 2 * FNET_N2), lambda b, i: (i, 0, 0)),
                  _const_spec(cs.shape)],
        out_specs=pl.BlockSpec((1, FNET_N2, kb, W), lambda b, i: (b, 0, i, 0)),
        out_shape=jax.ShapeDtypeStruct((B, FNET_N2, n1, W), bf16),
        compiler_params=_cparams(("parallel", "parallel")),
        name="fnet2",
    )(a.reshape(B, 2, n1, FNET_N2, W), t, cs)
    return y.reshape(B, S, W)


def _hgrn_consts(C):
    L = int(math.log2(C))
    t = np.arange(C)
    m = np.zeros((2, (L + 2) * C, C), np.float32)
    lvl = np.full((2, C, C), -1, np.int32)
    for d in (0, 1):
        for li in range(L):
            n = C >> li
            half = n // 2
            blk = t // n
            mid = blk * n + half
            upper = (t % n) >= half
            for r in range(C):
                if d == 0:
                    if upper[r]:
                        m[d, li * C + r, mid[r]:r + 1] = 1.0
                    else:
                        m[d, li * C + r, r + 1:mid[r]] = 1.0
                else:
                    if upper[r]:
                        m[d, li * C + r, mid[r]:r] = 1.0
                    else:
                        m[d, li * C + r, r:mid[r]] = 1.0
            same = blk[:, None] == blk[None, :]
            if d == 0:
                msk = same & upper[:, None] & (~upper)[None, :]
            else:
                msk = same & (~upper)[:, None] & upper[None, :]
            lvl[d][msk] = li
        for r in range(C):
            if d == 0:
                m[d, L * C + r, :r + 1] = 1.0
                m[d, (L + 1) * C + r, r + 1:] = 1.0
            else:
                m[d, L * C + r, r:] = 1.0
                m[d, (L + 1) * C + r, :r] = 1.0
    c = np.arange(BRANCH)
    headsum = ((c[:, None] // HGRN_DIM) == (c[None, :] // HGRN_DIM)).astype(np.float32)
    lvl = np.tile(lvl, (1, 1, HGRN_HEADS))
    return jnp.asarray(m, f32).astype(bf16), jnp.asarray(lvl), jnp.asarray(headsum, f32).astype(bf16)


def _hgrn_kernel(qf_ref, vf_ref, kf_ref, lff_ref, qb_ref, vb_ref, kb_ref, lfb_ref, m_ref, lvl_ref, hs_ref,
                 of_ref, ob_ref, st_ref, *, C, L):
    @pl.when(pl.program_id(1) == 0)
    def _():
        st_ref[...] = jnp.zeros(st_ref.shape, f32)

    _hgrn_chunk(0, qf_ref, vf_ref, kf_ref, lff_ref, m_ref, lvl_ref, hs_ref, of_ref, st_ref, C, L)
    _hgrn_chunk(1, qb_ref, vb_ref, kb_ref, lfb_ref, m_ref, lvl_ref, hs_ref, ob_ref, st_ref, C, L)


def _hgrn_chunk(d, q_ref, v_ref, k_ref, lf_ref, m_ref, lvl_ref, hs_ref, o_ref, st_ref, C, L):
    q = q_ref[0].astype(f32)
    k = k_ref[0].astype(f32)
    v = v_ref[0]
    e2 = jnp.dot(m_ref[d], lf_ref[0], preferred_element_type=f32)
    e = e2[:, :BRANCH] + e2[:, BRANCH:]
    head = lax.broadcasted_iota(jnp.int32, (1, BRANCH), 1) // HGRN_DIM
    lvl = lvl_ref[d]

    scores = jnp.zeros((C, HGRN_HEADS * C), f32)
    for li in range(L):
        ex = jnp.exp(e[li * C:(li + 1) * C])
        qt = (q * ex).astype(bf16)
        kt = (k * ex).astype(bf16)
        kstack = jnp.concatenate([jnp.where(head == hh, kt, jnp.zeros_like(kt)) for hh in range(HGRN_HEADS)], axis=0)
        s = lax.dot_general(qt, kstack, _NT, preferred_element_type=f32)
        scores = jnp.where(lvl == li, s, scores)
    scores = scores.astype(bf16)
    o = jnp.zeros((C, BRANCH), f32)
    for hh in range(HGRN_HEADS):
        vh = jnp.where(head == hh, v, jnp.zeros_like(v))
        o = o + jnp.dot(scores[:, hh * C:(hh + 1) * C], vh, preferred_element_type=f32)
    o = o + jnp.dot((q * k).astype(bf16), hs_ref[...], preferred_element_type=f32) * v.astype(f32)
    ein = jnp.exp(e[L * C:(L + 1) * C])
    st = st_ref[d]
    o = o + lax.dot_general((q * ein).astype(bf16), st.astype(bf16), _NT, preferred_element_type=f32)
    o_ref[0] = o.astype(bf16)
    kout = (k * jnp.exp(e[(L + 1) * C:(L + 2) * C])).astype(bf16)
    upd = lax.dot_general(v, kout, _TN, preferred_element_type=f32)
    total = ein[C - 1:C] if d == 0 else ein[0:1]
    r = lax.broadcasted_iota(jnp.int32, (BRANCH, BRANCH), 0) // HGRN_DIM
    cidx = lax.broadcasted_iota(jnp.int32, (BRANCH, BRANCH), 1) // HGRN_DIM
    st_ref[d] = st * total + jnp.where(r == cidx, upd, 0.0)


def _hgrn(hq, hv, hk, hlf, consts, C):
    B, S, W = hq.shape
    m, lvl, hs = consts
    L = int(math.log2(C))
    nb = S // C
    fwd = lambda w, c: pl.BlockSpec((1, C, w), lambda b, i: (b, i, c))
    bwd = lambda w, c: pl.BlockSpec((1, C, w), lambda b, i: (b, nb - 1 - i, c))
    return pl.pallas_call(
        functools.partial(_hgrn_kernel, C=C, L=L),
        grid=(B, nb),
        in_specs=[fwd(W, 0), fwd(W, 0), fwd(W, 0), fwd(2 * W, 0), bwd(W, 0), bwd(W, 0), bwd(W, 1), bwd(2 * W, 1),
                  _const_spec(m.shape), _const_spec(lvl.shape), _const_spec(hs.shape)],
        out_specs=[fwd(W, 0), bwd(W, 0)],
        out_shape=[jax.ShapeDtypeStruct((B, S, W), bf16)] * 2,
        scratch_shapes=[pltpu.VMEM((2, BRANCH, BRANCH), f32)],
        compiler_params=_cparams(("parallel", "arbitrary")),
        name="hgrn",
    )(hq, hv, hk, hlf, hq, hv, hk, hlf, m, lvl, hs)


def _merge_kernel(x_ref, om_ref, of_ref, od_ref, ohf_ref, ohb_ref, hg_ref, ln_ref, wg_ref, wbm_ref, wbf_ref, wbd_ref,
                  wbh_ref, wo_ref, gon_ref, g64_ref, o_ref):
    x = x_ref[0]
    h = _rms(x, ln_ref[...]).astype(bf16)
    oh = ohf_ref[0].astype(f32) + ohb_ref[0].astype(f32)
    ms = jnp.dot((oh * oh).astype(bf16), g64_ref[...], preferred_element_type=f32)
    oh = (oh * lax.rsqrt(ms + EPS) * gon_ref[...] * _sigmoid(hg_ref[0].astype(f32))).astype(bf16)
    merged = None
    for n, (o_n, w_ref) in enumerate(((om_ref[0], wbm_ref), (of_ref[0], wbf_ref), (od_ref[0], wbd_ref), (oh, wbh_ref))):
        gate = _sigmoid(jnp.dot(h, wg_ref[:, n * D_MODEL:(n + 1) * D_MODEL], preferred_element_type=f32))
        y = gate * jnp.dot(o_n, w_ref[...], preferred_element_type=f32)
        merged = y if merged is None else merged + y
    o_ref[0] = x + jnp.dot(merged.astype(bf16), wo_ref[...], preferred_element_type=f32)


def _merge(x, om, of, od, oh, hg, p, tm):
    B, S, _ = x.shape
    row = lambda w: pl.BlockSpec((1, tm, w), lambda b, i: (b, i, 0))
    consts = (p["ln_mix"], p["wgate"], p["wbm"], p["wbf"], p["wbd"], p["wbh"], p["wout"], p["gon"], p["g64"])
    return pl.pallas_call(
        _merge_kernel,
        grid=(B, S // tm),
        in_specs=[row(D_MODEL), row(512), row(BRANCH), row(512), row(BRANCH), row(BRANCH), row(BRANCH)]
                 + [_const_spec(c.shape) for c in consts],
        out_specs=row(D_MODEL),
        out_shape=jax.ShapeDtypeStruct((B, S, D_MODEL), f32),
        compiler_params=_cparams(("parallel", "parallel")),
        name="merge",
    )(x, om, of, od, oh[0], oh[1], hg, *consts)


def _ffn_kernel(x_ref, ln_ref, wgu_ref, wd_ref, o_ref):
    x = x_ref[0]
    h = _rms(x, ln_ref[...]).astype(bf16)

    def chunk(c, acc):
        gu = jnp.dot(h, wgu_ref[c], preferred_element_type=f32)
        g = gu[:, :FF_CHUNK]
        a = (g * _sigmoid(g) * gu[:, FF_CHUNK:]).astype(bf16)
        return acc + jnp.dot(a, wd_ref[c], preferred_element_type=f32)

    o_ref[0] = lax.fori_loop(0, D_FF // FF_CHUNK, chunk, x, unroll=True)


def _ffn(x, p, tm):
    B, S, _ = x.shape
    row = pl.BlockSpec((1, tm, D_MODEL), lambda b, i: (b, i, 0))
    consts = (p["ln_ffn"], p["wgu"], p["wd"])
    return pl.pallas_call(
        _ffn_kernel,
        grid=(B, S // tm),
        in_specs=[row] + [_const_spec(c.shape) for c in consts],
        out_specs=row,
        out_shape=jax.ShapeDtypeStruct((B, S, D_MODEL), f32),
        compiler_params=_cparams(("parallel", "parallel")),
        name="ffn",
    )(x, *consts)


def _pad_heads(w, heads, width):
    lead = w.shape[:-1]
    w = w.reshape(lead + (heads, width))
    w = jnp.pad(w, [(0, 0)] * len(lead) + [(0, 0), (0, LANE - width)])
    return w.reshape(lead + (heads * LANE,))


def _rope_tables(S):
    def cs(dim):
        inv = 1.0 / (ROPE_THETA ** (jnp.arange(0, dim, 2, dtype=f32) / dim))
        ang = jnp.arange(S, dtype=f32)[:, None] * inv[None, :]
        return jnp.cos(ang), jnp.sin(ang)

    cm, sm = cs(MLA_ROPE)
    one, zero = jnp.ones((S, MLA_NOPE), f32), jnp.zeros((S, MLA_NOPE), f32)
    z16, z32, o32 = jnp.zeros((S, 16), f32), jnp.zeros((S, 32), f32), jnp.ones((S, 32), f32)
    rope_m = jnp.stack([jnp.concatenate([one, cm, cm, o32], 1),
                        jnp.concatenate([zero, -sm, z16, z32], 1),
                        jnp.concatenate([zero, z16, sm, z32], 1)])
    cd, sd = cs(DIFF_HEAD_DIM)
    rope_d = jnp.stack([jnp.tile(jnp.concatenate([cd, cd], 1), (1, 4)),
                        jnp.tile(jnp.concatenate([-sd, z16], 1), (1, 4)),
                        jnp.tile(jnp.concatenate([z16, sd], 1), (1, 4))])
    return rope_m, rope_d


def _block_mean(width, group):
    c = np.arange(width)
    return jnp.asarray(((c[:, None] // group) == (c[None, :] // group)) / group, f32).astype(bf16)


def _layer_params(l, a, lower_bounds):
    w = a["w_in"][l]
    offs = np.cumsum([0, 384, 128, 32, 256, 256, 256, 256, 256, 256, 256, 256, 256, 4096])
    col = lambda i: w[:, offs[i]:offs[i + 1]]
    wcat = jnp.concatenate([col(0), col(1), jnp.pad(col(2), ((0, 0), (MLA_NOPE, LANE - MLA_QK))), col(3), col(4),
                            col(5), _pad_heads(col(6), DIFF_HEADS, 2 * DIFF_HEAD_DIM), col(7), col(8), col(9),
                            col(10), col(11)], axis=1).astype(bf16)
    wukv = a["mla_w_ukv"][l].reshape(MLA_KV_LORA, MLA_HEADS, MLA_NOPE + MLA_V)
    wukv = jnp.concatenate([_pad_heads(wukv[:, :, :MLA_NOPE].reshape(MLA_KV_LORA, -1), MLA_HEADS, MLA_NOPE),
                            _pad_heads(wukv[:, :, MLA_NOPE:].reshape(MLA_KV_LORA, -1), MLA_HEADS, MLA_V)], axis=1)
    lam_init = 0.8 - 0.6 * math.exp(-0.3 * l)
    lam = (jnp.exp(jnp.sum(a["diff_lq1"][l] * a["diff_lk1"][l])) - jnp.exp(jnp.sum(a["diff_lq2"][l] * a["diff_lk2"][l]))
           + lam_init)
    wb = a["w_branch"][l]
    wgu = a["w_gate_up"][l]
    nc = D_FF // FF_CHUNK
    wgu = jnp.concatenate([wgu[:, :D_FF].reshape(D_MODEL, nc, FF_CHUNK), wgu[:, D_FF:].reshape(D_MODEL, nc, FF_CHUNK)],
                          axis=2).transpose(1, 0, 2)
    row = lambda v: v.reshape(1, -1).astype(f32)
    return {
        "ln_mix": row(a["ln_mix"][l]),
        "wcat": wcat,
        "gqa": row(a["mla_g_qa"][l]),
        "wuq": _pad_heads(a["mla_w_uq"][l], MLA_HEADS, MLA_QK).astype(bf16),
        "gkva": row(a["mla_g_kva"][l]),
        "wukv": wukv.astype(bf16),
        "gqn": row(jnp.pad(a["mla_g_qn"][l], (0, LANE - MLA_QK))) * (MLA_QK ** -0.5 * LOG2E),
        "gkn": row(jnp.pad(a["mla_g_kn"][l], (0, LANE - MLA_QK))),
        "gdq": row(jnp.tile(a["diff_g_qn"][l], 2 * DIFF_HEADS)) * (DIFF_HEAD_DIM ** -0.5 * LOG2E),
        "gdk": row(jnp.tile(a["diff_g_kn"][l], 2 * DIFF_HEADS)),
        "g32": _block_mean(BRANCH, DIFF_HEAD_DIM),
        "lb": lower_bounds[:, l].astype(f32),
        "lam": lam.reshape(1).astype(f32),
        "mla_bound": MLA_QK ** 0.5 * LOG2E * jnp.max(jnp.abs(a["mla_g_qn"][l])) * jnp.max(jnp.abs(a["mla_g_kn"][l])),
        "diff_bound": (DIFF_HEAD_DIM ** 0.5 * LOG2E * jnp.max(jnp.abs(a["diff_g_qn"][l]))
                       * jnp.max(jnp.abs(a["diff_g_kn"][l]))),
        "gsub": row(jnp.pad(a["diff_g_sub"][l], (0, LANE - 2 * DIFF_HEAD_DIM))) * (1.0 - lam_init),
        "wgate": col(12).astype(bf16),
        "wbm": _pad_heads(wb[0].T, MLA_HEADS, MLA_V).T.astype(bf16),
        "wbf": wb[1].astype(bf16),
        "wbd": _pad_heads(wb[2].T, DIFF_HEADS, 2 * DIFF_HEAD_DIM).T.astype(bf16),
        "wbh": wb[3].astype(bf16),
        "wout": a["w_out"][l].astype(bf16),
        "gon": row(jnp.tile(a["hgrn_g_on"][l], HGRN_HEADS)),
        "g64": _block_mean(BRANCH, HGRN_DIM),
        "ln_ffn": row(a["ln_ffn"][l]),
        "wgu": wgu.astype(bf16),
        "wd": a["w_down"][l].reshape(nc, FF_CHUNK, D_MODEL).astype(bf16),
    }


def _tiles(S):
    return {"tm_in": min(256, S), "tq": min(512, S), "tk": min(512, S), "tqb": min(512, S), "tkb": min(1024, S),
            "hgrn_chunk": min(128, S),
            "tm_merge": min(256, S), "tm_ffn": min(512, S)}


def kernel(x, ln_mix, w_in, mla_g_qa, mla_w_uq, mla_g_kva, mla_w_ukv, mla_g_qn, mla_g_kn, diff_g_qn, diff_g_kn,
           diff_lq1, diff_lk1, diff_lq2, diff_lk2, diff_g_sub, hgrn_lb_logits, hgrn_g_on, w_branch, w_out, ln_ffn,
           w_gate_up, w_down):
    a = dict(ln_mix=ln_mix, w_in=w_in, mla_g_qa=mla_g_qa, mla_w_uq=mla_w_uq, mla_g_kva=mla_g_kva,
             mla_w_ukv=mla_w_ukv, mla_g_qn=mla_g_qn, mla_g_kn=mla_g_kn, diff_g_qn=diff_g_qn, diff_g_kn=diff_g_kn,
             diff_lq1=diff_lq1, diff_lk1=diff_lk1, diff_lq2=diff_lq2, diff_lk2=diff_lk2, diff_g_sub=diff_g_sub,
             hgrn_g_on=hgrn_g_on, w_branch=w_branch, w_out=w_out, ln_ffn=ln_ffn, w_gate_up=w_gate_up, w_down=w_down)
    S = x.shape[1]
    t = _tiles(S)
    rope_m, rope_d = _rope_tables(S)
    lb_p = jax.nn.softmax(hgrn_lb_logits.astype(f32), axis=1)
    lower_bounds = jnp.cumsum(lb_p, axis=1) - lb_p[:, :1]
    fnet_consts = _fnet_consts(S)
    hgrn_consts = _hgrn_consts(t["hgrn_chunk"])
    for l in range(DEPTH):
        p = _layer_params(l, a, lower_bounds)
        qm, km, vm, fn, qd, kd, vd, hq, hv, hk, hlf, hg = _in_proj(x, p, rope_m, rope_d, t["tm_in"])
        om = lax.cond(p["mla_bound"] <= SCORE_BOUND_LOG2,
                      lambda: _mla_attn_bounded(qm, km, vm, t["tqb"], t["tkb"]),
                      lambda: _mla_attn(qm, km, vm, t["tq"], t["tk"]))
        od = lax.cond(p["diff_bound"] <= SCORE_BOUND_LOG2,
                      lambda: _diff_attn_bounded(p["lam"], qd, kd, vd, p["gsub"], t["tqb"], t["tkb"]),
                      lambda: _diff_attn(p["lam"], qd, kd, vd, p["gsub"], t["tq"], t["tk"]))
        of = _fnet(fn, fnet_consts)
        oh = _hgrn(hq, hv, hk, hlf, hgrn_consts, t["hgrn_chunk"])
        x = _merge(x, om, of, od, oh, hg, p, t["tm_merge"])
        x = _ffn(x, p, t["tm_ffn"])
    return x
```

```python
import functools
import math

import numpy as np
import jax
import jax.numpy as jnp
from jax import lax
from jax.experimental import pallas as pl
from jax.experimental.pallas import tpu as pltpu

f32 = jnp.float32
bf16 = jnp.bfloat16

D_MODEL = 1024
DEPTH = 2
MLA_HEADS = 4
MLA_Q_LORA = 384
MLA_KV_LORA = 128
MLA_NOPE = 64
MLA_ROPE = 32
MLA_V = 64
MLA_QK = MLA_NOPE + MLA_ROPE
DIFF_HEADS = 4
DIFF_HEAD_DIM = 32
HGRN_HEADS = 4
HGRN_DIM = 64
BRANCH = 256
D_FF = 2816
FF_CHUNK = 256
ROPE_THETA = 10000.0
EPS = 1e-6
LOG2E = 1.4426950408889634
LANE = 128
FNET_N2 = 64
FNET_K1_BLOCK = 8
VMEM_LIMIT = 56 * 1024 * 1024
SCORE_BOUND_LOG2 = 60.0

_SEG = {}
_off = 0
for _name, _w in (("cq", 384), ("ckv", 128), ("krope", 128), ("fnet", 256), ("dq", 256), ("dk", 256),
                  ("dv", 512), ("hq", 256), ("hi", 256), ("hff", 256), ("hfb", 256), ("hg", 256)):
    _SEG[_name] = (_off, _off + _w)
    _off += _w
W_CAT = _off


def _cparams(sem):
    return pltpu.CompilerParams(dimension_semantics=sem, vmem_limit_bytes=VMEM_LIMIT)


def _const_spec(shape):
    nd = len(shape)
    return pl.BlockSpec(shape, lambda *_: (0,) * nd, pipeline_mode=pl.Buffered(1))


def _rms(x, g):
    return x * lax.rsqrt(jnp.mean(x * x, axis=-1, keepdims=True) + EPS) * g


def _sigmoid(x):
    return 1.0 / (1.0 + jnp.exp(-x))


def _rope(t, tab_ref):
    return t * tab_ref[0] + pltpu.roll(t, LANE - 16, 1) * tab_ref[1] + pltpu.roll(t, 16, 1) * tab_ref[2]


def _in_proj_kernel(x_ref, ln_ref, wcat_ref, gqa_ref, wuq_ref, gkva_ref, wukv_ref, gqn_ref, gkn_ref,
                    ropem_ref, gdq_ref, gdk_ref, g32_ref, roped_ref, lb_ref,
                    qm_ref, km_ref, vm_ref, fn_ref, qd_ref, kd_ref, vd_ref, hq_ref, hv_ref, hk_ref,
                    hlf_ref, hg_ref):
    h = _rms(x_ref[0], ln_ref[...]).astype(bf16)

    def seg(name):
        a, b = _SEG[name]
        return jnp.dot(h, wcat_ref[:, a:b], preferred_element_type=f32)

    ones_lane = (lax.broadcasted_iota(jnp.int32, (1, LANE), 1) == MLA_V).astype(f32)

    cqn = _rms(seg("cq"), gqa_ref[...]).astype(bf16)
    q = jnp.dot(cqn, wuq_ref[...], preferred_element_type=f32)
    for hh in range(MLA_HEADS):
        sl = slice(hh * LANE, (hh + 1) * LANE)
        qh = q[:, sl]
        ms = jnp.sum(qh * qh, axis=-1, keepdims=True) * (1.0 / MLA_QK)
        qm_ref[0, :, sl] = _rope(qh * lax.rsqrt(ms + EPS) * gqn_ref[...], ropem_ref).astype(bf16)

    ckvn = _rms(seg("ckv"), gkva_ref[...]).astype(bf16)
    kv = jnp.dot(ckvn, wukv_ref[...], preferred_element_type=f32)
    krope = seg("krope")
    for hh in range(MLA_HEADS):
        sl = slice(hh * LANE, (hh + 1) * LANE)
        kh = kv[:, sl] + krope
        ms = jnp.sum(kh * kh, axis=-1, keepdims=True) * (1.0 / MLA_QK)
        km_ref[0, :, sl] = _rope(kh * lax.rsqrt(ms + EPS) * gkn_ref[...], ropem_ref).astype(bf16)
        vsl = slice((MLA_HEADS + hh) * LANE, (MLA_HEADS + hh + 1) * LANE)
        vm_ref[0, :, sl] = (kv[:, vsl] + ones_lane).astype(bf16)

    fn_ref[0] = seg("fnet").astype(bf16)

    for name, g_ref, o_ref in (("dq", gdq_ref, qd_ref), ("dk", gdk_ref, kd_ref)):
        t = seg(name)
        ms = jnp.dot((t * t).astype(bf16), g32_ref[...], preferred_element_type=f32)
        t = t * lax.rsqrt(ms + EPS) * g_ref[...]
        for c in range(BRANCH // LANE):
            sl = slice(c * LANE, (c + 1) * LANE)
            o_ref[0, :, sl] = _rope(t[:, sl], roped_ref).astype(bf16)
    dv = seg("dv")
    for hh in range(DIFF_HEADS):
        sl = slice(hh * LANE, (hh + 1) * LANE)
        vd_ref[0, :, sl] = (dv[:, sl] + ones_lane).astype(bf16)

    hq_ref[0] = seg("hq").astype(bf16)
    hv_ref[0] = seg("hi").astype(bf16)
    hg_ref[0] = seg("hg").astype(bf16)
    for d, name in enumerate(("hff", "hfb")):
        lb = lb_ref[d:d + 1, :]
        f = lb + (1.0 - lb) * _sigmoid(seg(name))
        hk_ref[0, :, d * BRANCH:(d + 1) * BRANCH] = (1.0 - f).astype(bf16)
        lf = jnp.log(f)
        hi = lf.astype(bf16)
        hlf_ref[0, :, 2 * d * BRANCH:(2 * d + 1) * BRANCH] = hi
        hlf_ref[0, :, (2 * d + 1) * BRANCH:(2 * d + 2) * BRANCH] = (lf - hi.astype(f32)).astype(bf16)


def _in_proj(x, p, rope_m, rope_d, tm):
    B, S, _ = x.shape
    row = lambda w: pl.BlockSpec((1, tm, w), lambda b, i: (b, i, 0))
    tab = pl.BlockSpec((3, tm, LANE), lambda b, i: (0, i, 0))
    out_w = (512, 512, 512, 256, 256, 256, 512, 256, 256, 512, 1024, 256)
    consts = (p["ln_mix"], p["wcat"], p["gqa"], p["wuq"], p["gkva"], p["wukv"], p["gqn"], p["gkn"])
    consts2 = (p["gdq"], p["gdk"], p["g32"])
    in_specs = ([row(D_MODEL)] + [_const_spec(c.shape) for c in consts] + [tab]
                + [_const_spec(c.shape) for c in consts2] + [tab, _const_spec(p["lb"].shape)])
    return pl.pallas_call(
        _in_proj_kernel,
        grid=(B, S // tm),
        in_specs=in_specs,
        out_specs=[row(w) for w in out_w],
        out_shape=[jax.ShapeDtypeStruct((B, S, w), bf16) for w in out_w],
        compiler_params=_cparams(("parallel", "parallel")),
        name="in_proj",
    )(x, *consts, rope_m, *consts2, rope_d, p["lb"])


def _online_softmax_step(s, v, m_ref, acc_ref, idx):
    m_prev = m_ref[idx]
    m_new = jnp.maximum(m_prev, jnp.max(s, axis=1, keepdims=True))
    alpha = jnp.exp2(m_prev - m_new)
    p = jnp.exp2((s - m_new[:, :1]).astype(bf16))
    acc_ref[idx] = alpha * acc_ref[idx] + jnp.dot(p, v, preferred_element_type=f32)
    m_ref[idx] = m_new


_NT = (((1,), (1,)), ((), ()))
_TN = (((0,), (0,)), ((), ()))


def _mla_attn_kernel(q_ref, k_ref, v_ref, o_ref, m_ref, acc_ref):
    j = pl.program_id(2)

    @pl.when(j == 0)
    def _():
        m_ref[...] = jnp.full(m_ref.shape, -jnp.inf, f32)
        acc_ref[...] = jnp.zeros(acc_ref.shape, f32)

    for hh in range(MLA_HEADS):
        sl = slice(hh * LANE, (hh + 1) * LANE)
        s = lax.dot_general(q_ref[0, :, sl], k_ref[0, :, sl], _NT, preferred_element_type=f32)
        _online_softmax_step(s, v_ref[0, :, sl], m_ref, acc_ref, hh)

    @pl.when(j == pl.num_programs(2) - 1)
    def _():
        for hh in range(MLA_HEADS):
            acc = acc_ref[hh]
            o_ref[0, :, hh * LANE:(hh + 1) * LANE] = (acc / acc[:, MLA_V:MLA_V + 1]).astype(bf16)


def _diff_finalize(lam_ref, gsub_ref, acc_ref, o_ref):
    lam = lam_ref[0]
    vmask = lax.broadcasted_iota(jnp.int32, (1, LANE), 1) < 2 * DIFF_HEAD_DIM
    for hh in range(DIFF_HEADS):
        a1 = acc_ref[2 * hh]
        a2 = acc_ref[2 * hh + 1]
        o = a1 / a1[:, 64:65] - lam * (a2 / a2[:, 64:65])
        o = jnp.where(vmask, o, 0.0)
        ms = jnp.sum(o * o, axis=-1, keepdims=True) * (1.0 / (2 * DIFF_HEAD_DIM))
        o_ref[0, :, hh * LANE:(hh + 1) * LANE] = (o * lax.rsqrt(ms + EPS) * gsub_ref[...]).astype(bf16)


def _diff_attn_kernel(lam_ref, q_ref, k_ref, v_ref, gsub_ref, o_ref, m_ref, acc_ref):
    j = pl.program_id(2)

    @pl.when(j == 0)
    def _():
        m_ref[...] = jnp.full(m_ref.shape, -jnp.inf, f32)
        acc_ref[...] = jnp.zeros(acc_ref.shape, f32)

    q = q_ref[0]
    k = k_ref[0]
    group = lax.broadcasted_iota(jnp.int32, (1, BRANCH), 1) // DIFF_HEAD_DIM
    for g in range(2 * DIFF_HEADS):
        qg = jnp.where(group == g, q, jnp.zeros_like(q))
        s = lax.dot_general(qg, k, _NT, preferred_element_type=f32)
        hh = g // 2
        _online_softmax_step(s, v_ref[0, :, hh * LANE:(hh + 1) * LANE], m_ref, acc_ref, g)

    @pl.when(j == pl.num_programs(2) - 1)
    def _():
        _diff_finalize(lam_ref, gsub_ref, acc_ref, o_ref)


def _mla_attn_bounded_kernel(q_ref, k_ref, v_ref, o_ref, acc_ref, *, tk):
    acc_ref[...] = jnp.zeros(acc_ref.shape, f32)

    def kv_block(j, carry):
        rows = pl.ds(pl.multiple_of(j * tk, tk), tk)
        for hh in range(MLA_HEADS):
            sl = slice(hh * LANE, (hh + 1) * LANE)
            s = lax.dot_general(q_ref[0, :, sl], k_ref[0, rows, sl], _NT, preferred_element_type=f32)
            acc_ref[hh] += jnp.dot(jnp.exp2(s).astype(bf16), v_ref[0, rows, sl], preferred_element_type=f32)
        return carry

    lax.fori_loop(0, k_ref.shape[1] // tk, kv_block, 0)
    for hh in range(MLA_HEADS):
        acc = acc_ref[hh]
        o_ref[0, :, hh * LANE:(hh + 1) * LANE] = (acc / acc[:, MLA_V:MLA_V + 1]).astype(bf16)


def _diff_attn_bounded_kernel(lam_ref, q_ref, k_ref, v_ref, gsub_ref, o_ref, acc_ref, *, tk):
    acc_ref[...] = jnp.zeros(acc_ref.shape, f32)
    q = q_ref[0]
    group = lax.broadcasted_iota(jnp.int32, (1, BRANCH), 1) // DIFF_HEAD_DIM
    qs = [jnp.where(group == g, q, jnp.zeros_like(q)) for g in range(2 * DIFF_HEADS)]

    def kv_block(j, carry):
        rows = pl.ds(pl.multiple_of(j * tk, tk), tk)
        k = k_ref[0, rows, :]
        for g in range(2 * DIFF_HEADS):
            s = lax.dot_general(qs[g], k, _NT, preferred_element_type=f32)
            hh = g // 2
            acc_ref[g] += jnp.dot(jnp.exp2(s).astype(bf16), v_ref[0, rows, hh * LANE:(hh + 1) * LANE],
                                  preferred_element_type=f32)
        return carry

    lax.fori_loop(0, k_ref.shape[1] // tk, kv_block, 0)
    _diff_finalize(lam_ref, gsub_ref, acc_ref, o_ref)


def _attn_bounded_call(kernel_fn, n_acc, name, args, in_specs, S, W, B, tq):
    return pl.pallas_call(
        kernel_fn,
        grid=(B, S // tq),
        in_specs=in_specs,
        out_specs=pl.BlockSpec((1, tq, W), lambda b, i: (b, i, 0)),
        out_shape=jax.ShapeDtypeStruct((B, S, W), bf16),
        scratch_shapes=[pltpu.VMEM((n_acc, tq, LANE), f32)],
        compiler_params=_cparams(("parallel", "parallel")),
        name=name,
    )(*args)


def _mla_attn_bounded(q, k, v, tq, tk):
    B, S, W = q.shape
    full = pl.BlockSpec((1, S, W), lambda b, i: (b, 0, 0))
    in_specs = [pl.BlockSpec((1, tq, W), lambda b, i: (b, i, 0)), full, full]
    return _attn_bounded_call(functools.partial(_mla_attn_bounded_kernel, tk=tk), MLA_HEADS, "mla_attn_bounded",
                              (q, k, v), in_specs, S, W, B, tq)


def _diff_attn_bounded(lam, q, k, v, gsub, tq, tk):
    B, S, W = v.shape
    in_specs = [pl.BlockSpec(memory_space=pltpu.SMEM),
                pl.BlockSpec((1, tq, BRANCH), lambda b, i: (b, i, 0)),
                pl.BlockSpec((1, S, BRANCH), lambda b, i: (b, 0, 0)),
                pl.BlockSpec((1, S, W), lambda b, i: (b, 0, 0)),
                pl.BlockSpec((1, LANE), lambda b, i: (0, 0))]
    return _attn_bounded_call(functools.partial(_diff_attn_bounded_kernel, tk=tk), 2 * DIFF_HEADS,
                              "diff_attn_bounded", (lam, q, k, v, gsub), in_specs, S, W, B, tq)


def _mla_attn(q, k, v, tq, tk):
    B, S, W = q.shape
    return pl.pallas_call(
        _mla_attn_kernel,
        grid=(B, S // tq, S // tk),
        in_specs=[pl.BlockSpec((1, tq, W), lambda b, i, j: (b, i, 0)),
                  pl.BlockSpec((1, tk, W), lambda b, i, j: (b, j, 0)),
                  pl.BlockSpec((1, tk, W), lambda b, i, j: (b, j, 0))],
        out_specs=pl.BlockSpec((1, tq, W), lambda b, i, j: (b, i, 0)),
        out_shape=jax.ShapeDtypeStruct((B, S, W), bf16),
        scratch_shapes=[pltpu.VMEM((MLA_HEADS, tq, LANE), f32), pltpu.VMEM((MLA_HEADS, tq, LANE), f32)],
        compiler_params=_cparams(("parallel", "parallel", "arbitrary")),
        name="mla_attn",
    )(q, k, v)


def _diff_attn(lam, q, k, v, gsub, tq, tk):
    B, S, W = v.shape
    return pl.pallas_call(
        _diff_attn_kernel,
        grid=(B, S // tq, S // tk),
        in_specs=[pl.BlockSpec(memory_space=pltpu.SMEM),
                  pl.BlockSpec((1, tq, BRANCH), lambda b, i, j: (b, i, 0)),
                  pl.BlockSpec((1, tk, BRANCH), lambda b, i, j: (b, j, 0)),
                  pl.BlockSpec((1, tk, W), lambda b, i, j: (b, j, 0)),
                  pl.BlockSpec((1, LANE), lambda b, i, j: (0, 0))],
        out_specs=pl.BlockSpec((1, tq, W), lambda b, i, j: (b, i, 0)),
        out_shape=jax.ShapeDtypeStruct((B, S, W), bf16),
        scratch_shapes=[pltpu.VMEM((2 * DIFF_HEADS, tq, LANE), f32), pltpu.VMEM((2 * DIFF_HEADS, tq, LANE), f32)],
        compiler_params=_cparams(("parallel", "parallel", "arbitrary")),
        name="diff_attn",
    )(lam, q, k, v, gsub)


def _fnet1_kernel(x_ref, w_ref, a_ref):
    a_ref[0] = jnp.dot(w_ref[...], x_ref[0], preferred_element_type=f32).astype(bf16)


def _fnet2_kernel(a_ref, t_ref, cs_ref, o_ref):
    for jj in range(FNET_K1_BLOCK):
        a = jnp.concatenate([a_ref[0, 0, jj], a_ref[0, 1, jj]], axis=0)
        z = jnp.dot(t_ref[jj], a, preferred_element_type=f32)
        zc = jnp.concatenate([z[:FNET_N2], z[FNET_N2:]], axis=1).astype(bf16)
        o_ref[0, :, jj, :] = jnp.dot(zc, cs_ref[...], preferred_element_type=f32).astype(bf16)


def _fnet_consts(S):
    n1 = S // FNET_N2
    a = np.arange(n1)
    ang1 = 2.0 * np.pi * np.outer(a, a) / n1
    w1 = np.concatenate([np.cos(ang1), -np.sin(ang1)], axis=0)
    n2 = np.arange(FNET_N2)
    phi = 2.0 * np.pi * (n2[None, None, :] * a[:, None, None] / S + n2[None, None, :] * n2[None, :, None] / FNET_N2)
    tr, ti = np.cos(phi), -np.sin(phi)
    t = np.concatenate([np.concatenate([tr, -ti], axis=2), np.concatenate([ti, tr], axis=2)], axis=1)
    c = np.arange(BRANCH)
    same = (c[:, None] // FNET_N2) == (c[None, :] // FNET_N2)
    angc = 2.0 * np.pi * np.outer(c % FNET_N2, c % FNET_N2) / FNET_N2
    norm = 1.0 / math.sqrt(S * FNET_N2)
    cs = np.concatenate([np.where(same, np.cos(angc), 0.0), np.where(same, np.sin(angc), 0.0)], axis=0) * norm
    return (jnp.asarray(w1, f32).astype(bf16), jnp.asarray(t, f32).astype(bf16), jnp.asarray(cs, f32).astype(bf16))


def _fnet(u, consts):
    B, S, W = u.shape
    w1, t, cs = consts
    n1 = S // FNET_N2
    cols = FNET_N2 * W
    tn = min(cols, 4096)
    a = pl.pallas_call(
        _fnet1_kernel,
        grid=(B, cols // tn),
        in_specs=[pl.BlockSpec((1, n1, tn), lambda b, i: (b, 0, i)), _const_spec(w1.shape)],
        out_specs=pl.BlockSpec((1, 2 * n1, tn), lambda b, i: (b, 0, i)),
        out_shape=jax.ShapeDtypeStruct((B, 2 * n1, cols), bf16),
        compiler_params=_cparams(("parallel", "parallel")),
        name="fnet1",
    )(u.reshape(B, n1, cols), w1)
    kb = FNET_K1_BLOCK
    y = pl.pallas_call(
        _fnet2_kernel,
        grid=(B, n1 // kb),
        in_specs=[pl.BlockSpec((1, 2, kb, FNET_N2, W), lambda b, i: (b, 0, i, 0, 0)),
                  pl.BlockSpec((kb, 2 * FNET_N2, 2 * FNET_N2), lambda b, i: (i, 0, 0)),
                  _const_spec(cs.shape)],
        out_specs=pl.BlockSpec((1, FNET_N2, kb, W), lambda b, i: (b, 0, i, 0)),
        out_shape=jax.ShapeDtypeStruct((B, FNET_N2, n1, W), bf16),
        compiler_params=_cparams(("parallel", "parallel")),
        name="fnet2",
    )(a.reshape(B, 2, n1, FNET_N2, W), t, cs)
    return y.reshape(B, S, W)


def _hgrn_consts(C):
    L = int(math.log2(C))
    t = np.arange(C)
    m = np.zeros((2, (L + 2) * C, C), np.float32)
    lvl = np.full((2, C, C), -1, np.int32)
    for d in (0, 1):
        for li in range(L):
            n = C >> li
            half = n // 2
            blk = t // n
            mid = blk * n + half
            upper = (t % n) >= half
            for r in range(C):
                if d == 0:
                    if upper[r]:
                        m[d, li * C + r, mid[r]:r + 1] = 1.0
                    else:
                        m[d, li * C + r, r + 1:mid[r]] = 1.0
                else:
                    if upper[r]:
                        m[d, li * C + r, mid[r]:r] = 1.0
                    else:
                        m[d, li * C + r, r:mid[r]] = 1.0
            same = blk[:, None] == blk[None, :]
            if d == 0:
                msk = same & upper[:, None] & (~upper)[None, :]
            else:
                msk = same & (~upper)[:, None] & upper[None, :]
            lvl[d][msk] = li
        for r in range(C):
            if d == 0:
                m[d, L * C + r, :r + 1] = 1.0
                m[d, (L + 1) * C + r, r + 1:] = 1.0
            else:
                m[d, L * C + r, r:] = 1.0
                m[d, (L + 1) * C + r, :r] = 1.0
    c = np.arange(BRANCH)
    headsum = ((c[:, None] // HGRN_DIM) == (c[None, :] // HGRN_DIM)).astype(np.float32)
    lvl = np.tile(lvl, (1, 1, HGRN_HEADS))
    return jnp.asarray(m, f32).astype(bf16), jnp.asarray(lvl), jnp.asarray(headsum, f32).astype(bf16)


def _hgrn_kernel(qf_ref, vf_ref, kf_ref, lff_ref, qb_ref, vb_ref, kb_ref, lfb_ref, m_ref, lvl_ref, hs_ref,
                 of_ref, ob_ref, st_ref, *, C, L):
    @pl.when(pl.program_id(1) == 0)
    def _():
        st_ref[...] = jnp.zeros(st_ref.shape, f32)

    refs = ((qf_ref, vf_ref, kf_ref, lff_ref, of_ref), (qb_ref, vb_ref, kb_ref, lfb_ref, ob_ref))
    head = lax.broadcasted_iota(jnp.int32, (1, BRANCH), 1) // HGRN_DIM
    r = lax.broadcasted_iota(jnp.int32, (BRANCH, BRANCH), 0) // HGRN_DIM
    cidx = lax.broadcasted_iota(jnp.int32, (BRANCH, BRANCH), 1) // HGRN_DIM
    dirs = (0, 1)
    q = [refs[d][0][0].astype(f32) for d in dirs]
    v = [refs[d][1][0] for d in dirs]
    k = [refs[d][2][0].astype(f32) for d in dirs]
    e = []
    for d in dirs:
        e2 = jnp.dot(m_ref[d], refs[d][3][0], preferred_element_type=f32)
        e.append(e2[:, :BRANCH] + e2[:, BRANCH:])

    o = []
    for d in dirs:
        ein = jnp.exp(e[d][L * C:(L + 1) * C])
        st = st_ref[d]
        o.append(lax.dot_general((q[d] * ein).astype(bf16), st.astype(bf16), _NT, preferred_element_type=f32))
        kout = (k[d] * jnp.exp(e[d][(L + 1) * C:(L + 2) * C])).astype(bf16)
        upd = lax.dot_general(v[d], kout, _TN, preferred_element_type=f32)
        total = ein[C - 1:C] if d == 0 else ein[0:1]
        st_ref[d] = st * total + jnp.where(r == cidx, upd, 0.0)
        o[d] = o[d] + jnp.dot((q[d] * k[d]).astype(bf16), hs_ref[...], preferred_element_type=f32) * v[d].astype(f32)

    scores = [jnp.zeros((C, HGRN_HEADS * C), f32) for _ in dirs]
    for li in range(L):
        for d in dirs:
            ex = jnp.exp(e[d][li * C:(li + 1) * C])
            qt = (q[d] * ex).astype(bf16)
            kt = (k[d] * ex).astype(bf16)
            kstack = jnp.concatenate([jnp.where(head == hh, kt, jnp.zeros_like(kt)) for hh in range(HGRN_HEADS)],
                                     axis=0)
            s = lax.dot_general(qt, kstack, _NT, preferred_element_type=f32)
            scores[d] = jnp.where(lvl_ref[d] == li, s, scores[d])
    for d in dirs:
        sc = scores[d].astype(bf16)
        od = o[d]
        for hh in range(HGRN_HEADS):
            vh = jnp.where(head == hh, v[d], jnp.zeros_like(v[d]))
            od = od + jnp.dot(sc[:, hh * C:(hh + 1) * C], vh, preferred_element_type=f32)
        refs[d][4][0] = od.astype(bf16)


def _hgrn(hq, hv, hk, hlf, consts, C):
    B, S, W = hq.shape
    m, lvl, hs = consts
    L = int(math.log2(C))
    nb = S // C
    fwd = lambda w, c: pl.BlockSpec((1, C, w), lambda b, i: (b, i, c))
    bwd = lambda w, c: pl.BlockSpec((1, C, w), lambda b, i: (b, nb - 1 - i, c))
    return pl.pallas_call(
        functools.partial(_hgrn_kernel, C=C, L=L),
        grid=(B, nb),
        in_specs=[fwd(W, 0), fwd(W, 0), fwd(W, 0), fwd(2 * W, 0), bwd(W, 0), bwd(W, 0), bwd(W, 1), bwd(2 * W, 1),
                  _const_spec(m.shape), _const_spec(lvl.shape), _const_spec(hs.shape)],
        out_specs=[fwd(W, 0), bwd(W, 0)],
        out_shape=[jax.ShapeDtypeStruct((B, S, W), bf16)] * 2,
        scratch_shapes=[pltpu.VMEM((2, BRANCH, BRANCH), f32)],
        compiler_params=_cparams(("parallel", "arbitrary")),
        name="hgrn",
    )(hq, hv, hk, hlf, hq, hv, hk, hlf, m, lvl, hs)


def _merge_kernel(x_ref, om_ref, of_ref, od_ref, ohf_ref, ohb_ref, hg_ref, ln_ref, wg_ref, wbm_ref, wbf_ref, wbd_ref,
                  wbh_ref, wo_ref, gon_ref, g64_ref, o_ref):
    x = x_ref[0]
    h = _rms(x, ln_ref[...]).astype(bf16)
    oh = ohf_ref[0].astype(f32) + ohb_ref[0].astype(f32)
    ms = jnp.dot((oh * oh).astype(bf16), g64_ref[...], preferred_element_type=f32)
    oh = (oh * lax.rsqrt(ms + EPS) * gon_ref[...] * _sigmoid(hg_ref[0].astype(f32))).astype(bf16)
    merged = None
    for n, (o_n, w_ref) in enumerate(((om_ref[0], wbm_ref), (of_ref[0], wbf_ref), (od_ref[0], wbd_ref), (oh, wbh_ref))):
        gate = _sigmoid(jnp.dot(h, wg_ref[:, n * D_MODEL:(n + 1) * D_MODEL], preferred_element_type=f32))
        y = gate * jnp.dot(o_n, w_ref[...], preferred_element_type=f32)
        merged = y if merged is None else merged + y
    o_ref[0] = x + jnp.dot(merged.astype(bf16), wo_ref[...], preferred_element_type=f32)


def _merge(x, om, of, od, oh, hg, p, tm):
    B, S, _ = x.shape
    row = lambda w: pl.BlockSpec((1, tm, w), lambda b, i: (b, i, 0))
    consts = (p["ln_mix"], p["wgate"], p["wbm"], p["wbf"], p["wbd"], p["wbh"], p["wout"], p["gon"], p["g64"])
    return pl.pallas_call(
        _merge_kernel,
        grid=(B, S // tm),
        in_specs=[row(D_MODEL), row(512), row(BRANCH), row(512), row(BRANCH), row(BRANCH), row(BRANCH)]
                 + [_const_spec(c.shape) for c in consts],
        out_specs=row(D_MODEL),
        out_shape=jax.ShapeDtypeStruct((B, S, D_MODEL), f32),
        compiler_params=_cparams(("parallel", "parallel")),
        name="merge",
    )(x, om, of, od, oh[0], oh[1], hg, *consts)


def _ffn_kernel(x_ref, ln_ref, wgu_ref, wd_ref, o_ref):
    x = x_ref[0]
    h = _rms(x, ln_ref[...]).astype(bf16)

    acc = x
    for c in range(D_FF // FF_CHUNK):
        g = jnp.dot(h, wgu_ref[:, c * FF_CHUNK:(c + 1) * FF_CHUNK], preferred_element_type=f32)
        u = jnp.dot(h, wgu_ref[:, D_FF + c * FF_CHUNK:D_FF + (c + 1) * FF_CHUNK], preferred_element_type=f32)
        a = (g * _sigmoid(g) * u).astype(bf16)
        acc = acc + jnp.dot(a, wd_ref[c], preferred_element_type=f32)
    o_ref[0] = acc


def _ffn(x, p, tm):
    B, S, _ = x.shape
    row = pl.BlockSpec((1, tm, D_MODEL), lambda b, i: (b, i, 0))
    consts = (p["ln_ffn"], p["wgu"], p["wd"])
    return pl.pallas_call(
        _ffn_kernel,
        grid=(B, S // tm),
        in_specs=[row] + [_const_spec(c.shape) for c in consts],
        out_specs=row,
        out_shape=jax.ShapeDtypeStruct((B, S, D_MODEL), f32),
        compiler_params=_cparams(("parallel", "parallel")),
        name="ffn",
    )(x, *consts)


def _pad_heads(w, heads, width):
    lead = w.shape[:-1]
    w = w.reshape(lead + (heads, width))
    w = jnp.pad(w, [(0, 0)] * len(lead) + [(0, 0), (0, LANE - width)])
    return w.reshape(lead + (heads * LANE,))


def _rope_tables(S):
    def cs(dim):
        inv = 1.0 / (ROPE_THETA ** (jnp.arange(0, dim, 2, dtype=f32) / dim))
        ang = jnp.arange(S, dtype=f32)[:, None] * inv[None, :]
        return jnp.cos(ang), jnp.sin(ang)

    cm, sm = cs(MLA_ROPE)
    one, zero = jnp.ones((S, MLA_NOPE), f32), jnp.zeros((S, MLA_NOPE), f32)
    z16, z32, o32 = jnp.zeros((S, 16), f32), jnp.zeros((S, 32), f32), jnp.ones((S, 32), f32)
    rope_m = jnp.stack([jnp.concatenate([one, cm, cm, o32], 1),
                        jnp.concatenate([zero, -sm, z16, z32], 1),
                        jnp.concatenate([zero, z16, sm, z32], 1)])
    cd, sd = cs(DIFF_HEAD_DIM)
    rope_d = jnp.stack([jnp.tile(jnp.concatenate([cd, cd], 1), (1, 4)),
                        jnp.tile(jnp.concatenate([-sd, z16], 1), (1, 4)),
                        jnp.tile(jnp.concatenate([z16, sd], 1), (1, 4))])
    return rope_m, rope_d


def _block_mean(width, group):
    c = np.arange(width)
    return jnp.asarray(((c[:, None] // group) == (c[None, :] // group)) / group, f32).astype(bf16)


def _layer_params(l, a, lower_bounds):
    w = a["w_in"][l]
    offs = np.cumsum([0, 384, 128, 32, 256, 256, 256, 256, 256, 256, 256, 256, 256, 4096])
    col = lambda i: w[:, offs[i]:offs[i + 1]]
    wcat = jnp.concatenate([col(0), col(1), jnp.pad(col(2), ((0, 0), (MLA_NOPE, LANE - MLA_QK))), col(3), col(4),
                            col(5), _pad_heads(col(6), DIFF_HEADS, 2 * DIFF_HEAD_DIM), col(7), col(8), col(9),
                            col(10), col(11)], axis=1).astype(bf16)
    wukv = a["mla_w_ukv"][l].reshape(MLA_KV_LORA, MLA_HEADS, MLA_NOPE + MLA_V)
    wukv = jnp.concatenate([_pad_heads(wukv[:, :, :MLA_NOPE].reshape(MLA_KV_LORA, -1), MLA_HEADS, MLA_NOPE),
                            _pad_heads(wukv[:, :, MLA_NOPE:].reshape(MLA_KV_LORA, -1), MLA_HEADS, MLA_V)], axis=1)
    lam_init = 0.8 - 0.6 * math.exp(-0.3 * l)
    lam = (jnp.exp(jnp.sum(a["diff_lq1"][l] * a["diff_lk1"][l])) - jnp.exp(jnp.sum(a["diff_lq2"][l] * a["diff_lk2"][l]))
           + lam_init)
    wb = a["w_branch"][l]
    nc = D_FF // FF_CHUNK
    row = lambda v: v.reshape(1, -1).astype(f32)
    return {
        "ln_mix": row(a["ln_mix"][l]),
        "wcat": wcat,
        "gqa": row(a["mla_g_qa"][l]),
        "wuq": _pad_heads(a["mla_w_uq"][l], MLA_HEADS, MLA_QK).astype(bf16),
        "gkva": row(a["mla_g_kva"][l]),
        "wukv": wukv.astype(bf16),
        "gqn": row(jnp.pad(a["mla_g_qn"][l], (0, LANE - MLA_QK))) * (MLA_QK ** -0.5 * LOG2E),
        "gkn": row(jnp.pad(a["mla_g_kn"][l], (0, LANE - MLA_QK))),
        "gdq": row(jnp.tile(a["diff_g_qn"][l], 2 * DIFF_HEADS)) * (DIFF_HEAD_DIM ** -0.5 * LOG2E),
        "gdk": row(jnp.tile(a["diff_g_kn"][l], 2 * DIFF_HEADS)),
        "g32": _block_mean(BRANCH, DIFF_HEAD_DIM),
        "lb": lower_bounds[:, l].astype(f32),
        "lam": lam.reshape(1).astype(f32),
        "mla_bound": MLA_QK ** 0.5 * LOG2E * jnp.max(jnp.abs(a["mla_g_qn"][l])) * jnp.max(jnp.abs(a["mla_g_kn"][l])),
        "diff_bound": (DIFF_HEAD_DIM ** 0.5 * LOG2E * jnp.max(jnp.abs(a["diff_g_qn"][l]))
                       * jnp.max(jnp.abs(a["diff_g_kn"][l]))),
        "gsub": row(jnp.pad(a["diff_g_sub"][l], (0, LANE - 2 * DIFF_HEAD_DIM))) * (1.0 - lam_init),
        "wgate": col(12).astype(bf16),
        "wbm": _pad_heads(wb[0].T, MLA_HEADS, MLA_V).T.astype(bf16),
        "wbf": wb[1].astype(bf16),
        "wbd": _pad_heads(wb[2].T, DIFF_HEADS, 2 * DIFF_HEAD_DIM).T.astype(bf16),
        "wbh": wb[3].astype(bf16),
        "wout": a["w_out"][l].astype(bf16),
        "gon": row(jnp.tile(a["hgrn_g_on"][l], HGRN_HEADS)),
        "g64": _block_mean(BRANCH, HGRN_DIM),
        "ln_ffn": row(a["ln_ffn"][l]),
        "wgu": a["w_gate_up"][l].astype(bf16),
        "wd": a["w_down"][l].reshape(nc, FF_CHUNK, D_MODEL).astype(bf16),
    }


def _tiles(S):
    return {"tm_in": min(256, S),"tq": min(512, S), "tk": min(512, S), "tqb": min(512, S), "tkb": min(1024, S),
            "hgrn_chunk": min(128, S),
            "tm_merge": min(512, S), "tm_ffn": min(512, S)}


def kernel(x, ln_mix, w_in, mla_g_qa, mla_w_uq, mla_g_kva, mla_w_ukv, mla_g_qn, mla_g_kn, diff_g_qn, diff_g_kn,
           diff_lq1, diff_lk1, diff_lq2, diff_lk2, diff_g_sub, hgrn_lb_logits, hgrn_g_on, w_branch, w_out, ln_ffn,
           w_gate_up, w_down):
    a = dict(ln_mix=ln_mix, w_in=w_in, mla_g_qa=mla_g_qa, mla_w_uq=mla_w_uq, mla_g_kva=mla_g_kva,
             mla_w_ukv=mla_w_ukv, mla_g_qn=mla_g_qn, mla_g_kn=mla_g_kn, diff_g_qn=diff_g_qn, diff_g_kn=diff_g_kn,
             diff_lq1=diff_lq1, diff_lk1=diff_lk1, diff_lq2=diff_lq2, diff_lk2=diff_lk2, diff_g_sub=diff_g_sub,
             hgrn_g_on=hgrn_g_on, w_branch=w_branch, w_out=w_out, ln_ffn=ln_ffn, w_gate_up=w_gate_up, w_down=w_down)
    S = x.shape[1]
    t = _tiles(S)
    rope_m, rope_d = _rope_tables(S)
    lb_p = jax.nn.softmax(hgrn_lb_logits.astype(f32), axis=1)
    lower_bounds = jnp.cumsum(lb_p, axis=1) - lb_p[:, :1]
    fnet_consts = _fnet_consts(S)
    hgrn_consts = _hgrn_consts(t["hgrn_chunk"])
    for l in range(DEPTH):
        p = _layer_params(l, a, lower_bounds)
        qm, km, vm, fn, qd, kd, vd, hq, hv, hk, hlf, hg = _in_proj(x, p, rope_m, rope_d, t["tm_in"])
        om = lax.cond(p["mla_bound"] <= SCORE_BOUND_LOG2,
                      lambda: _mla_attn_bounded(qm, km, vm, t["tqb"], t["tkb"]),
                      lambda: _mla_attn(qm, km, vm, t["tq"], t["tk"]))
        od = lax.cond(p["diff_bound"] <= SCORE_BOUND_LOG2,
                      lambda: _diff_attn_bounded(p["lam"], qd, kd, vd, p["gsub"], t["tqb"], t["tkb"]),
                      lambda: _diff_attn(p["lam"], qd, kd, vd, p["gsub"], t["tq"], t["tk"]))
        of = _fnet(fn, fnet_consts)
        oh = _hgrn(hq, hv, hk, hlf, hgrn_consts, t["hgrn_chunk"])
        x = _merge(x, om, of, od, oh, hg, p, t["tm_merge"])
        x = _ffn(x, p, t["tm_ffn"])
    return x
```

```python
import functools
import math

import numpy as np
import jax
import jax.numpy as jnp
from jax import lax
from jax.experimental import pallas as pl
from jax.experimental.pallas import tpu as pltpu

f32 = jnp.float32
bf16 = jnp.bfloat16

D_MODEL = 1024
DEPTH = 2
MLA_HEADS = 4
MLA_Q_LORA = 384
MLA_KV_LORA = 128
MLA_NOPE = 64
MLA_ROPE = 32
MLA_V = 64
MLA_QK = MLA_NOPE + MLA_ROPE
DIFF_HEADS = 4
DIFF_HEAD_DIM = 32
HGRN_HEADS = 4
HGRN_DIM = 64
BRANCH = 256
D_FF = 2816
FF_CHUNK = 256
ROPE_THETA = 10000.0
EPS = 1e-6
LOG2E = 1.4426950408889634
LANE = 128
FNET_N2 = 64
FNET_K1_BLOCK = 8
HGRN_CHUNKS_PER_STEP = 2
VMEM_LIMIT = 56 * 1024 * 1024
SCORE_BOUND_LOG2 = 60.0

_SEG = {}
_off = 0
for _name, _w in (("cq", 384), ("ckv", 128), ("krope", 128), ("fnet", 256), ("dq", 256), ("dk", 256),
                  ("dv", 512), ("hq", 256), ("hi", 256), ("hff", 256), ("hfb", 256), ("hg", 256)):
    _SEG[_name] = (_off, _off + _w)
    _off += _w
W_CAT = _off


def _cparams(sem):
    return pltpu.CompilerParams(dimension_semantics=sem, vmem_limit_bytes=VMEM_LIMIT)


def _const_spec(shape):
    nd = len(shape)
    return pl.BlockSpec(shape, lambda *_: (0,) * nd, pipeline_mode=pl.Buffered(1))


def _rms(x, g):
    return x * lax.rsqrt(jnp.mean(x * x, axis=-1, keepdims=True) + EPS) * g


def _sigmoid(x):
    return 1.0 / (1.0 + jnp.exp(-x))


def _rope(t, tab_ref):
    return t * tab_ref[0] + pltpu.roll(t, LANE - 16, 1) * tab_ref[1] + pltpu.roll(t, 16, 1) * tab_ref[2]


def _in_proj_kernel(x_ref, ln_ref, wcat_ref, gqa_ref, wuq_ref, gkva_ref, wukv_ref, gqn_ref, gkn_ref,
                    ropem_ref, gdq_ref, gdk_ref, g32_ref, roped_ref, lb_ref,
                    qm_ref, km_ref, vm_ref, fn_ref, qd_ref, kd_ref, vd_ref, hq_ref, hv_ref, hk_ref,
                    hlf_ref, hg_ref):
    h = _rms(x_ref[0], ln_ref[...]).astype(bf16)

    def seg(name):
        a, b = _SEG[name]
        return jnp.dot(h, wcat_ref[:, a:b], preferred_element_type=f32)

    ones_lane = (lax.broadcasted_iota(jnp.int32, (1, LANE), 1) == MLA_V).astype(f32)

    cqn = _rms(seg("cq"), gqa_ref[...]).astype(bf16)
    q = jnp.dot(cqn, wuq_ref[...], preferred_element_type=f32)
    for hh in range(MLA_HEADS):
        sl = slice(hh * LANE, (hh + 1) * LANE)
        qh = q[:, sl]
        ms = jnp.sum(qh * qh, axis=-1, keepdims=True) * (1.0 / MLA_QK)
        qm_ref[0, :, sl] = _rope(qh * lax.rsqrt(ms + EPS) * gqn_ref[...], ropem_ref).astype(bf16)

    ckvn = _rms(seg("ckv"), gkva_ref[...]).astype(bf16)
    kv = jnp.dot(ckvn, wukv_ref[...], preferred_element_type=f32)
    krope = seg("krope")
    for hh in range(MLA_HEADS):
        sl = slice(hh * LANE, (hh + 1) * LANE)
        kh = kv[:, sl] + krope
        ms = jnp.sum(kh * kh, axis=-1, keepdims=True) * (1.0 / MLA_QK)
        km_ref[0, :, sl] = _rope(kh * lax.rsqrt(ms + EPS) * gkn_ref[...], ropem_ref).astype(bf16)
        vsl = slice((MLA_HEADS + hh) * LANE, (MLA_HEADS + hh + 1) * LANE)
        vm_ref[0, :, sl] = (kv[:, vsl] + ones_lane).astype(bf16)

    fn_ref[0] = seg("fnet").astype(bf16)

    for name, g_ref, o_ref in (("dq", gdq_ref, qd_ref), ("dk", gdk_ref, kd_ref)):
        t = seg(name)
        ms = jnp.dot((t * t).astype(bf16), g32_ref[...], preferred_element_type=f32)
        t = t * lax.rsqrt(ms + EPS) * g_ref[...]
        for c in range(BRANCH // LANE):
            sl = slice(c * LANE, (c + 1) * LANE)
            o_ref[0, :, sl] = _rope(t[:, sl], roped_ref).astype(bf16)
    dv = seg("dv")
    for hh in range(DIFF_HEADS):
        sl = slice(hh * LANE, (hh + 1) * LANE)
        vd_ref[0, :, sl] = (dv[:, sl] + ones_lane).astype(bf16)

    hq_ref[0] = seg("hq").astype(bf16)
    hv_ref[0] = seg("hi").astype(bf16)
    hg_ref[0] = seg("hg").astype(bf16)
    for d, name in enumerate(("hff", "hfb")):
        lb = lb_ref[d:d + 1, :]
        f = lb + (1.0 - lb) * _sigmoid(seg(name))
        hk_ref[0, :, d * BRANCH:(d + 1) * BRANCH] = (1.0 - f).astype(bf16)
        lf = jnp.log(f)
        hi = lf.astype(bf16)
        hlf_ref[0, :, 2 * d * BRANCH:(2 * d + 1) * BRANCH] = hi
        hlf_ref[0, :, (2 * d + 1) * BRANCH:(2 * d + 2) * BRANCH] = (lf - hi.astype(f32)).astype(bf16)


def _in_proj(x, p, rope_m, rope_d, tm):
    B, S, _ = x.shape
    row = lambda w: pl.BlockSpec((1, tm, w), lambda b, i: (b, i, 0))
    tab = pl.BlockSpec((3, tm, LANE), lambda b, i: (0, i, 0))
    out_w = (512, 512, 512, 256, 256, 256, 512, 256, 256, 512, 1024, 256)
    consts = (p["ln_mix"], p["wcat"], p["gqa"], p["wuq"], p["gkva"], p["wukv"], p["gqn"], p["gkn"])
    consts2 = (p["gdq"], p["gdk"], p["g32"])
    in_specs = ([row(D_MODEL)] + [_const_spec(c.shape) for c in consts] + [tab]
                + [_const_spec(c.shape) for c in consts2] + [tab, _const_spec(p["lb"].shape)])
    return pl.pallas_call(
        _in_proj_kernel,
        grid=(B, S // tm),
        in_specs=in_specs,
        out_specs=[row(w) for w in out_w],
        out_shape=[jax.ShapeDtypeStruct((B, S, w), bf16) for w in out_w],
        compiler_params=_cparams(("parallel", "parallel")),
        name="in_proj",
    )(x, *consts, rope_m, *consts2, rope_d, p["lb"])


def _online_softmax_step(s, v, m_ref, acc_ref, idx):
    m_prev = m_ref[idx]
    m_new = jnp.maximum(m_prev, jnp.max(s, axis=1, keepdims=True))
    alpha = jnp.exp2(m_prev - m_new)
    p = jnp.exp2((s - m_new[:, :1]).astype(bf16))
    acc_ref[idx] = alpha * acc_ref[idx] + jnp.dot(p, v, preferred_element_type=f32)
    m_ref[idx] = m_new


_NT = (((1,), (1,)), ((), ()))
_TN = (((0,), (0,)), ((), ()))


def _mla_attn_kernel(q_ref, k_ref, v_ref, o_ref, m_ref, acc_ref):
    j = pl.program_id(2)

    @pl.when(j == 0)
    def _():
        m_ref[...] = jnp.full(m_ref.shape, -jnp.inf, f32)
        acc_ref[...] = jnp.zeros(acc_ref.shape, f32)

    for hh in range(MLA_HEADS):
        sl = slice(hh * LANE, (hh + 1) * LANE)
        s = lax.dot_general(q_ref[0, :, sl], k_ref[0, :, sl], _NT, preferred_element_type=f32)
        _online_softmax_step(s, v_ref[0, :, sl], m_ref, acc_ref, hh)

    @pl.when(j == pl.num_programs(2) - 1)
    def _():
        for hh in range(MLA_HEADS):
            acc = acc_ref[hh]
            o_ref[0, :, hh * LANE:(hh + 1) * LANE] = (acc / acc[:, MLA_V:MLA_V + 1]).astype(bf16)


def _diff_finalize(lam_ref, gsub_ref, acc_ref, o_ref):
    lam = lam_ref[0]
    vmask = lax.broadcasted_iota(jnp.int32, (1, LANE), 1) < 2 * DIFF_HEAD_DIM
    for hh in range(DIFF_HEADS):
        a1 = acc_ref[2 * hh]
        a2 = acc_ref[2 * hh + 1]
        o = a1 / a1[:, 64:65] - lam * (a2 / a2[:, 64:65])
        o = jnp.where(vmask, o, 0.0)
        ms = jnp.sum(o * o, axis=-1, keepdims=True) * (1.0 / (2 * DIFF_HEAD_DIM))
        o_ref[0, :, hh * LANE:(hh + 1) * LANE] = (o * lax.rsqrt(ms + EPS) * gsub_ref[...]).astype(bf16)


def _diff_attn_kernel(lam_ref, q_ref, k_ref, v_ref, gsub_ref, o_ref, m_ref, acc_ref):
    j = pl.program_id(2)

    @pl.when(j == 0)
    def _():
        m_ref[...] = jnp.full(m_ref.shape, -jnp.inf, f32)
        acc_ref[...] = jnp.zeros(acc_ref.shape, f32)

    q = q_ref[0]
    k = k_ref[0]
    group = lax.broadcasted_iota(jnp.int32, (1, BRANCH), 1) // DIFF_HEAD_DIM
    for g in range(2 * DIFF_HEADS):
        qg = jnp.where(group == g, q, jnp.zeros_like(q))
        s = lax.dot_general(qg, k, _NT, preferred_element_type=f32)
        hh = g // 2
        _online_softmax_step(s, v_ref[0, :, hh * LANE:(hh + 1) * LANE], m_ref, acc_ref, g)

    @pl.when(j == pl.num_programs(2) - 1)
    def _():
        _diff_finalize(lam_ref, gsub_ref, acc_ref, o_ref)


def _mla_attn_bounded_kernel(q_ref, k_ref, v_ref, o_ref, acc_ref, *, tk):
    acc_ref[...] = jnp.zeros(acc_ref.shape, f32)

    def kv_block(j, carry):
        rows = pl.ds(pl.multiple_of(j * tk, tk), tk)
        for hh in range(MLA_HEADS):
            sl = slice(hh * LANE, (hh + 1) * LANE)
            s = lax.dot_general(q_ref[0, :, sl], k_ref[0, rows, sl], _NT, preferred_element_type=f32)
            acc_ref[hh] += jnp.dot(jnp.exp2(s).astype(bf16), v_ref[0, rows, sl], preferred_element_type=f32)
        return carry

    lax.fori_loop(0, k_ref.shape[1] // tk, kv_block, 0)
    for hh in range(MLA_HEADS):
        acc = acc_ref[hh]
        o_ref[0, :, hh * LANE:(hh + 1) * LANE] = (acc / acc[:, MLA_V:MLA_V + 1]).astype(bf16)


def _diff_attn_bounded_kernel(lam_ref, q_ref, k_ref, v_ref, gsub_ref, o_ref, acc_ref, *, tk):
    acc_ref[...] = jnp.zeros(acc_ref.shape, f32)
    q = q_ref[0]
    group = lax.broadcasted_iota(jnp.int32, (1, BRANCH), 1) // DIFF_HEAD_DIM
    qs = [jnp.where(group == g, q, jnp.zeros_like(q)) for g in range(2 * DIFF_HEADS)]

    def kv_block(j, carry):
        rows = pl.ds(pl.multiple_of(j * tk, tk), tk)
        k = k_ref[0, rows, :]
        for g in range(2 * DIFF_HEADS):
            s = lax.dot_general(qs[g], k, _NT, preferred_element_type=f32)
            hh = g // 2
            acc_ref[g] += jnp.dot(jnp.exp2(s).astype(bf16), v_ref[0, rows, hh * LANE:(hh + 1) * LANE],
                                  preferred_element_type=f32)
        return carry

    lax.fori_loop(0, k_ref.shape[1] // tk, kv_block, 0)
    _diff_finalize(lam_ref, gsub_ref, acc_ref, o_ref)


def _attn_bounded_call(kernel_fn, n_acc, name, args, in_specs, S, W, B, tq):
    return pl.pallas_call(
        kernel_fn,
        grid=(B, S // tq),
        in_specs=in_specs,
        out_specs=pl.BlockSpec((1, tq, W), lambda b, i: (b, i, 0)),
        out_shape=jax.ShapeDtypeStruct((B, S, W), bf16),
        scratch_shapes=[pltpu.VMEM((n_acc, tq, LANE), f32)],
        compiler_params=_cparams(("parallel", "parallel")),
        name=name,
    )(*args)


def _mla_attn_bounded(q, k, v, tq, tk):
    B, S, W = q.shape
    full = pl.BlockSpec((1, S, W), lambda b, i: (b, 0, 0))
    in_specs = [pl.BlockSpec((1, tq, W), lambda b, i: (b, i, 0)), full, full]
    return _attn_bounded_call(functools.partial(_mla_attn_bounded_kernel, tk=tk), MLA_HEADS, "mla_attn_bounded",
                              (q, k, v), in_specs, S, W, B, tq)


def _diff_attn_bounded(lam, q, k, v, gsub, tq, tk):
    B, S, W = v.shape
    in_specs = [pl.BlockSpec(memory_space=pltpu.SMEM),
                pl.BlockSpec((1, tq, BRANCH), lambda b, i: (b, i, 0)),
                pl.BlockSpec((1, S, BRANCH), lambda b, i: (b, 0, 0)),
                pl.BlockSpec((1, S, W), lambda b, i: (b, 0, 0)),
                pl.BlockSpec((1, LANE), lambda b, i: (0, 0))]
    return _attn_bounded_call(functools.partial(_diff_attn_bounded_kernel, tk=tk), 2 * DIFF_HEADS,
                              "diff_attn_bounded", (lam, q, k, v, gsub), in_specs, S, W, B, tq)


def _mla_attn(q, k, v, tq, tk):
    B, S, W = q.shape
    return pl.pallas_call(
        _mla_attn_kernel,
        grid=(B, S // tq, S // tk),
        in_specs=[pl.BlockSpec((1, tq, W), lambda b, i, j: (b, i, 0)),
                  pl.BlockSpec((1, tk, W), lambda b, i, j: (b, j, 0)),
                  pl.BlockSpec((1, tk, W), lambda b, i, j: (b, j, 0))],
        out_specs=pl.BlockSpec((1, tq, W), lambda b, i, j: (b, i, 0)),
        out_shape=jax.ShapeDtypeStruct((B, S, W), bf16),
        scratch_shapes=[pltpu.VMEM((MLA_HEADS, tq, LANE), f32), pltpu.VMEM((MLA_HEADS, tq, LANE), f32)],
        compiler_params=_cparams(("parallel", "parallel", "arbitrary")),
        name="mla_attn",
    )(q, k, v)


def _diff_attn(lam, q, k, v, gsub, tq, tk):
    B, S, W = v.shape
    return pl.pallas_call(
        _diff_attn_kernel,
        grid=(B, S // tq, S // tk),
        in_specs=[pl.BlockSpec(memory_space=pltpu.SMEM),
                  pl.BlockSpec((1, tq, BRANCH), lambda b, i, j: (b, i, 0)),
                  pl.BlockSpec((1, tk, BRANCH), lambda b, i, j: (b, j, 0)),
                  pl.BlockSpec((1, tk, W), lambda b, i, j: (b, j, 0)),
                  pl.BlockSpec((1, LANE), lambda b, i, j: (0, 0))],
        out_specs=pl.BlockSpec((1, tq, W), lambda b, i, j: (b, i, 0)),
        out_shape=jax.ShapeDtypeStruct((B, S, W), bf16),
        scratch_shapes=[pltpu.VMEM((2 * DIFF_HEADS, tq, LANE), f32), pltpu.VMEM((2 * DIFF_HEADS, tq, LANE), f32)],
        compiler_params=_cparams(("parallel", "parallel", "arbitrary")),
        name="diff_attn",
    )(lam, q, k, v, gsub)


def _fnet1_kernel(x_ref, w_ref, a_ref):
    a_ref[0] = jnp.dot(w_ref[...], x_ref[0], preferred_element_type=f32).astype(bf16)


def _fnet2_kernel(a_ref, t_ref, cs_ref, o_ref):
    for jj in range(FNET_K1_BLOCK):
        a = jnp.concatenate([a_ref[0, 0, jj], a_ref[0, 1, jj]], axis=0)
        z = jnp.dot(t_ref[jj], a, preferred_element_type=f32)
        zc = jnp.concatenate([z[:FNET_N2], z[FNET_N2:]], axis=1).astype(bf16)
        o_ref[0, :, jj, :] = jnp.dot(zc, cs_ref[...], preferred_element_type=f32).astype(bf16)


def _fnet_consts(S):
    n1 = S // FNET_N2
    a = np.arange(n1)
    ang1 = 2.0 * np.pi * np.outer(a, a) / n1
    w1 = np.concatenate([np.cos(ang1), -np.sin(ang1)], axis=0)
    n2 = np.arange(FNET_N2)
    phi = 2.0 * np.pi * (n2[None, None, :] * a[:, None, None] / S + n2[None, None, :] * n2[None, :, None] / FNET_N2)
    tr, ti = np.cos(phi), -np.sin(phi)
    t = np.concatenate([np.concatenate([tr, -ti], axis=2), np.concatenate([ti, tr], axis=2)], axis=1)
    c = np.arange(BRANCH)
    same = (c[:, None] // FNET_N2) == (c[None, :] // FNET_N2)
    angc = 2.0 * np.pi * np.outer(c % FNET_N2, c % FNET_N2) / FNET_N2
    norm = 1.0 / math.sqrt(S * FNET_N2)
    cs = np.concatenate([np.where(same, np.cos(angc), 0.0), np.where(same, np.sin(angc), 0.0)], axis=0) * norm
    return (jnp.asarray(w1, f32).astype(bf16), jnp.asarray(t, f32).astype(bf16), jnp.asarray(cs, f32).astype(bf16))


def _fnet(u, consts):
    B, S, W = u.shape
    w1, t, cs = consts
    n1 = S // FNET_N2
    cols = FNET_N2 * W
    tn = min(cols, 4096)
    a = pl.pallas_call(
        _fnet1_kernel,
        grid=(B, cols // tn),
        in_specs=[pl.BlockSpec((1, n1, tn), lambda b, i: (b, 0, i)), _const_spec(w1.shape)],
        out_specs=pl.BlockSpec((1, 2 * n1, tn), lambda b, i: (b, 0, i)),
        out_shape=jax.ShapeDtypeStruct((B, 2 * n1, cols), bf16),
        compiler_params=_cparams(("parallel", "parallel")),
        name="fnet1",
    )(u.reshape(B, n1, cols), w1)
    kb = FNET_K1_BLOCK
    y = pl.pallas_call(
        _fnet2_kernel,
        grid=(B, n1 // kb),
        in_specs=[pl.BlockSpec((1, 2, kb, FNET_N2, W), lambda b, i: (b, 0, i, 0, 0)),
                  pl.BlockSpec((kb, 2 * FNET_N2, 2 * FNET_N2), lambda b, i: (i, 0, 0)),
                  _const_spec(cs.shape)],
        out_specs=pl.BlockSpec((1, FNET_N2, kb, W), lambda b, i: (b, 0, i, 0)),
        out_shape=jax.ShapeDtypeStruct((B, FNET_N2, n1, W), bf16),
        compiler_params=_cparams(("parallel", "parallel")),
        name="fnet2",
    )(a.reshape(B, 2, n1, FNET_N2, W), t, cs)
    return y.reshape(B, S, W)


def _hgrn_consts(C):
    L = int(math.log2(C))
    t = np.arange(C)
    m = np.zeros((2, (L + 2) * C, C), np.float32)
    lvl = np.full((2, C, C), -1, np.int32)
    for d in (0, 1):
        for li in range(L):
            n = C >> li
            half = n // 2
            blk = t // n
            mid = blk * n + half
            upper = (t % n) >= half
            for r in range(C):
                if d == 0:
                    if upper[r]:
                        m[d, li * C + r, mid[r]:r + 1] = 1.0
                    else:
                        m[d, li * C + r, r + 1:mid[r]] = 1.0
                else:
                    if upper[r]:
                        m[d, li * C + r, mid[r]:r] = 1.0
                    else:
                        m[d, li * C + r, r:mid[r]] = 1.0
            same = blk[:, None] == blk[None, :]
            if d == 0:
                msk = same & upper[:, None] & (~upper)[None, :]
            else:
                msk = same & (~upper)[:, None] & upper[None, :]
            lvl[d][msk] = li
        for r in range(C):
            if d == 0:
                m[d, L * C + r, :r + 1] = 1.0
                m[d, (L + 1) * C + r, r + 1:] = 1.0
            else:
                m[d, L * C + r, r:] = 1.0
                m[d, (L + 1) * C + r, :r] = 1.0
    c = np.arange(BRANCH)
    headsum = ((c[:, None] // HGRN_DIM) == (c[None, :] // HGRN_DIM)).astype(np.float32)
    lvl = np.tile(lvl, (1, 1, HGRN_HEADS))
    m = np.tile(m, (1, 1, 2))
    return jnp.asarray(m, f32).astype(bf16), jnp.asarray(lvl), jnp.asarray(headsum, f32).astype(bf16)


def _hgrn_kernel(qf_ref, vf_ref, kf_ref, lff_ref, qb_ref, vb_ref, kb_ref, lfb_ref, m_ref, lvl_ref, hs_ref,
                 of_ref, ob_ref, st_ref, *, C, L):
    @pl.when(pl.program_id(1) == 0)
    def _():
        st_ref[...] = jnp.zeros(st_ref.shape, f32)

    refs = ((qf_ref, vf_ref, kf_ref, lff_ref, of_ref), (qb_ref, vb_ref, kb_ref, lfb_ref, ob_ref))
    head = lax.broadcasted_iota(jnp.int32, (1, BRANCH), 1) // HGRN_DIM
    r = lax.broadcasted_iota(jnp.int32, (BRANCH, BRANCH), 0) // HGRN_DIM
    cidx = lax.broadcasted_iota(jnp.int32, (BRANCH, BRANCH), 1) // HGRN_DIM
    n_chunks = qf_ref.shape[1] // C
    order = (tuple(range(n_chunks)), tuple(reversed(range(n_chunks))))
    chains = [(d, order[d][j]) for j in range(n_chunks) for d in (0, 1)]
    rows = {c: slice(c[1] * C, (c[1] + 1) * C) for c in chains}
    q = {c: refs[c[0]][0][0, rows[c], :].astype(f32) for c in chains}
    v = {c: refs[c[0]][1][0, rows[c], :] for c in chains}
    k = {c: refs[c[0]][2][0, rows[c], :].astype(f32) for c in chains}
    e = {}
    for c in chains:
        lf = refs[c[0]][3][0, rows[c], :]
        lf2 = jnp.concatenate([lf[:, :BRANCH], lf[:, BRANCH:]], axis=0)
        e[c] = jnp.dot(m_ref[c[0]], lf2, preferred_element_type=f32)

    o = {}
    st = [st_ref[0], st_ref[1]]
    for c in chains:
        d = c[0]
        ein = jnp.exp(e[c][L * C:(L + 1) * C])
        o[c] = lax.dot_general((q[c] * ein).astype(bf16), st[d].astype(bf16), _NT, preferred_element_type=f32)
        kout = (k[c] * jnp.exp(e[c][(L + 1) * C:(L + 2) * C])).astype(bf16)
        upd = lax.dot_general(v[c], kout, _TN, preferred_element_type=f32)
        total = ein[C - 1:C] if d == 0 else ein[0:1]
        st[d] = st[d] * total + jnp.where(r == cidx, upd, 0.0)
        o[c] = o[c] + jnp.dot((q[c] * k[c]).astype(bf16), hs_ref[...], preferred_element_type=f32) * v[c].astype(f32)
    st_ref[0] = st[0]
    st_ref[1] = st[1]

    scores = {c: jnp.zeros((C, HGRN_HEADS * C), f32) for c in chains}
    for li in range(L):
        for c in chains:
            ex = jnp.exp(e[c][li * C:(li + 1) * C])
            qt = (q[c] * ex).astype(bf16)
            kt = (k[c] * ex).astype(bf16)
            kstack = jnp.concatenate([jnp.where(head == hh, kt, jnp.zeros_like(kt)) for hh in range(HGRN_HEADS)],
                                     axis=0)
            s = lax.dot_general(qt, kstack, _NT, preferred_element_type=f32)
            scores[c] = jnp.where(lvl_ref[c[0]] == li, s, scores[c])
    for c in chains:
        vstack = jnp.concatenate([jnp.where(head == hh, v[c], jnp.zeros_like(v[c])) for hh in range(HGRN_HEADS)],
                                 axis=0)
        oc = o[c] + jnp.dot(scores[c].astype(bf16), vstack, preferred_element_type=f32)
        refs[c[0]][4][0, rows[c], :] = oc.astype(bf16)


def _hgrn(hq, hv, hk, hlf, consts, C):
    B, S, W = hq.shape
    m, lvl, hs = consts
    L = int(math.log2(C))
    rows = min(S, HGRN_CHUNKS_PER_STEP * C)
    nb = S // rows
    fwd = lambda w, c: pl.BlockSpec((1, rows, w), lambda b, i: (b, i, c))
    bwd = lambda w, c: pl.BlockSpec((1, rows, w), lambda b, i: (b, nb - 1 - i, c))
    return pl.pallas_call(
        functools.partial(_hgrn_kernel, C=C, L=L),
        grid=(B, nb),
        in_specs=[fwd(W, 0), fwd(W, 0), fwd(W, 0), fwd(2 * W, 0), bwd(W, 0), bwd(W, 0), bwd(W, 1), bwd(2 * W, 1),
                  _const_spec(m.shape), _const_spec(lvl.shape), _const_spec(hs.shape)],
        out_specs=[fwd(W, 0), bwd(W, 0)],
        out_shape=[jax.ShapeDtypeStruct((B, S, W), bf16)] * 2,
        scratch_shapes=[pltpu.VMEM((2, BRANCH, BRANCH), f32)],
        compiler_params=_cparams(("parallel", "arbitrary")),
        name="hgrn",
    )(hq, hv, hk, hlf, hq, hv, hk, hlf, m, lvl, hs)


def _merge_kernel(x_ref, om_ref, of_ref, od_ref, ohf_ref, ohb_ref, hg_ref, ln_ref, wg_ref, wbm_ref, wbf_ref, wbd_ref,
                  wbh_ref, wo_ref, gon_ref, g64_ref, o_ref):
    x = x_ref[0]
    h = _rms(x, ln_ref[...]).astype(bf16)
    oh = ohf_ref[0].astype(f32) + ohb_ref[0].astype(f32)
    ms = jnp.dot((oh * oh).astype(bf16), g64_ref[...], preferred_element_type=f32)
    oh = (oh * lax.rsqrt(ms + EPS) * gon_ref[...] * _sigmoid(hg_ref[0].astype(f32))).astype(bf16)
    merged = None
    for n, (o_n, w_ref) in enumerate(((om_ref[0], wbm_ref), (of_ref[0], wbf_ref), (od_ref[0], wbd_ref), (oh, wbh_ref))):
        gate = _sigmoid(jnp.dot(h, wg_ref[:, n * D_MODEL:(n + 1) * D_MODEL], preferred_element_type=f32))
        y = gate * jnp.dot(o_n, w_ref[...], preferred_element_type=f32)
        merged = y if merged is None else merged + y
    o_ref[0] = x + jnp.dot(merged.astype(bf16), wo_ref[...], preferred_element_type=f32)


def _merge(x, om, of, od, oh, hg, p, tm):
    B, S, _ = x.shape
    row = lambda w: pl.BlockSpec((1, tm, w), lambda b, i: (b, i, 0))
    consts = (p["ln_mix"], p["wgate"], p["wbm"], p["wbf"], p["wbd"], p["wbh"], p["wout"], p["gon"], p["g64"])
    return pl.pallas_call(
        _merge_kernel,
        grid=(B, S // tm),
        in_specs=[row(D_MODEL), row(512), row(BRANCH), row(512), row(BRANCH), row(BRANCH), row(BRANCH)]
                 + [_const_spec(c.shape) for c in consts],
        out_specs=row(D_MODEL),
        out_shape=jax.ShapeDtypeStruct((B, S, D_MODEL), f32),
        compiler_params=_cparams(("parallel", "parallel")),
        name="merge",
    )(x, om, of, od, oh[0], oh[1], hg, *consts)


def _ffn_kernel(x_ref, ln_ref, wgu_ref, wd_ref, o_ref):
    x = x_ref[0]
    h = _rms(x, ln_ref[...]).astype(bf16)

    acc = x
    for c in range(D_FF // FF_CHUNK):
        g = jnp.dot(h, wgu_ref[:, c * FF_CHUNK:(c + 1) * FF_CHUNK], preferred_element_type=f32)
        u = jnp.dot(h, wgu_ref[:, D_FF + c * FF_CHUNK:D_FF + (c + 1) * FF_CHUNK], preferred_element_type=f32)
        a = (g * _sigmoid(g) * u).astype(bf16)
        acc = acc + jnp.dot(a, wd_ref[c], preferred_element_type=f32)
    o_ref[0] = acc


def _ffn(x, p, tm):
    B, S, _ = x.shape
    row = pl.BlockSpec((1, tm, D_MODEL), lambda b, i: (b, i, 0))
    consts = (p["ln_ffn"], p["wgu"], p["wd"])
    return pl.pallas_call(
        _ffn_kernel,
        grid=(B, S // tm),
        in_specs=[row] + [_const_spec(c.shape) for c in consts],
        out_specs=row,
        out_shape=jax.ShapeDtypeStruct((B, S, D_MODEL), f32),
        compiler_params=_cparams(("parallel", "parallel")),
        name="ffn",
    )(x, *consts)


def _pad_heads(w, heads, width):
    lead = w.shape[:-1]
    w = w.reshape(lead + (heads, width))
    w = jnp.pad(w, [(0, 0)] * len(lead) + [(0, 0), (0, LANE - width)])
    return w.reshape(lead + (heads * LANE,))


def _rope_tables(S):
    def cs(dim):
        inv = 1.0 / (ROPE_THETA ** (jnp.arange(0, dim, 2, dtype=f32) / dim))
        ang = jnp.arange(S, dtype=f32)[:, None] * inv[None, :]
        return jnp.cos(ang), jnp.sin(ang)

    cm, sm = cs(MLA_ROPE)
    one, zero = jnp.ones((S, MLA_NOPE), f32), jnp.zeros((S, MLA_NOPE), f32)
    z16, z32, o32 = jnp.zeros((S, 16), f32), jnp.zeros((S, 32), f32), jnp.ones((S, 32), f32)
    rope_m = jnp.stack([jnp.concatenate([one, cm, cm, o32], 1),
                        jnp.concatenate([zero, -sm, z16, z32], 1),
                        jnp.concatenate([zero, z16, sm, z32], 1)])
    cd, sd = cs(DIFF_HEAD_DIM)
    rope_d = jnp.stack([jnp.tile(jnp.concatenate([cd, cd], 1), (1, 4)),
                        jnp.tile(jnp.concatenate([-sd, z16], 1), (1, 4)),
                        jnp.tile(jnp.concatenate([z16, sd], 1), (1, 4))])
    return rope_m, rope_d


def _block_mean(width, group):
    c = np.arange(width)
    return jnp.asarray(((c[:, None] // group) == (c[None, :] // group)) / group, f32).astype(bf16)


def _layer_params(l, a, lower_bounds):
    w = a["w_in"][l]
    offs = np.cumsum([0, 384, 128, 32, 256, 256, 256, 256, 256, 256, 256, 256, 256, 4096])
    col = lambda i: w[:, offs[i]:offs[i + 1]]
    wcat = jnp.concatenate([col(0), col(1), jnp.pad(col(2), ((0, 0), (MLA_NOPE, LANE - MLA_QK))), col(3), col(4),
                            col(5), _pad_heads(col(6), DIFF_HEADS, 2 * DIFF_HEAD_DIM), col(7), col(8), col(9),
                            col(10), col(11)], axis=1).astype(bf16)
    wukv = a["mla_w_ukv"][l].reshape(MLA_KV_LORA, MLA_HEADS, MLA_NOPE + MLA_V)
    wukv = jnp.concatenate([_pad_heads(wukv[:, :, :MLA_NOPE].reshape(MLA_KV_LORA, -1), MLA_HEADS, MLA_NOPE),
                            _pad_heads(wukv[:, :, MLA_NOPE:].reshape(MLA_KV_LORA, -1), MLA_HEADS, MLA_V)], axis=1)
    lam_init = 0.8 - 0.6 * math.exp(-0.3 * l)
    lam = (jnp.exp(jnp.sum(a["diff_lq1"][l] * a["diff_lk1"][l])) - jnp.exp(jnp.sum(a["diff_lq2"][l] * a["diff_lk2"][l]))
           + lam_init)
    wb = a["w_branch"][l]
    nc = D_FF // FF_CHUNK
    row = lambda v: v.reshape(1, -1).astype(f32)
    return {
        "ln_mix": row(a["ln_mix"][l]),
        "wcat": wcat,
        "gqa": row(a["mla_g_qa"][l]),
        "wuq": _pad_heads(a["mla_w_uq"][l], MLA_HEADS, MLA_QK).astype(bf16),
        "gkva": row(a["mla_g_kva"][l]),
        "wukv": wukv.astype(bf16),
        "gqn": row(jnp.pad(a["mla_g_qn"][l], (0, LANE - MLA_QK))) * (MLA_QK ** -0.5 * LOG2E),
        "gkn": row(jnp.pad(a["mla_g_kn"][l], (0, LANE - MLA_QK))),
        "gdq": row(jnp.tile(a["diff_g_qn"][l], 2 * DIFF_HEADS)) * (DIFF_HEAD_DIM ** -0.5 * LOG2E),
        "gdk": row(jnp.tile(a["diff_g_kn"][l], 2 * DIFF_HEADS)),
        "g32": _block_mean(BRANCH, DIFF_HEAD_DIM),
        "lb": lower_bounds[:, l].astype(f32),
        "lam": lam.reshape(1).astype(f32),
        "mla_bound": MLA_QK ** 0.5 * LOG2E * jnp.max(jnp.abs(a["mla_g_qn"][l])) * jnp.max(jnp.abs(a["mla_g_kn"][l])),
        "diff_bound": (DIFF_HEAD_DIM ** 0.5 * LOG2E * jnp.max(jnp.abs(a["diff_g_qn"][l]))
                       * jnp.max(jnp.abs(a["diff_g_kn"][l]))),
        "gsub": row(jnp.pad(a["diff_g_sub"][l], (0, LANE - 2 * DIFF_HEAD_DIM))) * (1.0 - lam_init),
        "wgate": col(12).astype(bf16),
        "wbm": _pad_heads(wb[0].T, MLA_HEADS, MLA_V).T.astype(bf16),
        "wbf": wb[1].astype(bf16),
        "wbd": _pad_heads(wb[2].T, DIFF_HEADS, 2 * DIFF_HEAD_DIM).T.astype(bf16),
        "wbh": wb[3].astype(bf16),
        "wout": a["w_out"][l].astype(bf16),
        "gon": row(jnp.tile(a["hgrn_g_on"][l], HGRN_HEADS)),
        "g64": _block_mean(BRANCH, HGRN_DIM),
        "ln_ffn": row(a["ln_ffn"][l]),
        "wgu": a["w_gate_up"][l].astype(bf16),
        "wd": a["w_down"][l].reshape(nc, FF_CHUNK, D_MODEL).astype(bf16),
    }


def _tiles(S):
    return {"tm_in": min(256, S),"tq": min(512, S), "tk": min(512, S), "tqb": min(512, S), "tkb": min(1024, S),
            "hgrn_chunk": min(128, S),
            "tm_merge": min(512, S), "tm_ffn": min(512, S)}


def kernel(x, ln_mix, w_in, mla_g_qa, mla_w_uq, mla_g_kva, mla_w_ukv, mla_g_qn, mla_g_kn, diff_g_qn, diff_g_kn,
           diff_lq1, diff_lk1, diff_lq2, diff_lk2, diff_g_sub, hgrn_lb_logits, hgrn_g_on, w_branch, w_out, ln_ffn,
           w_gate_up, w_down):
    a = dict(ln_mix=ln_mix, w_in=w_in, mla_g_qa=mla_g_qa, mla_w_uq=mla_w_uq, mla_g_kva=mla_g_kva,
             mla_w_ukv=mla_w_ukv, mla_g_qn=mla_g_qn, mla_g_kn=mla_g_kn, diff_g_qn=diff_g_qn, diff_g_kn=diff_g_kn,
             diff_lq1=diff_lq1, diff_lk1=diff_lk1, diff_lq2=diff_lq2, diff_lk2=diff_lk2, diff_g_sub=diff_g_sub,
             hgrn_g_on=hgrn_g_on, w_branch=w_branch, w_out=w_out, ln_ffn=ln_ffn, w_gate_up=w_gate_up, w_down=w_down)
    S = x.shape[1]
    t = _tiles(S)
    rope_m, rope_d = _rope_tables(S)
    lb_p = jax.nn.softmax(hgrn_lb_logits.astype(f32), axis=1)
    lower_bounds = jnp.cumsum(lb_p, axis=1) - lb_p[:, :1]
    fnet_consts = _fnet_consts(S)
    hgrn_consts = _hgrn_consts(t["hgrn_chunk"])
    for l in range(DEPTH):
        p = _layer_params(l, a, lower_bounds)
        qm, km, vm, fn, qd, kd, vd, hq, hv, hk, hlf, hg = _in_proj(x, p, rope_m, rope_d, t["tm_in"])
        om = lax.cond(p["mla_bound"] <= SCORE_BOUND_LOG2,
                      lambda: _mla_attn_bounded(qm, km, vm, t["tqb"], t["tkb"]),
                      lambda: _mla_attn(qm, km, vm, t["tq"], t["tk"]))
        od = lax.cond(p["diff_bound"] <= SCORE_BOUND_LOG2,
                      lambda: _diff_attn_bounded(p["lam"], qd, kd, vd, p["gsub"], t["tqb"], t["tkb"]),
                      lambda: _diff_attn(p["lam"], qd, kd, vd, p["gsub"], t["tq"], t["tk"]))
        of = _fnet(fn, fnet_consts)
        oh = _hgrn(hq, hv, hk, hlf, hgrn_consts, t["hgrn_chunk"])
        x = _merge(x, om, of, od, oh, hg, p, t["tm_merge"])
        x = _ffn(x, p, t["tm_ffn"])
    return x
```

```python
import functools
import math

import numpy as np
import jax
import jax.numpy as jnp
from jax import lax
from jax.experimental import pallas as pl
from jax.experimental.pallas import tpu as pltpu

f32 = jnp.float32
bf16 = jnp.bfloat16

D_MODEL = 1024
DEPTH = 2
MLA_HEADS = 4
MLA_Q_LORA = 384
MLA_KV_LORA = 128
MLA_NOPE = 64
MLA_ROPE = 32
MLA_V = 64
MLA_QK = MLA_NOPE + MLA_ROPE
DIFF_HEADS = 4
DIFF_HEAD_DIM = 32
HGRN_HEADS = 4
HGRN_DIM = 64
BRANCH = 256
D_FF = 2816
FF_CHUNK = 256
ROPE_THETA = 10000.0
EPS = 1e-6
LOG2E = 1.4426950408889634
LANE = 128
FNET_N2 = 64
FNET_K1_BLOCK = 8
HGRN_CHUNKS_PER_STEP = 8
VMEM_LIMIT = 56 * 1024 * 1024
SCORE_BOUND_LOG2 = 60.0

_SEG = {}
_off = 0
for _name, _w in (("cq", 384), ("ckv", 128), ("krope", 128), ("fnet", 256), ("dq", 256), ("dk", 256),
                  ("dv", 512), ("hq", 256), ("hi", 256), ("hff", 256), ("hfb", 256), ("hg", 256)):
    _SEG[_name] = (_off, _off + _w)
    _off += _w
W_CAT = _off


def _cparams(sem):
    return pltpu.CompilerParams(dimension_semantics=sem, vmem_limit_bytes=VMEM_LIMIT)


def _const_spec(shape):
    nd = len(shape)
    return pl.BlockSpec(shape, lambda *_: (0,) * nd, pipeline_mode=pl.Buffered(1))


def _rms(x, g):
    return x * lax.rsqrt(jnp.mean(x * x, axis=-1, keepdims=True) + EPS) * g


def _sigmoid(x):
    return 1.0 / (1.0 + jnp.exp(-x))


def _rope(t, tab_ref):
    return t * tab_ref[0] + pltpu.roll(t, LANE - 16, 1) * tab_ref[1] + pltpu.roll(t, 16, 1) * tab_ref[2]


def _in_proj_kernel(x_ref, ln_ref, wcat_ref, gqa_ref, wuq_ref, gkva_ref, wukv_ref, gqn_ref, gkn_ref,
                    ropem_ref, gdq_ref, gdk_ref, g32_ref, roped_ref, lb_ref,
                    qm_ref, km_ref, vm_ref, fn_ref, qd_ref, kd_ref, vd_ref, hq_ref, hv_ref, hk_ref,
                    hlf_ref, hg_ref):
    h = _rms(x_ref[0], ln_ref[...]).astype(bf16)

    def seg(name):
        a, b = _SEG[name]
        return jnp.dot(h, wcat_ref[:, a:b], preferred_element_type=f32)

    ones_lane = (lax.broadcasted_iota(jnp.int32, (1, LANE), 1) == MLA_V).astype(f32)

    cqn = _rms(seg("cq"), gqa_ref[...]).astype(bf16)
    q = jnp.dot(cqn, wuq_ref[...], preferred_element_type=f32)
    for hh in range(MLA_HEADS):
        sl = slice(hh * LANE, (hh + 1) * LANE)
        qh = q[:, sl]
        ms = jnp.sum(qh * qh, axis=-1, keepdims=True) * (1.0 / MLA_QK)
        qm_ref[0, :, sl] = _rope(qh * lax.rsqrt(ms + EPS) * gqn_ref[...], ropem_ref).astype(bf16)

    ckvn = _rms(seg("ckv"), gkva_ref[...]).astype(bf16)
    kv = jnp.dot(ckvn, wukv_ref[...], preferred_element_type=f32)
    krope = seg("krope")
    for hh in range(MLA_HEADS):
        sl = slice(hh * LANE, (hh + 1) * LANE)
        kh = kv[:, sl] + krope
        ms = jnp.sum(kh * kh, axis=-1, keepdims=True) * (1.0 / MLA_QK)
        km_ref[0, :, sl] = _rope(kh * lax.rsqrt(ms + EPS) * gkn_ref[...], ropem_ref).astype(bf16)
        vsl = slice((MLA_HEADS + hh) * LANE, (MLA_HEADS + hh + 1) * LANE)
        vm_ref[0, :, sl] = (kv[:, vsl] + ones_lane).astype(bf16)

    fn_ref[0] = seg("fnet").astype(bf16)

    for name, g_ref, o_ref in (("dq", gdq_ref, qd_ref), ("dk", gdk_ref, kd_ref)):
        t = seg(name)
        ms = jnp.dot((t * t).astype(bf16), g32_ref[...], preferred_element_type=f32)
        t = t * lax.rsqrt(ms + EPS) * g_ref[...]
        for c in range(BRANCH // LANE):
            sl = slice(c * LANE, (c + 1) * LANE)
            o_ref[0, :, sl] = _rope(t[:, sl], roped_ref).astype(bf16)
    dv = seg("dv")
    for hh in range(DIFF_HEADS):
        sl = slice(hh * LANE, (hh + 1) * LANE)
        vd_ref[0, :, sl] = (dv[:, sl] + ones_lane).astype(bf16)

    hq_ref[0] = seg("hq").astype(bf16)
    hv_ref[0] = seg("hi").astype(bf16)
    hg_ref[0] = seg("hg").astype(bf16)
    for d, name in enumerate(("hff", "hfb")):
        lb = lb_ref[d:d + 1, :]
        f = lb + (1.0 - lb) * _sigmoid(seg(name))
        hk_ref[0, :, d * BRANCH:(d + 1) * BRANCH] = (1.0 - f).astype(bf16)
        lf = jnp.log(f)
        hi = lf.astype(bf16)
        hlf_ref[0, :, 2 * d * BRANCH:(2 * d + 1) * BRANCH] = hi
        hlf_ref[0, :, (2 * d + 1) * BRANCH:(2 * d + 2) * BRANCH] = (lf - hi.astype(f32)).astype(bf16)


def _in_proj(x, p, rope_m, rope_d, tm):
    B, S, _ = x.shape
    row = lambda w: pl.BlockSpec((1, tm, w), lambda b, i: (b, i, 0))
    tab = pl.BlockSpec((3, tm, LANE), lambda b, i: (0, i, 0))
    out_w = (512, 512, 512, 256, 256, 256, 512, 256, 256, 512, 1024, 256)
    consts = (p["ln_mix"], p["wcat"], p["gqa"], p["wuq"], p["gkva"], p["wukv"], p["gqn"], p["gkn"])
    consts2 = (p["gdq"], p["gdk"], p["g32"])
    in_specs = ([row(D_MODEL)] + [_const_spec(c.shape) for c in consts] + [tab]
                + [_const_spec(c.shape) for c in consts2] + [tab, _const_spec(p["lb"].shape)])
    return pl.pallas_call(
        _in_proj_kernel,
        grid=(B, S // tm),
        in_specs=in_specs,
        out_specs=[row(w) for w in out_w],
        out_shape=[jax.ShapeDtypeStruct((B, S, w), bf16) for w in out_w],
        compiler_params=_cparams(("parallel", "parallel")),
        name="in_proj",
    )(x, *consts, rope_m, *consts2, rope_d, p["lb"])


def _online_softmax_step(s, v, m_ref, acc_ref, idx):
    m_prev = m_ref[idx]
    m_new = jnp.maximum(m_prev, jnp.max(s, axis=1, keepdims=True))
    alpha = jnp.exp2(m_prev - m_new)
    p = jnp.exp2((s - m_new[:, :1]).astype(bf16))
    acc_ref[idx] = alpha * acc_ref[idx] + jnp.dot(p, v, preferred_element_type=f32)
    m_ref[idx] = m_new


_NT = (((1,), (1,)), ((), ()))
_TN = (((0,), (0,)), ((), ()))


def _mla_attn_kernel(q_ref, k_ref, v_ref, o_ref, m_ref, acc_ref):
    j = pl.program_id(2)

    @pl.when(j == 0)
    def _():
        m_ref[...] = jnp.full(m_ref.shape, -jnp.inf, f32)
        acc_ref[...] = jnp.zeros(acc_ref.shape, f32)

    for hh in range(MLA_HEADS):
        sl = slice(hh * LANE, (hh + 1) * LANE)
        s = lax.dot_general(q_ref[0, :, sl], k_ref[0, :, sl], _NT, preferred_element_type=f32)
        _online_softmax_step(s, v_ref[0, :, sl], m_ref, acc_ref, hh)

    @pl.when(j == pl.num_programs(2) - 1)
    def _():
        for hh in range(MLA_HEADS):
            acc = acc_ref[hh]
            o_ref[0, :, hh * LANE:(hh + 1) * LANE] = (acc / acc[:, MLA_V:MLA_V + 1]).astype(bf16)


def _diff_finalize(lam_ref, gsub_ref, acc_ref, o_ref):
    lam = lam_ref[0]
    vmask = lax.broadcasted_iota(jnp.int32, (1, LANE), 1) < 2 * DIFF_HEAD_DIM
    for hh in range(DIFF_HEADS):
        a1 = acc_ref[2 * hh]
        a2 = acc_ref[2 * hh + 1]
        o = a1 / a1[:, 64:65] - lam * (a2 / a2[:, 64:65])
        o = jnp.where(vmask, o, 0.0)
        ms = jnp.sum(o * o, axis=-1, keepdims=True) * (1.0 / (2 * DIFF_HEAD_DIM))
        o_ref[0, :, hh * LANE:(hh + 1) * LANE] = (o * lax.rsqrt(ms + EPS) * gsub_ref[...]).astype(bf16)


def _diff_attn_kernel(lam_ref, q_ref, k_ref, v_ref, gsub_ref, o_ref, m_ref, acc_ref):
    j = pl.program_id(2)

    @pl.when(j == 0)
    def _():
        m_ref[...] = jnp.full(m_ref.shape, -jnp.inf, f32)
        acc_ref[...] = jnp.zeros(acc_ref.shape, f32)

    q = q_ref[0]
    k = k_ref[0]
    group = lax.broadcasted_iota(jnp.int32, (1, BRANCH), 1) // DIFF_HEAD_DIM
    for g in range(2 * DIFF_HEADS):
        qg = jnp.where(group == g, q, jnp.zeros_like(q))
        s = lax.dot_general(qg, k, _NT, preferred_element_type=f32)
        hh = g // 2
        _online_softmax_step(s, v_ref[0, :, hh * LANE:(hh + 1) * LANE], m_ref, acc_ref, g)

    @pl.when(j == pl.num_programs(2) - 1)
    def _():
        _diff_finalize(lam_ref, gsub_ref, acc_ref, o_ref)


def _mla_attn_bounded_kernel(q_ref, k_ref, v_ref, o_ref, acc_ref, *, tk):
    acc_ref[...] = jnp.zeros(acc_ref.shape, f32)

    def kv_block(j, carry):
        rows = pl.ds(pl.multiple_of(j * tk, tk), tk)
        for hh in range(MLA_HEADS):
            sl = slice(hh * LANE, (hh + 1) * LANE)
            s = lax.dot_general(q_ref[0, :, sl], k_ref[0, rows, sl], _NT, preferred_element_type=f32)
            acc_ref[hh] += jnp.dot(jnp.exp2(s).astype(bf16), v_ref[0, rows, sl], preferred_element_type=f32)
        return carry

    lax.fori_loop(0, k_ref.shape[1] // tk, kv_block, 0)
    for hh in range(MLA_HEADS):
        acc = acc_ref[hh]
        o_ref[0, :, hh * LANE:(hh + 1) * LANE] = (acc / acc[:, MLA_V:MLA_V + 1]).astype(bf16)


def _diff_attn_bounded_kernel(lam_ref, q_ref, k_ref, v_ref, gsub_ref, o_ref, acc_ref, *, tk):
    acc_ref[...] = jnp.zeros(acc_ref.shape, f32)
    q = q_ref[0]
    group = lax.broadcasted_iota(jnp.int32, (1, BRANCH), 1) // DIFF_HEAD_DIM
    qs = [jnp.where(group == g, q, jnp.zeros_like(q)) for g in range(2 * DIFF_HEADS)]

    def kv_block(j, carry):
        rows = pl.ds(pl.multiple_of(j * tk, tk), tk)
        k = k_ref[0, rows, :]
        for g in range(2 * DIFF_HEADS):
            s = lax.dot_general(qs[g], k, _NT, preferred_element_type=f32)
            hh = g // 2
            acc_ref[g] += jnp.dot(jnp.exp2(s).astype(bf16), v_ref[0, rows, hh * LANE:(hh + 1) * LANE],
                                  preferred_element_type=f32)
        return carry

    lax.fori_loop(0, k_ref.shape[1] // tk, kv_block, 0)
    _diff_finalize(lam_ref, gsub_ref, acc_ref, o_ref)


def _attn_bounded_call(kernel_fn, n_acc, name, args, in_specs, S, W, B, tq):
    return pl.pallas_call(
        kernel_fn,
        grid=(B, S // tq),
        in_specs=in_specs,
        out_specs=pl.BlockSpec((1, tq, W), lambda b, i: (b, i, 0)),
        out_shape=jax.ShapeDtypeStruct((B, S, W), bf16),
        scratch_shapes=[pltpu.VMEM((n_acc, tq, LANE), f32)],
        compiler_params=_cparams(("parallel", "parallel")),
        name=name,
    )(*args)


def _mla_attn_bounded(q, k, v, tq, tk):
    B, S, W = q.shape
    full = pl.BlockSpec((1, S, W), lambda b, i: (b, 0, 0))
    in_specs = [pl.BlockSpec((1, tq, W), lambda b, i: (b, i, 0)), full, full]
    return _attn_bounded_call(functools.partial(_mla_attn_bounded_kernel, tk=tk), MLA_HEADS, "mla_attn_bounded",
                              (q, k, v), in_specs, S, W, B, tq)


def _diff_attn_bounded(lam, q, k, v, gsub, tq, tk):
    B, S, W = v.shape
    in_specs = [pl.BlockSpec(memory_space=pltpu.SMEM),
                pl.BlockSpec((1, tq, BRANCH), lambda b, i: (b, i, 0)),
                pl.BlockSpec((1, S, BRANCH), lambda b, i: (b, 0, 0)),
                pl.BlockSpec((1, S, W), lambda b, i: (b, 0, 0)),
                pl.BlockSpec((1, LANE), lambda b, i: (0, 0))]
    return _attn_bounded_call(functools.partial(_diff_attn_bounded_kernel, tk=tk), 2 * DIFF_HEADS,
                              "diff_attn_bounded", (lam, q, k, v, gsub), in_specs, S, W, B, tq)


def _mla_attn(q, k, v, tq, tk):
    B, S, W = q.shape
    return pl.pallas_call(
        _mla_attn_kernel,
        grid=(B, S // tq, S // tk),
        in_specs=[pl.BlockSpec((1, tq, W), lambda b, i, j: (b, i, 0)),
                  pl.BlockSpec((1, tk, W), lambda b, i, j: (b, j, 0)),
                  pl.BlockSpec((1, tk, W), lambda b, i, j: (b, j, 0))],
        out_specs=pl.BlockSpec((1, tq, W), lambda b, i, j: (b, i, 0)),
        out_shape=jax.ShapeDtypeStruct((B, S, W), bf16),
        scratch_shapes=[pltpu.VMEM((MLA_HEADS, tq, LANE), f32), pltpu.VMEM((MLA_HEADS, tq, LANE), f32)],
        compiler_params=_cparams(("parallel", "parallel", "arbitrary")),
        name="mla_attn",
    )(q, k, v)


def _diff_attn(lam, q, k, v, gsub, tq, tk):
    B, S, W = v.shape
    return pl.pallas_call(
        _diff_attn_kernel,
        grid=(B, S // tq, S // tk),
        in_specs=[pl.BlockSpec(memory_space=pltpu.SMEM),
                  pl.BlockSpec((1, tq, BRANCH), lambda b, i, j: (b, i, 0)),
                  pl.BlockSpec((1, tk, BRANCH), lambda b, i, j: (b, j, 0)),
                  pl.BlockSpec((1, tk, W), lambda b, i, j: (b, j, 0)),
                  pl.BlockSpec((1, LANE), lambda b, i, j: (0, 0))],
        out_specs=pl.BlockSpec((1, tq, W), lambda b, i, j: (b, i, 0)),
        out_shape=jax.ShapeDtypeStruct((B, S, W), bf16),
        scratch_shapes=[pltpu.VMEM((2 * DIFF_HEADS, tq, LANE), f32), pltpu.VMEM((2 * DIFF_HEADS, tq, LANE), f32)],
        compiler_params=_cparams(("parallel", "parallel", "arbitrary")),
        name="diff_attn",
    )(lam, q, k, v, gsub)


def _fnet1_kernel(x_ref, w_ref, a_ref):
    a_ref[0] = jnp.dot(w_ref[...], x_ref[0], preferred_element_type=f32).astype(bf16)


def _fnet2_kernel(a_ref, t_ref, cs_ref, o_ref):
    for jj in range(FNET_K1_BLOCK):
        a = jnp.concatenate([a_ref[0, 0, jj], a_ref[0, 1, jj]], axis=0)
        z = jnp.dot(t_ref[jj], a, preferred_element_type=f32)
        zc = jnp.concatenate([z[:FNET_N2], z[FNET_N2:]], axis=1).astype(bf16)
        o_ref[0, :, jj, :] = jnp.dot(zc, cs_ref[...], preferred_element_type=f32).astype(bf16)


def _fnet_consts(S):
    n1 = S // FNET_N2
    a = np.arange(n1)
    ang1 = 2.0 * np.pi * np.outer(a, a) / n1
    w1 = np.concatenate([np.cos(ang1), -np.sin(ang1)], axis=0)
    n2 = np.arange(FNET_N2)
    phi = 2.0 * np.pi * (n2[None, None, :] * a[:, None, None] / S + n2[None, None, :] * n2[None, :, None] / FNET_N2)
    tr, ti = np.cos(phi), -np.sin(phi)
    t = np.concatenate([np.concatenate([tr, -ti], axis=2), np.concatenate([ti, tr], axis=2)], axis=1)
    c = np.arange(BRANCH)
    same = (c[:, None] // FNET_N2) == (c[None, :] // FNET_N2)
    angc = 2.0 * np.pi * np.outer(c % FNET_N2, c % FNET_N2) / FNET_N2
    norm = 1.0 / math.sqrt(S * FNET_N2)
    cs = np.concatenate([np.where(same, np.cos(angc), 0.0), np.where(same, np.sin(angc), 0.0)], axis=0) * norm
    return (jnp.asarray(w1, f32).astype(bf16), jnp.asarray(t, f32).astype(bf16), jnp.asarray(cs, f32).astype(bf16))


def _fnet(u, consts):
    B, S, W = u.shape
    w1, t, cs = consts
    n1 = S // FNET_N2
    cols = FNET_N2 * W
    tn = min(cols, 4096)
    a = pl.pallas_call(
        _fnet1_kernel,
        grid=(B, cols // tn),
        in_specs=[pl.BlockSpec((1, n1, tn), lambda b, i: (b, 0, i)), _const_spec(w1.shape)],
        out_specs=pl.BlockSpec((1, 2 * n1, tn), lambda b, i: (b, 0, i)),
        out_shape=jax.ShapeDtypeStruct((B, 2 * n1, cols), bf16),
        compiler_params=_cparams(("parallel", "parallel")),
        name="fnet1",
    )(u.reshape(B, n1, cols), w1)
    kb = FNET_K1_BLOCK
    y = pl.pallas_call(
        _fnet2_kernel,
        grid=(B, n1 // kb),
        in_specs=[pl.BlockSpec((1, 2, kb, FNET_N2, W), lambda b, i: (b, 0, i, 0, 0)),
                  pl.BlockSpec((kb, 2 * FNET_N2, 2 * FNET_N2), lambda b, i: (i, 0, 0)),
                  _const_spec(cs.shape)],
        out_specs=pl.BlockSpec((1, FNET_N2, kb, W), lambda b, i: (b, 0, i, 0)),
        out_shape=jax.ShapeDtypeStruct((B, FNET_N2, n1, W), bf16),
        compiler_params=_cparams(("parallel", "parallel")),
        name="fnet2",
    )(a.reshape(B, 2, n1, FNET_N2, W), t, cs)
    return y.reshape(B, S, W)


def _hgrn_consts(C):
    L = int(math.log2(C))
    t = np.arange(C)
    m = np.zeros((2, (L + 2) * C, C), np.float32)
    lvl = np.full((2, C, C), -1, np.int32)
    for d in (0, 1):
        for li in range(L):
            n = C >> li
            half = n // 2
            blk = t // n
            mid = blk * n + half
            upper = (t % n) >= half
            for r in range(C):
                if d == 0:
                    if upper[r]:
                        m[d, li * C + r, mid[r]:r + 1] = 1.0
                    else:
                        m[d, li * C + r, r + 1:mid[r]] = 1.0
                else:
                    if upper[r]:
                        m[d, li * C + r, mid[r]:r] = 1.0
                    else:
                        m[d, li * C + r, r:mid[r]] = 1.0
            same = blk[:, None] == blk[None, :]
            if d == 0:
                msk = same & upper[:, None] & (~upper)[None, :]
            else:
                msk = same & (~upper)[:, None] & upper[None, :]
            lvl[d][msk] = li
        for r in range(C):
            if d == 0:
                m[d, L * C + r, :r + 1] = 1.0
                m[d, (L + 1) * C + r, r + 1:] = 1.0
            else:
                m[d, L * C + r, r:] = 1.0
                m[d, (L + 1) * C + r, :r] = 1.0
    c = np.arange(BRANCH)
    headsum = ((c[:, None] // HGRN_DIM) == (c[None, :] // HGRN_DIM)).astype(np.float32)
    lvl = np.tile(lvl, (1, 1, HGRN_HEADS))
    m = np.tile(m, (1, 1, 2))
    return jnp.asarray(m, f32).astype(bf16), jnp.asarray(lvl), jnp.asarray(headsum, f32).astype(bf16)


def _hgrn_kernel(qf_ref, vf_ref, kf_ref, lff_ref, qb_ref, vb_ref, kb_ref, lfb_ref, m_ref, lvl_ref, hs_ref,
                 of_ref, ob_ref, st_ref, *, C, L):
    @pl.when(pl.program_id(1) == 0)
    def _():
        st_ref[...] = jnp.zeros(st_ref.shape, f32)

    refs = ((qf_ref, vf_ref, kf_ref, lff_ref, of_ref), (qb_ref, vb_ref, kb_ref, lfb_ref, ob_ref))
    head = lax.broadcasted_iota(jnp.int32, (1, BRANCH), 1) // HGRN_DIM
    r = lax.broadcasted_iota(jnp.int32, (BRANCH, BRANCH), 0) // HGRN_DIM
    cidx = lax.broadcasted_iota(jnp.int32, (BRANCH, BRANCH), 1) // HGRN_DIM
    n_chunks = qf_ref.shape[1] // C
    order = (tuple(range(n_chunks)), tuple(reversed(range(n_chunks))))
    chains = [(d, order[d][j]) for j in range(n_chunks) for d in (0, 1)]
    rows = {c: slice(c[1] * C, (c[1] + 1) * C) for c in chains}
    q = {c: refs[c[0]][0][0, rows[c], :].astype(f32) for c in chains}
    v = {c: refs[c[0]][1][0, rows[c], :] for c in chains}
    k = {c: refs[c[0]][2][0, rows[c], :].astype(f32) for c in chains}
    e = {}
    for c in chains:
        lf = refs[c[0]][3][0, rows[c], :]
        lf2 = jnp.concatenate([lf[:, :BRANCH], lf[:, BRANCH:]], axis=0)
        e[c] = jnp.dot(m_ref[c[0]], lf2, preferred_element_type=f32)

    o = {}
    st = [st_ref[0], st_ref[1]]
    for c in chains:
        d = c[0]
        ein = jnp.exp(e[c][L * C:(L + 1) * C])
        o[c] = lax.dot_general((q[c] * ein).astype(bf16), st[d].astype(bf16), _NT, preferred_element_type=f32)
        kout = (k[c] * jnp.exp(e[c][(L + 1) * C:(L + 2) * C])).astype(bf16)
        upd = lax.dot_general(v[c], kout, _TN, preferred_element_type=f32)
        total = ein[C - 1:C] if d == 0 else ein[0:1]
        st[d] = st[d] * total + jnp.where(r == cidx, upd, 0.0)
        o[c] = o[c] + jnp.dot((q[c] * k[c]).astype(bf16), hs_ref[...], preferred_element_type=f32) * v[c].astype(f32)
    st_ref[0] = st[0]
    st_ref[1] = st[1]

    scores = {c: jnp.zeros((C, HGRN_HEADS * C), f32) for c in chains}
    for li in range(L):
        for c in chains:
            ex = jnp.exp(e[c][li * C:(li + 1) * C])
            qt = (q[c] * ex).astype(bf16)
            kt = (k[c] * ex).astype(bf16)
            kstack = jnp.concatenate([jnp.where(head == hh, kt, jnp.zeros_like(kt)) for hh in range(HGRN_HEADS)],
                                     axis=0)
            s = lax.dot_general(qt, kstack, _NT, preferred_element_type=f32)
            scores[c] = jnp.where(lvl_ref[c[0]] == li, s, scores[c])
    for c in chains:
        vstack = jnp.concatenate([jnp.where(head == hh, v[c], jnp.zeros_like(v[c])) for hh in range(HGRN_HEADS)],
                                 axis=0)
        oc = o[c] + jnp.dot(scores[c].astype(bf16), vstack, preferred_element_type=f32)
        refs[c[0]][4][0, rows[c], :] = oc.astype(bf16)


def _hgrn(hq, hv, hk, hlf, consts, C):
    B, S, W = hq.shape
    m, lvl, hs = consts
    L = int(math.log2(C))
    rows = min(S, HGRN_CHUNKS_PER_STEP * C)
    nb = S // rows
    fwd = lambda w, c: pl.BlockSpec((1, rows, w), lambda b, i: (b, i, c))
    bwd = lambda w, c: pl.BlockSpec((1, rows, w), lambda b, i: (b, nb - 1 - i, c))
    return pl.pallas_call(
        functools.partial(_hgrn_kernel, C=C, L=L),
        grid=(B, nb),
        in_specs=[fwd(W, 0), fwd(W, 0), fwd(W, 0), fwd(2 * W, 0), bwd(W, 0), bwd(W, 0), bwd(W, 1), bwd(2 * W, 1),
                  _const_spec(m.shape), _const_spec(lvl.shape), _const_spec(hs.shape)],
        out_specs=[fwd(W, 0), bwd(W, 0)],
        out_shape=[jax.ShapeDtypeStruct((B, S, W), bf16)] * 2,
        scratch_shapes=[pltpu.VMEM((2, BRANCH, BRANCH), f32)],
        compiler_params=_cparams(("parallel", "arbitrary")),
        name="hgrn",
    )(hq, hv, hk, hlf, hq, hv, hk, hlf, m, lvl, hs)


def _merge_ffn_kernel(x_ref, om_ref, of_ref, od_ref, ohf_ref, ohb_ref, hg_ref, ln_ref, wg_ref, wbm_ref, wbf_ref,
                      wbd_ref, wbh_ref, wo_ref, gon_ref, g64_ref, lnf_ref, wgu_ref, wd_ref, o_ref):
    x = x_ref[0]
    h = _rms(x, ln_ref[...]).astype(bf16)
    oh = ohf_ref[0].astype(f32) + ohb_ref[0].astype(f32)
    ms = jnp.dot((oh * oh).astype(bf16), g64_ref[...], preferred_element_type=f32)
    oh = (oh * lax.rsqrt(ms + EPS) * gon_ref[...] * _sigmoid(hg_ref[0].astype(f32))).astype(bf16)
    merged = None
    for n, (o_n, w_ref) in enumerate(((om_ref[0], wbm_ref), (of_ref[0], wbf_ref), (od_ref[0], wbd_ref), (oh, wbh_ref))):
        gate = _sigmoid(jnp.dot(h, wg_ref[:, n * D_MODEL:(n + 1) * D_MODEL], preferred_element_type=f32))
        y = gate * jnp.dot(o_n, w_ref[...], preferred_element_type=f32)
        merged = y if merged is None else merged + y
    x = x + jnp.dot(merged.astype(bf16), wo_ref[...], preferred_element_type=f32)

    h = _rms(x, lnf_ref[...]).astype(bf16)
    acc = x
    for c in range(D_FF // FF_CHUNK):
        g = jnp.dot(h, wgu_ref[:, c * FF_CHUNK:(c + 1) * FF_CHUNK], preferred_element_type=f32)
        u = jnp.dot(h, wgu_ref[:, D_FF + c * FF_CHUNK:D_FF + (c + 1) * FF_CHUNK], preferred_element_type=f32)
        a = (g * _sigmoid(g) * u).astype(bf16)
        acc = acc + jnp.dot(a, wd_ref[c], preferred_element_type=f32)
    o_ref[0] = acc


def _merge_ffn(x, om, of, od, oh, hg, p, tm):
    B, S, _ = x.shape
    row = lambda w: pl.BlockSpec((1, tm, w), lambda b, i: (b, i, 0))
    consts = (p["ln_mix"], p["wgate"], p["wbm"], p["wbf"], p["wbd"], p["wbh"], p["wout"], p["gon"], p["g64"],
              p["ln_ffn"], p["wgu"], p["wd"])
    return pl.pallas_call(
        _merge_ffn_kernel,
        grid=(B, S // tm),
        in_specs=[row(D_MODEL), row(512), row(BRANCH), row(512), row(BRANCH), row(BRANCH), row(BRANCH)]
                 + [_const_spec(c.shape) for c in consts],
        out_specs=row(D_MODEL),
        out_shape=jax.ShapeDtypeStruct((B, S, D_MODEL), f32),
        compiler_params=_cparams(("parallel", "parallel")),
        name="merge_ffn",
    )(x, om, of, od, oh[0], oh[1], hg, *consts)


def _pad_heads(w, heads, width):
    lead = w.shape[:-1]
    w = w.reshape(lead + (heads, width))
    w = jnp.pad(w, [(0, 0)] * len(lead) + [(0, 0), (0, LANE - width)])
    return w.reshape(lead + (heads * LANE,))


def _rope_tables(S):
    def cs(dim):
        inv = 1.0 / (ROPE_THETA ** (jnp.arange(0, dim, 2, dtype=f32) / dim))
        ang = jnp.arange(S, dtype=f32)[:, None] * inv[None, :]
        return jnp.cos(ang), jnp.sin(ang)

    cm, sm = cs(MLA_ROPE)
    one, zero = jnp.ones((S, MLA_NOPE), f32), jnp.zeros((S, MLA_NOPE), f32)
    z16, z32, o32 = jnp.zeros((S, 16), f32), jnp.zeros((S, 32), f32), jnp.ones((S, 32), f32)
    rope_m = jnp.stack([jnp.concatenate([one, cm, cm, o32], 1),
                        jnp.concatenate([zero, -sm, z16, z32], 1),
                        jnp.concatenate([zero, z16, sm, z32], 1)])
    cd, sd = cs(DIFF_HEAD_DIM)
    rope_d = jnp.stack([jnp.tile(jnp.concatenate([cd, cd], 1), (1, 4)),
                        jnp.tile(jnp.concatenate([-sd, z16], 1), (1, 4)),
                        jnp.tile(jnp.concatenate([z16, sd], 1), (1, 4))])
    return rope_m, rope_d


def _block_mean(width, group):
    c = np.arange(width)
    return jnp.asarray(((c[:, None] // group) == (c[None, :] // group)) / group, f32).astype(bf16)


def _layer_params(l, a, lower_bounds):
    w = a["w_in"][l]
    offs = np.cumsum([0, 384, 128, 32, 256, 256, 256, 256, 256, 256, 256, 256, 256, 4096])
    col = lambda i: w[:, offs[i]:offs[i + 1]]
    wcat = jnp.concatenate([col(0), col(1), jnp.pad(col(2), ((0, 0), (MLA_NOPE, LANE - MLA_QK))), col(3), col(4),
                            col(5), _pad_heads(col(6), DIFF_HEADS, 2 * DIFF_HEAD_DIM), col(7), col(8), col(9),
                            col(10), col(11)], axis=1).astype(bf16)
    wukv = a["mla_w_ukv"][l].reshape(MLA_KV_LORA, MLA_HEADS, MLA_NOPE + MLA_V)
    wukv = jnp.concatenate([_pad_heads(wukv[:, :, :MLA_NOPE].reshape(MLA_KV_LORA, -1), MLA_HEADS, MLA_NOPE),
                            _pad_heads(wukv[:, :, MLA_NOPE:].reshape(MLA_KV_LORA, -1), MLA_HEADS, MLA_V)], axis=1)
    lam_init = 0.8 - 0.6 * math.exp(-0.3 * l)
    lam = (jnp.exp(jnp.sum(a["diff_lq1"][l] * a["diff_lk1"][l])) - jnp.exp(jnp.sum(a["diff_lq2"][l] * a["diff_lk2"][l]))
           + lam_init)
    wb = a["w_branch"][l]
    nc = D_FF // FF_CHUNK
    row = lambda v: v.reshape(1, -1).astype(f32)
    return {
        "ln_mix": row(a["ln_mix"][l]),
        "wcat": wcat,
        "gqa": row(a["mla_g_qa"][l]),
        "wuq": _pad_heads(a["mla_w_uq"][l], MLA_HEADS, MLA_QK).astype(bf16),
        "gkva": row(a["mla_g_kva"][l]),
        "wukv": wukv.astype(bf16),
        "gqn": row(jnp.pad(a["mla_g_qn"][l], (0, LANE - MLA_QK))) * (MLA_QK ** -0.5 * LOG2E),
        "gkn": row(jnp.pad(a["mla_g_kn"][l], (0, LANE - MLA_QK))),
        "gdq": row(jnp.tile(a["diff_g_qn"][l], 2 * DIFF_HEADS)) * (DIFF_HEAD_DIM ** -0.5 * LOG2E),
        "gdk": row(jnp.tile(a["diff_g_kn"][l], 2 * DIFF_HEADS)),
        "g32": _block_mean(BRANCH, DIFF_HEAD_DIM),
        "lb": lower_bounds[:, l].astype(f32),
        "lam": lam.reshape(1).astype(f32),
        "mla_bound": MLA_QK ** 0.5 * LOG2E * jnp.max(jnp.abs(a["mla_g_qn"][l])) * jnp.max(jnp.abs(a["mla_g_kn"][l])),
        "diff_bound": (DIFF_HEAD_DIM ** 0.5 * LOG2E * jnp.max(jnp.abs(a["diff_g_qn"][l]))
                       * jnp.max(jnp.abs(a["diff_g_kn"][l]))),
        "gsub": row(jnp.pad(a["diff_g_sub"][l], (0, LANE - 2 * DIFF_HEAD_DIM))) * (1.0 - lam_init),
        "wgate": col(12).astype(bf16),
        "wbm": _pad_heads(wb[0].T, MLA_HEADS, MLA_V).T.astype(bf16),
        "wbf": wb[1].astype(bf16),
        "wbd": _pad_heads(wb[2].T, DIFF_HEADS, 2 * DIFF_HEAD_DIM).T.astype(bf16),
        "wbh": wb[3].astype(bf16),
        "wout": a["w_out"][l].astype(bf16),
        "gon": row(jnp.tile(a["hgrn_g_on"][l], HGRN_HEADS)),
        "g64": _block_mean(BRANCH, HGRN_DIM),
        "ln_ffn": row(a["ln_ffn"][l]),
        "wgu": a["w_gate_up"][l].astype(bf16),
        "wd": a["w_down"][l].reshape(nc, FF_CHUNK, D_MODEL).astype(bf16),
    }


def _tiles(S):
    return {"tm_in": min(256, S),"tq": min(512, S), "tk": min(512, S), "tqb": min(512, S), "tkb": min(1024, S),
            "hgrn_chunk": min(64, S),
            "tm_merge": min(512, S)}


def kernel(x, ln_mix, w_in, mla_g_qa, mla_w_uq, mla_g_kva, mla_w_ukv, mla_g_qn, mla_g_kn, diff_g_qn, diff_g_kn,
           diff_lq1, diff_lk1, diff_lq2, diff_lk2, diff_g_sub, hgrn_lb_logits, hgrn_g_on, w_branch, w_out, ln_ffn,
           w_gate_up, w_down):
    a = dict(ln_mix=ln_mix, w_in=w_in, mla_g_qa=mla_g_qa, mla_w_uq=mla_w_uq, mla_g_kva=mla_g_kva,
             mla_w_ukv=mla_w_ukv, mla_g_qn=mla_g_qn, mla_g_kn=mla_g_kn, diff_g_qn=diff_g_qn, diff_g_kn=diff_g_kn,
             diff_lq1=diff_lq1, diff_lk1=diff_lk1, diff_lq2=diff_lq2, diff_lk2=diff_lk2, diff_g_sub=diff_g_sub,
             hgrn_g_on=hgrn_g_on, w_branch=w_branch, w_out=w_out, ln_ffn=ln_ffn, w_gate_up=w_gate_up, w_down=w_down)
    S = x.shape[1]
    t = _tiles(S)
    rope_m, rope_d = _rope_tables(S)
    lb_p = jax.nn.softmax(hgrn_lb_logits.astype(f32), axis=1)
    lower_bounds = jnp.cumsum(lb_p, axis=1) - lb_p[:, :1]
    fnet_consts = _fnet_consts(S)
    hgrn_consts = _hgrn_consts(t["hgrn_chunk"])
    for l in range(DEPTH):
        p = _layer_params(l, a, lower_bounds)
        qm, km, vm, fn, qd, kd, vd, hq, hv, hk, hlf, hg = _in_proj(x, p, rope_m, rope_d, t["tm_in"])
        om = lax.cond(p["mla_bound"] <= SCORE_BOUND_LOG2,
                      lambda: _mla_attn_bounded(qm, km, vm, t["tqb"], t["tkb"]),
                      lambda: _mla_attn(qm, km, vm, t["tq"], t["tk"]))
        od = lax.cond(p["diff_bound"] <= SCORE_BOUND_LOG2,
                      lambda: _diff_attn_bounded(p["lam"], qd, kd, vd, p["gsub"], t["tqb"], t["tkb"]),
                      lambda: _diff_attn(p["lam"], qd, kd, vd, p["gsub"], t["tq"], t["tk"]))
        of = _fnet(fn, fnet_consts)
        oh = _hgrn(hq, hv, hk, hlf, hgrn_consts, t["hgrn_chunk"])
        x = _merge_ffn(x, om, of, od, oh, hg, p, t["tm_merge"])
    return x
```

```python
import functools
import math

import numpy as np
import jax
import jax.numpy as jnp
from jax import lax
from jax.experimental import pallas as pl
from jax.experimental.pallas import tpu as pltpu

f32 = jnp.float32
bf16 = jnp.bfloat16

D_MODEL = 1024
DEPTH = 2
MLA_HEADS = 4
MLA_Q_LORA = 384
MLA_KV_LORA = 128
MLA_NOPE = 64
MLA_ROPE = 32
MLA_V = 64
MLA_QK = MLA_NOPE + MLA_ROPE
DIFF_HEADS = 4
DIFF_HEAD_DIM = 32
HGRN_HEADS = 4
HGRN_DIM = 64
BRANCH = 256
D_FF = 2816
FF_CHUNK = 256
ROPE_THETA = 10000.0
EPS = 1e-6
LOG2E = 1.4426950408889634
LANE = 128
FNET_N2 = 64
FNET_K1_BLOCK = 8
HGRN_CHUNKS_PER_STEP = 8
VMEM_LIMIT = 56 * 1024 * 1024
SCORE_BOUND_LOG2 = 60.0

_SEG = {}
_off = 0
for _name, _w in (("cq", 384), ("ckv", 128), ("krope", 128), ("fnet", 256), ("dq", 256), ("dk", 256),
                  ("dv", 512), ("hq", 256), ("hi", 256), ("hff", 256), ("hfb", 256), ("hg", 256)):
    _SEG[_name] = (_off, _off + _w)
    _off += _w
W_CAT = _off


def _cparams(sem):
    return pltpu.CompilerParams(dimension_semantics=sem, vmem_limit_bytes=VMEM_LIMIT)


def _const_spec(shape):
    nd = len(shape)
    return pl.BlockSpec(shape, lambda *_: (0,) * nd, pipeline_mode=pl.Buffered(1))


def _rms(x, g):
    return x * lax.rsqrt(jnp.mean(x * x, axis=-1, keepdims=True) + EPS) * g


def _sigmoid(x):
    return 1.0 / (1.0 + jnp.exp(-x))


def _rope(t, tab_ref):
    return t * tab_ref[0] + pltpu.roll(t, LANE - 16, 1) * tab_ref[1] + pltpu.roll(t, 16, 1) * tab_ref[2]


def _in_proj_kernel(x_ref, ln_ref, wcat_ref, gqa_ref, wuq_ref, gkva_ref, wukv_ref, gqn_ref, gkn_ref,
                    ropem_ref, gdq_ref, gdk_ref, g32_ref, roped_ref, lb_ref,
                    qm_ref, km_ref, vm_ref, fn_ref, qd_ref, kd_ref, vd_ref, hq_ref, hv_ref, hk_ref,
                    hlf_ref, hg_ref):
    h = _rms(x_ref[0], ln_ref[...]).astype(bf16)

    def seg(name):
        a, b = _SEG[name]
        return jnp.dot(h, wcat_ref[:, a:b], preferred_element_type=f32)

    ones_lane = (lax.broadcasted_iota(jnp.int32, (1, LANE), 1) == MLA_V).astype(f32)

    cqn = _rms(seg("cq"), gqa_ref[...]).astype(bf16)
    q = jnp.dot(cqn, wuq_ref[...], preferred_element_type=f32)
    for hh in range(MLA_HEADS):
        sl = slice(hh * LANE, (hh + 1) * LANE)
        qh = q[:, sl]
        ms = jnp.sum(qh * qh, axis=-1, keepdims=True) * (1.0 / MLA_QK)
        qm_ref[0, :, sl] = _rope(qh * lax.rsqrt(ms + EPS) * gqn_ref[...], ropem_ref).astype(bf16)

    ckvn = _rms(seg("ckv"), gkva_ref[...]).astype(bf16)
    kv = jnp.dot(ckvn, wukv_ref[...], preferred_element_type=f32)
    krope = seg("krope")
    for hh in range(MLA_HEADS):
        sl = slice(hh * LANE, (hh + 1) * LANE)
        kh = kv[:, sl] + krope
        ms = jnp.sum(kh * kh, axis=-1, keepdims=True) * (1.0 / MLA_QK)
        km_ref[0, :, sl] = _rope(kh * lax.rsqrt(ms + EPS) * gkn_ref[...], ropem_ref).astype(bf16)
        vsl = slice((MLA_HEADS + hh) * LANE, (MLA_HEADS + hh + 1) * LANE)
        vm_ref[0, :, sl] = (kv[:, vsl] + ones_lane).astype(bf16)

    fn_ref[0] = seg("fnet").astype(bf16)

    for name, g_ref, o_ref in (("dq", gdq_ref, qd_ref), ("dk", gdk_ref, kd_ref)):
        t = seg(name)
        ms = jnp.dot((t * t).astype(bf16), g32_ref[...], preferred_element_type=f32)
        t = t * lax.rsqrt(ms + EPS) * g_ref[...]
        for c in range(BRANCH // LANE):
            sl = slice(c * LANE, (c + 1) * LANE)
            o_ref[0, :, sl] = _rope(t[:, sl], roped_ref).astype(bf16)
    dv = seg("dv")
    for hh in range(DIFF_HEADS):
        sl = slice(hh * LANE, (hh + 1) * LANE)
        vd_ref[0, :, sl] = (dv[:, sl] + ones_lane).astype(bf16)

    hq_ref[0] = seg("hq").astype(bf16)
    hv_ref[0] = seg("hi").astype(bf16)
    hg_ref[0] = seg("hg").astype(bf16)
    for d, name in enumerate(("hff", "hfb")):
        lb = lb_ref[d:d + 1, :]
        f = lb + (1.0 - lb) * _sigmoid(seg(name))
        hk_ref[0, :, d * BRANCH:(d + 1) * BRANCH] = (1.0 - f).astype(bf16)
        lf = jnp.log(f)
        hi = lf.astype(bf16)
        hlf_ref[0, :, 2 * d * BRANCH:(2 * d + 1) * BRANCH] = hi
        hlf_ref[0, :, (2 * d + 1) * BRANCH:(2 * d + 2) * BRANCH] = (lf - hi.astype(f32)).astype(bf16)


def _in_proj(x, p, rope_m, rope_d, tm):
    B, S, _ = x.shape
    row = lambda w: pl.BlockSpec((1, tm, w), lambda b, i: (b, i, 0))
    tab = pl.BlockSpec((3, tm, LANE), lambda b, i: (0, i, 0))
    out_w = (512, 512, 512, 256, 256, 256, 512, 256, 256, 512, 1024, 256)
    consts = (p["ln_mix"], p["wcat"], p["gqa"], p["wuq"], p["gkva"], p["wukv"], p["gqn"], p["gkn"])
    consts2 = (p["gdq"], p["gdk"], p["g32"])
    in_specs = ([row(D_MODEL)] + [_const_spec(c.shape) for c in consts] + [tab]
                + [_const_spec(c.shape) for c in consts2] + [tab, _const_spec(p["lb"].shape)])
    return pl.pallas_call(
        _in_proj_kernel,
        grid=(B, S // tm),
        in_specs=in_specs,
        out_specs=[row(w) for w in out_w],
        out_shape=[jax.ShapeDtypeStruct((B, S, w), bf16) for w in out_w],
        compiler_params=_cparams(("parallel", "parallel")),
        name="in_proj",
    )(x, *consts, rope_m, *consts2, rope_d, p["lb"])


def _online_softmax_step(s, v, m_ref, acc_ref, idx):
    m_prev = m_ref[idx]
    m_new = jnp.maximum(m_prev, jnp.max(s, axis=1, keepdims=True))
    alpha = jnp.exp2(m_prev - m_new)
    p = jnp.exp2((s - m_new[:, :1]).astype(bf16))
    acc_ref[idx] = alpha * acc_ref[idx] + jnp.dot(p, v, preferred_element_type=f32)
    m_ref[idx] = m_new


_NT = (((1,), (1,)), ((), ()))
_TN = (((0,), (0,)), ((), ()))


def _mla_attn_kernel(q_ref, k_ref, v_ref, o_ref, m_ref, acc_ref):
    j = pl.program_id(2)

    @pl.when(j == 0)
    def _():
        m_ref[...] = jnp.full(m_ref.shape, -jnp.inf, f32)
        acc_ref[...] = jnp.zeros(acc_ref.shape, f32)

    for hh in range(MLA_HEADS):
        sl = slice(hh * LANE, (hh + 1) * LANE)
        s = lax.dot_general(q_ref[0, :, sl], k_ref[0, :, sl], _NT, preferred_element_type=f32)
        _online_softmax_step(s, v_ref[0, :, sl], m_ref, acc_ref, hh)

    @pl.when(j == pl.num_programs(2) - 1)
    def _():
        _mla_finalize(acc_ref, o_ref)


def _mla_finalize(acc_ref, o_ref):
    for hh in range(MLA_HEADS):
        acc = acc_ref[hh]
        o = acc[:, :MLA_V] * (1.0 / acc[:, MLA_V:MLA_V + 1])
        o_ref[0, :, hh * MLA_V:(hh + 1) * MLA_V] = o.astype(bf16)


def _diff_finalize(lam_ref, gsub_ref, acc_ref, o_ref):
    lam = lam_ref[0]
    dv = 2 * DIFF_HEAD_DIM
    for hh in range(DIFF_HEADS):
        a1 = acc_ref[2 * hh]
        a2 = acc_ref[2 * hh + 1]
        o = a1[:, :dv] * (1.0 / a1[:, dv:dv + 1]) - a2[:, :dv] * (lam / a2[:, dv:dv + 1])
        ms = jnp.mean(o * o, axis=-1, keepdims=True)
        o_ref[0, :, hh * dv:(hh + 1) * dv] = (o * lax.rsqrt(ms + EPS) * gsub_ref[...]).astype(bf16)


def _diff_attn_kernel(lam_ref, q_ref, k_ref, v_ref, gsub_ref, o_ref, m_ref, acc_ref):
    j = pl.program_id(2)

    @pl.when(j == 0)
    def _():
        m_ref[...] = jnp.full(m_ref.shape, -jnp.inf, f32)
        acc_ref[...] = jnp.zeros(acc_ref.shape, f32)

    q = q_ref[0]
    k = k_ref[0]
    group = lax.broadcasted_iota(jnp.int32, (1, BRANCH), 1) // DIFF_HEAD_DIM
    for g in range(2 * DIFF_HEADS):
        qg = jnp.where(group == g, q, jnp.zeros_like(q))
        s = lax.dot_general(qg, k, _NT, preferred_element_type=f32)
        hh = g // 2
        _online_softmax_step(s, v_ref[0, :, hh * LANE:(hh + 1) * LANE], m_ref, acc_ref, g)

    @pl.when(j == pl.num_programs(2) - 1)
    def _():
        _diff_finalize(lam_ref, gsub_ref, acc_ref, o_ref)


def _mla_attn_bounded_kernel(q_ref, k_ref, v_ref, o_ref, acc_ref, *, tk):
    acc_ref[...] = jnp.zeros(acc_ref.shape, f32)

    def kv_block(j, carry):
        rows = pl.ds(pl.multiple_of(j * tk, tk), tk)
        for hh in range(MLA_HEADS):
            sl = slice(hh * LANE, (hh + 1) * LANE)
            s = lax.dot_general(q_ref[0, :, sl], k_ref[0, rows, sl], _NT, preferred_element_type=f32)
            acc_ref[hh] += jnp.dot(jnp.exp2(s).astype(bf16), v_ref[0, rows, sl], preferred_element_type=f32)
        return carry

    lax.fori_loop(0, k_ref.shape[1] // tk, kv_block, 0)
    _mla_finalize(acc_ref, o_ref)


def _diff_attn_bounded_kernel(lam_ref, q_ref, k_ref, v_ref, gsub_ref, o_ref, acc_ref, *, tk):
    acc_ref[...] = jnp.zeros(acc_ref.shape, f32)
    q = q_ref[0]
    group = lax.broadcasted_iota(jnp.int32, (1, BRANCH), 1) // DIFF_HEAD_DIM
    qs = [jnp.where(group == g, q, jnp.zeros_like(q)) for g in range(2 * DIFF_HEADS)]

    def kv_block(j, carry):
        rows = pl.ds(pl.multiple_of(j * tk, tk), tk)
        k = k_ref[0, rows, :]
        for g in range(2 * DIFF_HEADS):
            s = lax.dot_general(qs[g], k, _NT, preferred_element_type=f32)
            hh = g // 2
            acc_ref[g] += jnp.dot(jnp.exp2(s).astype(bf16), v_ref[0, rows, hh * LANE:(hh + 1) * LANE],
                                  preferred_element_type=f32)
        return carry

    lax.fori_loop(0, k_ref.shape[1] // tk, kv_block, 0)
    _diff_finalize(lam_ref, gsub_ref, acc_ref, o_ref)


def _attn_bounded_call(kernel_fn, n_acc, name, args, in_specs, S, W, B, tq):
    return pl.pallas_call(
        kernel_fn,
        grid=(B, S // tq),
        in_specs=in_specs,
        out_specs=pl.BlockSpec((1, tq, BRANCH), lambda b, i: (b, i, 0)),
        out_shape=jax.ShapeDtypeStruct((B, S, BRANCH), bf16),
        scratch_shapes=[pltpu.VMEM((n_acc, tq, LANE), f32)],
        compiler_params=_cparams(("parallel", "parallel")),
        name=name,
    )(*args)


def _mla_attn_bounded(q, k, v, tq, tk):
    B, S, W = q.shape
    full = pl.BlockSpec((1, S, W), lambda b, i: (b, 0, 0))
    in_specs = [pl.BlockSpec((1, tq, W), lambda b, i: (b, i, 0)), full, full]
    return _attn_bounded_call(functools.partial(_mla_attn_bounded_kernel, tk=tk), MLA_HEADS, "mla_attn_bounded",
                              (q, k, v), in_specs, S, W, B, tq)


def _diff_attn_bounded(lam, q, k, v, gsub, tq, tk):
    B, S, W = v.shape
    in_specs = [pl.BlockSpec(memory_space=pltpu.SMEM),
                pl.BlockSpec((1, tq, BRANCH), lambda b, i: (b, i, 0)),
                pl.BlockSpec((1, S, BRANCH), lambda b, i: (b, 0, 0)),
                pl.BlockSpec((1, S, W), lambda b, i: (b, 0, 0)),
                pl.BlockSpec((1, 2 * DIFF_HEAD_DIM), lambda b, i: (0, 0))]
    return _attn_bounded_call(functools.partial(_diff_attn_bounded_kernel, tk=tk), 2 * DIFF_HEADS,
                              "diff_attn_bounded", (lam, q, k, v, gsub), in_specs, S, W, B, tq)


def _mla_attn(q, k, v, tq, tk):
    B, S, W = q.shape
    return pl.pallas_call(
        _mla_attn_kernel,
        grid=(B, S // tq, S // tk),
        in_specs=[pl.BlockSpec((1, tq, W), lambda b, i, j: (b, i, 0)),
                  pl.BlockSpec((1, tk, W), lambda b, i, j: (b, j, 0)),
                  pl.BlockSpec((1, tk, W), lambda b, i, j: (b, j, 0))],
        out_specs=pl.BlockSpec((1, tq, BRANCH), lambda b, i, j: (b, i, 0)),
        out_shape=jax.ShapeDtypeStruct((B, S, BRANCH), bf16),
        scratch_shapes=[pltpu.VMEM((MLA_HEADS, tq, LANE), f32), pltpu.VMEM((MLA_HEADS, tq, LANE), f32)],
        compiler_params=_cparams(("parallel", "parallel", "arbitrary")),
        name="mla_attn",
    )(q, k, v)


def _diff_attn(lam, q, k, v, gsub, tq, tk):
    B, S, W = v.shape
    return pl.pallas_call(
        _diff_attn_kernel,
        grid=(B, S // tq, S // tk),
        in_specs=[pl.BlockSpec(memory_space=pltpu.SMEM),
                  pl.BlockSpec((1, tq, BRANCH), lambda b, i, j: (b, i, 0)),
                  pl.BlockSpec((1, tk, BRANCH), lambda b, i, j: (b, j, 0)),
                  pl.BlockSpec((1, tk, W), lambda b, i, j: (b, j, 0)),
                  pl.BlockSpec((1, 2 * DIFF_HEAD_DIM), lambda b, i, j: (0, 0))],
        out_specs=pl.BlockSpec((1, tq, BRANCH), lambda b, i, j: (b, i, 0)),
        out_shape=jax.ShapeDtypeStruct((B, S, BRANCH), bf16),
        scratch_shapes=[pltpu.VMEM((2 * DIFF_HEADS, tq, LANE), f32), pltpu.VMEM((2 * DIFF_HEADS, tq, LANE), f32)],
        compiler_params=_cparams(("parallel", "parallel", "arbitrary")),
        name="diff_attn",
    )(lam, q, k, v, gsub)


def _fnet1_kernel(x_ref, w_ref, a_ref):
    a_ref[0] = jnp.dot(w_ref[...], x_ref[0], preferred_element_type=f32).astype(bf16)


def _fnet2_kernel(a_ref, t_ref, cs_ref, o_ref):
    for jj in range(FNET_K1_BLOCK):
        a = jnp.concatenate([a_ref[0, 0, jj], a_ref[0, 1, jj]], axis=0)
        z = jnp.dot(t_ref[jj], a, preferred_element_type=f32)
        zc = jnp.concatenate([z[:FNET_N2], z[FNET_N2:]], axis=1).astype(bf16)
        o_ref[0, :, jj, :] = jnp.dot(zc, cs_ref[...], preferred_element_type=f32).astype(bf16)


def _fnet_consts(S):
    n1 = S // FNET_N2
    a = np.arange(n1)
    ang1 = 2.0 * np.pi * np.outer(a, a) / n1
    w1 = np.concatenate([np.cos(ang1), -np.sin(ang1)], axis=0)
    n2 = np.arange(FNET_N2)
    phi = 2.0 * np.pi * (n2[None, None, :] * a[:, None, None] / S + n2[None, None, :] * n2[None, :, None] / FNET_N2)
    tr, ti = np.cos(phi), -np.sin(phi)
    t = np.concatenate([np.concatenate([tr, -ti], axis=2), np.concatenate([ti, tr], axis=2)], axis=1)
    c = np.arange(BRANCH)
    same = (c[:, None] // FNET_N2) == (c[None, :] // FNET_N2)
    angc = 2.0 * np.pi * np.outer(c % FNET_N2, c % FNET_N2) / FNET_N2
    norm = 1.0 / math.sqrt(S * FNET_N2)
    cs = np.concatenate([np.where(same, np.cos(angc), 0.0), np.where(same, np.sin(angc), 0.0)], axis=0) * norm
    return (jnp.asarray(w1, f32).astype(bf16), jnp.asarray(t, f32).astype(bf16), jnp.asarray(cs, f32).astype(bf16))


def _fnet(u, consts):
    B, S, W = u.shape
    w1, t, cs = consts
    n1 = S // FNET_N2
    cols = FNET_N2 * W
    tn = min(cols, 4096)
    a = pl.pallas_call(
        _fnet1_kernel,
        grid=(B, cols // tn),
        in_specs=[pl.BlockSpec((1, n1, tn), lambda b, i: (b, 0, i)), _const_spec(w1.shape)],
        out_specs=pl.BlockSpec((1, 2 * n1, tn), lambda b, i: (b, 0, i)),
        out_shape=jax.ShapeDtypeStruct((B, 2 * n1, cols), bf16),
        compiler_params=_cparams(("parallel", "parallel")),
        name="fnet1",
    )(u.reshape(B, n1, cols), w1)
    kb = FNET_K1_BLOCK
    y = pl.pallas_call(
        _fnet2_kernel,
        grid=(B, n1 // kb),
        in_specs=[pl.BlockSpec((1, 2, kb, FNET_N2, W), lambda b, i: (b, 0, i, 0, 0)),
                  pl.BlockSpec((kb, 2 * FNET_N2, 2 * FNET_N2), lambda b, i: (i, 0, 0)),
                  _const_spec(cs.shape)],
        out_specs=pl.BlockSpec((1, FNET_N2, kb, W), lambda b, i: (b, 0, i, 0)),
        out_shape=jax.ShapeDtypeStruct((B, FNET_N2, n1, W), bf16),
        compiler_params=_cparams(("parallel", "parallel")),
        name="fnet2",
    )(a.reshape(B, 2, n1, FNET_N2, W), t, cs)
    return y.reshape(B, S, W)


def _hgrn_consts(C):
    L = int(math.log2(C))
    t = np.arange(C)
    m = np.zeros((2, (L + 2) * C, C), np.float32)
    lvl = np.full((2, C, C), -1, np.int32)
    for d in (0, 1):
        for li in range(L):
            n = C >> li
            half = n // 2
            blk = t // n
            mid = blk * n + half
            upper = (t % n) >= half
            for r in range(C):
                if d == 0:
                    if upper[r]:
                        m[d, li * C + r, mid[r]:r + 1] = 1.0
                    else:
                        m[d, li * C + r, r + 1:mid[r]] = 1.0
                else:
                    if upper[r]:
                        m[d, li * C + r, mid[r]:r] = 1.0
                    else:
                        m[d, li * C + r, r:mid[r]] = 1.0
            same = blk[:, None] == blk[None, :]
            if d == 0:
                msk = same & upper[:, None] & (~upper)[None, :]
            else:
                msk = same & (~upper)[:, None] & upper[None, :]
            lvl[d][msk] = li
        for r in range(C):
            if d == 0:
                m[d, L * C + r, :r + 1] = 1.0
                m[d, (L + 1) * C + r, r + 1:] = 1.0
            else:
                m[d, L * C + r, r:] = 1.0
                m[d, (L + 1) * C + r, :r] = 1.0
    c = np.arange(BRANCH)
    headsum = ((c[:, None] // HGRN_DIM) == (c[None, :] // HGRN_DIM)).astype(np.float32)
    lvl = np.tile(lvl, (1, 1, HGRN_HEADS))
    m = np.tile(m, (1, 1, 2))
    return jnp.asarray(m, f32).astype(bf16), jnp.asarray(lvl), jnp.asarray(headsum, f32).astype(bf16)


def _hgrn_kernel(qf_ref, vf_ref, kf_ref, lff_ref, qb_ref, vb_ref, kb_ref, lfb_ref, m_ref, lvl_ref, hs_ref,
                 of_ref, ob_ref, st_ref, *, C, L):
    @pl.when(pl.program_id(1) == 0)
    def _():
        st_ref[...] = jnp.zeros(st_ref.shape, f32)

    refs = ((qf_ref, vf_ref, kf_ref, lff_ref, of_ref), (qb_ref, vb_ref, kb_ref, lfb_ref, ob_ref))
    head = lax.broadcasted_iota(jnp.int32, (1, BRANCH), 1) // HGRN_DIM
    r = lax.broadcasted_iota(jnp.int32, (BRANCH, BRANCH), 0) // HGRN_DIM
    cidx = lax.broadcasted_iota(jnp.int32, (BRANCH, BRANCH), 1) // HGRN_DIM
    n_chunks = qf_ref.shape[1] // C
    order = (tuple(range(n_chunks)), tuple(reversed(range(n_chunks))))
    chains = [(d, order[d][j]) for j in range(n_chunks) for d in (0, 1)]
    rows = {c: slice(c[1] * C, (c[1] + 1) * C) for c in chains}
    q = {c: refs[c[0]][0][0, rows[c], :].astype(f32) for c in chains}
    v = {c: refs[c[0]][1][0, rows[c], :] for c in chains}
    k = {c: refs[c[0]][2][0, rows[c], :].astype(f32) for c in chains}
    e = {}
    for c in chains:
        lf = refs[c[0]][3][0, rows[c], :]
        lf2 = jnp.concatenate([lf[:, :BRANCH], lf[:, BRANCH:]], axis=0)
        e[c] = jnp.dot(m_ref[c[0]], lf2, preferred_element_type=f32)

    o = {}
    st = [st_ref[0], st_ref[1]]
    for c in chains:
        d = c[0]
        ein = jnp.exp(e[c][L * C:(L + 1) * C])
        o[c] = lax.dot_general((q[c] * ein).astype(bf16), st[d].astype(bf16), _NT, preferred_element_type=f32)
        kout = (k[c] * jnp.exp(e[c][(L + 1) * C:(L + 2) * C])).astype(bf16)
        upd = lax.dot_general(v[c], kout, _TN, preferred_element_type=f32)
        total = ein[C - 1:C] if d == 0 else ein[0:1]
        st[d] = st[d] * total + jnp.where(r == cidx, upd, 0.0)
        o[c] = o[c] + jnp.dot((q[c] * k[c]).astype(bf16), hs_ref[...], preferred_element_type=f32) * v[c].astype(f32)
    st_ref[0] = st[0]
    st_ref[1] = st[1]

    scores = {c: jnp.zeros((C, HGRN_HEADS * C), f32) for c in chains}
    for li in range(L):
        for c in chains:
            ex = jnp.exp(e[c][li * C:(li + 1) * C])
            qt = (q[c] * ex).astype(bf16)
            kt = (k[c] * ex).astype(bf16)
            kstack = jnp.concatenate([jnp.where(head == hh, kt, jnp.zeros_like(kt)) for hh in range(HGRN_HEADS)],
                                     axis=0)
            s = lax.dot_general(qt, kstack, _NT, preferred_element_type=f32)
            scores[c] = jnp.where(lvl_ref[c[0]] == li, s, scores[c])
    for c in chains:
        vstack = jnp.concatenate([jnp.where(head == hh, v[c], jnp.zeros_like(v[c])) for hh in range(HGRN_HEADS)],
                                 axis=0)
        oc = o[c] + jnp.dot(scores[c].astype(bf16), vstack, preferred_element_type=f32)
        refs[c[0]][4][0, rows[c], :] = oc.astype(bf16)


def _hgrn(hq, hv, hk, hlf, consts, C):
    B, S, W = hq.shape
    m, lvl, hs = consts
    L = int(math.log2(C))
    rows = min(S, HGRN_CHUNKS_PER_STEP * C)
    nb = S // rows
    fwd = lambda w, c: pl.BlockSpec((1, rows, w), lambda b, i: (b, i, c))
    bwd = lambda w, c: pl.BlockSpec((1, rows, w), lambda b, i: (b, nb - 1 - i, c))
    return pl.pallas_call(
        functools.partial(_hgrn_kernel, C=C, L=L),
        grid=(B, nb),
        in_specs=[fwd(W, 0), fwd(W, 0), fwd(W, 0), fwd(2 * W, 0), bwd(W, 0), bwd(W, 0), bwd(W, 1), bwd(2 * W, 1),
                  _const_spec(m.shape), _const_spec(lvl.shape), _const_spec(hs.shape)],
        out_specs=[fwd(W, 0), bwd(W, 0)],
        out_shape=[jax.ShapeDtypeStruct((B, S, W), bf16)] * 2,
        scratch_shapes=[pltpu.VMEM((2, BRANCH, BRANCH), f32)],
        compiler_params=_cparams(("parallel", "arbitrary")),
        name="hgrn",
    )(hq, hv, hk, hlf, hq, hv, hk, hlf, m, lvl, hs)


def _merge_ffn_kernel(x_ref, om_ref, of_ref, od_ref, ohf_ref, ohb_ref, hg_ref, ln_ref, wg_ref, wbm_ref, wbf_ref,
                      wbd_ref, wbh_ref, wo_ref, gon_ref, g64_ref, lnf_ref, wgu_ref, wd_ref, o_ref):
    x = x_ref[0]
    h = _rms(x, ln_ref[...]).astype(bf16)
    oh = ohf_ref[0].astype(f32) + ohb_ref[0].astype(f32)
    ms = jnp.dot((oh * oh).astype(bf16), g64_ref[...], preferred_element_type=f32)
    oh = (oh * lax.rsqrt(ms + EPS) * gon_ref[...] * _sigmoid(hg_ref[0].astype(f32))).astype(bf16)
    merged = None
    for n, (o_n, w_ref) in enumerate(((om_ref[0], wbm_ref), (of_ref[0], wbf_ref), (od_ref[0], wbd_ref), (oh, wbh_ref))):
        gate = _sigmoid(jnp.dot(h, wg_ref[:, n * D_MODEL:(n + 1) * D_MODEL], preferred_element_type=f32))
        y = gate * jnp.dot(o_n, w_ref[...], preferred_element_type=f32)
        merged = y if merged is None else merged + y
    x = x + jnp.dot(merged.astype(bf16), wo_ref[...], preferred_element_type=f32)

    h = _rms(x, lnf_ref[...]).astype(bf16)
    acc = x
    for c in range(D_FF // FF_CHUNK):
        g = jnp.dot(h, wgu_ref[:, c * FF_CHUNK:(c + 1) * FF_CHUNK], preferred_element_type=f32)
        u = jnp.dot(h, wgu_ref[:, D_FF + c * FF_CHUNK:D_FF + (c + 1) * FF_CHUNK], preferred_element_type=f32)
        a = (g * _sigmoid(g) * u).astype(bf16)
        acc = acc + jnp.dot(a, wd_ref[c], preferred_element_type=f32)
    o_ref[0] = acc


def _merge_ffn(x, om, of, od, oh, hg, p, tm):
    B, S, _ = x.shape
    row = lambda w: pl.BlockSpec((1, tm, w), lambda b, i: (b, i, 0))
    consts = (p["ln_mix"], p["wgate"], p["wbm"], p["wbf"], p["wbd"], p["wbh"], p["wout"], p["gon"], p["g64"],
              p["ln_ffn"], p["wgu"], p["wd"])
    return pl.pallas_call(
        _merge_ffn_kernel,
        grid=(B, S // tm),
        in_specs=[row(D_MODEL)] + [row(BRANCH)] * 6
                 + [_const_spec(c.shape) for c in consts],
        out_specs=row(D_MODEL),
        out_shape=jax.ShapeDtypeStruct((B, S, D_MODEL), f32),
        compiler_params=_cparams(("parallel", "parallel")),
        name="merge_ffn",
    )(x, om, of, od, oh[0], oh[1], hg, *consts)


def _pad_heads(w, heads, width):
    lead = w.shape[:-1]
    w = w.reshape(lead + (heads, width))
    w = jnp.pad(w, [(0, 0)] * len(lead) + [(0, 0), (0, LANE - width)])
    return w.reshape(lead + (heads * LANE,))


def _rope_tables(S):
    def cs(dim):
        inv = 1.0 / (ROPE_THETA ** (jnp.arange(0, dim, 2, dtype=f32) / dim))
        ang = jnp.arange(S, dtype=f32)[:, None] * inv[None, :]
        return jnp.cos(ang), jnp.sin(ang)

    cm, sm = cs(MLA_ROPE)
    one, zero = jnp.ones((S, MLA_NOPE), f32), jnp.zeros((S, MLA_NOPE), f32)
    z16, z32, o32 = jnp.zeros((S, 16), f32), jnp.zeros((S, 32), f32), jnp.ones((S, 32), f32)
    rope_m = jnp.stack([jnp.concatenate([one, cm, cm, o32], 1),
                        jnp.concatenate([zero, -sm, z16, z32], 1),
                        jnp.concatenate([zero, z16, sm, z32], 1)])
    cd, sd = cs(DIFF_HEAD_DIM)
    rope_d = jnp.stack([jnp.tile(jnp.concatenate([cd, cd], 1), (1, 4)),
                        jnp.tile(jnp.concatenate([-sd, z16], 1), (1, 4)),
                        jnp.tile(jnp.concatenate([z16, sd], 1), (1, 4))])
    return rope_m, rope_d


def _block_mean(width, group):
    c = np.arange(width)
    return jnp.asarray(((c[:, None] // group) == (c[None, :] // group)) / group, f32).astype(bf16)


def _layer_params(l, a, lower_bounds):
    w = a["w_in"][l]
    offs = np.cumsum([0, 384, 128, 32, 256, 256, 256, 256, 256, 256, 256, 256, 256, 4096])
    col = lambda i: w[:, offs[i]:offs[i + 1]]
    wcat = jnp.concatenate([col(0), col(1), jnp.pad(col(2), ((0, 0), (MLA_NOPE, LANE - MLA_QK))), col(3), col(4),
                            col(5), _pad_heads(col(6), DIFF_HEADS, 2 * DIFF_HEAD_DIM), col(7), col(8), col(9),
                            col(10), col(11)], axis=1).astype(bf16)
    wukv = a["mla_w_ukv"][l].reshape(MLA_KV_LORA, MLA_HEADS, MLA_NOPE + MLA_V)
    wukv = jnp.concatenate([_pad_heads(wukv[:, :, :MLA_NOPE].reshape(MLA_KV_LORA, -1), MLA_HEADS, MLA_NOPE),
                            _pad_heads(wukv[:, :, MLA_NOPE:].reshape(MLA_KV_LORA, -1), MLA_HEADS, MLA_V)], axis=1)
    lam_init = 0.8 - 0.6 * math.exp(-0.3 * l)
    lam = (jnp.exp(jnp.sum(a["diff_lq1"][l] * a["diff_lk1"][l])) - jnp.exp(jnp.sum(a["diff_lq2"][l] * a["diff_lk2"][l]))
           + lam_init)
    wb = a["w_branch"][l]
    nc = D_FF // FF_CHUNK
    row = lambda v: v.reshape(1, -1).astype(f32)
    return {
        "ln_mix": row(a["ln_mix"][l]),
        "wcat": wcat,
        "gqa": row(a["mla_g_qa"][l]),
        "wuq": _pad_heads(a["mla_w_uq"][l], MLA_HEADS, MLA_QK).astype(bf16),
        "gkva": row(a["mla_g_kva"][l]),
        "wukv": wukv.astype(bf16),
        "gqn": row(jnp.pad(a["mla_g_qn"][l], (0, LANE - MLA_QK))) * (MLA_QK ** -0.5 * LOG2E),
        "gkn": row(jnp.pad(a["mla_g_kn"][l], (0, LANE - MLA_QK))),
        "gdq": row(jnp.tile(a["diff_g_qn"][l], 2 * DIFF_HEADS)) * (DIFF_HEAD_DIM ** -0.5 * LOG2E),
        "gdk": row(jnp.tile(a["diff_g_kn"][l], 2 * DIFF_HEADS)),
        "g32": _block_mean(BRANCH, DIFF_HEAD_DIM),
        "lb": lower_bounds[:, l].astype(f32),
        "lam": lam.reshape(1).astype(f32),
        "mla_bound": MLA_QK ** 0.5 * LOG2E * jnp.max(jnp.abs(a["mla_g_qn"][l])) * jnp.max(jnp.abs(a["mla_g_kn"][l])),
        "diff_bound": (DIFF_HEAD_DIM ** 0.5 * LOG2E * jnp.max(jnp.abs(a["diff_g_qn"][l]))
                       * jnp.max(jnp.abs(a["diff_g_kn"][l]))),
        "gsub": row(a["diff_g_sub"][l]) * (1.0 - lam_init),
        "wgate": col(12).astype(bf16),
        "wbm": wb[0].astype(bf16),
        "wbf": wb[1].astype(bf16),
        "wbd": wb[2].astype(bf16),
        "wbh": wb[3].astype(bf16),
        "wout": a["w_out"][l].astype(bf16),
        "gon": row(jnp.tile(a["hgrn_g_on"][l], HGRN_HEADS)),
        "g64": _block_mean(BRANCH, HGRN_DIM),
        "ln_ffn": row(a["ln_ffn"][l]),
        "wgu": a["w_gate_up"][l].astype(bf16),
        "wd": a["w_down"][l].reshape(nc, FF_CHUNK, D_MODEL).astype(bf16),
    }


def _tiles(S):
    return {"tm_in": min(256, S),"tq": min(512, S), "tk": min(512, S), "tqb": min(512, S), "tkb": min(1024, S),
            "hgrn_chunk": min(64, S),
            "tm_merge": min(512, S)}


def kernel(x, ln_mix, w_in, mla_g_qa, mla_w_uq, mla_g_kva, mla_w_ukv, mla_g_qn, mla_g_kn, diff_g_qn, diff_g_kn,
           diff_lq1, diff_lk1, diff_lq2, diff_lk2, diff_g_sub, hgrn_lb_logits, hgrn_g_on, w_branch, w_out, ln_ffn,
           w_gate_up, w_down):
    a = dict(ln_mix=ln_mix, w_in=w_in, mla_g_qa=mla_g_qa, mla_w_uq=mla_w_uq, mla_g_kva=mla_g_kva,
             mla_w_ukv=mla_w_ukv, mla_g_qn=mla_g_qn, mla_g_kn=mla_g_kn, diff_g_qn=diff_g_qn, diff_g_kn=diff_g_kn,
             diff_lq1=diff_lq1, diff_lk1=diff_lk1, diff_lq2=diff_lq2, diff_lk2=diff_lk2, diff_g_sub=diff_g_sub,
             hgrn_g_on=hgrn_g_on, w_branch=w_branch, w_out=w_out, ln_ffn=ln_ffn, w_gate_up=w_gate_up, w_down=w_down)
    S = x.shape[1]
    t = _tiles(S)
    rope_m, rope_d = _rope_tables(S)
    lb_p = jax.nn.softmax(hgrn_lb_logits.astype(f32), axis=1)
    lower_bounds = jnp.cumsum(lb_p, axis=1) - lb_p[:, :1]
    fnet_consts = _fnet_consts(S)
    hgrn_consts = _hgrn_consts(t["hgrn_chunk"])
    for l in range(DEPTH):
        p = _layer_params(l, a, lower_bounds)
        qm, km, vm, fn, qd, kd, vd, hq, hv, hk, hlf, hg = _in_proj(x, p, rope_m, rope_d, t["tm_in"])
        om = lax.cond(p["mla_bound"] <= SCORE_BOUND_LOG2,
                      lambda: _mla_attn_bounded(qm, km, vm, t["tqb"], t["tkb"]),
                      lambda: _mla_attn(qm, km, vm, t["tq"], t["tk"]))
        od = lax.cond(p["diff_bound"] <= SCORE_BOUND_LOG2,
                      lambda: _diff_attn_bounded(p["lam"], qd, kd, vd, p["gsub"], t["tqb"], t["tkb"]),
                      lambda: _diff_attn(p["lam"], qd, kd, vd, p["gsub"], t["tq"], t["tk"]))
        of = _fnet(fn, fnet_consts)
        oh = _hgrn(hq, hv, hk, hlf, hgrn_consts, t["hgrn_chunk"])
        x = _merge_ffn(x, om, of, od, oh, hg, p, t["tm_merge"])
    return x
```

```python
import functools
import math

import numpy as np
import jax
import jax.numpy as jnp
from jax import lax
from jax.experimental import pallas as pl
from jax.experimental.pallas import tpu as pltpu

f32 = jnp.float32
bf16 = jnp.bfloat16

D_MODEL = 1024
DEPTH = 2
MLA_HEADS = 4
MLA_Q_LORA = 384
MLA_KV_LORA = 128
MLA_NOPE = 64
MLA_ROPE = 32
MLA_V = 64
MLA_QK = MLA_NOPE + MLA_ROPE
DIFF_HEADS = 4
DIFF_HEAD_DIM = 32
HGRN_HEADS = 4
HGRN_DIM = 64
BRANCH = 256
D_FF = 2816
FF_CHUNK = 256
ROPE_THETA = 10000.0
EPS = 1e-6
LOG2E = 1.4426950408889634
LANE = 128
FNET_N2 = 64
FNET_K1_BLOCK = 8
HGRN_CHUNKS_PER_STEP = 8
IN_PROJ_CHAINS_IN_FLIGHT = 4
VMEM_LIMIT = 56 * 1024 * 1024
SCORE_BOUND_LOG2 = 60.0

_SEG = {}
_off = 0
for _name, _w in (("cq", 384), ("ckv", 128), ("krope", 128), ("fnet", 256), ("dq", 256), ("dk", 256),
                  ("dv", 512), ("hq", 256), ("hi", 256), ("hff", 256), ("hfb", 256), ("hg", 256)):
    _SEG[_name] = (_off, _off + _w)
    _off += _w
W_CAT = _off


def _cparams(sem):
    return pltpu.CompilerParams(dimension_semantics=sem, vmem_limit_bytes=VMEM_LIMIT)


def _const_spec(shape):
    nd = len(shape)
    return pl.BlockSpec(shape, lambda *_: (0,) * nd, pipeline_mode=pl.Buffered(1))


def _rms(x, g):
    return x * lax.rsqrt(jnp.mean(x * x, axis=-1, keepdims=True) + EPS) * g


def _sigmoid(x):
    return 1.0 / (1.0 + jnp.exp(-x))


def _rope(t, tab_ref):
    return t * tab_ref[0] + pltpu.roll(t, LANE - 16, 1) * tab_ref[1] + pltpu.roll(t, 16, 1) * tab_ref[2]


def _in_proj_kernel(x_ref, ln_ref, wcat_ref, gqa_ref, wuq_ref, gkva_ref, wukv_ref, gqn_ref, gkn_ref,
                    ropem_ref, gdq_ref, gdk_ref, g32_ref, roped_ref, lb_ref,
                    qm_ref, km_ref, vm_ref, fn_ref, qd_ref, kd_ref, vd_ref, hq_ref, hv_ref, hk_ref,
                    hlf_ref, hg_ref):
    h = _rms(x_ref[0], ln_ref[...]).astype(bf16)

    def seg(name):
        a, b = _SEG[name]
        return jnp.dot(h, wcat_ref[:, a:b], preferred_element_type=f32)

    ones_lane = (lax.broadcasted_iota(jnp.int32, (1, LANE), 1) == MLA_V).astype(f32)


    def mla_q():
        cq = seg("cq")
        yield
        q = jnp.dot(_rms(cq, gqa_ref[...]).astype(bf16), wuq_ref[...], preferred_element_type=f32)
        yield
        for hh in range(MLA_HEADS):
            sl = slice(hh * LANE, (hh + 1) * LANE)
            qh = q[:, sl]
            ms = jnp.sum(qh * qh, axis=-1, keepdims=True) * (1.0 / MLA_QK)
            qm_ref[0, :, sl] = _rope(qh * lax.rsqrt(ms + EPS) * gqn_ref[...], ropem_ref).astype(bf16)

    def mla_kv():
        ckv = seg("ckv")
        krope = seg("krope")
        yield
        kv = jnp.dot(_rms(ckv, gkva_ref[...]).astype(bf16), wukv_ref[...], preferred_element_type=f32)
        yield
        for hh in range(MLA_HEADS):
            sl = slice(hh * LANE, (hh + 1) * LANE)
            kh = kv[:, sl] + krope
            ms = jnp.sum(kh * kh, axis=-1, keepdims=True) * (1.0 / MLA_QK)
            km_ref[0, :, sl] = _rope(kh * lax.rsqrt(ms + EPS) * gkn_ref[...], ropem_ref).astype(bf16)
            vsl = slice((MLA_HEADS + hh) * LANE, (MLA_HEADS + hh + 1) * LANE)
            vm_ref[0, :, sl] = (kv[:, vsl] + ones_lane).astype(bf16)

    def diff_qk(name, g_ref, o_ref):
        t = seg(name)
        yield
        ms = jnp.dot((t * t).astype(bf16), g32_ref[...], preferred_element_type=f32)
        yield
        t = t * lax.rsqrt(ms + EPS) * g_ref[...]
        for c in range(BRANCH // LANE):
            sl = slice(c * LANE, (c + 1) * LANE)
            o_ref[0, :, sl] = _rope(t[:, sl], roped_ref).astype(bf16)

    def diff_v():
        dv = seg("dv")
        yield
        for hh in range(DIFF_HEADS):
            sl = slice(hh * LANE, (hh + 1) * LANE)
            vd_ref[0, :, sl] = (dv[:, sl] + ones_lane).astype(bf16)

    def plain(name, o_ref):
        z = seg(name)
        yield
        o_ref[0] = z.astype(bf16)

    def hgrn_gate(d, name):
        z = seg(name)
        yield
        lb = lb_ref[d:d + 1, :]
        f = lb + (1.0 - lb) * _sigmoid(z)
        hk_ref[0, :, d * BRANCH:(d + 1) * BRANCH] = (1.0 - f).astype(bf16)
        lf = jnp.log(f)
        hi = lf.astype(bf16)
        hlf_ref[0, :, 2 * d * BRANCH:(2 * d + 1) * BRANCH] = hi
        hlf_ref[0, :, (2 * d + 1) * BRANCH:(2 * d + 2) * BRANCH] = (lf - hi.astype(f32)).astype(bf16)

    pending = [mla_q(), hgrn_gate(0, "hff"), mla_kv(), hgrn_gate(1, "hfb"), diff_qk("dq", gdq_ref, qd_ref),
               plain("fnet", fn_ref), diff_qk("dk", gdk_ref, kd_ref), plain("hq", hq_ref), diff_v(),
               plain("hi", hv_ref), plain("hg", hg_ref)]
    active = []
    while pending or active:
        while pending and len(active) < IN_PROJ_CHAINS_IN_FLIGHT:
            active.append(pending.pop(0))
        for chain in list(active):
            if next(chain, "done") == "done":
                active.remove(chain)


def _in_proj(x, p, rope_m, rope_d, tm):
    B, S, _ = x.shape
    row = lambda w: pl.BlockSpec((1, tm, w), lambda b, i: (b, i, 0))
    tab = pl.BlockSpec((3, tm, LANE), lambda b, i: (0, i, 0))
    out_w = (512, 512, 512, 256, 256, 256, 512, 256, 256, 512, 1024, 256)
    consts = (p["ln_mix"], p["wcat"], p["gqa"], p["wuq"], p["gkva"], p["wukv"], p["gqn"], p["gkn"])
    consts2 = (p["gdq"], p["gdk"], p["g32"])
    in_specs = ([row(D_MODEL)] + [_const_spec(c.shape) for c in consts] + [tab]
                + [_const_spec(c.shape) for c in consts2] + [tab, _const_spec(p["lb"].shape)])
    return pl.pallas_call(
        _in_proj_kernel,
        grid=(B, S // tm),
        in_specs=in_specs,
        out_specs=[row(w) for w in out_w],
        out_shape=[jax.ShapeDtypeStruct((B, S, w), bf16) for w in out_w],
        compiler_params=_cparams(("parallel", "parallel")),
        name="in_proj",
    )(x, *consts, rope_m, *consts2, rope_d, p["lb"])


def _online_softmax_step(s, v, m_ref, acc_ref, idx):
    m_prev = m_ref[idx]
    m_new = jnp.maximum(m_prev, jnp.max(s, axis=1, keepdims=True))
    alpha = jnp.exp2(m_prev - m_new)
    p = jnp.exp2((s - m_new[:, :1]).astype(bf16))
    acc_ref[idx] = alpha * acc_ref[idx] + jnp.dot(p, v, preferred_element_type=f32)
    m_ref[idx] = m_new


_NT = (((1,), (1,)), ((), ()))
_TN = (((0,), (0,)), ((), ()))


def _mla_attn_kernel(q_ref, k_ref, v_ref, o_ref, m_ref, acc_ref):
    j = pl.program_id(2)

    @pl.when(j == 0)
    def _():
        m_ref[...] = jnp.full(m_ref.shape, -jnp.inf, f32)
        acc_ref[...] = jnp.zeros(acc_ref.shape, f32)

    for hh in range(MLA_HEADS):
        sl = slice(hh * LANE, (hh + 1) * LANE)
        s = lax.dot_general(q_ref[0, :, sl], k_ref[0, :, sl], _NT, preferred_element_type=f32)
        _online_softmax_step(s, v_ref[0, :, sl], m_ref, acc_ref, hh)

    @pl.when(j == pl.num_programs(2) - 1)
    def _():
        _mla_finalize(acc_ref, o_ref)


def _mla_finalize(acc_ref, o_ref):
    for hh in range(MLA_HEADS):
        acc = acc_ref[hh]
        o = acc[:, :MLA_V] * (1.0 / acc[:, MLA_V:MLA_V + 1])
        o_ref[0, :, hh * MLA_V:(hh + 1) * MLA_V] = o.astype(bf16)


def _diff_finalize(lam_ref, gsub_ref, acc_ref, o_ref):
    lam = lam_ref[0]
    dv = 2 * DIFF_HEAD_DIM
    for hh in range(DIFF_HEADS):
        a1 = acc_ref[2 * hh]
        a2 = acc_ref[2 * hh + 1]
        o = a1[:, :dv] * (1.0 / a1[:, dv:dv + 1]) - a2[:, :dv] * (lam / a2[:, dv:dv + 1])
        ms = jnp.mean(o * o, axis=-1, keepdims=True)
        o_ref[0, :, hh * dv:(hh + 1) * dv] = (o * lax.rsqrt(ms + EPS) * gsub_ref[...]).astype(bf16)


def _diff_attn_kernel(lam_ref, q_ref, k_ref, v_ref, gsub_ref, o_ref, m_ref, acc_ref):
    j = pl.program_id(2)

    @pl.when(j == 0)
    def _():
        m_ref[...] = jnp.full(m_ref.shape, -jnp.inf, f32)
        acc_ref[...] = jnp.zeros(acc_ref.shape, f32)

    q = q_ref[0]
    k = k_ref[0]
    group = lax.broadcasted_iota(jnp.int32, (1, BRANCH), 1) // DIFF_HEAD_DIM
    for g in range(2 * DIFF_HEADS):
        qg = jnp.where(group == g, q, jnp.zeros_like(q))
        s = lax.dot_general(qg, k, _NT, preferred_element_type=f32)
        hh = g // 2
        _online_softmax_step(s, v_ref[0, :, hh * LANE:(hh + 1) * LANE], m_ref, acc_ref, g)

    @pl.when(j == pl.num_programs(2) - 1)
    def _():
        _diff_finalize(lam_ref, gsub_ref, acc_ref, o_ref)


def _mla_attn_bounded_kernel(q_ref, k_ref, v_ref, o_ref, acc_ref, *, tk):
    acc_ref[...] = jnp.zeros(acc_ref.shape, f32)

    def kv_block(j, carry):
        rows = pl.ds(pl.multiple_of(j * tk, tk), tk)
        for hh in range(MLA_HEADS):
            sl = slice(hh * LANE, (hh + 1) * LANE)
            s = lax.dot_general(q_ref[0, :, sl], k_ref[0, rows, sl], _NT, preferred_element_type=f32)
            acc_ref[hh] += jnp.dot(jnp.exp2(s).astype(bf16), v_ref[0, rows, sl], preferred_element_type=f32)
        return carry

    lax.fori_loop(0, k_ref.shape[1] // tk, kv_block, 0)
    _mla_finalize(acc_ref, o_ref)


def _diff_attn_bounded_kernel(lam_ref, q_ref, k_ref, v_ref, gsub_ref, o_ref, acc_ref, *, tk):
    acc_ref[...] = jnp.zeros(acc_ref.shape, f32)
    q = q_ref[0]
    group = lax.broadcasted_iota(jnp.int32, (1, BRANCH), 1) // DIFF_HEAD_DIM
    qs = [jnp.where(group == g, q, jnp.zeros_like(q)) for g in range(2 * DIFF_HEADS)]

    def kv_block(j, carry):
        rows = pl.ds(pl.multiple_of(j * tk, tk), tk)
        k = k_ref[0, rows, :]
        for g in range(2 * DIFF_HEADS):
            s = lax.dot_general(qs[g], k, _NT, preferred_element_type=f32)
            hh = g // 2
            acc_ref[g] += jnp.dot(jnp.exp2(s).astype(bf16), v_ref[0, rows, hh * LANE:(hh + 1) * LANE],
                                  preferred_element_type=f32)
        return carry

    lax.fori_loop(0, k_ref.shape[1] // tk, kv_block, 0)
    _diff_finalize(lam_ref, gsub_ref, acc_ref, o_ref)


def _attn_bounded_call(kernel_fn, n_acc, name, args, in_specs, S, W, B, tq):
    return pl.pallas_call(
        kernel_fn,
        grid=(B, S // tq),
        in_specs=in_specs,
        out_specs=pl.BlockSpec((1, tq, BRANCH), lambda b, i: (b, i, 0)),
        out_shape=jax.ShapeDtypeStruct((B, S, BRANCH), bf16),
        scratch_shapes=[pltpu.VMEM((n_acc, tq, LANE), f32)],
        compiler_params=_cparams(("parallel", "parallel")),
        name=name,
    )(*args)


def _mla_attn_bounded(q, k, v, tq, tk):
    B, S, W = q.shape
    full = pl.BlockSpec((1, S, W), lambda b, i: (b, 0, 0))
    in_specs = [pl.BlockSpec((1, tq, W), lambda b, i: (b, i, 0)), full, full]
    return _attn_bounded_call(functools.partial(_mla_attn_bounded_kernel, tk=tk), MLA_HEADS, "mla_attn_bounded",
                              (q, k, v), in_specs, S, W, B, tq)


def _diff_attn_bounded(lam, q, k, v, gsub, tq, tk):
    B, S, W = v.shape
    in_specs = [pl.BlockSpec(memory_space=pltpu.SMEM),
                pl.BlockSpec((1, tq, BRANCH), lambda b, i: (b, i, 0)),
                pl.BlockSpec((1, S, BRANCH), lambda b, i: (b, 0, 0)),
                pl.BlockSpec((1, S, W), lambda b, i: (b, 0, 0)),
                pl.BlockSpec((1, 2 * DIFF_HEAD_DIM), lambda b, i: (0, 0))]
    return _attn_bounded_call(functools.partial(_diff_attn_bounded_kernel, tk=tk), 2 * DIFF_HEADS,
                              "diff_attn_bounded", (lam, q, k, v, gsub), in_specs, S, W, B, tq)


def _mla_attn(q, k, v, tq, tk):
    B, S, W = q.shape
    return pl.pallas_call(
        _mla_attn_kernel,
        grid=(B, S // tq, S // tk),
        in_specs=[pl.BlockSpec((1, tq, W), lambda b, i, j: (b, i, 0)),
                  pl.BlockSpec((1, tk, W), lambda b, i, j: (b, j, 0)),
                  pl.BlockSpec((1, tk, W), lambda b, i, j: (b, j, 0))],
        out_specs=pl.BlockSpec((1, tq, BRANCH), lambda b, i, j: (b, i, 0)),
        out_shape=jax.ShapeDtypeStruct((B, S, BRANCH), bf16),
        scratch_shapes=[pltpu.VMEM((MLA_HEADS, tq, LANE), f32), pltpu.VMEM((MLA_HEADS, tq, LANE), f32)],
        compiler_params=_cparams(("parallel", "parallel", "arbitrary")),
        name="mla_attn",
    )(q, k, v)


def _diff_attn(lam, q, k, v, gsub, tq, tk):
    B, S, W = v.shape
    return pl.pallas_call(
        _diff_attn_kernel,
        grid=(B, S // tq, S // tk),
        in_specs=[pl.BlockSpec(memory_space=pltpu.SMEM),
                  pl.BlockSpec((1, tq, BRANCH), lambda b, i, j: (b, i, 0)),
                  pl.BlockSpec((1, tk, BRANCH), lambda b, i, j: (b, j, 0)),
                  pl.BlockSpec((1, tk, W), lambda b, i, j: (b, j, 0)),
                  pl.BlockSpec((1, 2 * DIFF_HEAD_DIM), lambda b, i, j: (0, 0))],
        out_specs=pl.BlockSpec((1, tq, BRANCH), lambda b, i, j: (b, i, 0)),
        out_shape=jax.ShapeDtypeStruct((B, S, BRANCH), bf16),
        scratch_shapes=[pltpu.VMEM((2 * DIFF_HEADS, tq, LANE), f32), pltpu.VMEM((2 * DIFF_HEADS, tq, LANE), f32)],
        compiler_params=_cparams(("parallel", "parallel", "arbitrary")),
        name="diff_attn",
    )(lam, q, k, v, gsub)


def _fnet1_kernel(x_ref, w_ref, a_ref):
    a_ref[0] = jnp.dot(w_ref[...], x_ref[0], preferred_element_type=f32).astype(bf16)


def _fnet2_kernel(a_ref, t_ref, cs_ref, o_ref):
    for jj in range(FNET_K1_BLOCK):
        a = jnp.concatenate([a_ref[0, 0, jj], a_ref[0, 1, jj]], axis=0)
        z = jnp.dot(t_ref[jj], a, preferred_element_type=f32)
        zc = jnp.concatenate([z[:FNET_N2], z[FNET_N2:]], axis=1).astype(bf16)
        o_ref[0, :, jj, :] = jnp.dot(zc, cs_ref[...], preferred_element_type=f32).astype(bf16)


def _fnet_consts(S):
    n1 = S // FNET_N2
    a = np.arange(n1)
    ang1 = 2.0 * np.pi * np.outer(a, a) / n1
    w1 = np.concatenate([np.cos(ang1), -np.sin(ang1)], axis=0)
    n2 = np.arange(FNET_N2)
    phi = 2.0 * np.pi * (n2[None, None, :] * a[:, None, None] / S + n2[None, None, :] * n2[None, :, None] / FNET_N2)
    tr, ti = np.cos(phi), -np.sin(phi)
    t = np.concatenate([np.concatenate([tr, -ti], axis=2), np.concatenate([ti, tr], axis=2)], axis=1)
    c = np.arange(BRANCH)
    same = (c[:, None] // FNET_N2) == (c[None, :] // FNET_N2)
    angc = 2.0 * np.pi * np.outer(c % FNET_N2, c % FNET_N2) / FNET_N2
    norm = 1.0 / math.sqrt(S * FNET_N2)
    cs = np.concatenate([np.where(same, np.cos(angc), 0.0), np.where(same, np.sin(angc), 0.0)], axis=0) * norm
    return (jnp.asarray(w1, f32).astype(bf16), jnp.asarray(t, f32).astype(bf16), jnp.asarray(cs, f32).astype(bf16))


def _fnet(u, consts):
    B, S, W = u.shape
    w1, t, cs = consts
    n1 = S // FNET_N2
    cols = FNET_N2 * W
    tn = min(cols, 4096)
    a = pl.pallas_call(
        _fnet1_kernel,
        grid=(B, cols // tn),
        in_specs=[pl.BlockSpec((1, n1, tn), lambda b, i: (b, 0, i)), _const_spec(w1.shape)],
        out_specs=pl.BlockSpec((1, 2 * n1, tn), lambda b, i: (b, 0, i)),
        out_shape=jax.ShapeDtypeStruct((B, 2 * n1, cols), bf16),
        compiler_params=_cparams(("parallel", "parallel")),
        name="fnet1",
    )(u.reshape(B, n1, cols), w1)
    kb = FNET_K1_BLOCK
    y = pl.pallas_call(
        _fnet2_kernel,
        grid=(B, n1 // kb),
        in_specs=[pl.BlockSpec((1, 2, kb, FNET_N2, W), lambda b, i: (b, 0, i, 0, 0)),
                  pl.BlockSpec((kb, 2 * FNET_N2, 2 * FNET_N2), lambda b, i: (i, 0, 0)),
                  _const_spec(cs.shape)],
        out_specs=pl.BlockSpec((1, FNET_N2, kb, W), lambda b, i: (b, 0, i, 0)),
        out_shape=jax.ShapeDtypeStruct((B, FNET_N2, n1, W), bf16),
        compiler_params=_cparams(("parallel", "parallel")),
        name="fnet2",
    )(a.reshape(B, 2, n1, FNET_N2, W), t, cs)
    return y.reshape(B, S, W)


def _hgrn_consts(C):
    L = int(math.log2(C))
    t = np.arange(C)
    m = np.zeros((2, (L + 2) * C, C), np.float32)
    lvl = np.full((2, C, C), -1, np.int32)
    for d in (0, 1):
        for li in range(L):
            n = C >> li
            half = n // 2
            blk = t // n
            mid = blk * n + half
            upper = (t % n) >= half
            for r in range(C):
                if d == 0:
                    if upper[r]:
                        m[d, li * C + r, mid[r]:r + 1] = 1.0
                    else:
                        m[d, li * C + r, r + 1:mid[r]] = 1.0
                else:
                    if upper[r]:
                        m[d, li * C + r, mid[r]:r] = 1.0
                    else:
                        m[d, li * C + r, r:mid[r]] = 1.0
            same = blk[:, None] == blk[None, :]
            if d == 0:
                msk = same & upper[:, None] & (~upper)[None, :]
            else:
                msk = same & (~upper)[:, None] & upper[None, :]
            lvl[d][msk] = li
        for r in range(C):
            if d == 0:
                m[d, L * C + r, :r + 1] = 1.0
                m[d, (L + 1) * C + r, r + 1:] = 1.0
            else:
                m[d, L * C + r, r:] = 1.0
                m[d, (L + 1) * C + r, :r] = 1.0
    c = np.arange(BRANCH)
    headsum = ((c[:, None] // HGRN_DIM) == (c[None, :] // HGRN_DIM)).astype(np.float32)
    lvl = np.tile(lvl, (1, 1, HGRN_HEADS))
    m = np.tile(m, (1, 1, 2))
    return jnp.asarray(m, f32).astype(bf16), jnp.asarray(lvl), jnp.asarray(headsum, f32).astype(bf16)


def _hgrn_kernel(qf_ref, vf_ref, kf_ref, lff_ref, qb_ref, vb_ref, kb_ref, lfb_ref, m_ref, lvl_ref, hs_ref,
                 of_ref, ob_ref, st_ref, *, C, L):
    @pl.when(pl.program_id(1) == 0)
    def _():
        st_ref[...] = jnp.zeros(st_ref.shape, f32)

    refs = ((qf_ref, vf_ref, kf_ref, lff_ref, of_ref), (qb_ref, vb_ref, kb_ref, lfb_ref, ob_ref))
    head = lax.broadcasted_iota(jnp.int32, (1, BRANCH), 1) // HGRN_DIM
    r = lax.broadcasted_iota(jnp.int32, (BRANCH, BRANCH), 0) // HGRN_DIM
    cidx = lax.broadcasted_iota(jnp.int32, (BRANCH, BRANCH), 1) // HGRN_DIM
    n_chunks = qf_ref.shape[1] // C
    order = (tuple(range(n_chunks)), tuple(reversed(range(n_chunks))))
    chains = [(d, order[d][j]) for j in range(n_chunks) for d in (0, 1)]
    rows = {c: slice(c[1] * C, (c[1] + 1) * C) for c in chains}
    q = {c: refs[c[0]][0][0, rows[c], :].astype(f32) for c in chains}
    v = {c: refs[c[0]][1][0, rows[c], :] for c in chains}
    k = {c: refs[c[0]][2][0, rows[c], :].astype(f32) for c in chains}
    e = {}
    for c in chains:
        lf = refs[c[0]][3][0, rows[c], :]
        lf2 = jnp.concatenate([lf[:, :BRANCH], lf[:, BRANCH:]], axis=0)
        e[c] = jnp.dot(m_ref[c[0]], lf2, preferred_element_type=f32)

    o = {}
    st = [st_ref[0], st_ref[1]]
    for c in chains:
        d = c[0]
        ein = jnp.exp(e[c][L * C:(L + 1) * C])
        o[c] = lax.dot_general((q[c] * ein).astype(bf16), st[d].astype(bf16), _NT, preferred_element_type=f32)
        kout = (k[c] * jnp.exp(e[c][(L + 1) * C:(L + 2) * C])).astype(bf16)
        upd = lax.dot_general(v[c], kout, _TN, preferred_element_type=f32)
        total = ein[C - 1:C] if d == 0 else ein[0:1]
        st[d] = st[d] * total + jnp.where(r == cidx, upd, 0.0)
        o[c] = o[c] + jnp.dot((q[c] * k[c]).astype(bf16), hs_ref[...], preferred_element_type=f32) * v[c].astype(f32)
    st_ref[0] = st[0]
    st_ref[1] = st[1]

    scores = {c: jnp.zeros((C, HGRN_HEADS * C), f32) for c in chains}
    for li in range(L):
        for c in chains:
            ex = jnp.exp(e[c][li * C:(li + 1) * C])
            qt = (q[c] * ex).astype(bf16)
            kt = (k[c] * ex).astype(bf16)
            kstack = jnp.concatenate([jnp.where(head == hh, kt, jnp.zeros_like(kt)) for hh in range(HGRN_HEADS)],
                                     axis=0)
            s = lax.dot_general(qt, kstack, _NT, preferred_element_type=f32)
            scores[c] = jnp.where(lvl_ref[c[0]] == li, s, scores[c])
    for c in chains:
        vstack = jnp.concatenate([jnp.where(head == hh, v[c], jnp.zeros_like(v[c])) for hh in range(HGRN_HEADS)],
                                 axis=0)
        oc = o[c] + jnp.dot(scores[c].astype(bf16), vstack, preferred_element_type=f32)
        refs[c[0]][4][0, rows[c], :] = oc.astype(bf16)


def _hgrn(hq, hv, hk, hlf, consts, C):
    B, S, W = hq.shape
    m, lvl, hs = consts
    L = int(math.log2(C))
    rows = min(S, HGRN_CHUNKS_PER_STEP * C)
    nb = S // rows
    fwd = lambda w, c: pl.BlockSpec((1, rows, w), lambda b, i: (b, i, c))
    bwd = lambda w, c: pl.BlockSpec((1, rows, w), lambda b, i: (b, nb - 1 - i, c))
    return pl.pallas_call(
        functools.partial(_hgrn_kernel, C=C, L=L),
        grid=(B, nb),
        in_specs=[fwd(W, 0), fwd(W, 0), fwd(W, 0), fwd(2 * W, 0), bwd(W, 0), bwd(W, 0), bwd(W, 1), bwd(2 * W, 1),
                  _const_spec(m.shape), _const_spec(lvl.shape), _const_spec(hs.shape)],
        out_specs=[fwd(W, 0), bwd(W, 0)],
        out_shape=[jax.ShapeDtypeStruct((B, S, W), bf16)] * 2,
        scratch_shapes=[pltpu.VMEM((2, BRANCH, BRANCH), f32)],
        compiler_params=_cparams(("parallel", "arbitrary")),
        name="hgrn",
    )(hq, hv, hk, hlf, hq, hv, hk, hlf, m, lvl, hs)


def _merge_ffn_kernel(x_ref, om_ref, of_ref, od_ref, ohf_ref, ohb_ref, hg_ref, ln_ref, wg_ref, wbm_ref, wbf_ref,
                      wbd_ref, wbh_ref, wo_ref, gon_ref, g64_ref, lnf_ref, wgu_ref, wd_ref, o_ref):
    x = x_ref[0]
    h = _rms(x, ln_ref[...]).astype(bf16)
    oh = ohf_ref[0].astype(f32) + ohb_ref[0].astype(f32)
    ms = jnp.dot((oh * oh).astype(bf16), g64_ref[...], preferred_element_type=f32)
    oh = (oh * lax.rsqrt(ms + EPS) * gon_ref[...] * _sigmoid(hg_ref[0].astype(f32))).astype(bf16)
    merged = None
    for n, (o_n, w_ref) in enumerate(((om_ref[0], wbm_ref), (of_ref[0], wbf_ref), (od_ref[0], wbd_ref), (oh, wbh_ref))):
        gate = _sigmoid(jnp.dot(h, wg_ref[:, n * D_MODEL:(n + 1) * D_MODEL], preferred_element_type=f32))
        y = gate * jnp.dot(o_n, w_ref[...], preferred_element_type=f32)
        merged = y if merged is None else merged + y
    x = x + jnp.dot(merged.astype(bf16), wo_ref[...], preferred_element_type=f32)

    h = _rms(x, lnf_ref[...]).astype(bf16)
    acc = x
    for c in range(D_FF // FF_CHUNK):
        g = jnp.dot(h, wgu_ref[:, c * FF_CHUNK:(c + 1) * FF_CHUNK], preferred_element_type=f32)
        u = jnp.dot(h, wgu_ref[:, D_FF + c * FF_CHUNK:D_FF + (c + 1) * FF_CHUNK], preferred_element_type=f32)
        a = (g * _sigmoid(g) * u).astype(bf16)
        acc = acc + jnp.dot(a, wd_ref[c], preferred_element_type=f32)
    o_ref[0] = acc


def _merge_ffn(x, om, of, od, oh, hg, p, tm):
    B, S, _ = x.shape
    row = lambda w: pl.BlockSpec((1, tm, w), lambda b, i: (b, i, 0))
    consts = (p["ln_mix"], p["wgate"], p["wbm"], p["wbf"], p["wbd"], p["wbh"], p["wout"], p["gon"], p["g64"],
              p["ln_ffn"], p["wgu"], p["wd"])
    return pl.pallas_call(
        _merge_ffn_kernel,
        grid=(B, S // tm),
        in_specs=[row(D_MODEL)] + [row(BRANCH)] * 6
                 + [_const_spec(c.shape) for c in consts],
        out_specs=row(D_MODEL),
        out_shape=jax.ShapeDtypeStruct((B, S, D_MODEL), f32),
        compiler_params=_cparams(("parallel", "parallel")),
        name="merge_ffn",
    )(x, om, of, od, oh[0], oh[1], hg, *consts)


def _pad_heads(w, heads, width):
    lead = w.shape[:-1]
    w = w.reshape(lead + (heads, width))
    w = jnp.pad(w, [(0, 0)] * len(lead) + [(0, 0), (0, LANE - width)])
    return w.reshape(lead + (heads * LANE,))


def _rope_tables(S):
    def cs(dim):
        inv = 1.0 / (ROPE_THETA ** (jnp.arange(0, dim, 2, dtype=f32) / dim))
        ang = jnp.arange(S, dtype=f32)[:, None] * inv[None, :]
        return jnp.cos(ang), jnp.sin(ang)

    cm, sm = cs(MLA_ROPE)
    one, zero = jnp.ones((S, MLA_NOPE), f32), jnp.zeros((S, MLA_NOPE), f32)
    z16, z32, o32 = jnp.zeros((S, 16), f32), jnp.zeros((S, 32), f32), jnp.ones((S, 32), f32)
    rope_m = jnp.stack([jnp.concatenate([one, cm, cm, o32], 1),
                        jnp.concatenate([zero, -sm, z16, z32], 1),
                        jnp.concatenate([zero, z16, sm, z32], 1)])
    cd, sd = cs(DIFF_HEAD_DIM)
    rope_d = jnp.stack([jnp.tile(jnp.concatenate([cd, cd], 1), (1, 4)),
                        jnp.tile(jnp.concatenate([-sd, z16], 1), (1, 4)),
                        jnp.tile(jnp.concatenate([z16, sd], 1), (1, 4))])
    return rope_m, rope_d


def _block_mean(width, group):
    c = np.arange(width)
    return jnp.asarray(((c[:, None] // group) == (c[None, :] // group)) / group, f32).astype(bf16)


def _layer_params(l, a, lower_bounds):
    w = a["w_in"][l]
    offs = np.cumsum([0, 384, 128, 32, 256, 256, 256, 256, 256, 256, 256, 256, 256, 4096])
    col = lambda i: w[:, offs[i]:offs[i + 1]]
    wcat = jnp.concatenate([col(0), col(1), jnp.pad(col(2), ((0, 0), (MLA_NOPE, LANE - MLA_QK))), col(3), col(4),
                            col(5), _pad_heads(col(6), DIFF_HEADS, 2 * DIFF_HEAD_DIM), col(7), col(8), col(9),
                            col(10), col(11)], axis=1).astype(bf16)
    wukv = a["mla_w_ukv"][l].reshape(MLA_KV_LORA, MLA_HEADS, MLA_NOPE + MLA_V)
    wukv = jnp.concatenate([_pad_heads(wukv[:, :, :MLA_NOPE].reshape(MLA_KV_LORA, -1), MLA_HEADS, MLA_NOPE),
                            _pad_heads(wukv[:, :, MLA_NOPE:].reshape(MLA_KV_LORA, -1), MLA_HEADS, MLA_V)], axis=1)
    lam_init = 0.8 - 0.6 * math.exp(-0.3 * l)
    lam = (jnp.exp(jnp.sum(a["diff_lq1"][l] * a["diff_lk1"][l])) - jnp.exp(jnp.sum(a["diff_lq2"][l] * a["diff_lk2"][l]))
           + lam_init)
    wb = a["w_branch"][l]
    nc = D_FF // FF_CHUNK
    row = lambda v: v.reshape(1, -1).astype(f32)
    return {
        "ln_mix": row(a["ln_mix"][l]),
        "wcat": wcat,
        "gqa": row(a["mla_g_qa"][l]),
        "wuq": _pad_heads(a["mla_w_uq"][l], MLA_HEADS, MLA_QK).astype(bf16),
        "gkva": row(a["mla_g_kva"][l]),
        "wukv": wukv.astype(bf16),
        "gqn": row(jnp.pad(a["mla_g_qn"][l], (0, LANE - MLA_QK))) * (MLA_QK ** -0.5 * LOG2E),
        "gkn": row(jnp.pad(a["mla_g_kn"][l], (0, LANE - MLA_QK))),
        "gdq": row(jnp.tile(a["diff_g_qn"][l], 2 * DIFF_HEADS)) * (DIFF_HEAD_DIM ** -0.5 * LOG2E),
        "gdk": row(jnp.tile(a["diff_g_kn"][l], 2 * DIFF_HEADS)),
        "g32": _block_mean(BRANCH, DIFF_HEAD_DIM),
        "lb": lower_bounds[:, l].astype(f32),
        "lam": lam.reshape(1).astype(f32),
        "mla_bound": MLA_QK ** 0.5 * LOG2E * jnp.max(jnp.abs(a["mla_g_qn"][l])) * jnp.max(jnp.abs(a["mla_g_kn"][l])),
        "diff_bound": (DIFF_HEAD_DIM ** 0.5 * LOG2E * jnp.max(jnp.abs(a["diff_g_qn"][l]))
                       * jnp.max(jnp.abs(a["diff_g_kn"][l]))),
        "gsub": row(a["diff_g_sub"][l]) * (1.0 - lam_init),
        "wgate": col(12).astype(bf16),
        "wbm": wb[0].astype(bf16),
        "wbf": wb[1].astype(bf16),
        "wbd": wb[2].astype(bf16),
        "wbh": wb[3].astype(bf16),
        "wout": a["w_out"][l].astype(bf16),
        "gon": row(jnp.tile(a["hgrn_g_on"][l], HGRN_HEADS)),
        "g64": _block_mean(BRANCH, HGRN_DIM),
        "ln_ffn": row(a["ln_ffn"][l]),
        "wgu": a["w_gate_up"][l].astype(bf16),
        "wd": a["w_down"][l].reshape(nc, FF_CHUNK, D_MODEL).astype(bf16),
    }


def _tiles(S):
    return {"tm_in": min(256, S),"tq": min(512, S), "tk": min(512, S), "tqb": min(512, S), "tkb": min(1024, S),
            "hgrn_chunk": min(64, S),
            "tm_merge": min(512, S)}


def kernel(x, ln_mix, w_in, mla_g_qa, mla_w_uq, mla_g_kva, mla_w_ukv, mla_g_qn, mla_g_kn, diff_g_qn, diff_g_kn,
           diff_lq1, diff_lk1, diff_lq2, diff_lk2, diff_g_sub, hgrn_lb_logits, hgrn_g_on, w_branch, w_out, ln_ffn,
           w_gate_up, w_down):
    a = dict(ln_mix=ln_mix, w_in=w_in, mla_g_qa=mla_g_qa, mla_w_uq=mla_w_uq, mla_g_kva=mla_g_kva,
             mla_w_ukv=mla_w_ukv, mla_g_qn=mla_g_qn, mla_g_kn=mla_g_kn, diff_g_qn=diff_g_qn, diff_g_kn=diff_g_kn,
             diff_lq1=diff_lq1, diff_lk1=diff_lk1, diff_lq2=diff_lq2, diff_lk2=diff_lk2, diff_g_sub=diff_g_sub,
             hgrn_g_on=hgrn_g_on, w_branch=w_branch, w_out=w_out, ln_ffn=ln_ffn, w_gate_up=w_gate_up, w_down=w_down)
    S = x.shape[1]
    t = _tiles(S)
    rope_m, rope_d = _rope_tables(S)
    lb_p = jax.nn.softmax(hgrn_lb_logits.astype(f32), axis=1)
    lower_bounds = jnp.cumsum(lb_p, axis=1) - lb_p[:, :1]
    fnet_consts = _fnet_consts(S)
    hgrn_consts = _hgrn_consts(t["hgrn_chunk"])
    for l in range(DEPTH):
        p = _layer_params(l, a, lower_bounds)
        qm, km, vm, fn, qd, kd, vd, hq, hv, hk, hlf, hg = _in_proj(x, p, rope_m, rope_d, t["tm_in"])
        om = lax.cond(p["mla_bound"] <= SCORE_BOUND_LOG2,
                      lambda: _mla_attn_bounded(qm, km, vm, t["tqb"], t["tkb"]),
                      lambda: _mla_attn(qm, km, vm, t["tq"], t["tk"]))
        od = lax.cond(p["diff_bound"] <= SCORE_BOUND_LOG2,
                      lambda: _diff_attn_bounded(p["lam"], qd, kd, vd, p["gsub"], t["tqb"], t["tkb"]),
                      lambda: _diff_attn(p["lam"], qd, kd, vd, p["gsub"], t["tq"], t["tk"]))
        of = _fnet(fn, fnet_consts)
        oh = _hgrn(hq, hv, hk, hlf, hgrn_consts, t["hgrn_chunk"])
        x = _merge_ffn(x, om, of, od, oh, hg, p, t["tm_merge"])
    return x
```

```python
import functools
import math

import numpy as np
import jax
import jax.numpy as jnp
from jax import lax
from jax.experimental import pallas as pl
from jax.experimental.pallas import tpu as pltpu

f32 = jnp.float32
bf16 = jnp.bfloat16

D_MODEL = 1024
DEPTH = 2
MLA_HEADS = 4
MLA_Q_LORA = 384
MLA_KV_LORA = 128
MLA_NOPE = 64
MLA_ROPE = 32
MLA_V = 64
MLA_QK = MLA_NOPE + MLA_ROPE
DIFF_HEADS = 4
DIFF_HEAD_DIM = 32
HGRN_HEADS = 4
HGRN_DIM = 64
BRANCH = 256
D_FF = 2816
FF_CHUNK = 256
ROPE_THETA = 10000.0
EPS = 1e-6
LOG2E = 1.4426950408889634
LANE = 128
FNET_N2 = 64
FNET_K1_BLOCK = 8
HGRN_CHUNKS_PER_STEP = 8
IN_PROJ_CHAINS_IN_FLIGHT = 4
ATTN_KEY_SUB = 256
ATTN_CHAINS_IN_FLIGHT = 8
VMEM_LIMIT = 56 * 1024 * 1024
SCORE_BOUND_LOG2 = 60.0

_SEG = {}
_off = 0
for _name, _w in (("cq", 384), ("ckv", 128), ("krope", 128), ("fnet", 256), ("dq", 256), ("dk", 256),
                  ("dv", 512), ("hq", 256), ("hi", 256), ("hff", 256), ("hfb", 256), ("hg", 256)):
    _SEG[_name] = (_off, _off + _w)
    _off += _w
W_CAT = _off


def _cparams(sem):
    return pltpu.CompilerParams(dimension_semantics=sem, vmem_limit_bytes=VMEM_LIMIT)


def _const_spec(shape):
    nd = len(shape)
    return pl.BlockSpec(shape, lambda *_: (0,) * nd, pipeline_mode=pl.Buffered(1))


def _rms(x, g):
    return x * lax.rsqrt(jnp.mean(x * x, axis=-1, keepdims=True) + EPS) * g


def _sigmoid(x):
    return 1.0 / (1.0 + jnp.exp(-x))


def _rope(t, tab_ref):
    return t * tab_ref[0] + pltpu.roll(t, LANE - 16, 1) * tab_ref[1] + pltpu.roll(t, 16, 1) * tab_ref[2]


def _in_proj_kernel(x_ref, ln_ref, wcat_ref, gqa_ref, wuq_ref, gkva_ref, wukv_ref, gqn_ref, gkn_ref,
                    ropem_ref, gdq_ref, gdk_ref, g32_ref, roped_ref, lb_ref,
                    qm_ref, km_ref, vm_ref, fn_ref, qd_ref, kd_ref, vd_ref, hq_ref, hv_ref, hk_ref,
                    hlf_ref, hg_ref):
    h = _rms(x_ref[0], ln_ref[...]).astype(bf16)

    def seg(name):
        a, b = _SEG[name]
        return jnp.dot(h, wcat_ref[:, a:b], preferred_element_type=f32)

    ones_lane = (lax.broadcasted_iota(jnp.int32, (1, LANE), 1) == MLA_V).astype(f32)


    def mla_q():
        cq = seg("cq")
        yield
        q = jnp.dot(_rms(cq, gqa_ref[...]).astype(bf16), wuq_ref[...], preferred_element_type=f32)
        yield
        for hh in range(MLA_HEADS):
            sl = slice(hh * LANE, (hh + 1) * LANE)
            qh = q[:, sl]
            ms = jnp.sum(qh * qh, axis=-1, keepdims=True) * (1.0 / MLA_QK)
            qm_ref[0, :, sl] = _rope(qh * lax.rsqrt(ms + EPS) * gqn_ref[...], ropem_ref).astype(bf16)

    def mla_kv():
        ckv = seg("ckv")
        krope = seg("krope")
        yield
        kv = jnp.dot(_rms(ckv, gkva_ref[...]).astype(bf16), wukv_ref[...], preferred_element_type=f32)
        yield
        for hh in range(MLA_HEADS):
            sl = slice(hh * LANE, (hh + 1) * LANE)
            kh = kv[:, sl] + krope
            ms = jnp.sum(kh * kh, axis=-1, keepdims=True) * (1.0 / MLA_QK)
            km_ref[0, :, sl] = _rope(kh * lax.rsqrt(ms + EPS) * gkn_ref[...], ropem_ref).astype(bf16)
            vsl = slice((MLA_HEADS + hh) * LANE, (MLA_HEADS + hh + 1) * LANE)
            vm_ref[0, :, sl] = (kv[:, vsl] + ones_lane).astype(bf16)

    def diff_qk(name, g_ref, o_ref):
        t = seg(name)
        yield
        ms = jnp.dot((t * t).astype(bf16), g32_ref[...], preferred_element_type=f32)
        yield
        t = t * lax.rsqrt(ms + EPS) * g_ref[...]
        for c in range(BRANCH // LANE):
            sl = slice(c * LANE, (c + 1) * LANE)
            o_ref[0, :, sl] = _rope(t[:, sl], roped_ref).astype(bf16)

    def diff_v():
        dv = seg("dv")
        yield
        for hh in range(DIFF_HEADS):
            sl = slice(hh * LANE, (hh + 1) * LANE)
            vd_ref[0, :, sl] = (dv[:, sl] + ones_lane).astype(bf16)

    def plain(name, o_ref):
        z = seg(name)
        yield
        o_ref[0] = z.astype(bf16)

    def hgrn_gate(d, name):
        z = seg(name)
        yield
        lb = lb_ref[d:d + 1, :]
        f = lb + (1.0 - lb) * _sigmoid(z)
        hk_ref[0, :, d * BRANCH:(d + 1) * BRANCH] = (1.0 - f).astype(bf16)
        lf = jnp.log(f)
        hi = lf.astype(bf16)
        hlf_ref[0, :, 2 * d * BRANCH:(2 * d + 1) * BRANCH] = hi
        hlf_ref[0, :, (2 * d + 1) * BRANCH:(2 * d + 2) * BRANCH] = (lf - hi.astype(f32)).astype(bf16)

    _round_robin([mla_q(), hgrn_gate(0, "hff"), mla_kv(), hgrn_gate(1, "hfb"), diff_qk("dq", gdq_ref, qd_ref),
                  plain("fnet", fn_ref), diff_qk("dk", gdk_ref, kd_ref), plain("hq", hq_ref), diff_v(),
                  plain("hi", hv_ref), plain("hg", hg_ref)], IN_PROJ_CHAINS_IN_FLIGHT)


def _in_proj(x, p, rope_m, rope_d, tm):
    B, S, _ = x.shape
    row = lambda w: pl.BlockSpec((1, tm, w), lambda b, i: (b, i, 0))
    tab = pl.BlockSpec((3, tm, LANE), lambda b, i: (0, i, 0))
    out_w = (512, 512, 512, 256, 256, 256, 512, 256, 256, 512, 1024, 256)
    consts = (p["ln_mix"], p["wcat"], p["gqa"], p["wuq"], p["gkva"], p["wukv"], p["gqn"], p["gkn"])
    consts2 = (p["gdq"], p["gdk"], p["g32"])
    in_specs = ([row(D_MODEL)] + [_const_spec(c.shape) for c in consts] + [tab]
                + [_const_spec(c.shape) for c in consts2] + [tab, _const_spec(p["lb"].shape)])
    return pl.pallas_call(
        _in_proj_kernel,
        grid=(B, S // tm),
        in_specs=in_specs,
        out_specs=[row(w) for w in out_w],
        out_shape=[jax.ShapeDtypeStruct((B, S, w), bf16) for w in out_w],
        compiler_params=_cparams(("parallel", "parallel")),
        name="in_proj",
    )(x, *consts, rope_m, *consts2, rope_d, p["lb"])


def _online_softmax_step(s, v, m_ref, acc_ref, idx):
    m_prev = m_ref[idx]
    m_new = jnp.maximum(m_prev, jnp.max(s, axis=1, keepdims=True))
    alpha = jnp.exp2(m_prev - m_new)
    p = jnp.exp2((s - m_new[:, :1]).astype(bf16))
    acc_ref[idx] = alpha * acc_ref[idx] + jnp.dot(p, v, preferred_element_type=f32)
    m_ref[idx] = m_new


_NT = (((1,), (1,)), ((), ()))
_TN = (((0,), (0,)), ((), ()))


def _mla_attn_kernel(q_ref, k_ref, v_ref, o_ref, m_ref, acc_ref):
    j = pl.program_id(2)

    @pl.when(j == 0)
    def _():
        m_ref[...] = jnp.full(m_ref.shape, -jnp.inf, f32)
        acc_ref[...] = jnp.zeros(acc_ref.shape, f32)

    for hh in range(MLA_HEADS):
        sl = slice(hh * LANE, (hh + 1) * LANE)
        s = lax.dot_general(q_ref[0, :, sl], k_ref[0, :, sl], _NT, preferred_element_type=f32)
        _online_softmax_step(s, v_ref[0, :, sl], m_ref, acc_ref, hh)

    @pl.when(j == pl.num_programs(2) - 1)
    def _():
        _mla_finalize(acc_ref, o_ref)


def _mla_finalize(acc_ref, o_ref, transposed=False):
    for hh in range(MLA_HEADS):
        acc = acc_ref[hh].T if transposed else acc_ref[hh]
        o = acc[:, :MLA_V] * (1.0 / acc[:, MLA_V:MLA_V + 1])
        o_ref[0, :, hh * MLA_V:(hh + 1) * MLA_V] = o.astype(bf16)


def _diff_finalize(lam_ref, gsub_ref, acc_ref, o_ref, transposed=False):
    lam = lam_ref[0]
    dv = 2 * DIFF_HEAD_DIM
    for hh in range(DIFF_HEADS):
        a1 = acc_ref[2 * hh].T if transposed else acc_ref[2 * hh]
        a2 = acc_ref[2 * hh + 1].T if transposed else acc_ref[2 * hh + 1]
        o = a1[:, :dv] * (1.0 / a1[:, dv:dv + 1]) - a2[:, :dv] * (lam / a2[:, dv:dv + 1])
        ms = jnp.mean(o * o, axis=-1, keepdims=True)
        o_ref[0, :, hh * dv:(hh + 1) * dv] = (o * lax.rsqrt(ms + EPS) * gsub_ref[...]).astype(bf16)


def _diff_attn_kernel(lam_ref, q_ref, k_ref, v_ref, gsub_ref, o_ref, m_ref, acc_ref):
    j = pl.program_id(2)

    @pl.when(j == 0)
    def _():
        m_ref[...] = jnp.full(m_ref.shape, -jnp.inf, f32)
        acc_ref[...] = jnp.zeros(acc_ref.shape, f32)

    q = q_ref[0]
    k = k_ref[0]
    group = lax.broadcasted_iota(jnp.int32, (1, BRANCH), 1) // DIFF_HEAD_DIM
    for g in range(2 * DIFF_HEADS):
        qg = jnp.where(group == g, q, jnp.zeros_like(q))
        s = lax.dot_general(qg, k, _NT, preferred_element_type=f32)
        hh = g // 2
        _online_softmax_step(s, v_ref[0, :, hh * LANE:(hh + 1) * LANE], m_ref, acc_ref, g)

    @pl.when(j == pl.num_programs(2) - 1)
    def _():
        _diff_finalize(lam_ref, gsub_ref, acc_ref, o_ref)


def _round_robin(chains, in_flight):
    pending, active = list(chains), []
    while pending or active:
        while pending and len(active) < in_flight:
            active.append(pending.pop(0))
        for c in list(active):
            if next(c, "done") == "done":
                active.remove(c)


def _bounded_attn_loop(maps, k_ref, v_ref, acc_ref, tk):
    acc_ref[...] = jnp.zeros(acc_ref.shape, f32)

    def kv_block(j, carry):
        def chain(g, r0):
            q, ksl, vsl = maps[g]
            rows = pl.ds(pl.multiple_of(j * tk + r0, ATTN_KEY_SUB), ATTN_KEY_SUB)
            st = lax.dot_general(k_ref[0, rows, ksl], q, _NT, preferred_element_type=f32)
            yield
            pt = jnp.exp2(st).astype(bf16)
            yield
            acc_ref[g] += lax.dot_general(v_ref[0, rows, vsl], pt, _TN, preferred_element_type=f32)

        _round_robin([chain(g, r0) for r0 in range(0, tk, ATTN_KEY_SUB) for g in range(len(maps))],
                     ATTN_CHAINS_IN_FLIGHT)
        return carry

    lax.fori_loop(0, k_ref.shape[1] // tk, kv_block, 0)


def _mla_attn_bounded_kernel(q_ref, k_ref, v_ref, o_ref, acc_ref, *, tk):
    heads = [slice(hh * LANE, (hh + 1) * LANE) for hh in range(MLA_HEADS)]
    _bounded_attn_loop([(q_ref[0, :, sl], sl, sl) for sl in heads], k_ref, v_ref, acc_ref, tk)
    _mla_finalize(acc_ref, o_ref, transposed=True)


def _diff_attn_bounded_kernel(lam_ref, q_ref, k_ref, v_ref, gsub_ref, o_ref, acc_ref, *, tk):
    q = q_ref[0]
    group = lax.broadcasted_iota(jnp.int32, (1, BRANCH), 1) // DIFF_HEAD_DIM
    maps = [(jnp.where(group == g, q, jnp.zeros_like(q)), slice(0, BRANCH),
             slice((g // 2) * LANE, (g // 2 + 1) * LANE)) for g in range(2 * DIFF_HEADS)]
    _bounded_attn_loop(maps, k_ref, v_ref, acc_ref, tk)
    _diff_finalize(lam_ref, gsub_ref, acc_ref, o_ref, transposed=True)


def _attn_bounded_call(kernel_fn, n_acc, name, args, in_specs, S, W, B, tq):
    return pl.pallas_call(
        kernel_fn,
        grid=(B, S // tq),
        in_specs=in_specs,
        out_specs=pl.BlockSpec((1, tq, BRANCH), lambda b, i: (b, i, 0)),
        out_shape=jax.ShapeDtypeStruct((B, S, BRANCH), bf16),
        scratch_shapes=[pltpu.VMEM((n_acc, LANE, tq), f32)],
        compiler_params=_cparams(("parallel", "parallel")),
        name=name,
    )(*args)


def _mla_attn_bounded(q, k, v, tq, tk):
    B, S, W = q.shape
    full = pl.BlockSpec((1, S, W), lambda b, i: (b, 0, 0))
    in_specs = [pl.BlockSpec((1, tq, W), lambda b, i: (b, i, 0)), full, full]
    return _attn_bounded_call(functools.partial(_mla_attn_bounded_kernel, tk=tk), MLA_HEADS, "mla_attn_bounded",
                              (q, k, v), in_specs, S, W, B, tq)


def _diff_attn_bounded(lam, q, k, v, gsub, tq, tk):
    B, S, W = v.shape
    in_specs = [pl.BlockSpec(memory_space=pltpu.SMEM),
                pl.BlockSpec((1, tq, BRANCH), lambda b, i: (b, i, 0)),
                pl.BlockSpec((1, S, BRANCH), lambda b, i: (b, 0, 0)),
                pl.BlockSpec((1, S, W), lambda b, i: (b, 0, 0)),
                pl.BlockSpec((1, 2 * DIFF_HEAD_DIM), lambda b, i: (0, 0))]
    return _attn_bounded_call(functools.partial(_diff_attn_bounded_kernel, tk=tk), 2 * DIFF_HEADS,
                              "diff_attn_bounded", (lam, q, k, v, gsub), in_specs, S, W, B, tq)


def _mla_attn(q, k, v, tq, tk):
    B, S, W = q.shape
    return pl.pallas_call(
        _mla_attn_kernel,
        grid=(B, S // tq, S // tk),
        in_specs=[pl.BlockSpec((1, tq, W), lambda b, i, j: (b, i, 0)),
                  pl.BlockSpec((1, tk, W), lambda b, i, j: (b, j, 0)),
                  pl.BlockSpec((1, tk, W), lambda b, i, j: (b, j, 0))],
        out_specs=pl.BlockSpec((1, tq, BRANCH), lambda b, i, j: (b, i, 0)),
        out_shape=jax.ShapeDtypeStruct((B, S, BRANCH), bf16),
        scratch_shapes=[pltpu.VMEM((MLA_HEADS, tq, LANE), f32), pltpu.VMEM((MLA_HEADS, tq, LANE), f32)],
        compiler_params=_cparams(("parallel", "parallel", "arbitrary")),
        name="mla_attn",
    )(q, k, v)


def _diff_attn(lam, q, k, v, gsub, tq, tk):
    B, S, W = v.shape
    return pl.pallas_call(
        _diff_attn_kernel,
        grid=(B, S // tq, S // tk),
        in_specs=[pl.BlockSpec(memory_space=pltpu.SMEM),
                  pl.BlockSpec((1, tq, BRANCH), lambda b, i, j: (b, i, 0)),
                  pl.BlockSpec((1, tk, BRANCH), lambda b, i, j: (b, j, 0)),
                  pl.BlockSpec((1, tk, W), lambda b, i, j: (b, j, 0)),
                  pl.BlockSpec((1, 2 * DIFF_HEAD_DIM), lambda b, i, j: (0, 0))],
        out_specs=pl.BlockSpec((1, tq, BRANCH), lambda b, i, j: (b, i, 0)),
        out_shape=jax.ShapeDtypeStruct((B, S, BRANCH), bf16),
        scratch_shapes=[pltpu.VMEM((2 * DIFF_HEADS, tq, LANE), f32), pltpu.VMEM((2 * DIFF_HEADS, tq, LANE), f32)],
        compiler_params=_cparams(("parallel", "parallel", "arbitrary")),
        name="diff_attn",
    )(lam, q, k, v, gsub)


def _fnet1_kernel(x_ref, w_ref, a_ref):
    a_ref[0] = jnp.dot(w_ref[...], x_ref[0], preferred_element_type=f32).astype(bf16)


def _fnet2_kernel(a_ref, t_ref, cs_ref, o_ref):
    zs = []
    for jj in range(FNET_K1_BLOCK):
        a = jnp.concatenate([a_ref[0, 0, jj], a_ref[0, 1, jj]], axis=0)
        zs.append(jnp.dot(t_ref[jj], a, preferred_element_type=f32))
    zc = jnp.concatenate([jnp.concatenate([z[:FNET_N2], z[FNET_N2:]], axis=1) for z in zs], axis=0).astype(bf16)
    y = jnp.dot(zc, cs_ref[...], preferred_element_type=f32).astype(bf16)
    for jj in range(FNET_K1_BLOCK):
        o_ref[0, :, jj, :] = y[jj * FNET_N2:(jj + 1) * FNET_N2]


def _fnet_consts(S):
    n1 = S // FNET_N2
    a = np.arange(n1)
    ang1 = 2.0 * np.pi * np.outer(a, a) / n1
    w1 = np.concatenate([np.cos(ang1), -np.sin(ang1)], axis=0)
    n2 = np.arange(FNET_N2)
    phi = 2.0 * np.pi * (n2[None, None, :] * a[:, None, None] / S + n2[None, None, :] * n2[None, :, None] / FNET_N2)
    tr, ti = np.cos(phi), -np.sin(phi)
    t = np.concatenate([np.concatenate([tr, -ti], axis=2), np.concatenate([ti, tr], axis=2)], axis=1)
    c = np.arange(BRANCH)
    same = (c[:, None] // FNET_N2) == (c[None, :] // FNET_N2)
    angc = 2.0 * np.pi * np.outer(c % FNET_N2, c % FNET_N2) / FNET_N2
    norm = 1.0 / math.sqrt(S * FNET_N2)
    cs = np.concatenate([np.where(same, np.cos(angc), 0.0), np.where(same, np.sin(angc), 0.0)], axis=0) * norm
    return (jnp.asarray(w1, f32).astype(bf16), jnp.asarray(t, f32).astype(bf16), jnp.asarray(cs, f32).astype(bf16))


def _fnet(u, consts):
    B, S, W = u.shape
    w1, t, cs = consts
    n1 = S // FNET_N2
    cols = FNET_N2 * W
    tn = min(cols, 4096)
    a = pl.pallas_call(
        _fnet1_kernel,
        grid=(B, cols // tn),
        in_specs=[pl.BlockSpec((1, n1, tn), lambda b, i: (b, 0, i)), _const_spec(w1.shape)],
        out_specs=pl.BlockSpec((1, 2 * n1, tn), lambda b, i: (b, 0, i)),
        out_shape=jax.ShapeDtypeStruct((B, 2 * n1, cols), bf16),
        compiler_params=_cparams(("parallel", "parallel")),
        name="fnet1",
    )(u.reshape(B, n1, cols), w1)
    kb = FNET_K1_BLOCK
    y = pl.pallas_call(
        _fnet2_kernel,
        grid=(B, n1 // kb),
        in_specs=[pl.BlockSpec((1, 2, kb, FNET_N2, W), lambda b, i: (b, 0, i, 0, 0)),
                  pl.BlockSpec((kb, 2 * FNET_N2, 2 * FNET_N2), lambda b, i: (i, 0, 0)),
                  _const_spec(cs.shape)],
        out_specs=pl.BlockSpec((1, FNET_N2, kb, W), lambda b, i: (b, 0, i, 0)),
        out_shape=jax.ShapeDtypeStruct((B, FNET_N2, n1, W), bf16),
        compiler_params=_cparams(("parallel", "parallel")),
        name="fnet2",
    )(a.reshape(B, 2, n1, FNET_N2, W), t, cs)
    return y.reshape(B, S, W)


def _hgrn_consts(C):
    L = int(math.log2(C))
    t = np.arange(C)
    m = np.zeros((2, (L + 2) * C, C), np.float32)
    lvl = np.full((2, C, C), -1, np.int32)
    for d in (0, 1):
        for li in range(L):
            n = C >> li
            half = n // 2
            blk = t // n
            mid = blk * n + half
            upper = (t % n) >= half
            for r in range(C):
                if d == 0:
                    if upper[r]:
                        m[d, li * C + r, mid[r]:r + 1] = 1.0
                    else:
                        m[d, li * C + r, r + 1:mid[r]] = 1.0
                else:
                    if upper[r]:
                        m[d, li * C + r, mid[r]:r] = 1.0
                    else:
                        m[d, li * C + r, r:mid[r]] = 1.0
            same = blk[:, None] == blk[None, :]
            if d == 0:
                msk = same & upper[:, None] & (~upper)[None, :]
            else:
                msk = same & (~upper)[:, None] & upper[None, :]
            lvl[d][msk] = li
        for r in range(C):
            if d == 0:
                m[d, L * C + r, :r + 1] = 1.0
                m[d, (L + 1) * C + r, r + 1:] = 1.0
            else:
                m[d, L * C + r, r:] = 1.0
                m[d, (L + 1) * C + r, :r] = 1.0
    c = np.arange(BRANCH)
    headsum = ((c[:, None] // HGRN_DIM) == (c[None, :] // HGRN_DIM)).astype(np.float32)
    lvl = np.tile(lvl, (1, 1, HGRN_HEADS))
    m = np.tile(m, (1, 1, 2))
    return jnp.asarray(m, f32).astype(bf16), jnp.asarray(lvl), jnp.asarray(headsum, f32).astype(bf16)


def _hgrn_kernel(qf_ref, vf_ref, kf_ref, lff_ref, qb_ref, vb_ref, kb_ref, lfb_ref, m_ref, lvl_ref, hs_ref,
                 of_ref, ob_ref, st_ref, *, C, L):
    @pl.when(pl.program_id(1) == 0)
    def _():
        st_ref[...] = jnp.zeros(st_ref.shape, f32)

    refs = ((qf_ref, vf_ref, kf_ref, lff_ref, of_ref), (qb_ref, vb_ref, kb_ref, lfb_ref, ob_ref))
    head = lax.broadcasted_iota(jnp.int32, (1, BRANCH), 1) // HGRN_DIM
    r = lax.broadcasted_iota(jnp.int32, (BRANCH, BRANCH), 0) // HGRN_DIM
    cidx = lax.broadcasted_iota(jnp.int32, (BRANCH, BRANCH), 1) // HGRN_DIM
    n_chunks = qf_ref.shape[1] // C
    order = (tuple(range(n_chunks)), tuple(reversed(range(n_chunks))))
    chains = [(d, order[d][j]) for j in range(n_chunks) for d in (0, 1)]
    rows = {c: slice(c[1] * C, (c[1] + 1) * C) for c in chains}
    q = {c: refs[c[0]][0][0, rows[c], :].astype(f32) for c in chains}
    v = {c: refs[c[0]][1][0, rows[c], :] for c in chains}
    k = {c: refs[c[0]][2][0, rows[c], :].astype(f32) for c in chains}
    e = {}
    for c in chains:
        lf = refs[c[0]][3][0, rows[c], :]
        lf2 = jnp.concatenate([lf[:, :BRANCH], lf[:, BRANCH:]], axis=0)
        e[c] = jnp.dot(m_ref[c[0]], lf2, preferred_element_type=f32)

    o = {}
    st = [st_ref[0], st_ref[1]]
    for c in chains:
        d = c[0]
        ein = jnp.exp(e[c][L * C:(L + 1) * C])
        o[c] = lax.dot_general((q[c] * ein).astype(bf16), st[d].astype(bf16), _NT, preferred_element_type=f32)
        kout = (k[c] * jnp.exp(e[c][(L + 1) * C:(L + 2) * C])).astype(bf16)
        upd = lax.dot_general(v[c], kout, _TN, preferred_element_type=f32)
        total = ein[C - 1:C] if d == 0 else ein[0:1]
        st[d] = st[d] * total + jnp.where(r == cidx, upd, 0.0)
        o[c] = o[c] + jnp.dot((q[c] * k[c]).astype(bf16), hs_ref[...], preferred_element_type=f32) * v[c].astype(f32)
    st_ref[0] = st[0]
    st_ref[1] = st[1]

    scores = {c: jnp.zeros((C, HGRN_HEADS * C), f32) for c in chains}
    for li in range(L):
        for c in chains:
            ex = jnp.exp(e[c][li * C:(li + 1) * C])
            qt = (q[c] * ex).astype(bf16)
            kt = (k[c] * ex).astype(bf16)
            kstack = jnp.concatenate([jnp.where(head == hh, kt, jnp.zeros_like(kt)) for hh in range(HGRN_HEADS)],
                                     axis=0)
            s = lax.dot_general(qt, kstack, _NT, preferred_element_type=f32)
            scores[c] = jnp.where(lvl_ref[c[0]] == li, s, scores[c])
    for c in chains:
        vstack = jnp.concatenate([jnp.where(head == hh, v[c], jnp.zeros_like(v[c])) for hh in range(HGRN_HEADS)],
                                 axis=0)
        oc = o[c] + jnp.dot(scores[c].astype(bf16), vstack, preferred_element_type=f32)
        refs[c[0]][4][0, rows[c], :] = oc.astype(bf16)


def _hgrn(hq, hv, hk, hlf, consts, C):
    B, S, W = hq.shape
    m, lvl, hs = consts
    L = int(math.log2(C))
    rows = min(S, HGRN_CHUNKS_PER_STEP * C)
    nb = S // rows
    fwd = lambda w, c: pl.BlockSpec((1, rows, w), lambda b, i: (b, i, c))
    bwd = lambda w, c: pl.BlockSpec((1, rows, w), lambda b, i: (b, nb - 1 - i, c))
    return pl.pallas_call(
        functools.partial(_hgrn_kernel, C=C, L=L),
        grid=(B, nb),
        in_specs=[fwd(W, 0), fwd(W, 0), fwd(W, 0), fwd(2 * W, 0), bwd(W, 0), bwd(W, 0), bwd(W, 1), bwd(2 * W, 1),
                  _const_spec(m.shape), _const_spec(lvl.shape), _const_spec(hs.shape)],
        out_specs=[fwd(W, 0), bwd(W, 0)],
        out_shape=[jax.ShapeDtypeStruct((B, S, W), bf16)] * 2,
        scratch_shapes=[pltpu.VMEM((2, BRANCH, BRANCH), f32)],
        compiler_params=_cparams(("parallel", "arbitrary")),
        name="hgrn",
    )(hq, hv, hk, hlf, hq, hv, hk, hlf, m, lvl, hs)


def _merge_ffn_kernel(x_ref, om_ref, of_ref, od_ref, ohf_ref, ohb_ref, hg_ref, ln_ref, wg_ref, wbm_ref, wbf_ref,
                      wbd_ref, wbh_ref, wo_ref, gon_ref, g64_ref, lnf_ref, wgu_ref, wd_ref, o_ref):
    x = x_ref[0]
    h = _rms(x, ln_ref[...]).astype(bf16)
    oh = ohf_ref[0].astype(f32) + ohb_ref[0].astype(f32)
    ms = jnp.dot((oh * oh).astype(bf16), g64_ref[...], preferred_element_type=f32)
    oh = (oh * lax.rsqrt(ms + EPS) * gon_ref[...] * _sigmoid(hg_ref[0].astype(f32))).astype(bf16)
    merged = None
    for n, (o_n, w_ref) in enumerate(((om_ref[0], wbm_ref), (of_ref[0], wbf_ref), (od_ref[0], wbd_ref), (oh, wbh_ref))):
        gate = _sigmoid(jnp.dot(h, wg_ref[:, n * D_MODEL:(n + 1) * D_MODEL], preferred_element_type=f32))
        y = gate * jnp.dot(o_n, w_ref[...], preferred_element_type=f32)
        merged = y if merged is None else merged + y
    x = x + jnp.dot(merged.astype(bf16), wo_ref[...], preferred_element_type=f32)

    h = _rms(x, lnf_ref[...]).astype(bf16)
    acc = x
    for c in range(D_FF // FF_CHUNK):
        g = jnp.dot(h, wgu_ref[:, c * FF_CHUNK:(c + 1) * FF_CHUNK], preferred_element_type=f32)
        u = jnp.dot(h, wgu_ref[:, D_FF + c * FF_CHUNK:D_FF + (c + 1) * FF_CHUNK], preferred_element_type=f32)
        a = (g * _sigmoid(g) * u).astype(bf16)
        acc = acc + jnp.dot(a, wd_ref[c], preferred_element_type=f32)
    o_ref[0] = acc


def _merge_ffn(x, om, of, od, oh, hg, p, tm):
    B, S, _ = x.shape
    row = lambda w: pl.BlockSpec((1, tm, w), lambda b, i: (b, i, 0))
    consts = (p["ln_mix"], p["wgate"], p["wbm"], p["wbf"], p["wbd"], p["wbh"], p["wout"], p["gon"], p["g64"],
              p["ln_ffn"], p["wgu"], p["wd"])
    return pl.pallas_call(
        _merge_ffn_kernel,
        grid=(B, S // tm),
        in_specs=[row(D_MODEL)] + [row(BRANCH)] * 6
                 + [_const_spec(c.shape) for c in consts],
        out_specs=row(D_MODEL),
        out_shape=jax.ShapeDtypeStruct((B, S, D_MODEL), f32),
        compiler_params=_cparams(("parallel", "parallel")),
        name="merge_ffn",
    )(x, om, of, od, oh[0], oh[1], hg, *consts)


def _pad_heads(w, heads, width):
    lead = w.shape[:-1]
    w = w.reshape(lead + (heads, width))
    w = jnp.pad(w, [(0, 0)] * len(lead) + [(0, 0), (0, LANE - width)])
    return w.reshape(lead + (heads * LANE,))


def _rope_tables(S):
    def cs(dim):
        inv = 1.0 / (ROPE_THETA ** (jnp.arange(0, dim, 2, dtype=f32) / dim))
        ang = jnp.arange(S, dtype=f32)[:, None] * inv[None, :]
        return jnp.cos(ang), jnp.sin(ang)

    cm, sm = cs(MLA_ROPE)
    one, zero = jnp.ones((S, MLA_NOPE), f32), jnp.zeros((S, MLA_NOPE), f32)
    z16, z32, o32 = jnp.zeros((S, 16), f32), jnp.zeros((S, 32), f32), jnp.ones((S, 32), f32)
    rope_m = jnp.stack([jnp.concatenate([one, cm, cm, o32], 1),
                        jnp.concatenate([zero, -sm, z16, z32], 1),
                        jnp.concatenate([zero, z16, sm, z32], 1)])
    cd, sd = cs(DIFF_HEAD_DIM)
    rope_d = jnp.stack([jnp.tile(jnp.concatenate([cd, cd], 1), (1, 4)),
                        jnp.tile(jnp.concatenate([-sd, z16], 1), (1, 4)),
                        jnp.tile(jnp.concatenate([z16, sd], 1), (1, 4))])
    return rope_m, rope_d


def _block_mean(width, group):
    c = np.arange(width)
    return jnp.asarray(((c[:, None] // group) == (c[None, :] // group)) / group, f32).astype(bf16)


def _layer_params(l, a, lower_bounds):
    w = a["w_in"][l]
    offs = np.cumsum([0, 384, 128, 32, 256, 256, 256, 256, 256, 256, 256, 256, 256, 4096])
    col = lambda i: w[:, offs[i]:offs[i + 1]]
    wcat = jnp.concatenate([col(0), col(1), jnp.pad(col(2), ((0, 0), (MLA_NOPE, LANE - MLA_QK))), col(3), col(4),
                            col(5), _pad_heads(col(6), DIFF_HEADS, 2 * DIFF_HEAD_DIM), col(7), col(8), col(9),
                            col(10), col(11)], axis=1).astype(bf16)
    wukv = a["mla_w_ukv"][l].reshape(MLA_KV_LORA, MLA_HEADS, MLA_NOPE + MLA_V)
    wukv = jnp.concatenate([_pad_heads(wukv[:, :, :MLA_NOPE].reshape(MLA_KV_LORA, -1), MLA_HEADS, MLA_NOPE),
                            _pad_heads(wukv[:, :, MLA_NOPE:].reshape(MLA_KV_LORA, -1), MLA_HEADS, MLA_V)], axis=1)
    lam_init = 0.8 - 0.6 * math.exp(-0.3 * l)
    lam = (jnp.exp(jnp.sum(a["diff_lq1"][l] * a["diff_lk1"][l])) - jnp.exp(jnp.sum(a["diff_lq2"][l] * a["diff_lk2"][l]))
           + lam_init)
    wb = a["w_branch"][l]
    nc = D_FF // FF_CHUNK
    row = lambda v: v.reshape(1, -1).astype(f32)
    return {
        "ln_mix": row(a["ln_mix"][l]),
        "wcat": wcat,
        "gqa": row(a["mla_g_qa"][l]),
        "wuq": _pad_heads(a["mla_w_uq"][l], MLA_HEADS, MLA_QK).astype(bf16),
        "gkva": row(a["mla_g_kva"][l]),
        "wukv": wukv.astype(bf16),
        "gqn": row(jnp.pad(a["mla_g_qn"][l], (0, LANE - MLA_QK))) * (MLA_QK ** -0.5 * LOG2E),
        "gkn": row(jnp.pad(a["mla_g_kn"][l], (0, LANE - MLA_QK))),
        "gdq": row(jnp.tile(a["diff_g_qn"][l], 2 * DIFF_HEADS)) * (DIFF_HEAD_DIM ** -0.5 * LOG2E),
        "gdk": row(jnp.tile(a["diff_g_kn"][l], 2 * DIFF_HEADS)),
        "g32": _block_mean(BRANCH, DIFF_HEAD_DIM),
        "lb": lower_bounds[:, l].astype(f32),
        "lam": lam.reshape(1).astype(f32),
        "mla_bound": MLA_QK ** 0.5 * LOG2E * jnp.max(jnp.abs(a["mla_g_qn"][l])) * jnp.max(jnp.abs(a["mla_g_kn"][l])),
        "diff_bound": (DIFF_HEAD_DIM ** 0.5 * LOG2E * jnp.max(jnp.abs(a["diff_g_qn"][l]))
                       * jnp.max(jnp.abs(a["diff_g_kn"][l]))),
        "gsub": row(a["diff_g_sub"][l]) * (1.0 - lam_init),
        "wgate": col(12).astype(bf16),
        "wbm": wb[0].astype(bf16),
        "wbf": wb[1].astype(bf16),
        "wbd": wb[2].astype(bf16),
        "wbh": wb[3].astype(bf16),
        "wout": a["w_out"][l].astype(bf16),
        "gon": row(jnp.tile(a["hgrn_g_on"][l], HGRN_HEADS)),
        "g64": _block_mean(BRANCH, HGRN_DIM),
        "ln_ffn": row(a["ln_ffn"][l]),
        "wgu": a["w_gate_up"][l].astype(bf16),
        "wd": a["w_down"][l].reshape(nc, FF_CHUNK, D_MODEL).astype(bf16),
    }


def _tiles(S):
    return {"tm_in": min(256, S),"tq": min(512, S), "tk": min(512, S), "tqb": min(512, S), "tkb": min(1024, S),
            "hgrn_chunk": min(64, S),
            "tm_merge": min(512, S)}


def kernel(x, ln_mix, w_in, mla_g_qa, mla_w_uq, mla_g_kva, mla_w_ukv, mla_g_qn, mla_g_kn, diff_g_qn, diff_g_kn,
           diff_lq1, diff_lk1, diff_lq2, diff_lk2, diff_g_sub, hgrn_lb_logits, hgrn_g_on, w_branch, w_out, ln_ffn,
           w_gate_up, w_down):
    a = dict(ln_mix=ln_mix, w_in=w_in, mla_g_qa=mla_g_qa, mla_w_uq=mla_w_uq, mla_g_kva=mla_g_kva,
             mla_w_ukv=mla_w_ukv, mla_g_qn=mla_g_qn, mla_g_kn=mla_g_kn, diff_g_qn=diff_g_qn, diff_g_kn=diff_g_kn,
             diff_lq1=diff_lq1, diff_lk1=diff_lk1, diff_lq2=diff_lq2, diff_lk2=diff_lk2, diff_g_sub=diff_g_sub,
             hgrn_g_on=hgrn_g_on, w_branch=w_branch, w_out=w_out, ln_ffn=ln_ffn, w_gate_up=w_gate_up, w_down=w_down)
    S = x.shape[1]
    t = _tiles(S)
    rope_m, rope_d = _rope_tables(S)
    lb_p = jax.nn.softmax(hgrn_lb_logits.astype(f32), axis=1)
    lower_bounds = jnp.cumsum(lb_p, axis=1) - lb_p[:, :1]
    fnet_consts = _fnet_consts(S)
    hgrn_consts = _hgrn_consts(t["hgrn_chunk"])
    for l in range(DEPTH):
        p = _layer_params(l, a, lower_bounds)
        qm, km, vm, fn, qd, kd, vd, hq, hv, hk, hlf, hg = _in_proj(x, p, rope_m, rope_d, t["tm_in"])
        om = lax.cond(p["mla_bound"] <= SCORE_BOUND_LOG2,
                      lambda: _mla_attn_bounded(qm, km, vm, t["tqb"], t["tkb"]),
                      lambda: _mla_attn(qm, km, vm, t["tq"], t["tk"]))
        od = lax.cond(p["diff_bound"] <= SCORE_BOUND_LOG2,
                      lambda: _diff_attn_bounded(p["lam"], qd, kd, vd, p["gsub"], t["tqb"], t["tkb"]),
                      lambda: _diff_attn(p["lam"], qd, kd, vd, p["gsub"], t["tq"], t["tk"]))
        of = _fnet(fn, fnet_consts)
        oh = _hgrn(hq, hv, hk, hlf, hgrn_consts, t["hgrn_chunk"])
        x = _merge_ffn(x, om, of, od, oh, hg, p, t["tm_merge"])
    return x
```

```python
import functools
import math

import numpy as np
import jax
import jax.numpy as jnp
from jax import lax
from jax.experimental import pallas as pl
from jax.experimental.pallas import tpu as pltpu

f32 = jnp.float32
bf16 = jnp.bfloat16

D_MODEL = 1024
DEPTH = 2
MLA_HEADS = 4
MLA_Q_LORA = 384
MLA_KV_LORA = 128
MLA_NOPE = 64
MLA_ROPE = 32
MLA_V = 64
MLA_QK = MLA_NOPE + MLA_ROPE
DIFF_HEADS = 4
DIFF_HEAD_DIM = 32
HGRN_HEADS = 4
HGRN_DIM = 64
BRANCH = 256
D_FF = 2816
FF_CHUNK = 256
ROPE_THETA = 10000.0
EPS = 1e-6
LOG2E = 1.4426950408889634
LANE = 128
FNET_N2 = 64
FNET_K1_BLOCK = 8
HGRN_CHUNKS_PER_STEP = 8
IN_PROJ_CHAINS_IN_FLIGHT = 4
ATTN_KEY_SUB = 256
VT_ROWS = 80
ATTN_CHAINS_IN_FLIGHT = 32
VMEM_LIMIT = 56 * 1024 * 1024
SCORE_BOUND_LOG2 = 60.0

_SEG = {}
_off = 0
for _name, _w in (("cq", 384), ("ckv", 128), ("krope", 128), ("fnet", 256), ("dq", 256), ("dk", 256),
                  ("dv", 512), ("hq", 256), ("hi", 256), ("hff", 256), ("hfb", 256), ("hg", 256)):
    _SEG[_name] = (_off, _off + _w)
    _off += _w
W_CAT = _off


def _cparams(sem):
    return pltpu.CompilerParams(dimension_semantics=sem, vmem_limit_bytes=VMEM_LIMIT)


def _const_spec(shape):
    nd = len(shape)
    return pl.BlockSpec(shape, lambda *_: (0,) * nd, pipeline_mode=pl.Buffered(1))


def _rms(x, g):
    return x * lax.rsqrt(jnp.mean(x * x, axis=-1, keepdims=True) + EPS) * g


def _sigmoid(x):
    return 1.0 / (1.0 + jnp.exp(-x))


def _rope(t, tab_ref):
    return t * tab_ref[0] + pltpu.roll(t, LANE - 16, 1) * tab_ref[1] + pltpu.roll(t, 16, 1) * tab_ref[2]


def _in_proj_kernel(x_ref, ln_ref, wcat_ref, gqa_ref, wuq_ref, gkva_ref, wukv_ref, gqn_ref, gkn_ref,
                    ropem_ref, gdq_ref, gdk_ref, g32_ref, roped_ref, lb_ref,
                    qm_ref, km_ref, vm_ref, fn_ref, qd_ref, kd_ref, vd_ref, hq_ref, hv_ref, hk_ref,
                    hlf_ref, hg_ref):
    h = _rms(x_ref[0], ln_ref[...]).astype(bf16)

    def seg(name):
        a, b = _SEG[name]
        return jnp.dot(h, wcat_ref[:, a:b], preferred_element_type=f32)

    ones_lane = (lax.broadcasted_iota(jnp.int32, (1, LANE), 1) == MLA_V).astype(f32)


    def mla_q():
        cq = seg("cq")
        yield
        q = jnp.dot(_rms(cq, gqa_ref[...]).astype(bf16), wuq_ref[...], preferred_element_type=f32)
        yield
        for hh in range(MLA_HEADS):
            sl = slice(hh * LANE, (hh + 1) * LANE)
            qh = q[:, sl]
            ms = jnp.sum(qh * qh, axis=-1, keepdims=True) * (1.0 / MLA_QK)
            qm_ref[0, :, sl] = _rope(qh * lax.rsqrt(ms + EPS) * gqn_ref[...], ropem_ref).astype(bf16)

    def mla_kv():
        ckv = seg("ckv")
        krope = seg("krope")
        yield
        kv = jnp.dot(_rms(ckv, gkva_ref[...]).astype(bf16), wukv_ref[...], preferred_element_type=f32)
        yield
        for hh in range(MLA_HEADS):
            sl = slice(hh * LANE, (hh + 1) * LANE)
            kh = kv[:, sl] + krope
            ms = jnp.sum(kh * kh, axis=-1, keepdims=True) * (1.0 / MLA_QK)
            km_ref[0, :, sl] = _rope(kh * lax.rsqrt(ms + EPS) * gkn_ref[...], ropem_ref).astype(bf16)
            vsl = slice((MLA_HEADS + hh) * LANE, (MLA_HEADS + hh + 1) * LANE)
            vm_ref[0, :, sl] = (kv[:, vsl] + ones_lane).astype(bf16)

    def diff_qk(name, g_ref, o_ref):
        t = seg(name)
        yield
        ms = jnp.dot((t * t).astype(bf16), g32_ref[...], preferred_element_type=f32)
        yield
        t = t * lax.rsqrt(ms + EPS) * g_ref[...]
        for c in range(BRANCH // LANE):
            sl = slice(c * LANE, (c + 1) * LANE)
            o_ref[0, :, sl] = _rope(t[:, sl], roped_ref).astype(bf16)

    def diff_v():
        dv = seg("dv")
        yield
        for hh in range(DIFF_HEADS):
            sl = slice(hh * LANE, (hh + 1) * LANE)
            vd_ref[0, :, sl] = (dv[:, sl] + ones_lane).astype(bf16)

    def plain(name, o_ref):
        z = seg(name)
        yield
        o_ref[0] = z.astype(bf16)

    def hgrn_gate(d, name):
        z = seg(name)
        yield
        lb = lb_ref[d:d + 1, :]
        f = lb + (1.0 - lb) * _sigmoid(z)
        hk_ref[0, :, d * BRANCH:(d + 1) * BRANCH] = (1.0 - f).astype(bf16)
        lf = jnp.log(f)
        hi = lf.astype(bf16)
        hlf_ref[0, :, 2 * d * BRANCH:(2 * d + 1) * BRANCH] = hi
        hlf_ref[0, :, (2 * d + 1) * BRANCH:(2 * d + 2) * BRANCH] = (lf - hi.astype(f32)).astype(bf16)

    _round_robin([mla_q(), hgrn_gate(0, "hff"), mla_kv(), hgrn_gate(1, "hfb"), diff_qk("dq", gdq_ref, qd_ref),
                  plain("fnet", fn_ref), diff_qk("dk", gdk_ref, kd_ref), plain("hq", hq_ref), diff_v(),
                  plain("hi", hv_ref), plain("hg", hg_ref)], IN_PROJ_CHAINS_IN_FLIGHT)


def _in_proj(x, p, rope_m, rope_d, tm):
    B, S, _ = x.shape
    row = lambda w: pl.BlockSpec((1, tm, w), lambda b, i: (b, i, 0))
    tab = pl.BlockSpec((3, tm, LANE), lambda b, i: (0, i, 0))
    out_w = (512, 512, 512, 256, 256, 256, 512, 256, 256, 512, 1024, 256)
    consts = (p["ln_mix"], p["wcat"], p["gqa"], p["wuq"], p["gkva"], p["wukv"], p["gqn"], p["gkn"])
    consts2 = (p["gdq"], p["gdk"], p["g32"])
    in_specs = ([row(D_MODEL)] + [_const_spec(c.shape) for c in consts] + [tab]
                + [_const_spec(c.shape) for c in consts2] + [tab, _const_spec(p["lb"].shape)])
    return pl.pallas_call(
        _in_proj_kernel,
        grid=(B, S // tm),
        in_specs=in_specs,
        out_specs=[row(w) for w in out_w],
        out_shape=[jax.ShapeDtypeStruct((B, S, w), bf16) for w in out_w],
        compiler_params=_cparams(("parallel", "parallel")),
        name="in_proj",
    )(x, *consts, rope_m, *consts2, rope_d, p["lb"])


def _online_softmax_step(s, v, m_ref, acc_ref, idx):
    m_prev = m_ref[idx]
    m_new = jnp.maximum(m_prev, jnp.max(s, axis=1, keepdims=True))
    alpha = jnp.exp2(m_prev - m_new)
    p = jnp.exp2((s - m_new[:, :1]).astype(bf16))
    acc_ref[idx] = alpha * acc_ref[idx] + jnp.dot(p, v, preferred_element_type=f32)
    m_ref[idx] = m_new


_NT = (((1,), (1,)), ((), ()))
_TN = (((0,), (0,)), ((), ()))


def _mla_attn_kernel(q_ref, k_ref, v_ref, o_ref, m_ref, acc_ref):
    j = pl.program_id(2)

    @pl.when(j == 0)
    def _():
        m_ref[...] = jnp.full(m_ref.shape, -jnp.inf, f32)
        acc_ref[...] = jnp.zeros(acc_ref.shape, f32)

    for hh in range(MLA_HEADS):
        sl = slice(hh * LANE, (hh + 1) * LANE)
        s = lax.dot_general(q_ref[0, :, sl], k_ref[0, :, sl], _NT, preferred_element_type=f32)
        _online_softmax_step(s, v_ref[0, :, sl], m_ref, acc_ref, hh)

    @pl.when(j == pl.num_programs(2) - 1)
    def _():
        _mla_finalize(acc_ref, o_ref)


def _mla_finalize(acc_ref, o_ref, transposed=False):
    for hh in range(MLA_HEADS):
        acc = acc_ref[hh].T if transposed else acc_ref[hh]
        o = acc[:, :MLA_V] * (1.0 / acc[:, MLA_V:MLA_V + 1])
        o_ref[0, :, hh * MLA_V:(hh + 1) * MLA_V] = o.astype(bf16)


def _diff_finalize(lam_ref, gsub_ref, acc_ref, o_ref, transposed=False):
    lam = lam_ref[0]
    dv = 2 * DIFF_HEAD_DIM
    for hh in range(DIFF_HEADS):
        a1 = acc_ref[2 * hh].T if transposed else acc_ref[2 * hh]
        a2 = acc_ref[2 * hh + 1].T if transposed else acc_ref[2 * hh + 1]
        o = a1[:, :dv] * (1.0 / a1[:, dv:dv + 1]) - a2[:, :dv] * (lam / a2[:, dv:dv + 1])
        ms = jnp.mean(o * o, axis=-1, keepdims=True)
        o_ref[0, :, hh * dv:(hh + 1) * dv] = (o * lax.rsqrt(ms + EPS) * gsub_ref[...]).astype(bf16)


def _diff_attn_kernel(lam_ref, q_ref, k_ref, v_ref, gsub_ref, o_ref, m_ref, acc_ref):
    j = pl.program_id(2)

    @pl.when(j == 0)
    def _():
        m_ref[...] = jnp.full(m_ref.shape, -jnp.inf, f32)
        acc_ref[...] = jnp.zeros(acc_ref.shape, f32)

    q = q_ref[0]
    k = k_ref[0]
    group = lax.broadcasted_iota(jnp.int32, (1, BRANCH), 1) // DIFF_HEAD_DIM
    for g in range(2 * DIFF_HEADS):
        qg = jnp.where(group == g, q, jnp.zeros_like(q))
        s = lax.dot_general(qg, k, _NT, preferred_element_type=f32)
        hh = g // 2
        _online_softmax_step(s, v_ref[0, :, hh * LANE:(hh + 1) * LANE], m_ref, acc_ref, g)

    @pl.when(j == pl.num_programs(2) - 1)
    def _():
        _diff_finalize(lam_ref, gsub_ref, acc_ref, o_ref)


def _round_robin(chains, in_flight):
    pending, active = list(chains), []
    while pending or active:
        while pending and len(active) < in_flight:
            active.append(pending.pop(0))
        for c in list(active):
            if next(c, "done") == "done":
                active.remove(c)


def _bounded_attn_loop(maps, k_ref, v_ref, acc_ref, tk):
    acc_ref[...] = jnp.zeros(acc_ref.shape, f32)

    def kv_block(j, carry):
        def chain(g, r0):
            q, ksl, vsl = maps[g]
            keys = pl.ds(pl.multiple_of(j * tk + r0, ATTN_KEY_SUB), ATTN_KEY_SUB)
            st = lax.dot_general(k_ref[0, keys, ksl], q, _NT, preferred_element_type=f32)
            yield
            pt = jnp.exp2(st).astype(bf16)
            yield
            acc_ref[g] += lax.dot_general(v_ref[0, keys, vsl], pt, _TN, preferred_element_type=f32)

        _round_robin([chain(g, r0) for r0 in range(0, tk, ATTN_KEY_SUB) for g in range(len(maps))],
                     ATTN_CHAINS_IN_FLIGHT)
        return carry

    lax.fori_loop(0, k_ref.shape[1] // tk, kv_block, 0)


def _mla_attn_bounded_kernel(q_ref, k_ref, v_ref, o_ref, acc_ref, *, tk):
    maps = [(q_ref[0, :, hh * LANE:(hh + 1) * LANE], slice(hh * LANE, (hh + 1) * LANE),
             slice(hh * LANE, hh * LANE + VT_ROWS)) for hh in range(MLA_HEADS)]
    _bounded_attn_loop(maps, k_ref, v_ref, acc_ref, tk)
    _mla_finalize(acc_ref, o_ref, transposed=True)


def _diff_attn_bounded_kernel(lam_ref, q_ref, k_ref, v_ref, gsub_ref, o_ref, acc_ref, *, tk):
    q = q_ref[0]
    group = lax.broadcasted_iota(jnp.int32, (1, BRANCH), 1) // DIFF_HEAD_DIM
    maps = [(jnp.where(group == g, q, jnp.zeros_like(q)), slice(0, BRANCH),
             slice((g // 2) * LANE, (g // 2) * LANE + VT_ROWS)) for g in range(2 * DIFF_HEADS)]
    _bounded_attn_loop(maps, k_ref, v_ref, acc_ref, tk)
    _diff_finalize(lam_ref, gsub_ref, acc_ref, o_ref, transposed=True)


def _attn_bounded_call(kernel_fn, n_acc, name, args, in_specs, S, W, B, tq):
    return pl.pallas_call(
        kernel_fn,
        grid=(B, S // tq),
        in_specs=in_specs,
        out_specs=pl.BlockSpec((1, tq, BRANCH), lambda b, i: (b, i, 0)),
        out_shape=jax.ShapeDtypeStruct((B, S, BRANCH), bf16),
        scratch_shapes=[pltpu.VMEM((n_acc, VT_ROWS, tq), f32)],
        compiler_params=_cparams(("parallel", "parallel")),
        name=name,
    )(*args)


def _mla_attn_bounded(q, k, v, tq, tk):
    B, S, W = q.shape
    full = pl.BlockSpec((1, S, W), lambda b, i: (b, 0, 0))
    in_specs = [pl.BlockSpec((1, tq, W), lambda b, i: (b, i, 0)), full, full]
    return _attn_bounded_call(functools.partial(_mla_attn_bounded_kernel, tk=tk), MLA_HEADS, "mla_attn_bounded",
                              (q, k, v), in_specs, S, W, B, tq)


def _diff_attn_bounded(lam, q, k, v, gsub, tq, tk):
    B, S, W = v.shape
    in_specs = [pl.BlockSpec(memory_space=pltpu.SMEM),
                pl.BlockSpec((1, tq, BRANCH), lambda b, i: (b, i, 0)),
                pl.BlockSpec((1, S, BRANCH), lambda b, i: (b, 0, 0)),
                pl.BlockSpec((1, S, W), lambda b, i: (b, 0, 0)),
                pl.BlockSpec((1, 2 * DIFF_HEAD_DIM), lambda b, i: (0, 0))]
    return _attn_bounded_call(functools.partial(_diff_attn_bounded_kernel, tk=tk), 2 * DIFF_HEADS,
                              "diff_attn_bounded", (lam, q, k, v, gsub), in_specs, S, W, B, tq)


def _mla_attn(q, k, v, tq, tk):
    B, S, W = q.shape
    return pl.pallas_call(
        _mla_attn_kernel,
        grid=(B, S // tq, S // tk),
        in_specs=[pl.BlockSpec((1, tq, W), lambda b, i, j: (b, i, 0)),
                  pl.BlockSpec((1, tk, W), lambda b, i, j: (b, j, 0)),
                  pl.BlockSpec((1, tk, W), lambda b, i, j: (b, j, 0))],
        out_specs=pl.BlockSpec((1, tq, BRANCH), lambda b, i, j: (b, i, 0)),
        out_shape=jax.ShapeDtypeStruct((B, S, BRANCH), bf16),
        scratch_shapes=[pltpu.VMEM((MLA_HEADS, tq, LANE), f32), pltpu.VMEM((MLA_HEADS, tq, LANE), f32)],
        compiler_params=_cparams(("parallel", "parallel", "arbitrary")),
        name="mla_attn",
    )(q, k, v)


def _diff_attn(lam, q, k, v, gsub, tq, tk):
    B, S, W = v.shape
    return pl.pallas_call(
        _diff_attn_kernel,
        grid=(B, S // tq, S // tk),
        in_specs=[pl.BlockSpec(memory_space=pltpu.SMEM),
                  pl.BlockSpec((1, tq, BRANCH), lambda b, i, j: (b, i, 0)),
                  pl.BlockSpec((1, tk, BRANCH), lambda b, i, j: (b, j, 0)),
                  pl.BlockSpec((1, tk, W), lambda b, i, j: (b, j, 0)),
                  pl.BlockSpec((1, 2 * DIFF_HEAD_DIM), lambda b, i, j: (0, 0))],
        out_specs=pl.BlockSpec((1, tq, BRANCH), lambda b, i, j: (b, i, 0)),
        out_shape=jax.ShapeDtypeStruct((B, S, BRANCH), bf16),
        scratch_shapes=[pltpu.VMEM((2 * DIFF_HEADS, tq, LANE), f32), pltpu.VMEM((2 * DIFF_HEADS, tq, LANE), f32)],
        compiler_params=_cparams(("parallel", "parallel", "arbitrary")),
        name="diff_attn",
    )(lam, q, k, v, gsub)


def _fnet1_kernel(x_ref, w_ref, a_ref):
    a_ref[0] = jnp.dot(w_ref[...], x_ref[0], preferred_element_type=f32).astype(bf16)


def _fnet2_kernel(a_ref, t_ref, cs_ref, o_ref):
    zs = []
    for jj in range(FNET_K1_BLOCK):
        a = jnp.concatenate([a_ref[0, 0, jj], a_ref[0, 1, jj]], axis=0)
        zs.append(jnp.dot(t_ref[jj], a, preferred_element_type=f32))
    zc = jnp.concatenate([jnp.concatenate([z[:FNET_N2], z[FNET_N2:]], axis=1) for z in zs], axis=0).astype(bf16)
    y = jnp.dot(zc, cs_ref[...], preferred_element_type=f32).astype(bf16)
    for jj in range(FNET_K1_BLOCK):
        o_ref[0, :, jj, :] = y[jj * FNET_N2:(jj + 1) * FNET_N2]


def _fnet_consts(S):
    n1 = S // FNET_N2
    a = np.arange(n1)
    ang1 = 2.0 * np.pi * np.outer(a, a) / n1
    w1 = np.concatenate([np.cos(ang1), -np.sin(ang1)], axis=0)
    n2 = np.arange(FNET_N2)
    phi = 2.0 * np.pi * (n2[None, None, :] * a[:, None, None] / S + n2[None, None, :] * n2[None, :, None] / FNET_N2)
    tr, ti = np.cos(phi), -np.sin(phi)
    t = np.concatenate([np.concatenate([tr, -ti], axis=2), np.concatenate([ti, tr], axis=2)], axis=1)
    c = np.arange(BRANCH)
    same = (c[:, None] // FNET_N2) == (c[None, :] // FNET_N2)
    angc = 2.0 * np.pi * np.outer(c % FNET_N2, c % FNET_N2) / FNET_N2
    norm = 1.0 / math.sqrt(S * FNET_N2)
    cs = np.concatenate([np.where(same, np.cos(angc), 0.0), np.where(same, np.sin(angc), 0.0)], axis=0) * norm
    return (jnp.asarray(w1, f32).astype(bf16), jnp.asarray(t, f32).astype(bf16), jnp.asarray(cs, f32).astype(bf16))


def _fnet(u, consts):
    B, S, W = u.shape
    w1, t, cs = consts
    n1 = S // FNET_N2
    cols = FNET_N2 * W
    tn = min(cols, 4096)
    a = pl.pallas_call(
        _fnet1_kernel,
        grid=(B, cols // tn),
        in_specs=[pl.BlockSpec((1, n1, tn), lambda b, i: (b, 0, i)), _const_spec(w1.shape)],
        out_specs=pl.BlockSpec((1, 2 * n1, tn), lambda b, i: (b, 0, i)),
        out_shape=jax.ShapeDtypeStruct((B, 2 * n1, cols), bf16),
        compiler_params=_cparams(("parallel", "parallel")),
        name="fnet1",
    )(u.reshape(B, n1, cols), w1)
    kb = FNET_K1_BLOCK
    y = pl.pallas_call(
        _fnet2_kernel,
        grid=(B, n1 // kb),
        in_specs=[pl.BlockSpec((1, 2, kb, FNET_N2, W), lambda b, i: (b, 0, i, 0, 0)),
                  pl.BlockSpec((kb, 2 * FNET_N2, 2 * FNET_N2), lambda b, i: (i, 0, 0)),
                  _const_spec(cs.shape)],
        out_specs=pl.BlockSpec((1, FNET_N2, kb, W), lambda b, i: (b, 0, i, 0)),
        out_shape=jax.ShapeDtypeStruct((B, FNET_N2, n1, W), bf16),
        compiler_params=_cparams(("parallel", "parallel")),
        name="fnet2",
    )(a.reshape(B, 2, n1, FNET_N2, W), t, cs)
    return y.reshape(B, S, W)


def _hgrn_consts(C):
    L = int(math.log2(C))
    t = np.arange(C)
    m = np.zeros((2, (L + 2) * C, C), np.float32)
    lvl = np.full((2, C, C), -1, np.int32)
    for d in (0, 1):
        for li in range(L):
            n = C >> li
            half = n // 2
            blk = t // n
            mid = blk * n + half
            upper = (t % n) >= half
            for r in range(C):
                if d == 0:
                    if upper[r]:
                        m[d, li * C + r, mid[r]:r + 1] = 1.0
                    else:
                        m[d, li * C + r, r + 1:mid[r]] = 1.0
                else:
                    if upper[r]:
                        m[d, li * C + r, mid[r]:r] = 1.0
                    else:
                        m[d, li * C + r, r:mid[r]] = 1.0
            same = blk[:, None] == blk[None, :]
            if d == 0:
                msk = same & upper[:, None] & (~upper)[None, :]
            else:
                msk = same & (~upper)[:, None] & upper[None, :]
            lvl[d][msk] = li
        for r in range(C):
            if d == 0:
                m[d, L * C + r, :r + 1] = 1.0
                m[d, (L + 1) * C + r, r + 1:] = 1.0
            else:
                m[d, L * C + r, r:] = 1.0
                m[d, (L + 1) * C + r, :r] = 1.0
    c = np.arange(BRANCH)
    headsum = ((c[:, None] // HGRN_DIM) == (c[None, :] // HGRN_DIM)).astype(np.float32)
    lvl = np.tile(lvl, (1, 1, HGRN_HEADS))
    m = np.tile(m, (1, 1, 2))
    return jnp.asarray(m, f32).astype(bf16), jnp.asarray(lvl), jnp.asarray(headsum, f32).astype(bf16)


def _hgrn_kernel(qf_ref, vf_ref, kf_ref, lff_ref, qb_ref, vb_ref, kb_ref, lfb_ref, m_ref, lvl_ref, hs_ref,
                 of_ref, ob_ref, st_ref, *, C, L):
    @pl.when(pl.program_id(1) == 0)
    def _():
        st_ref[...] = jnp.zeros(st_ref.shape, f32)

    refs = ((qf_ref, vf_ref, kf_ref, lff_ref, of_ref), (qb_ref, vb_ref, kb_ref, lfb_ref, ob_ref))
    head = lax.broadcasted_iota(jnp.int32, (1, BRANCH), 1) // HGRN_DIM
    r = lax.broadcasted_iota(jnp.int32, (BRANCH, BRANCH), 0) // HGRN_DIM
    cidx = lax.broadcasted_iota(jnp.int32, (BRANCH, BRANCH), 1) // HGRN_DIM
    n_chunks = qf_ref.shape[1] // C
    order = (tuple(range(n_chunks)), tuple(reversed(range(n_chunks))))
    chains = [(d, order[d][j]) for j in range(n_chunks) for d in (0, 1)]
    rows = {c: slice(c[1] * C, (c[1] + 1) * C) for c in chains}
    q = {c: refs[c[0]][0][0, rows[c], :].astype(f32) for c in chains}
    v = {c: refs[c[0]][1][0, rows[c], :] for c in chains}
    k = {c: refs[c[0]][2][0, rows[c], :].astype(f32) for c in chains}
    e = {}
    for c in chains:
        lf = refs[c[0]][3][0, rows[c], :]
        lf2 = jnp.concatenate([lf[:, :BRANCH], lf[:, BRANCH:]], axis=0)
        e[c] = jnp.dot(m_ref[c[0]], lf2, preferred_element_type=f32)

    o = {}
    st = [st_ref[0], st_ref[1]]
    for c in chains:
        d = c[0]
        ein = jnp.exp(e[c][L * C:(L + 1) * C])
        o[c] = lax.dot_general((q[c] * ein).astype(bf16), st[d].astype(bf16), _NT, preferred_element_type=f32)
        kout = (k[c] * jnp.exp(e[c][(L + 1) * C:(L + 2) * C])).astype(bf16)
        upd = lax.dot_general(v[c], kout, _TN, preferred_element_type=f32)
        total = ein[C - 1:C] if d == 0 else ein[0:1]
        st[d] = st[d] * total + jnp.where(r == cidx, upd, 0.0)
        o[c] = o[c] + jnp.dot((q[c] * k[c]).astype(bf16), hs_ref[...], preferred_element_type=f32) * v[c].astype(f32)
    st_ref[0] = st[0]
    st_ref[1] = st[1]

    scores = {c: jnp.zeros((C, HGRN_HEADS * C), f32) for c in chains}
    for li in range(L):
        for c in chains:
            ex = jnp.exp(e[c][li * C:(li + 1) * C])
            qt = (q[c] * ex).astype(bf16)
            kt = (k[c] * ex).astype(bf16)
            kstack = jnp.concatenate([jnp.where(head == hh, kt, jnp.zeros_like(kt)) for hh in range(HGRN_HEADS)],
                                     axis=0)
            s = lax.dot_general(qt, kstack, _NT, preferred_element_type=f32)
            scores[c] = jnp.where(lvl_ref[c[0]] == li, s, scores[c])
    for c in chains:
        vstack = jnp.concatenate([jnp.where(head == hh, v[c], jnp.zeros_like(v[c])) for hh in range(HGRN_HEADS)],
                                 axis=0)
        oc = o[c] + jnp.dot(scores[c].astype(bf16), vstack, preferred_element_type=f32)
        refs[c[0]][4][0, rows[c], :] = oc.astype(bf16)


def _hgrn(hq, hv, hk, hlf, consts, C):
    B, S, W = hq.shape
    m, lvl, hs = consts
    L = int(math.log2(C))
    rows = min(S, HGRN_CHUNKS_PER_STEP * C)
    nb = S // rows
    fwd = lambda w, c: pl.BlockSpec((1, rows, w), lambda b, i: (b, i, c))
    bwd = lambda w, c: pl.BlockSpec((1, rows, w), lambda b, i: (b, nb - 1 - i, c))
    return pl.pallas_call(
        functools.partial(_hgrn_kernel, C=C, L=L),
        grid=(B, nb),
        in_specs=[fwd(W, 0), fwd(W, 0), fwd(W, 0), fwd(2 * W, 0), bwd(W, 0), bwd(W, 0), bwd(W, 1), bwd(2 * W, 1),
                  _const_spec(m.shape), _const_spec(lvl.shape), _const_spec(hs.shape)],
        out_specs=[fwd(W, 0), bwd(W, 0)],
        out_shape=[jax.ShapeDtypeStruct((B, S, W), bf16)] * 2,
        scratch_shapes=[pltpu.VMEM((2, BRANCH, BRANCH), f32)],
        compiler_params=_cparams(("parallel", "arbitrary")),
        name="hgrn",
    )(hq, hv, hk, hlf, hq, hv, hk, hlf, m, lvl, hs)


def _merge_ffn_kernel(x_ref, om_ref, of_ref, od_ref, ohf_ref, ohb_ref, hg_ref, ln_ref, wg_ref, wbm_ref, wbf_ref,
                      wbd_ref, wbh_ref, wo_ref, gon_ref, g64_ref, lnf_ref, wgu_ref, wd_ref, o_ref):
    x = x_ref[0]
    h = _rms(x, ln_ref[...]).astype(bf16)
    oh = ohf_ref[0].astype(f32) + ohb_ref[0].astype(f32)
    ms = jnp.dot((oh * oh).astype(bf16), g64_ref[...], preferred_element_type=f32)
    oh = (oh * lax.rsqrt(ms + EPS) * gon_ref[...] * _sigmoid(hg_ref[0].astype(f32))).astype(bf16)
    merged = None
    for n, (o_n, w_ref) in enumerate(((om_ref[0], wbm_ref), (of_ref[0], wbf_ref), (od_ref[0], wbd_ref), (oh, wbh_ref))):
        gate = _sigmoid(jnp.dot(h, wg_ref[:, n * D_MODEL:(n + 1) * D_MODEL], preferred_element_type=f32))
        y = gate * jnp.dot(o_n, w_ref[...], preferred_element_type=f32)
        merged = y if merged is None else merged + y
    x = x + jnp.dot(merged.astype(bf16), wo_ref[...], preferred_element_type=f32)

    h = _rms(x, lnf_ref[...]).astype(bf16)
    acc = x
    for c in range(D_FF // FF_CHUNK):
        g = jnp.dot(h, wgu_ref[:, c * FF_CHUNK:(c + 1) * FF_CHUNK], preferred_element_type=f32)
        u = jnp.dot(h, wgu_ref[:, D_FF + c * FF_CHUNK:D_FF + (c + 1) * FF_CHUNK], preferred_element_type=f32)
        a = (g * _sigmoid(g) * u).astype(bf16)
        acc = acc + jnp.dot(a, wd_ref[c], preferred_element_type=f32)
    o_ref[0] = acc


def _merge_ffn(x, om, of, od, oh, hg, p, tm):
    B, S, _ = x.shape
    row = lambda w: pl.BlockSpec((1, tm, w), lambda b, i: (b, i, 0))
    consts = (p["ln_mix"], p["wgate"], p["wbm"], p["wbf"], p["wbd"], p["wbh"], p["wout"], p["gon"], p["g64"],
              p["ln_ffn"], p["wgu"], p["wd"])
    return pl.pallas_call(
        _merge_ffn_kernel,
        grid=(B, S // tm),
        in_specs=[row(D_MODEL)] + [row(BRANCH)] * 6
                 + [_const_spec(c.shape) for c in consts],
        out_specs=row(D_MODEL),
        out_shape=jax.ShapeDtypeStruct((B, S, D_MODEL), f32),
        compiler_params=_cparams(("parallel", "parallel")),
        name="merge_ffn",
    )(x, om, of, od, oh[0], oh[1], hg, *consts)


def _pad_heads(w, heads, width):
    lead = w.shape[:-1]
    w = w.reshape(lead + (heads, width))
    w = jnp.pad(w, [(0, 0)] * len(lead) + [(0, 0), (0, LANE - width)])
    return w.reshape(lead + (heads * LANE,))


def _rope_tables(S):
    def cs(dim):
        inv = 1.0 / (ROPE_THETA ** (jnp.arange(0, dim, 2, dtype=f32) / dim))
        ang = jnp.arange(S, dtype=f32)[:, None] * inv[None, :]
        return jnp.cos(ang), jnp.sin(ang)

    cm, sm = cs(MLA_ROPE)
    one, zero = jnp.ones((S, MLA_NOPE), f32), jnp.zeros((S, MLA_NOPE), f32)
    z16, z32, o32 = jnp.zeros((S, 16), f32), jnp.zeros((S, 32), f32), jnp.ones((S, 32), f32)
    rope_m = jnp.stack([jnp.concatenate([one, cm, cm, o32], 1),
                        jnp.concatenate([zero, -sm, z16, z32], 1),
                        jnp.concatenate([zero, z16, sm, z32], 1)])
    cd, sd = cs(DIFF_HEAD_DIM)
    rope_d = jnp.stack([jnp.tile(jnp.concatenate([cd, cd], 1), (1, 4)),
                        jnp.tile(jnp.concatenate([-sd, z16], 1), (1, 4)),
                        jnp.tile(jnp.concatenate([z16, sd], 1), (1, 4))])
    return rope_m, rope_d


def _block_mean(width, group):
    c = np.arange(width)
    return jnp.asarray(((c[:, None] // group) == (c[None, :] // group)) / group, f32).astype(bf16)


def _layer_params(l, a, lower_bounds):
    w = a["w_in"][l]
    offs = np.cumsum([0, 384, 128, 32, 256, 256, 256, 256, 256, 256, 256, 256, 256, 4096])
    col = lambda i: w[:, offs[i]:offs[i + 1]]
    wcat = jnp.concatenate([col(0), col(1), jnp.pad(col(2), ((0, 0), (MLA_NOPE, LANE - MLA_QK))), col(3), col(4),
                            col(5), _pad_heads(col(6), DIFF_HEADS, 2 * DIFF_HEAD_DIM), col(7), col(8), col(9),
                            col(10), col(11)], axis=1).astype(bf16)
    wukv = a["mla_w_ukv"][l].reshape(MLA_KV_LORA, MLA_HEADS, MLA_NOPE + MLA_V)
    wukv = jnp.concatenate([_pad_heads(wukv[:, :, :MLA_NOPE].reshape(MLA_KV_LORA, -1), MLA_HEADS, MLA_NOPE),
                            _pad_heads(wukv[:, :, MLA_NOPE:].reshape(MLA_KV_LORA, -1), MLA_HEADS, MLA_V)], axis=1)
    lam_init = 0.8 - 0.6 * math.exp(-0.3 * l)
    lam = (jnp.exp(jnp.sum(a["diff_lq1"][l] * a["diff_lk1"][l])) - jnp.exp(jnp.sum(a["diff_lq2"][l] * a["diff_lk2"][l]))
           + lam_init)
    wb = a["w_branch"][l]
    nc = D_FF // FF_CHUNK
    row = lambda v: v.reshape(1, -1).astype(f32)
    return {
        "ln_mix": row(a["ln_mix"][l]),
        "wcat": wcat,
        "gqa": row(a["mla_g_qa"][l]),
        "wuq": _pad_heads(a["mla_w_uq"][l], MLA_HEADS, MLA_QK).astype(bf16),
        "gkva": row(a["mla_g_kva"][l]),
        "wukv": wukv.astype(bf16),
        "gqn": row(jnp.pad(a["mla_g_qn"][l], (0, LANE - MLA_QK))) * (MLA_QK ** -0.5 * LOG2E),
        "gkn": row(jnp.pad(a["mla_g_kn"][l], (0, LANE - MLA_QK))),
        "gdq": row(jnp.tile(a["diff_g_qn"][l], 2 * DIFF_HEADS)) * (DIFF_HEAD_DIM ** -0.5 * LOG2E),
        "gdk": row(jnp.tile(a["diff_g_kn"][l], 2 * DIFF_HEADS)),
        "g32": _block_mean(BRANCH, DIFF_HEAD_DIM),
        "lb": lower_bounds[:, l].astype(f32),
        "lam": lam.reshape(1).astype(f32),
        "mla_bound": MLA_QK ** 0.5 * LOG2E * jnp.max(jnp.abs(a["mla_g_qn"][l])) * jnp.max(jnp.abs(a["mla_g_kn"][l])),
        "diff_bound": (DIFF_HEAD_DIM ** 0.5 * LOG2E * jnp.max(jnp.abs(a["diff_g_qn"][l]))
                       * jnp.max(jnp.abs(a["diff_g_kn"][l]))),
        "gsub": row(a["diff_g_sub"][l]) * (1.0 - lam_init),
        "wgate": col(12).astype(bf16),
        "wbm": wb[0].astype(bf16),
        "wbf": wb[1].astype(bf16),
        "wbd": wb[2].astype(bf16),
        "wbh": wb[3].astype(bf16),
        "wout": a["w_out"][l].astype(bf16),
        "gon": row(jnp.tile(a["hgrn_g_on"][l], HGRN_HEADS)),
        "g64": _block_mean(BRANCH, HGRN_DIM),
        "ln_ffn": row(a["ln_ffn"][l]),
        "wgu": a["w_gate_up"][l].astype(bf16),
        "wd": a["w_down"][l].reshape(nc, FF_CHUNK, D_MODEL).astype(bf16),
    }


def _tiles(S):
    return {"tm_in": min(256, S),"tq": min(512, S), "tk": min(512, S), "tqb": min(512, S), "tkb": min(1024, S),
            "hgrn_chunk": min(64, S),
            "tm_merge": min(512, S)}


def kernel(x, ln_mix, w_in, mla_g_qa, mla_w_uq, mla_g_kva, mla_w_ukv, mla_g_qn, mla_g_kn, diff_g_qn, diff_g_kn,
           diff_lq1, diff_lk1, diff_lq2, diff_lk2, diff_g_sub, hgrn_lb_logits, hgrn_g_on, w_branch, w_out, ln_ffn,
           w_gate_up, w_down):
    a = dict(ln_mix=ln_mix, w_in=w_in, mla_g_qa=mla_g_qa, mla_w_uq=mla_w_uq, mla_g_kva=mla_g_kva,
             mla_w_ukv=mla_w_ukv, mla_g_qn=mla_g_qn, mla_g_kn=mla_g_kn, diff_g_qn=diff_g_qn, diff_g_kn=diff_g_kn,
             diff_lq1=diff_lq1, diff_lk1=diff_lk1, diff_lq2=diff_lq2, diff_lk2=diff_lk2, diff_g_sub=diff_g_sub,
             hgrn_g_on=hgrn_g_on, w_branch=w_branch, w_out=w_out, ln_ffn=ln_ffn, w_gate_up=w_gate_up, w_down=w_down)
    S = x.shape[1]
    t = _tiles(S)
    rope_m, rope_d = _rope_tables(S)
    lb_p = jax.nn.softmax(hgrn_lb_logits.astype(f32), axis=1)
    lower_bounds = jnp.cumsum(lb_p, axis=1) - lb_p[:, :1]
    fnet_consts = _fnet_consts(S)
    hgrn_consts = _hgrn_consts(t["hgrn_chunk"])
    for l in range(DEPTH):
        p = _layer_params(l, a, lower_bounds)
        qm, km, vm, fn, qd, kd, vd, hq, hv, hk, hlf, hg = _in_proj(x, p, rope_m, rope_d, t["tm_in"])
        om = lax.cond(p["mla_bound"] <= SCORE_BOUND_LOG2,
                      lambda: _mla_attn_bounded(qm, km, vm, t["tqb"], t["tkb"]),
                      lambda: _mla_attn(qm, km, vm, t["tq"], t["tk"]))
        od = lax.cond(p["diff_bound"] <= SCORE_BOUND_LOG2,
                      lambda: _diff_attn_bounded(p["lam"], qd, kd, vd, p["gsub"], t["tqb"], t["tkb"]),
                      lambda: _diff_attn(p["lam"], qd, kd, vd, p["gsub"], t["tq"], t["tk"]))
        of = _fnet(fn, fnet_consts)
        oh = _hgrn(hq, hv, hk, hlf, hgrn_consts, t["hgrn_chunk"])
        x = _merge_ffn(x, om, of, od, oh, hg, p, t["tm_merge"])
    return x
```

```python
import functools
import math

import numpy as np
import jax
import jax.numpy as jnp
from jax import lax
from jax.experimental import pallas as pl
from jax.experimental.pallas import tpu as pltpu

f32 = jnp.float32
bf16 = jnp.bfloat16

D_MODEL = 1024
DEPTH = 2
MLA_HEADS = 4
MLA_Q_LORA = 384
MLA_KV_LORA = 128
MLA_NOPE = 64
MLA_ROPE = 32
MLA_V = 64
MLA_QK = MLA_NOPE + MLA_ROPE
DIFF_HEADS = 4
DIFF_HEAD_DIM = 32
HGRN_HEADS = 4
HGRN_DIM = 64
BRANCH = 256
D_FF = 2816
FF_CHUNK = 256
ROPE_THETA = 10000.0
EPS = 1e-6
LOG2E = 1.4426950408889634
LANE = 128
FNET_N2 = 64
FNET_K1_BLOCK = 8
HGRN_CHUNKS_PER_STEP = 8
IN_PROJ_CHAINS_IN_FLIGHT = 4
ATTN_KEY_SUB = 512
VT_ROWS = 80
ATTN_CHAINS_IN_FLIGHT = 32
VMEM_LIMIT = 56 * 1024 * 1024
SCORE_BOUND_LOG2 = 60.0

_SEG = {}
_off = 0
for _name, _w in (("cq", 384), ("ckv", 128), ("krope", 128), ("fnet", 256), ("dq", 256), ("dk", 256),
                  ("dv", 512), ("hq", 256), ("hi", 256), ("hff", 256), ("hfb", 256), ("hg", 256)):
    _SEG[_name] = (_off, _off + _w)
    _off += _w
W_CAT = _off


def _cparams(sem):
    return pltpu.CompilerParams(dimension_semantics=sem, vmem_limit_bytes=VMEM_LIMIT)


def _const_spec(shape):
    nd = len(shape)
    return pl.BlockSpec(shape, lambda *_: (0,) * nd, pipeline_mode=pl.Buffered(1))


def _rms(x, g):
    return x * lax.rsqrt(jnp.mean(x * x, axis=-1, keepdims=True) + EPS) * g


def _sigmoid(x):
    return 1.0 / (1.0 + jnp.exp(-x))


def _rope(t, tab_ref):
    return t * tab_ref[0] + pltpu.roll(t, LANE - 16, 1) * tab_ref[1] + pltpu.roll(t, 16, 1) * tab_ref[2]


def _in_proj_kernel(x_ref, ln_ref, wcat_ref, gqa_ref, wuq_ref, gkva_ref, wukv_ref, gqn_ref, gkn_ref,
                    ropem_ref, gdq_ref, gdk_ref, g32_ref, roped_ref, lb_ref,
                    qm_ref, km_ref, vm_ref, fn_ref, qd_ref, kd_ref, vd_ref, hq_ref, hv_ref, hk_ref,
                    hlf_ref, hg_ref):
    h = _rms(x_ref[0], ln_ref[...]).astype(bf16)

    def seg(name):
        a, b = _SEG[name]
        return jnp.dot(h, wcat_ref[:, a:b], preferred_element_type=f32)

    ones_lane = (lax.broadcasted_iota(jnp.int32, (1, LANE), 1) == MLA_V).astype(f32)


    def mla_q():
        cq = seg("cq")
        yield
        q = jnp.dot(_rms(cq, gqa_ref[...]).astype(bf16), wuq_ref[...], preferred_element_type=f32)
        yield
        for hh in range(MLA_HEADS):
            sl = slice(hh * LANE, (hh + 1) * LANE)
            qh = q[:, sl]
            ms = jnp.sum(qh * qh, axis=-1, keepdims=True) * (1.0 / MLA_QK)
            qm_ref[0, :, sl] = _rope(qh * lax.rsqrt(ms + EPS) * gqn_ref[...], ropem_ref).astype(bf16)

    def mla_kv():
        ckv = seg("ckv")
        krope = seg("krope")
        yield
        kv = jnp.dot(_rms(ckv, gkva_ref[...]).astype(bf16), wukv_ref[...], preferred_element_type=f32)
        yield
        for hh in range(MLA_HEADS):
            sl = slice(hh * LANE, (hh + 1) * LANE)
            kh = kv[:, sl] + krope
            ms = jnp.sum(kh * kh, axis=-1, keepdims=True) * (1.0 / MLA_QK)
            km_ref[0, :, sl] = _rope(kh * lax.rsqrt(ms + EPS) * gkn_ref[...], ropem_ref).astype(bf16)
            vsl = slice((MLA_HEADS + hh) * LANE, (MLA_HEADS + hh + 1) * LANE)
            vm_ref[0, :, sl] = (kv[:, vsl] + ones_lane).astype(bf16)

    def diff_qk(name, g_ref, o_ref):
        t = seg(name)
        yield
        ms = jnp.dot((t * t).astype(bf16), g32_ref[...], preferred_element_type=f32)
        yield
        t = t * lax.rsqrt(ms + EPS) * g_ref[...]
        for c in range(BRANCH // LANE):
            sl = slice(c * LANE, (c + 1) * LANE)
            o_ref[0, :, sl] = _rope(t[:, sl], roped_ref).astype(bf16)

    def diff_v():
        dv = seg("dv")
        yield
        for hh in range(DIFF_HEADS):
            sl = slice(hh * LANE, (hh + 1) * LANE)
            vd_ref[0, :, sl] = (dv[:, sl] + ones_lane).astype(bf16)

    def plain(name, o_ref):
        z = seg(name)
        yield
        o_ref[0] = z.astype(bf16)

    def hgrn_gate(d, name):
        z = seg(name)
        yield
        lb = lb_ref[d:d + 1, :]
        f = lb + (1.0 - lb) * _sigmoid(z)
        hk_ref[0, :, d * BRANCH:(d + 1) * BRANCH] = (1.0 - f).astype(bf16)
        lf = jnp.log(f)
        hi = lf.astype(bf16)
        hlf_ref[0, :, 2 * d * BRANCH:(2 * d + 1) * BRANCH] = hi
        hlf_ref[0, :, (2 * d + 1) * BRANCH:(2 * d + 2) * BRANCH] = (lf - hi.astype(f32)).astype(bf16)

    _round_robin([mla_q(), hgrn_gate(0, "hff"), mla_kv(), hgrn_gate(1, "hfb"), diff_qk("dq", gdq_ref, qd_ref),
                  plain("fnet", fn_ref), diff_qk("dk", gdk_ref, kd_ref), plain("hq", hq_ref), diff_v(),
                  plain("hi", hv_ref), plain("hg", hg_ref)], IN_PROJ_CHAINS_IN_FLIGHT)


def _in_proj(x, p, rope_m, rope_d, tm):
    B, S, _ = x.shape
    row = lambda w: pl.BlockSpec((1, tm, w), lambda b, i: (b, i, 0))
    tab = pl.BlockSpec((3, tm, LANE), lambda b, i: (0, i, 0))
    out_w = (512, 512, 512, 256, 256, 256, 512, 256, 256, 512, 1024, 256)
    consts = (p["ln_mix"], p["wcat"], p["gqa"], p["wuq"], p["gkva"], p["wukv"], p["gqn"], p["gkn"])
    consts2 = (p["gdq"], p["gdk"], p["g32"])
    in_specs = ([row(D_MODEL)] + [_const_spec(c.shape) for c in consts] + [tab]
                + [_const_spec(c.shape) for c in consts2] + [tab, _const_spec(p["lb"].shape)])
    return pl.pallas_call(
        _in_proj_kernel,
        grid=(B, S // tm),
        in_specs=in_specs,
        out_specs=[row(w) for w in out_w],
        out_shape=[jax.ShapeDtypeStruct((B, S, w), bf16) for w in out_w],
        compiler_params=_cparams(("parallel", "parallel")),
        name="in_proj",
    )(x, *consts, rope_m, *consts2, rope_d, p["lb"])


def _online_softmax_step(s, v, m_ref, acc_ref, idx):
    m_prev = m_ref[idx]
    m_new = jnp.maximum(m_prev, jnp.max(s, axis=1, keepdims=True))
    alpha = jnp.exp2(m_prev - m_new)
    p = jnp.exp2((s - m_new[:, :1]).astype(bf16))
    acc_ref[idx] = alpha * acc_ref[idx] + jnp.dot(p, v, preferred_element_type=f32)
    m_ref[idx] = m_new


_NT = (((1,), (1,)), ((), ()))
_TN = (((0,), (0,)), ((), ()))


def _mla_attn_kernel(q_ref, k_ref, v_ref, o_ref, m_ref, acc_ref):
    j = pl.program_id(2)

    @pl.when(j == 0)
    def _():
        m_ref[...] = jnp.full(m_ref.shape, -jnp.inf, f32)
        acc_ref[...] = jnp.zeros(acc_ref.shape, f32)

    for hh in range(MLA_HEADS):
        sl = slice(hh * LANE, (hh + 1) * LANE)
        s = lax.dot_general(q_ref[0, :, sl], k_ref[0, :, sl], _NT, preferred_element_type=f32)
        _online_softmax_step(s, v_ref[0, :, sl], m_ref, acc_ref, hh)

    @pl.when(j == pl.num_programs(2) - 1)
    def _():
        _mla_finalize(acc_ref, o_ref)


def _mla_finalize(acc_ref, o_ref, transposed=False):
    for hh in range(MLA_HEADS):
        acc = acc_ref[hh].T if transposed else acc_ref[hh]
        o = acc[:, :MLA_V] * (1.0 / acc[:, MLA_V:MLA_V + 1])
        o_ref[0, :, hh * MLA_V:(hh + 1) * MLA_V] = o.astype(bf16)


def _diff_finalize(lam_ref, gsub_ref, acc_ref, o_ref, transposed=False):
    lam = lam_ref[0]
    dv = 2 * DIFF_HEAD_DIM
    for hh in range(DIFF_HEADS):
        a1 = acc_ref[2 * hh].T if transposed else acc_ref[2 * hh]
        a2 = acc_ref[2 * hh + 1].T if transposed else acc_ref[2 * hh + 1]
        o = a1[:, :dv] * (1.0 / a1[:, dv:dv + 1]) - a2[:, :dv] * (lam / a2[:, dv:dv + 1])
        ms = jnp.mean(o * o, axis=-1, keepdims=True)
        o_ref[0, :, hh * dv:(hh + 1) * dv] = (o * lax.rsqrt(ms + EPS) * gsub_ref[...]).astype(bf16)


def _diff_attn_kernel(lam_ref, q_ref, k_ref, v_ref, gsub_ref, o_ref, m_ref, acc_ref):
    j = pl.program_id(2)

    @pl.when(j == 0)
    def _():
        m_ref[...] = jnp.full(m_ref.shape, -jnp.inf, f32)
        acc_ref[...] = jnp.zeros(acc_ref.shape, f32)

    q = q_ref[0]
    k = k_ref[0]
    group = lax.broadcasted_iota(jnp.int32, (1, BRANCH), 1) // DIFF_HEAD_DIM
    for g in range(2 * DIFF_HEADS):
        qg = jnp.where(group == g, q, jnp.zeros_like(q))
        s = lax.dot_general(qg, k, _NT, preferred_element_type=f32)
        hh = g // 2
        _online_softmax_step(s, v_ref[0, :, hh * LANE:(hh + 1) * LANE], m_ref, acc_ref, g)

    @pl.when(j == pl.num_programs(2) - 1)
    def _():
        _diff_finalize(lam_ref, gsub_ref, acc_ref, o_ref)


def _round_robin(chains, in_flight):
    pending, active = list(chains), []
    while pending or active:
        while pending and len(active) < in_flight:
            active.append(pending.pop(0))
        for c in list(active):
            if next(c, "done") == "done":
                active.remove(c)


def _bounded_attn_loop(maps, k_ref, v_ref, acc_ref, tk):
    acc_ref[...] = jnp.zeros(acc_ref.shape, f32)

    def kv_block(j, carry):
        def chain(g, r0):
            q, ksl, vsl = maps[g]
            keys = pl.ds(pl.multiple_of(j * tk + r0, ATTN_KEY_SUB), ATTN_KEY_SUB)
            st = lax.dot_general(k_ref[0, keys, ksl], q, _NT, preferred_element_type=f32)
            yield
            pt = jnp.exp2(st).astype(bf16)
            yield
            acc_ref[g] += lax.dot_general(v_ref[0, keys, vsl], pt, _TN, preferred_element_type=f32)

        _round_robin([chain(g, r0) for r0 in range(0, tk, ATTN_KEY_SUB) for g in range(len(maps))],
                     ATTN_CHAINS_IN_FLIGHT)
        return carry

    lax.fori_loop(0, k_ref.shape[1] // tk, kv_block, 0)


def _mla_attn_bounded_kernel(q_ref, k_ref, v_ref, o_ref, acc_ref, *, tk):
    maps = [(q_ref[0, :, hh * LANE:(hh + 1) * LANE], slice(hh * LANE, (hh + 1) * LANE),
             slice(hh * LANE, hh * LANE + VT_ROWS)) for hh in range(MLA_HEADS)]
    _bounded_attn_loop(maps, k_ref, v_ref, acc_ref, tk)
    _mla_finalize(acc_ref, o_ref, transposed=True)


def _diff_attn_bounded_kernel(lam_ref, q_ref, k_ref, v_ref, gsub_ref, o_ref, acc_ref, *, tk):
    q = q_ref[0]
    group = lax.broadcasted_iota(jnp.int32, (1, BRANCH), 1) // DIFF_HEAD_DIM
    maps = [(jnp.where(group == g, q, jnp.zeros_like(q)), slice(0, BRANCH),
             slice((g // 2) * LANE, (g // 2) * LANE + VT_ROWS)) for g in range(2 * DIFF_HEADS)]
    _bounded_attn_loop(maps, k_ref, v_ref, acc_ref, tk)
    _diff_finalize(lam_ref, gsub_ref, acc_ref, o_ref, transposed=True)


def _attn_bounded_call(kernel_fn, n_acc, name, args, in_specs, S, W, B, tq):
    return pl.pallas_call(
        kernel_fn,
        grid=(B, S // tq),
        in_specs=in_specs,
        out_specs=pl.BlockSpec((1, tq, BRANCH), lambda b, i: (b, i, 0)),
        out_shape=jax.ShapeDtypeStruct((B, S, BRANCH), bf16),
        scratch_shapes=[pltpu.VMEM((n_acc, VT_ROWS, tq), f32)],
        compiler_params=_cparams(("parallel", "parallel")),
        name=name,
    )(*args)


def _mla_attn_bounded(q, k, v, tq, tk):
    B, S, W = q.shape
    full = pl.BlockSpec((1, S, W), lambda b, i: (b, 0, 0))
    in_specs = [pl.BlockSpec((1, tq, W), lambda b, i: (b, i, 0)), full, full]
    return _attn_bounded_call(functools.partial(_mla_attn_bounded_kernel, tk=tk), MLA_HEADS, "mla_attn_bounded",
                              (q, k, v), in_specs, S, W, B, tq)


def _diff_attn_bounded(lam, q, k, v, gsub, tq, tk):
    B, S, W = v.shape
    in_specs = [pl.BlockSpec(memory_space=pltpu.SMEM),
                pl.BlockSpec((1, tq, BRANCH), lambda b, i: (b, i, 0)),
                pl.BlockSpec((1, S, BRANCH), lambda b, i: (b, 0, 0)),
                pl.BlockSpec((1, S, W), lambda b, i: (b, 0, 0)),
                pl.BlockSpec((1, 2 * DIFF_HEAD_DIM), lambda b, i: (0, 0))]
    return _attn_bounded_call(functools.partial(_diff_attn_bounded_kernel, tk=tk), 2 * DIFF_HEADS,
                              "diff_attn_bounded", (lam, q, k, v, gsub), in_specs, S, W, B, tq)


def _mla_attn(q, k, v, tq, tk):
    B, S, W = q.shape
    return pl.pallas_call(
        _mla_attn_kernel,
        grid=(B, S // tq, S // tk),
        in_specs=[pl.BlockSpec((1, tq, W), lambda b, i, j: (b, i, 0)),
                  pl.BlockSpec((1, tk, W), lambda b, i, j: (b, j, 0)),
                  pl.BlockSpec((1, tk, W), lambda b, i, j: (b, j, 0))],
        out_specs=pl.BlockSpec((1, tq, BRANCH), lambda b, i, j: (b, i, 0)),
        out_shape=jax.ShapeDtypeStruct((B, S, BRANCH), bf16),
        scratch_shapes=[pltpu.VMEM((MLA_HEADS, tq, LANE), f32), pltpu.VMEM((MLA_HEADS, tq, LANE), f32)],
        compiler_params=_cparams(("parallel", "parallel", "arbitrary")),
        name="mla_attn",
    )(q, k, v)


def _diff_attn(lam, q, k, v, gsub, tq, tk):
    B, S, W = v.shape
    return pl.pallas_call(
        _diff_attn_kernel,
        grid=(B, S // tq, S // tk),
        in_specs=[pl.BlockSpec(memory_space=pltpu.SMEM),
                  pl.BlockSpec((1, tq, BRANCH), lambda b, i, j: (b, i, 0)),
                  pl.BlockSpec((1, tk, BRANCH), lambda b, i, j: (b, j, 0)),
                  pl.BlockSpec((1, tk, W), lambda b, i, j: (b, j, 0)),
                  pl.BlockSpec((1, 2 * DIFF_HEAD_DIM), lambda b, i, j: (0, 0))],
        out_specs=pl.BlockSpec((1, tq, BRANCH), lambda b, i, j: (b, i, 0)),
        out_shape=jax.ShapeDtypeStruct((B, S, BRANCH), bf16),
        scratch_shapes=[pltpu.VMEM((2 * DIFF_HEADS, tq, LANE), f32), pltpu.VMEM((2 * DIFF_HEADS, tq, LANE), f32)],
        compiler_params=_cparams(("parallel", "parallel", "arbitrary")),
        name="diff_attn",
    )(lam, q, k, v, gsub)


def _fnet1_kernel(x_ref, w_ref, a_ref):
    a_ref[0] = jnp.dot(w_ref[...], x_ref[0], preferred_element_type=f32).astype(bf16)


def _fnet2_kernel(a_ref, t_ref, cs_ref, o_ref):
    zs = []
    for jj in range(FNET_K1_BLOCK):
        a = jnp.concatenate([a_ref[0, 0, jj], a_ref[0, 1, jj]], axis=0)
        zs.append(jnp.dot(t_ref[jj], a, preferred_element_type=f32))
    zc = jnp.concatenate([jnp.concatenate([z[:FNET_N2], z[FNET_N2:]], axis=1) for z in zs], axis=0).astype(bf16)
    y = jnp.dot(zc, cs_ref[...], preferred_element_type=f32).astype(bf16)
    for jj in range(FNET_K1_BLOCK):
        o_ref[0, :, jj, :] = y[jj * FNET_N2:(jj + 1) * FNET_N2]


def _fnet_consts(S):
    n1 = S // FNET_N2
    a = np.arange(n1)
    ang1 = 2.0 * np.pi * np.outer(a, a) / n1
    w1 = np.concatenate([np.cos(ang1), -np.sin(ang1)], axis=0)
    n2 = np.arange(FNET_N2)
    phi = 2.0 * np.pi * (n2[None, None, :] * a[:, None, None] / S + n2[None, None, :] * n2[None, :, None] / FNET_N2)
    tr, ti = np.cos(phi), -np.sin(phi)
    t = np.concatenate([np.concatenate([tr, -ti], axis=2), np.concatenate([ti, tr], axis=2)], axis=1)
    c = np.arange(BRANCH)
    same = (c[:, None] // FNET_N2) == (c[None, :] // FNET_N2)
    angc = 2.0 * np.pi * np.outer(c % FNET_N2, c % FNET_N2) / FNET_N2
    norm = 1.0 / math.sqrt(S * FNET_N2)
    cs = np.concatenate([np.where(same, np.cos(angc), 0.0), np.where(same, np.sin(angc), 0.0)], axis=0) * norm
    return (jnp.asarray(w1, f32).astype(bf16), jnp.asarray(t, f32).astype(bf16), jnp.asarray(cs, f32).astype(bf16))


def _fnet(u, consts):
    B, S, W = u.shape
    w1, t, cs = consts
    n1 = S // FNET_N2
    cols = FNET_N2 * W
    tn = min(cols, 4096)
    a = pl.pallas_call(
        _fnet1_kernel,
        grid=(B, cols // tn),
        in_specs=[pl.BlockSpec((1, n1, tn), lambda b, i: (b, 0, i)), _const_spec(w1.shape)],
        out_specs=pl.BlockSpec((1, 2 * n1, tn), lambda b, i: (b, 0, i)),
        out_shape=jax.ShapeDtypeStruct((B, 2 * n1, cols), bf16),
        compiler_params=_cparams(("parallel", "parallel")),
        name="fnet1",
    )(u.reshape(B, n1, cols), w1)
    kb = FNET_K1_BLOCK
    y = pl.pallas_call(
        _fnet2_kernel,
        grid=(B, n1 // kb),
        in_specs=[pl.BlockSpec((1, 2, kb, FNET_N2, W), lambda b, i: (b, 0, i, 0, 0)),
                  pl.BlockSpec((kb, 2 * FNET_N2, 2 * FNET_N2), lambda b, i: (i, 0, 0)),
                  _const_spec(cs.shape)],
        out_specs=pl.BlockSpec((1, FNET_N2, kb, W), lambda b, i: (b, 0, i, 0)),
        out_shape=jax.ShapeDtypeStruct((B, FNET_N2, n1, W), bf16),
        compiler_params=_cparams(("parallel", "parallel")),
        name="fnet2",
    )(a.reshape(B, 2, n1, FNET_N2, W), t, cs)
    return y.reshape(B, S, W)


def _hgrn_consts(C):
    L = int(math.log2(C))
    t = np.arange(C)
    m = np.zeros((2, (L + 2) * C, C), np.float32)
    lvl = np.full((2, C, C), -1, np.int32)
    for d in (0, 1):
        for li in range(L):
            n = C >> li
            half = n // 2
            blk = t // n
            mid = blk * n + half
            upper = (t % n) >= half
            for r in range(C):
                if d == 0:
                    if upper[r]:
                        m[d, li * C + r, mid[r]:r + 1] = 1.0
                    else:
                        m[d, li * C + r, r + 1:mid[r]] = 1.0
                else:
                    if upper[r]:
                        m[d, li * C + r, mid[r]:r] = 1.0
                    else:
                        m[d, li * C + r, r:mid[r]] = 1.0
            same = blk[:, None] == blk[None, :]
            if d == 0:
                msk = same & upper[:, None] & (~upper)[None, :]
            else:
                msk = same & (~upper)[:, None] & upper[None, :]
            lvl[d][msk] = li
        for r in range(C):
            if d == 0:
                m[d, L * C + r, :r + 1] = 1.0
                m[d, (L + 1) * C + r, r + 1:] = 1.0
            else:
                m[d, L * C + r, r:] = 1.0
                m[d, (L + 1) * C + r, :r] = 1.0
    c = np.arange(BRANCH)
    headsum = ((c[:, None] // HGRN_DIM) == (c[None, :] // HGRN_DIM)).astype(np.float32)
    lvl = np.tile(lvl, (1, 1, HGRN_HEADS))
    m = np.tile(m, (1, 1, 2))
    return jnp.asarray(m, f32).astype(bf16), jnp.asarray(lvl), jnp.asarray(headsum, f32).astype(bf16)


def _hgrn_kernel(qf_ref, vf_ref, kf_ref, lff_ref, qb_ref, vb_ref, kb_ref, lfb_ref, m_ref, lvl_ref, hs_ref,
                 of_ref, ob_ref, st_ref, *, C, L):
    @pl.when(pl.program_id(1) == 0)
    def _():
        st_ref[...] = jnp.zeros(st_ref.shape, f32)

    refs = ((qf_ref, vf_ref, kf_ref, lff_ref, of_ref), (qb_ref, vb_ref, kb_ref, lfb_ref, ob_ref))
    head = lax.broadcasted_iota(jnp.int32, (1, BRANCH), 1) // HGRN_DIM
    r = lax.broadcasted_iota(jnp.int32, (BRANCH, BRANCH), 0) // HGRN_DIM
    cidx = lax.broadcasted_iota(jnp.int32, (BRANCH, BRANCH), 1) // HGRN_DIM
    n_chunks = qf_ref.shape[1] // C
    order = (tuple(range(n_chunks)), tuple(reversed(range(n_chunks))))
    chains = [(d, order[d][j]) for j in range(n_chunks) for d in (0, 1)]
    rows = {c: slice(c[1] * C, (c[1] + 1) * C) for c in chains}
    q = {c: refs[c[0]][0][0, rows[c], :].astype(f32) for c in chains}
    v = {c: refs[c[0]][1][0, rows[c], :] for c in chains}
    k = {c: refs[c[0]][2][0, rows[c], :].astype(f32) for c in chains}
    e = {}
    for c in chains:
        lf = refs[c[0]][3][0, rows[c], :]
        lf2 = jnp.concatenate([lf[:, :BRANCH], lf[:, BRANCH:]], axis=0)
        e[c] = jnp.dot(m_ref[c[0]], lf2, preferred_element_type=f32)

    o = {}
    st = [st_ref[0], st_ref[1]]
    for c in chains:
        d = c[0]
        ein = jnp.exp(e[c][L * C:(L + 1) * C])
        o[c] = lax.dot_general((q[c] * ein).astype(bf16), st[d].astype(bf16), _NT, preferred_element_type=f32)
        kout = (k[c] * jnp.exp(e[c][(L + 1) * C:(L + 2) * C])).astype(bf16)
        upd = lax.dot_general(v[c], kout, _TN, preferred_element_type=f32)
        total = ein[C - 1:C] if d == 0 else ein[0:1]
        st[d] = st[d] * total + jnp.where(r == cidx, upd, 0.0)
        o[c] = o[c] + jnp.dot((q[c] * k[c]).astype(bf16), hs_ref[...], preferred_element_type=f32) * v[c].astype(f32)
    st_ref[0] = st[0]
    st_ref[1] = st[1]

    scores = {c: jnp.zeros((C, HGRN_HEADS * C), f32) for c in chains}
    for li in range(L):
        for c in chains:
            ex = jnp.exp(e[c][li * C:(li + 1) * C])
            qt = (q[c] * ex).astype(bf16)
            kt = (k[c] * ex).astype(bf16)
            kstack = jnp.concatenate([jnp.where(head == hh, kt, jnp.zeros_like(kt)) for hh in range(HGRN_HEADS)],
                                     axis=0)
            s = lax.dot_general(qt, kstack, _NT, preferred_element_type=f32)
            scores[c] = jnp.where(lvl_ref[c[0]] == li, s, scores[c])
    for c in chains:
        vstack = jnp.concatenate([jnp.where(head == hh, v[c], jnp.zeros_like(v[c])) for hh in range(HGRN_HEADS)],
                                 axis=0)
        oc = o[c] + jnp.dot(scores[c].astype(bf16), vstack, preferred_element_type=f32)
        refs[c[0]][4][0, rows[c], :] = oc.astype(bf16)


def _hgrn(hq, hv, hk, hlf, consts, C):
    B, S, W = hq.shape
    m, lvl, hs = consts
    L = int(math.log2(C))
    rows = min(S, HGRN_CHUNKS_PER_STEP * C)
    nb = S // rows
    fwd = lambda w, c: pl.BlockSpec((1, rows, w), lambda b, i: (b, i, c))
    bwd = lambda w, c: pl.BlockSpec((1, rows, w), lambda b, i: (b, nb - 1 - i, c))
    return pl.pallas_call(
        functools.partial(_hgrn_kernel, C=C, L=L),
        grid=(B, nb),
        in_specs=[fwd(W, 0), fwd(W, 0), fwd(W, 0), fwd(2 * W, 0), bwd(W, 0), bwd(W, 0), bwd(W, 1), bwd(2 * W, 1),
                  _const_spec(m.shape), _const_spec(lvl.shape), _const_spec(hs.shape)],
        out_specs=[fwd(W, 0), bwd(W, 0)],
        out_shape=[jax.ShapeDtypeStruct((B, S, W), bf16)] * 2,
        scratch_shapes=[pltpu.VMEM((2, BRANCH, BRANCH), f32)],
        compiler_params=_cparams(("parallel", "arbitrary")),
        name="hgrn",
    )(hq, hv, hk, hlf, hq, hv, hk, hlf, m, lvl, hs)


def _merge_ffn_kernel(x_ref, om_ref, of_ref, od_ref, ohf_ref, ohb_ref, hg_ref, ln_ref, wg_ref, wbm_ref, wbf_ref,
                      wbd_ref, wbh_ref, wo_ref, gon_ref, g64_ref, lnf_ref, wgu_ref, wd_ref, o_ref):
    x = x_ref[0]
    h = _rms(x, ln_ref[...]).astype(bf16)
    oh = ohf_ref[0].astype(f32) + ohb_ref[0].astype(f32)
    ms = jnp.dot((oh * oh).astype(bf16), g64_ref[...], preferred_element_type=f32)
    oh = (oh * lax.rsqrt(ms + EPS) * gon_ref[...] * _sigmoid(hg_ref[0].astype(f32))).astype(bf16)
    merged = None
    for n, (o_n, w_ref) in enumerate(((om_ref[0], wbm_ref), (of_ref[0], wbf_ref), (od_ref[0], wbd_ref), (oh, wbh_ref))):
        gate = _sigmoid(jnp.dot(h, wg_ref[:, n * D_MODEL:(n + 1) * D_MODEL], preferred_element_type=f32))
        y = gate * jnp.dot(o_n, w_ref[...], preferred_element_type=f32)
        merged = y if merged is None else merged + y
    x = x + jnp.dot(merged.astype(bf16), wo_ref[...], preferred_element_type=f32)

    h = _rms(x, lnf_ref[...]).astype(bf16)
    acc = x
    for c in range(D_FF // FF_CHUNK):
        g = jnp.dot(h, wgu_ref[:, c * FF_CHUNK:(c + 1) * FF_CHUNK], preferred_element_type=f32)
        u = jnp.dot(h, wgu_ref[:, D_FF + c * FF_CHUNK:D_FF + (c + 1) * FF_CHUNK], preferred_element_type=f32)
        a = (g * _sigmoid(g) * u).astype(bf16)
        acc = acc + jnp.dot(a, wd_ref[c], preferred_element_type=f32)
    o_ref[0] = acc


def _merge_ffn(x, om, of, od, oh, hg, p, tm):
    B, S, _ = x.shape
    row = lambda w: pl.BlockSpec((1, tm, w), lambda b, i: (b, i, 0))
    consts = (p["ln_mix"], p["wgate"], p["wbm"], p["wbf"], p["wbd"], p["wbh"], p["wout"], p["gon"], p["g64"],
              p["ln_ffn"], p["wgu"], p["wd"])
    return pl.pallas_call(
        _merge_ffn_kernel,
        grid=(B, S // tm),
        in_specs=[row(D_MODEL)] + [row(BRANCH)] * 6
                 + [_const_spec(c.shape) for c in consts],
        out_specs=row(D_MODEL),
        out_shape=jax.ShapeDtypeStruct((B, S, D_MODEL), f32),
        compiler_params=_cparams(("parallel", "parallel")),
        name="merge_ffn",
    )(x, om, of, od, oh[0], oh[1], hg, *consts)


def _pad_heads(w, heads, width):
    lead = w.shape[:-1]
    w = w.reshape(lead + (heads, width))
    w = jnp.pad(w, [(0, 0)] * len(lead) + [(0, 0), (0, LANE - width)])
    return w.reshape(lead + (heads * LANE,))


def _rope_tables(S):
    def cs(dim):
        inv = 1.0 / (ROPE_THETA ** (jnp.arange(0, dim, 2, dtype=f32) / dim))
        ang = jnp.arange(S, dtype=f32)[:, None] * inv[None, :]
        return jnp.cos(ang), jnp.sin(ang)

    cm, sm = cs(MLA_ROPE)
    one, zero = jnp.ones((S, MLA_NOPE), f32), jnp.zeros((S, MLA_NOPE), f32)
    z16, z32, o32 = jnp.zeros((S, 16), f32), jnp.zeros((S, 32), f32), jnp.ones((S, 32), f32)
    rope_m = jnp.stack([jnp.concatenate([one, cm, cm, o32], 1),
                        jnp.concatenate([zero, -sm, z16, z32], 1),
                        jnp.concatenate([zero, z16, sm, z32], 1)])
    cd, sd = cs(DIFF_HEAD_DIM)
    rope_d = jnp.stack([jnp.tile(jnp.concatenate([cd, cd], 1), (1, 4)),
                        jnp.tile(jnp.concatenate([-sd, z16], 1), (1, 4)),
                        jnp.tile(jnp.concatenate([z16, sd], 1), (1, 4))])
    return rope_m, rope_d


def _block_mean(width, group):
    c = np.arange(width)
    return jnp.asarray(((c[:, None] // group) == (c[None, :] // group)) / group, f32).astype(bf16)


def _layer_params(l, a, lower_bounds):
    w = a["w_in"][l]
    offs = np.cumsum([0, 384, 128, 32, 256, 256, 256, 256, 256, 256, 256, 256, 256, 4096])
    col = lambda i: w[:, offs[i]:offs[i + 1]]
    wcat = jnp.concatenate([col(0), col(1), jnp.pad(col(2), ((0, 0), (MLA_NOPE, LANE - MLA_QK))), col(3), col(4),
                            col(5), _pad_heads(col(6), DIFF_HEADS, 2 * DIFF_HEAD_DIM), col(7), col(8), col(9),
                            col(10), col(11)], axis=1).astype(bf16)
    wukv = a["mla_w_ukv"][l].reshape(MLA_KV_LORA, MLA_HEADS, MLA_NOPE + MLA_V)
    wukv = jnp.concatenate([_pad_heads(wukv[:, :, :MLA_NOPE].reshape(MLA_KV_LORA, -1), MLA_HEADS, MLA_NOPE),
                            _pad_heads(wukv[:, :, MLA_NOPE:].reshape(MLA_KV_LORA, -1), MLA_HEADS, MLA_V)], axis=1)
    lam_init = 0.8 - 0.6 * math.exp(-0.3 * l)
    lam = (jnp.exp(jnp.sum(a["diff_lq1"][l] * a["diff_lk1"][l])) - jnp.exp(jnp.sum(a["diff_lq2"][l] * a["diff_lk2"][l]))
           + lam_init)
    wb = a["w_branch"][l]
    nc = D_FF // FF_CHUNK
    row = lambda v: v.reshape(1, -1).astype(f32)
    return {
        "ln_mix": row(a["ln_mix"][l]),
        "wcat": wcat,
        "gqa": row(a["mla_g_qa"][l]),
        "wuq": _pad_heads(a["mla_w_uq"][l], MLA_HEADS, MLA_QK).astype(bf16),
        "gkva": row(a["mla_g_kva"][l]),
        "wukv": wukv.astype(bf16),
        "gqn": row(jnp.pad(a["mla_g_qn"][l], (0, LANE - MLA_QK))) * (MLA_QK ** -0.5 * LOG2E),
        "gkn": row(jnp.pad(a["mla_g_kn"][l], (0, LANE - MLA_QK))),
        "gdq": row(jnp.tile(a["diff_g_qn"][l], 2 * DIFF_HEADS)) * (DIFF_HEAD_DIM ** -0.5 * LOG2E),
        "gdk": row(jnp.tile(a["diff_g_kn"][l], 2 * DIFF_HEADS)),
        "g32": _block_mean(BRANCH, DIFF_HEAD_DIM),
        "lb": lower_bounds[:, l].astype(f32),
        "lam": lam.reshape(1).astype(f32),
        "mla_bound": MLA_QK ** 0.5 * LOG2E * jnp.max(jnp.abs(a["mla_g_qn"][l])) * jnp.max(jnp.abs(a["mla_g_kn"][l])),
        "diff_bound": (DIFF_HEAD_DIM ** 0.5 * LOG2E * jnp.max(jnp.abs(a["diff_g_qn"][l]))
                       * jnp.max(jnp.abs(a["diff_g_kn"][l]))),
        "gsub": row(a["diff_g_sub"][l]) * (1.0 - lam_init),
        "wgate": col(12).astype(bf16),
        "wbm": wb[0].astype(bf16),
        "wbf": wb[1].astype(bf16),
        "wbd": wb[2].astype(bf16),
        "wbh": wb[3].astype(bf16),
        "wout": a["w_out"][l].astype(bf16),
        "gon": row(jnp.tile(a["hgrn_g_on"][l], HGRN_HEADS)),
        "g64": _block_mean(BRANCH, HGRN_DIM),
        "ln_ffn": row(a["ln_ffn"][l]),
        "wgu": a["w_gate_up"][l].astype(bf16),
        "wd": a["w_down"][l].reshape(nc, FF_CHUNK, D_MODEL).astype(bf16),
    }


def _tiles(S):
    return {"tm_in": min(256, S),"tq": min(512, S), "tk": min(512, S), "tqb": min(512, S), "tkb": min(1024, S),
            "hgrn_chunk": min(64, S),
            "tm_merge": min(512, S)}


def kernel(x, ln_mix, w_in, mla_g_qa, mla_w_uq, mla_g_kva, mla_w_ukv, mla_g_qn, mla_g_kn, diff_g_qn, diff_g_kn,
           diff_lq1, diff_lk1, diff_lq2, diff_lk2, diff_g_sub, hgrn_lb_logits, hgrn_g_on, w_branch, w_out, ln_ffn,
           w_gate_up, w_down):
    a = dict(ln_mix=ln_mix, w_in=w_in, mla_g_qa=mla_g_qa, mla_w_uq=mla_w_uq, mla_g_kva=mla_g_kva,
             mla_w_ukv=mla_w_ukv, mla_g_qn=mla_g_qn, mla_g_kn=mla_g_kn, diff_g_qn=diff_g_qn, diff_g_kn=diff_g_kn,
             diff_lq1=diff_lq1, diff_lk1=diff_lk1, diff_lq2=diff_lq2, diff_lk2=diff_lk2, diff_g_sub=diff_g_sub,
             hgrn_g_on=hgrn_g_on, w_branch=w_branch, w_out=w_out, ln_ffn=ln_ffn, w_gate_up=w_gate_up, w_down=w_down)
    S = x.shape[1]
    t = _tiles(S)
    rope_m, rope_d = _rope_tables(S)
    lb_p = jax.nn.softmax(hgrn_lb_logits.astype(f32), axis=1)
    lower_bounds = jnp.cumsum(lb_p, axis=1) - lb_p[:, :1]
    fnet_consts = _fnet_consts(S)
    hgrn_consts = _hgrn_consts(t["hgrn_chunk"])
    for l in range(DEPTH):
        p = _layer_params(l, a, lower_bounds)
        qm, km, vm, fn, qd, kd, vd, hq, hv, hk, hlf, hg = _in_proj(x, p, rope_m, rope_d, t["tm_in"])
        om = lax.cond(p["mla_bound"] <= SCORE_BOUND_LOG2,
                      lambda: _mla_attn_bounded(qm, km, vm, t["tqb"], t["tkb"]),
                      lambda: _mla_attn(qm, km, vm, t["tq"], t["tk"]))
        od = lax.cond(p["diff_bound"] <= SCORE_BOUND_LOG2,
                      lambda: _diff_attn_bounded(p["lam"], qd, kd, vd, p["gsub"], t["tqb"], t["tkb"]),
                      lambda: _diff_attn(p["lam"], qd, kd, vd, p["gsub"], t["tq"], t["tk"]))
        of = _fnet(fn, fnet_consts)
        oh = _hgrn(hq, hv, hk, hlf, hgrn_consts, t["hgrn_chunk"])
        x = _merge_ffn(x, om, of, od, oh, hg, p, t["tm_merge"])
    return x
```

```python
import functools
import math

import numpy as np
import jax
import jax.numpy as jnp
from jax import lax
from jax.experimental import pallas as pl
from jax.experimental.pallas import tpu as pltpu

f32 = jnp.float32
bf16 = jnp.bfloat16

D_MODEL = 1024
DEPTH = 2
MLA_HEADS = 4
MLA_Q_LORA = 384
MLA_KV_LORA = 128
MLA_NOPE = 64
MLA_ROPE = 32
MLA_V = 64
MLA_QK = MLA_NOPE + MLA_ROPE
DIFF_HEADS = 4
DIFF_HEAD_DIM = 32
HGRN_HEADS = 4
HGRN_DIM = 64
BRANCH = 256
D_FF = 2816
FF_CHUNK = 256
ROPE_THETA = 10000.0
EPS = 1e-6
LOG2E = 1.4426950408889634
LANE = 128
FNET_N2 = 64
FNET_K1_BLOCK = 8
HGRN_CHUNKS_PER_STEP = 8
IN_PROJ_CHAINS_IN_FLIGHT = 4
ATTN_KEY_SUB = 256
VT_ROWS = LANE
ATTN_CHAINS_IN_FLIGHT = 8
VMEM_LIMIT = 56 * 1024 * 1024
SCORE_BOUND_LOG2 = 60.0

_SEG = {}
_off = 0
for _name, _w in (("cq", 384), ("ckv", 128), ("krope", 128), ("fnet", 256), ("dq", 256), ("dk", 256),
                  ("dv", 512), ("hq", 256), ("hi", 256), ("hff", 256), ("hfb", 256), ("hg", 256)):
    _SEG[_name] = (_off, _off + _w)
    _off += _w
W_CAT = _off


def _cparams(sem):
    return pltpu.CompilerParams(dimension_semantics=sem, vmem_limit_bytes=VMEM_LIMIT)


def _const_spec(shape):
    nd = len(shape)
    return pl.BlockSpec(shape, lambda *_: (0,) * nd, pipeline_mode=pl.Buffered(1))


def _rms(x, g):
    return x * lax.rsqrt(jnp.mean(x * x, axis=-1, keepdims=True) + EPS) * g


def _sigmoid(x):
    return 1.0 / (1.0 + jnp.exp(-x))


def _rope(t, tab_ref):
    return t * tab_ref[0] + pltpu.roll(t, LANE - 16, 1) * tab_ref[1] + pltpu.roll(t, 16, 1) * tab_ref[2]


def _in_proj_kernel(x_ref, ln_ref, wcat_ref, gqa_ref, wuq_ref, gkva_ref, wukv_ref, gqn_ref, gkn_ref,
                    ropem_ref, gdq_ref, gdk_ref, g32_ref, roped_ref, lb_ref,
                    qm_ref, km_ref, vm_ref, fn_ref, qd_ref, kd_ref, vd_ref, hq_ref, hv_ref, hk_ref,
                    hlf_ref, hg_ref):
    h = _rms(x_ref[0], ln_ref[...]).astype(bf16)

    def seg(name):
        a, b = _SEG[name]
        return jnp.dot(h, wcat_ref[:, a:b], preferred_element_type=f32)

    ones_lane = (lax.broadcasted_iota(jnp.int32, (1, LANE), 1) == MLA_V).astype(f32)


    def mla_q():
        cq = seg("cq")
        yield
        q = jnp.dot(_rms(cq, gqa_ref[...]).astype(bf16), wuq_ref[...], preferred_element_type=f32)
        yield
        for hh in range(MLA_HEADS):
            sl = slice(hh * LANE, (hh + 1) * LANE)
            qh = q[:, sl]
            ms = jnp.sum(qh * qh, axis=-1, keepdims=True) * (1.0 / MLA_QK)
            qm_ref[0, :, sl] = _rope(qh * lax.rsqrt(ms + EPS) * gqn_ref[...], ropem_ref).astype(bf16)

    def mla_kv():
        ckv = seg("ckv")
        krope = seg("krope")
        yield
        kv = jnp.dot(_rms(ckv, gkva_ref[...]).astype(bf16), wukv_ref[...], preferred_element_type=f32)
        yield
        for hh in range(MLA_HEADS):
            sl = slice(hh * LANE, (hh + 1) * LANE)
            kh = kv[:, sl] + krope
            ms = jnp.sum(kh * kh, axis=-1, keepdims=True) * (1.0 / MLA_QK)
            km_ref[0, :, sl] = _rope(kh * lax.rsqrt(ms + EPS) * gkn_ref[...], ropem_ref).astype(bf16)
            vsl = slice((MLA_HEADS + hh) * LANE, (MLA_HEADS + hh + 1) * LANE)
            vm_ref[0, :, sl] = (kv[:, vsl] + ones_lane).astype(bf16)

    def diff_qk(name, g_ref, o_ref):
        t = seg(name)
        yield
        ms = jnp.dot((t * t).astype(bf16), g32_ref[...], preferred_element_type=f32)
        yield
        t = t * lax.rsqrt(ms + EPS) * g_ref[...]
        for c in range(BRANCH // LANE):
            sl = slice(c * LANE, (c + 1) * LANE)
            o_ref[0, :, sl] = _rope(t[:, sl], roped_ref).astype(bf16)

    def diff_v():
        dv = seg("dv")
        yield
        for hh in range(DIFF_HEADS):
            sl = slice(hh * LANE, (hh + 1) * LANE)
            vd_ref[0, :, sl] = (dv[:, sl] + ones_lane).astype(bf16)

    def plain(name, o_ref):
        z = seg(name)
        yield
        o_ref[0] = z.astype(bf16)

    def hgrn_gate(d, name):
        z = seg(name)
        yield
        lb = lb_ref[d:d + 1, :]
        f = lb + (1.0 - lb) * _sigmoid(z)
        hk_ref[0, :, d * BRANCH:(d + 1) * BRANCH] = (1.0 - f).astype(bf16)
        lf = jnp.log(f)
        hi = lf.astype(bf16)
        hlf_ref[0, :, 2 * d * BRANCH:(2 * d + 1) * BRANCH] = hi
        hlf_ref[0, :, (2 * d + 1) * BRANCH:(2 * d + 2) * BRANCH] = (lf - hi.astype(f32)).astype(bf16)

    _round_robin([mla_q(), hgrn_gate(0, "hff"), mla_kv(), hgrn_gate(1, "hfb"), diff_qk("dq", gdq_ref, qd_ref),
                  plain("fnet", fn_ref), diff_qk("dk", gdk_ref, kd_ref), plain("hq", hq_ref), diff_v(),
                  plain("hi", hv_ref), plain("hg", hg_ref)], IN_PROJ_CHAINS_IN_FLIGHT)


def _in_proj(x, p, rope_m, rope_d, tm):
    B, S, _ = x.shape
    row = lambda w: pl.BlockSpec((1, tm, w), lambda b, i: (b, i, 0))
    tab = pl.BlockSpec((3, tm, LANE), lambda b, i: (0, i, 0))
    out_w = (512, 512, 512, 256, 256, 256, 512, 256, 256, 512, 1024, 256)
    consts = (p["ln_mix"], p["wcat"], p["gqa"], p["wuq"], p["gkva"], p["wukv"], p["gqn"], p["gkn"])
    consts2 = (p["gdq"], p["gdk"], p["g32"])
    in_specs = ([row(D_MODEL)] + [_const_spec(c.shape) for c in consts] + [tab]
                + [_const_spec(c.shape) for c in consts2] + [tab, _const_spec(p["lb"].shape)])
    return pl.pallas_call(
        _in_proj_kernel,
        grid=(B, S // tm),
        in_specs=in_specs,
        out_specs=[row(w) for w in out_w],
        out_shape=[jax.ShapeDtypeStruct((B, S, w), bf16) for w in out_w],
        compiler_params=_cparams(("parallel", "parallel")),
        name="in_proj",
    )(x, *consts, rope_m, *consts2, rope_d, p["lb"])


def _online_softmax_step(s, v, m_ref, acc_ref, idx):
    m_prev = m_ref[idx]
    m_new = jnp.maximum(m_prev, jnp.max(s, axis=1, keepdims=True))
    alpha = jnp.exp2(m_prev - m_new)
    p = jnp.exp2((s - m_new[:, :1]).astype(bf16))
    acc_ref[idx] = alpha * acc_ref[idx] + jnp.dot(p, v, preferred_element_type=f32)
    m_ref[idx] = m_new


_NT = (((1,), (1,)), ((), ()))
_TN = (((0,), (0,)), ((), ()))


def _mla_attn_kernel(q_ref, k_ref, v_ref, o_ref, m_ref, acc_ref):
    j = pl.program_id(2)

    @pl.when(j == 0)
    def _():
        m_ref[...] = jnp.full(m_ref.shape, -jnp.inf, f32)
        acc_ref[...] = jnp.zeros(acc_ref.shape, f32)

    for hh in range(MLA_HEADS):
        sl = slice(hh * LANE, (hh + 1) * LANE)
        s = lax.dot_general(q_ref[0, :, sl], k_ref[0, :, sl], _NT, preferred_element_type=f32)
        _online_softmax_step(s, v_ref[0, :, sl], m_ref, acc_ref, hh)

    @pl.when(j == pl.num_programs(2) - 1)
    def _():
        _mla_finalize(acc_ref, o_ref)


def _mla_finalize(acc_ref, o_ref, transposed=False):
    for hh in range(MLA_HEADS):
        acc = acc_ref[hh]
        if transposed:
            o = (acc[:MLA_V] * (1.0 / acc[MLA_V:MLA_V + 1])).T
        else:
            o = acc[:, :MLA_V] * (1.0 / acc[:, MLA_V:MLA_V + 1])
        o_ref[0, :, hh * MLA_V:(hh + 1) * MLA_V] = o.astype(bf16)


def _diff_finalize(lam_ref, gsub_ref, acc_ref, o_ref, transposed=False):
    lam = lam_ref[0]
    dv = 2 * DIFF_HEAD_DIM
    for hh in range(DIFF_HEADS):
        a1 = acc_ref[2 * hh]
        a2 = acc_ref[2 * hh + 1]
        if transposed:
            o = a1[:dv] * (1.0 / a1[dv:dv + 1]) - a2[:dv] * (lam / a2[dv:dv + 1])
            o = (o * lax.rsqrt(jnp.mean(o * o, axis=0, keepdims=True) + EPS)).T
        else:
            o = a1[:, :dv] * (1.0 / a1[:, dv:dv + 1]) - a2[:, :dv] * (lam / a2[:, dv:dv + 1])
            o = o * lax.rsqrt(jnp.mean(o * o, axis=-1, keepdims=True) + EPS)
        o_ref[0, :, hh * dv:(hh + 1) * dv] = (o * gsub_ref[...]).astype(bf16)


def _diff_attn_kernel(lam_ref, q_ref, k_ref, v_ref, gsub_ref, o_ref, m_ref, acc_ref):
    j = pl.program_id(2)

    @pl.when(j == 0)
    def _():
        m_ref[...] = jnp.full(m_ref.shape, -jnp.inf, f32)
        acc_ref[...] = jnp.zeros(acc_ref.shape, f32)

    q = q_ref[0]
    k = k_ref[0]
    group = lax.broadcasted_iota(jnp.int32, (1, BRANCH), 1) // DIFF_HEAD_DIM
    for g in range(2 * DIFF_HEADS):
        qg = jnp.where(group == g, q, jnp.zeros_like(q))
        s = lax.dot_general(qg, k, _NT, preferred_element_type=f32)
        hh = g // 2
        _online_softmax_step(s, v_ref[0, :, hh * LANE:(hh + 1) * LANE], m_ref, acc_ref, g)

    @pl.when(j == pl.num_programs(2) - 1)
    def _():
        _diff_finalize(lam_ref, gsub_ref, acc_ref, o_ref)


def _round_robin(chains, in_flight):
    pending, active = list(chains), []
    while pending or active:
        while pending and len(active) < in_flight:
            active.append(pending.pop(0))
        for c in list(active):
            if next(c, "done") == "done":
                active.remove(c)


def _bounded_attn_loop(maps, k_ref, v_ref, acc_ref, tk):
    acc_ref[...] = jnp.zeros(acc_ref.shape, f32)

    def kv_block(j, carry):
        def chain(g, r0):
            q, ksl, vsl = maps[g]
            keys = pl.ds(pl.multiple_of(j * tk + r0, ATTN_KEY_SUB), ATTN_KEY_SUB)
            st = lax.dot_general(k_ref[0, keys, ksl], q, _NT, preferred_element_type=f32)
            yield
            pt = jnp.exp2(st).astype(bf16)
            yield
            acc_ref[g] += lax.dot_general(v_ref[0, keys, vsl], pt, _TN, preferred_element_type=f32)

        _round_robin([chain(g, r0) for r0 in range(0, tk, ATTN_KEY_SUB) for g in range(len(maps))],
                     ATTN_CHAINS_IN_FLIGHT)
        return carry

    lax.fori_loop(0, k_ref.shape[1] // tk, kv_block, 0)


def _mla_attn_bounded_kernel(q_ref, k_ref, v_ref, o_ref, acc_ref, *, tk):
    maps = [(q_ref[0, :, hh * LANE:(hh + 1) * LANE], slice(hh * LANE, (hh + 1) * LANE),
             slice(hh * LANE, hh * LANE + VT_ROWS)) for hh in range(MLA_HEADS)]
    _bounded_attn_loop(maps, k_ref, v_ref, acc_ref, tk)
    _mla_finalize(acc_ref, o_ref, transposed=True)


def _diff_attn_bounded_kernel(lam_ref, q_ref, k_ref, v_ref, gsub_ref, o_ref, acc_ref, *, tk):
    q = q_ref[0]
    group = lax.broadcasted_iota(jnp.int32, (1, BRANCH), 1) // DIFF_HEAD_DIM
    maps = [(jnp.where(group == g, q, jnp.zeros_like(q)), slice(0, BRANCH),
             slice((g // 2) * LANE, (g // 2) * LANE + VT_ROWS)) for g in range(2 * DIFF_HEADS)]
    _bounded_attn_loop(maps, k_ref, v_ref, acc_ref, tk)
    _diff_finalize(lam_ref, gsub_ref, acc_ref, o_ref, transposed=True)


def _attn_bounded_call(kernel_fn, n_acc, name, args, in_specs, S, W, B, tq):
    return pl.pallas_call(
        kernel_fn,
        grid=(B, S // tq),
        in_specs=in_specs,
        out_specs=pl.BlockSpec((1, tq, BRANCH), lambda b, i: (b, i, 0)),
        out_shape=jax.ShapeDtypeStruct((B, S, BRANCH), bf16),
        scratch_shapes=[pltpu.VMEM((n_acc, VT_ROWS, tq), f32)],
        compiler_params=_cparams(("parallel", "parallel")),
        name=name,
    )(*args)


def _mla_attn_bounded(q, k, v, tq, tk):
    B, S, W = q.shape
    full = pl.BlockSpec((1, S, W), lambda b, i: (b, 0, 0))
    in_specs = [pl.BlockSpec((1, tq, W), lambda b, i: (b, i, 0)), full, full]
    return _attn_bounded_call(functools.partial(_mla_attn_bounded_kernel, tk=tk), MLA_HEADS, "mla_attn_bounded",
                              (q, k, v), in_specs, S, W, B, tq)


def _diff_attn_bounded(lam, q, k, v, gsub, tq, tk):
    B, S, W = v.shape
    in_specs = [pl.BlockSpec(memory_space=pltpu.SMEM),
                pl.BlockSpec((1, tq, BRANCH), lambda b, i: (b, i, 0)),
                pl.BlockSpec((1, S, BRANCH), lambda b, i: (b, 0, 0)),
                pl.BlockSpec((1, S, W), lambda b, i: (b, 0, 0)),
                pl.BlockSpec((1, 2 * DIFF_HEAD_DIM), lambda b, i: (0, 0))]
    return _attn_bounded_call(functools.partial(_diff_attn_bounded_kernel, tk=tk), 2 * DIFF_HEADS,
                              "diff_attn_bounded", (lam, q, k, v, gsub), in_specs, S, W, B, tq)


def _mla_attn(q, k, v, tq, tk):
    B, S, W = q.shape
    return pl.pallas_call(
        _mla_attn_kernel,
        grid=(B, S // tq, S // tk),
        in_specs=[pl.BlockSpec((1, tq, W), lambda b, i, j: (b, i, 0)),
                  pl.BlockSpec((1, tk, W), lambda b, i, j: (b, j, 0)),
                  pl.BlockSpec((1, tk, W), lambda b, i, j: (b, j, 0))],
        out_specs=pl.BlockSpec((1, tq, BRANCH), lambda b, i, j: (b, i, 0)),
        out_shape=jax.ShapeDtypeStruct((B, S, BRANCH), bf16),
        scratch_shapes=[pltpu.VMEM((MLA_HEADS, tq, LANE), f32), pltpu.VMEM((MLA_HEADS, tq, LANE), f32)],
        compiler_params=_cparams(("parallel", "parallel", "arbitrary")),
        name="mla_attn",
    )(q, k, v)


def _diff_attn(lam, q, k, v, gsub, tq, tk):
    B, S, W = v.shape
    return pl.pallas_call(
        _diff_attn_kernel,
        grid=(B, S // tq, S // tk),
        in_specs=[pl.BlockSpec(memory_space=pltpu.SMEM),
                  pl.BlockSpec((1, tq, BRANCH), lambda b, i, j: (b, i, 0)),
                  pl.BlockSpec((1, tk, BRANCH), lambda b, i, j: (b, j, 0)),
                  pl.BlockSpec((1, tk, W), lambda b, i, j: (b, j, 0)),
                  pl.BlockSpec((1, 2 * DIFF_HEAD_DIM), lambda b, i, j: (0, 0))],
        out_specs=pl.BlockSpec((1, tq, BRANCH), lambda b, i, j: (b, i, 0)),
        out_shape=jax.ShapeDtypeStruct((B, S, BRANCH), bf16),
        scratch_shapes=[pltpu.VMEM((2 * DIFF_HEADS, tq, LANE), f32), pltpu.VMEM((2 * DIFF_HEADS, tq, LANE), f32)],
        compiler_params=_cparams(("parallel", "parallel", "arbitrary")),
        name="diff_attn",
    )(lam, q, k, v, gsub)


def _fnet1_kernel(x_ref, w_ref, a_ref):
    a_ref[0] = jnp.dot(w_ref[...], x_ref[0], preferred_element_type=f32).astype(bf16)


def _fnet2_kernel(a_ref, t_ref, cs_ref, o_ref):
    zs = []
    for jj in range(FNET_K1_BLOCK):
        a = jnp.concatenate([a_ref[0, 0, jj], a_ref[0, 1, jj]], axis=0)
        zs.append(jnp.dot(t_ref[jj], a, preferred_element_type=f32))
    zc = jnp.concatenate([jnp.concatenate([z[:FNET_N2], z[FNET_N2:]], axis=1) for z in zs], axis=0).astype(bf16)
    y = jnp.dot(zc, cs_ref[...], preferred_element_type=f32).astype(bf16)
    for jj in range(FNET_K1_BLOCK):
        o_ref[0, :, jj, :] = y[jj * FNET_N2:(jj + 1) * FNET_N2]


def _fnet_consts(S):
    n1 = S // FNET_N2
    a = np.arange(n1)
    ang1 = 2.0 * np.pi * np.outer(a, a) / n1
    w1 = np.concatenate([np.cos(ang1), -np.sin(ang1)], axis=0)
    n2 = np.arange(FNET_N2)
    phi = 2.0 * np.pi * (n2[None, None, :] * a[:, None, None] / S + n2[None, None, :] * n2[None, :, None] / FNET_N2)
    tr, ti = np.cos(phi), -np.sin(phi)
    t = np.concatenate([np.concatenate([tr, -ti], axis=2), np.concatenate([ti, tr], axis=2)], axis=1)
    c = np.arange(BRANCH)
    same = (c[:, None] // FNET_N2) == (c[None, :] // FNET_N2)
    angc = 2.0 * np.pi * np.outer(c % FNET_N2, c % FNET_N2) / FNET_N2
    norm = 1.0 / math.sqrt(S * FNET_N2)
    cs = np.concatenate([np.where(same, np.cos(angc), 0.0), np.where(same, np.sin(angc), 0.0)], axis=0) * norm
    return (jnp.asarray(w1, f32).astype(bf16), jnp.asarray(t, f32).astype(bf16), jnp.asarray(cs, f32).astype(bf16))


def _fnet(u, consts):
    B, S, W = u.shape
    w1, t, cs = consts
    n1 = S // FNET_N2
    cols = FNET_N2 * W
    tn = min(cols, 4096)
    a = pl.pallas_call(
        _fnet1_kernel,
        grid=(B, cols // tn),
        in_specs=[pl.BlockSpec((1, n1, tn), lambda b, i: (b, 0, i)), _const_spec(w1.shape)],
        out_specs=pl.BlockSpec((1, 2 * n1, tn), lambda b, i: (b, 0, i)),
        out_shape=jax.ShapeDtypeStruct((B, 2 * n1, cols), bf16),
        compiler_params=_cparams(("parallel", "parallel")),
        name="fnet1",
    )(u.reshape(B, n1, cols), w1)
    kb = FNET_K1_BLOCK
    y = pl.pallas_call(
        _fnet2_kernel,
        grid=(B, n1 // kb),
        in_specs=[pl.BlockSpec((1, 2, kb, FNET_N2, W), lambda b, i: (b, 0, i, 0, 0)),
                  pl.BlockSpec((kb, 2 * FNET_N2, 2 * FNET_N2), lambda b, i: (i, 0, 0)),
                  _const_spec(cs.shape)],
        out_specs=pl.BlockSpec((1, FNET_N2, kb, W), lambda b, i: (b, 0, i, 0)),
        out_shape=jax.ShapeDtypeStruct((B, FNET_N2, n1, W), bf16),
        compiler_params=_cparams(("parallel", "parallel")),
        name="fnet2",
    )(a.reshape(B, 2, n1, FNET_N2, W), t, cs)
    return y.reshape(B, S, W)


def _hgrn_consts(C):
    L = int(math.log2(C))
    t = np.arange(C)
    m = np.zeros((2, (L + 2) * C, C), np.float32)
    lvl = np.full((2, C, C), -1, np.int32)
    for d in (0, 1):
        for li in range(L):
            n = C >> li
            half = n // 2
            blk = t // n
            mid = blk * n + half
            upper = (t % n) >= half
            for r in range(C):
                if d == 0:
                    if upper[r]:
                        m[d, li * C + r, mid[r]:r + 1] = 1.0
                    else:
                        m[d, li * C + r, r + 1:mid[r]] = 1.0
                else:
                    if upper[r]:
                        m[d, li * C + r, mid[r]:r] = 1.0
                    else:
                        m[d, li * C + r, r:mid[r]] = 1.0
            same = blk[:, None] == blk[None, :]
            if d == 0:
                msk = same & upper[:, None] & (~upper)[None, :]
            else:
                msk = same & (~upper)[:, None] & upper[None, :]
            lvl[d][msk] = li
        for r in range(C):
            if d == 0:
                m[d, L * C + r, :r + 1] = 1.0
                m[d, (L + 1) * C + r, r + 1:] = 1.0
            else:
                m[d, L * C + r, r:] = 1.0
                m[d, (L + 1) * C + r, :r] = 1.0
    c = np.arange(BRANCH)
    headsum = ((c[:, None] // HGRN_DIM) == (c[None, :] // HGRN_DIM)).astype(np.float32)
    lvl = np.tile(lvl, (1, 1, HGRN_HEADS))
    m = np.tile(m, (1, 1, 2))
    return jnp.asarray(m, f32).astype(bf16), jnp.asarray(lvl), jnp.asarray(headsum, f32).astype(bf16)


def _hgrn_kernel(qf_ref, vf_ref, kf_ref, lff_ref, qb_ref, vb_ref, kb_ref, lfb_ref, m_ref, lvl_ref, hs_ref,
                 of_ref, ob_ref, st_ref, *, C, L):
    @pl.when(pl.program_id(1) == 0)
    def _():
        st_ref[...] = jnp.zeros(st_ref.shape, f32)

    refs = ((qf_ref, vf_ref, kf_ref, lff_ref, of_ref), (qb_ref, vb_ref, kb_ref, lfb_ref, ob_ref))
    head = lax.broadcasted_iota(jnp.int32, (1, BRANCH), 1) // HGRN_DIM
    r = lax.broadcasted_iota(jnp.int32, (BRANCH, BRANCH), 0) // HGRN_DIM
    cidx = lax.broadcasted_iota(jnp.int32, (BRANCH, BRANCH), 1) // HGRN_DIM
    n_chunks = qf_ref.shape[1] // C
    order = (tuple(range(n_chunks)), tuple(reversed(range(n_chunks))))
    chains = [(d, order[d][j]) for j in range(n_chunks) for d in (0, 1)]
    rows = {c: slice(c[1] * C, (c[1] + 1) * C) for c in chains}
    q = {c: refs[c[0]][0][0, rows[c], :].astype(f32) for c in chains}
    v = {c: refs[c[0]][1][0, rows[c], :] for c in chains}
    k = {c: refs[c[0]][2][0, rows[c], :].astype(f32) for c in chains}
    e = {}
    for c in chains:
        lf = refs[c[0]][3][0, rows[c], :]
        lf2 = jnp.concatenate([lf[:, :BRANCH], lf[:, BRANCH:]], axis=0)
        e[c] = jnp.dot(m_ref[c[0]], lf2, preferred_element_type=f32)

    o = {}
    st = [st_ref[0], st_ref[1]]
    for c in chains:
        d = c[0]
        ein = jnp.exp(e[c][L * C:(L + 1) * C])
        o[c] = lax.dot_general((q[c] * ein).astype(bf16), st[d].astype(bf16), _NT, preferred_element_type=f32)
        kout = (k[c] * jnp.exp(e[c][(L + 1) * C:(L + 2) * C])).astype(bf16)
        upd = lax.dot_general(v[c], kout, _TN, preferred_element_type=f32)
        total = ein[C - 1:C] if d == 0 else ein[0:1]
        st[d] = st[d] * total + jnp.where(r == cidx, upd, 0.0)
        o[c] = o[c] + jnp.dot((q[c] * k[c]).astype(bf16), hs_ref[...], preferred_element_type=f32) * v[c].astype(f32)
    st_ref[0] = st[0]
    st_ref[1] = st[1]

    scores = {c: jnp.zeros((C, HGRN_HEADS * C), f32) for c in chains}
    for li in range(L):
        for c in chains:
            ex = jnp.exp(e[c][li * C:(li + 1) * C])
            qt = (q[c] * ex).astype(bf16)
            kt = (k[c] * ex).astype(bf16)
            kstack = jnp.concatenate([jnp.where(head == hh, kt, jnp.zeros_like(kt)) for hh in range(HGRN_HEADS)],
                                     axis=0)
            s = lax.dot_general(qt, kstack, _NT, preferred_element_type=f32)
            scores[c] = jnp.where(lvl_ref[c[0]] == li, s, scores[c])
    for c in chains:
        vstack = jnp.concatenate([jnp.where(head == hh, v[c], jnp.zeros_like(v[c])) for hh in range(HGRN_HEADS)],
                                 axis=0)
        oc = o[c] + jnp.dot(scores[c].astype(bf16), vstack, preferred_element_type=f32)
        refs[c[0]][4][0, rows[c], :] = oc.astype(bf16)


def _hgrn(hq, hv, hk, hlf, consts, C):
    B, S, W = hq.shape
    m, lvl, hs = consts
    L = int(math.log2(C))
    rows = min(S, HGRN_CHUNKS_PER_STEP * C)
    nb = S // rows
    fwd = lambda w, c: pl.BlockSpec((1, rows, w), lambda b, i: (b, i, c))
    bwd = lambda w, c: pl.BlockSpec((1, rows, w), lambda b, i: (b, nb - 1 - i, c))
    return pl.pallas_call(
        functools.partial(_hgrn_kernel, C=C, L=L),
        grid=(B, nb),
        in_specs=[fwd(W, 0), fwd(W, 0), fwd(W, 0), fwd(2 * W, 0), bwd(W, 0), bwd(W, 0), bwd(W, 1), bwd(2 * W, 1),
                  _const_spec(m.shape), _const_spec(lvl.shape), _const_spec(hs.shape)],
        out_specs=[fwd(W, 0), bwd(W, 0)],
        out_shape=[jax.ShapeDtypeStruct((B, S, W), bf16)] * 2,
        scratch_shapes=[pltpu.VMEM((2, BRANCH, BRANCH), f32)],
        compiler_params=_cparams(("parallel", "arbitrary")),
        name="hgrn",
    )(hq, hv, hk, hlf, hq, hv, hk, hlf, m, lvl, hs)


def _merge_ffn_kernel(x_ref, om_ref, of_ref, od_ref, ohf_ref, ohb_ref, hg_ref, ln_ref, wg_ref, wbm_ref, wbf_ref,
                      wbd_ref, wbh_ref, wo_ref, gon_ref, g64_ref, lnf_ref, wgu_ref, wd_ref, o_ref):
    x = x_ref[0]
    h = _rms(x, ln_ref[...]).astype(bf16)
    oh = ohf_ref[0].astype(f32) + ohb_ref[0].astype(f32)
    ms = jnp.dot((oh * oh).astype(bf16), g64_ref[...], preferred_element_type=f32)
    oh = (oh * lax.rsqrt(ms + EPS) * gon_ref[...] * _sigmoid(hg_ref[0].astype(f32))).astype(bf16)
    merged = None
    for n, (o_n, w_ref) in enumerate(((om_ref[0], wbm_ref), (of_ref[0], wbf_ref), (od_ref[0], wbd_ref), (oh, wbh_ref))):
        gate = _sigmoid(jnp.dot(h, wg_ref[:, n * D_MODEL:(n + 1) * D_MODEL], preferred_element_type=f32))
        y = gate * jnp.dot(o_n, w_ref[...], preferred_element_type=f32)
        merged = y if merged is None else merged + y
    x = x + jnp.dot(merged.astype(bf16), wo_ref[...], preferred_element_type=f32)

    h = _rms(x, lnf_ref[...]).astype(bf16)
    acc = x
    for c in range(D_FF // FF_CHUNK):
        g = jnp.dot(h, wgu_ref[:, c * FF_CHUNK:(c + 1) * FF_CHUNK], preferred_element_type=f32)
        u = jnp.dot(h, wgu_ref[:, D_FF + c * FF_CHUNK:D_FF + (c + 1) * FF_CHUNK], preferred_element_type=f32)
        a = (g * _sigmoid(g) * u).astype(bf16)
        acc = acc + jnp.dot(a, wd_ref[c], preferred_element_type=f32)
    o_ref[0] = acc


def _merge_ffn(x, om, of, od, oh, hg, p, tm):
    B, S, _ = x.shape
    row = lambda w: pl.BlockSpec((1, tm, w), lambda b, i: (b, i, 0))
    consts = (p["ln_mix"], p["wgate"], p["wbm"], p["wbf"], p["wbd"], p["wbh"], p["wout"], p["gon"], p["g64"],
              p["ln_ffn"], p["wgu"], p["wd"])
    return pl.pallas_call(
        _merge_ffn_kernel,
        grid=(B, S // tm),
        in_specs=[row(D_MODEL)] + [row(BRANCH)] * 6
                 + [_const_spec(c.shape) for c in consts],
        out_specs=row(D_MODEL),
        out_shape=jax.ShapeDtypeStruct((B, S, D_MODEL), f32),
        compiler_params=_cparams(("parallel", "parallel")),
        name="merge_ffn",
    )(x, om, of, od, oh[0], oh[1], hg, *consts)


def _pad_heads(w, heads, width):
    lead = w.shape[:-1]
    w = w.reshape(lead + (heads, width))
    w = jnp.pad(w, [(0, 0)] * len(lead) + [(0, 0), (0, LANE - width)])
    return w.reshape(lead + (heads * LANE,))


def _rope_tables(S):
    def cs(dim):
        inv = 1.0 / (ROPE_THETA ** (jnp.arange(0, dim, 2, dtype=f32) / dim))
        ang = jnp.arange(S, dtype=f32)[:, None] * inv[None, :]
        return jnp.cos(ang), jnp.sin(ang)

    cm, sm = cs(MLA_ROPE)
    one, zero = jnp.ones((S, MLA_NOPE), f32), jnp.zeros((S, MLA_NOPE), f32)
    z16, z32, o32 = jnp.zeros((S, 16), f32), jnp.zeros((S, 32), f32), jnp.ones((S, 32), f32)
    rope_m = jnp.stack([jnp.concatenate([one, cm, cm, o32], 1),
                        jnp.concatenate([zero, -sm, z16, z32], 1),
                        jnp.concatenate([zero, z16, sm, z32], 1)])
    cd, sd = cs(DIFF_HEAD_DIM)
    rope_d = jnp.stack([jnp.tile(jnp.concatenate([cd, cd], 1), (1, 4)),
                        jnp.tile(jnp.concatenate([-sd, z16], 1), (1, 4)),
                        jnp.tile(jnp.concatenate([z16, sd], 1), (1, 4))])
    return rope_m, rope_d


def _block_mean(width, group):
    c = np.arange(width)
    return jnp.asarray(((c[:, None] // group) == (c[None, :] // group)) / group, f32).astype(bf16)


def _layer_params(l, a, lower_bounds):
    w = a["w_in"][l]
    offs = np.cumsum([0, 384, 128, 32, 256, 256, 256, 256, 256, 256, 256, 256, 256, 4096])
    col = lambda i: w[:, offs[i]:offs[i + 1]]
    wcat = jnp.concatenate([col(0), col(1), jnp.pad(col(2), ((0, 0), (MLA_NOPE, LANE - MLA_QK))), col(3), col(4),
                            col(5), _pad_heads(col(6), DIFF_HEADS, 2 * DIFF_HEAD_DIM), col(7), col(8), col(9),
                            col(10), col(11)], axis=1).astype(bf16)
    wukv = a["mla_w_ukv"][l].reshape(MLA_KV_LORA, MLA_HEADS, MLA_NOPE + MLA_V)
    wukv = jnp.concatenate([_pad_heads(wukv[:, :, :MLA_NOPE].reshape(MLA_KV_LORA, -1), MLA_HEADS, MLA_NOPE),
                            _pad_heads(wukv[:, :, MLA_NOPE:].reshape(MLA_KV_LORA, -1), MLA_HEADS, MLA_V)], axis=1)
    lam_init = 0.8 - 0.6 * math.exp(-0.3 * l)
    lam = (jnp.exp(jnp.sum(a["diff_lq1"][l] * a["diff_lk1"][l])) - jnp.exp(jnp.sum(a["diff_lq2"][l] * a["diff_lk2"][l]))
           + lam_init)
    wb = a["w_branch"][l]
    nc = D_FF // FF_CHUNK
    row = lambda v: v.reshape(1, -1).astype(f32)
    return {
        "ln_mix": row(a["ln_mix"][l]),
        "wcat": wcat,
        "gqa": row(a["mla_g_qa"][l]),
        "wuq": _pad_heads(a["mla_w_uq"][l], MLA_HEADS, MLA_QK).astype(bf16),
        "gkva": row(a["mla_g_kva"][l]),
        "wukv": wukv.astype(bf16),
        "gqn": row(jnp.pad(a["mla_g_qn"][l], (0, LANE - MLA_QK))) * (MLA_QK ** -0.5 * LOG2E),
        "gkn": row(jnp.pad(a["mla_g_kn"][l], (0, LANE - MLA_QK))),
        "gdq": row(jnp.tile(a["diff_g_qn"][l], 2 * DIFF_HEADS)) * (DIFF_HEAD_DIM ** -0.5 * LOG2E),
        "gdk": row(jnp.tile(a["diff_g_kn"][l], 2 * DIFF_HEADS)),
        "g32": _block_mean(BRANCH, DIFF_HEAD_DIM),
        "lb": lower_bounds[:, l].astype(f32),
        "lam": lam.reshape(1).astype(f32),
        "mla_bound": MLA_QK ** 0.5 * LOG2E * jnp.max(jnp.abs(a["mla_g_qn"][l])) * jnp.max(jnp.abs(a["mla_g_kn"][l])),
        "diff_bound": (DIFF_HEAD_DIM ** 0.5 * LOG2E * jnp.max(jnp.abs(a["diff_g_qn"][l]))
                       * jnp.max(jnp.abs(a["diff_g_kn"][l]))),
        "gsub": row(a["diff_g_sub"][l]) * (1.0 - lam_init),
        "wgate": col(12).astype(bf16),
        "wbm": wb[0].astype(bf16),
        "wbf": wb[1].astype(bf16),
        "wbd": wb[2].astype(bf16),
        "wbh": wb[3].astype(bf16),
        "wout": a["w_out"][l].astype(bf16),
        "gon": row(jnp.tile(a["hgrn_g_on"][l], HGRN_HEADS)),
        "g64": _block_mean(BRANCH, HGRN_DIM),
        "ln_ffn": row(a["ln_ffn"][l]),
        "wgu": a["w_gate_up"][l].astype(bf16),
        "wd": a["w_down"][l].reshape(nc, FF_CHUNK, D_MODEL).astype(bf16),
    }


def _tiles(S):
    return {"tm_in": min(256, S),"tq": min(512, S), "tk": min(512, S), "tqb": min(512, S), "tkb": min(1024, S),
            "hgrn_chunk": min(64, S),
            "tm_merge": min(512, S)}


def kernel(x, ln_mix, w_in, mla_g_qa, mla_w_uq, mla_g_kva, mla_w_ukv, mla_g_qn, mla_g_kn, diff_g_qn, diff_g_kn,
           diff_lq1, diff_lk1, diff_lq2, diff_lk2, diff_g_sub, hgrn_lb_logits, hgrn_g_on, w_branch, w_out, ln_ffn,
           w_gate_up, w_down):
    a = dict(ln_mix=ln_mix, w_in=w_in, mla_g_qa=mla_g_qa, mla_w_uq=mla_w_uq, mla_g_kva=mla_g_kva,
             mla_w_ukv=mla_w_ukv, mla_g_qn=mla_g_qn, mla_g_kn=mla_g_kn, diff_g_qn=diff_g_qn, diff_g_kn=diff_g_kn,
             diff_lq1=diff_lq1, diff_lk1=diff_lk1, diff_lq2=diff_lq2, diff_lk2=diff_lk2, diff_g_sub=diff_g_sub,
             hgrn_g_on=hgrn_g_on, w_branch=w_branch, w_out=w_out, ln_ffn=ln_ffn, w_gate_up=w_gate_up, w_down=w_down)
    S = x.shape[1]
    t = _tiles(S)
    rope_m, rope_d = _rope_tables(S)
    lb_p = jax.nn.softmax(hgrn_lb_logits.astype(f32), axis=1)
    lower_bounds = jnp.cumsum(lb_p, axis=1) - lb_p[:, :1]
    fnet_consts = _fnet_consts(S)
    hgrn_consts = _hgrn_consts(t["hgrn_chunk"])
    for l in range(DEPTH):
        p = _layer_params(l, a, lower_bounds)
        qm, km, vm, fn, qd, kd, vd, hq, hv, hk, hlf, hg = _in_proj(x, p, rope_m, rope_d, t["tm_in"])
        om = lax.cond(p["mla_bound"] <= SCORE_BOUND_LOG2,
                      lambda: _mla_attn_bounded(qm, km, vm, t["tqb"], t["tkb"]),
                      lambda: _mla_attn(qm, km, vm, t["tq"], t["tk"]))
        od = lax.cond(p["diff_bound"] <= SCORE_BOUND_LOG2,
                      lambda: _diff_attn_bounded(p["lam"], qd, kd, vd, p["gsub"], t["tqb"], t["tkb"]),
                      lambda: _diff_attn(p["lam"], qd, kd, vd, p["gsub"], t["tq"], t["tk"]))
        of = _fnet(fn, fnet_consts)
        oh = _hgrn(hq, hv, hk, hlf, hgrn_consts, t["hgrn_chunk"])
        x = _merge_ffn(x, om, of, od, oh, hg, p, t["tm_merge"])
    return x
```

```python
import functools
import math

import numpy as np
import jax
import jax.numpy as jnp
from jax import lax
from jax.experimental import pallas as pl
from jax.experimental.pallas import tpu as pltpu

f32 = jnp.float32
bf16 = jnp.bfloat16

D_MODEL = 1024
DEPTH = 2
MLA_HEADS = 4
MLA_Q_LORA = 384
MLA_KV_LORA = 128
MLA_NOPE = 64
MLA_ROPE = 32
MLA_V = 64
MLA_QK = MLA_NOPE + MLA_ROPE
DIFF_HEADS = 4
DIFF_HEAD_DIM = 32
HGRN_HEADS = 4
HGRN_DIM = 64
BRANCH = 256
D_FF = 2816
FF_CHUNK = 256
ROPE_THETA = 10000.0
EPS = 1e-6
LOG2E = 1.4426950408889634
LANE = 128
FNET_N2 = 64
FNET_K1_BLOCK = 8
HGRN_CHUNKS_PER_STEP = 8
IN_PROJ_CHAINS_IN_FLIGHT = 4
ATTN_KEY_SUB = 256
VT_ROWS = LANE
ATTN_CHAINS_IN_FLIGHT = 8
VMEM_LIMIT = 56 * 1024 * 1024
SCORE_BOUND_LOG2 = 60.0

_SEG = {}
_off = 0
for _name, _w in (("cq", 384), ("ckv", 128), ("krope", 128), ("fnet", 256), ("dq", 256), ("dk", 256),
                  ("dv", 512), ("hq", 256), ("hi", 256), ("hff", 256), ("hfb", 256), ("hg", 256)):
    _SEG[_name] = (_off, _off + _w)
    _off += _w
W_CAT = _off


def _cparams(sem):
    return pltpu.CompilerParams(dimension_semantics=sem, vmem_limit_bytes=VMEM_LIMIT)


def _const_spec(shape):
    nd = len(shape)
    return pl.BlockSpec(shape, lambda *_: (0,) * nd, pipeline_mode=pl.Buffered(1))


def _rms(x, g):
    return x * lax.rsqrt(jnp.mean(x * x, axis=-1, keepdims=True) + EPS) * g


def _sigmoid(x):
    return 1.0 / (1.0 + jnp.exp(-x))


def _rope(t, tab_ref):
    return t * tab_ref[0] + pltpu.roll(t, LANE - 16, 1) * tab_ref[1] + pltpu.roll(t, 16, 1) * tab_ref[2]


def _in_proj_kernel(x_ref, ln_ref, wcat_ref, gqa_ref, wuq_ref, gkva_ref, wukv_ref, gqn_ref, gkn_ref,
                    ropem_ref, gdq_ref, gdk_ref, g32_ref, roped_ref, lb_ref,
                    qm_ref, km_ref, vm_ref, fn_ref, qd_ref, kd_ref, vd_ref, hq_ref, hv_ref, hk_ref,
                    hlf_ref, hg_ref):
    h = _rms(x_ref[0], ln_ref[...]).astype(bf16)

    def seg(name):
        a, b = _SEG[name]
        return jnp.dot(h, wcat_ref[:, a:b], preferred_element_type=f32)

    ones_lane = (lax.broadcasted_iota(jnp.int32, (1, LANE), 1) == MLA_V).astype(f32)


    def mla_q():
        cq = seg("cq")
        yield
        q = jnp.dot(_rms(cq, gqa_ref[...]).astype(bf16), wuq_ref[...], preferred_element_type=f32)
        yield
        for hh in range(MLA_HEADS):
            sl = slice(hh * LANE, (hh + 1) * LANE)
            qh = q[:, sl]
            ms = jnp.sum(qh * qh, axis=-1, keepdims=True) * (1.0 / MLA_QK)
            qm_ref[0, :, sl] = _rope(qh * lax.rsqrt(ms + EPS) * gqn_ref[...], ropem_ref).astype(bf16)

    def mla_kv():
        ckv = seg("ckv")
        krope = seg("krope")
        yield
        kv = jnp.dot(_rms(ckv, gkva_ref[...]).astype(bf16), wukv_ref[...], preferred_element_type=f32)
        yield
        for hh in range(MLA_HEADS):
            sl = slice(hh * LANE, (hh + 1) * LANE)
            kh = kv[:, sl] + krope
            ms = jnp.sum(kh * kh, axis=-1, keepdims=True) * (1.0 / MLA_QK)
            km_ref[0, :, sl] = _rope(kh * lax.rsqrt(ms + EPS) * gkn_ref[...], ropem_ref).astype(bf16)
            vsl = slice((MLA_HEADS + hh) * LANE, (MLA_HEADS + hh + 1) * LANE)
            vm_ref[0, :, sl] = (kv[:, vsl] + ones_lane).astype(bf16)

    def diff_qk(name, g_ref, o_ref):
        t = seg(name)
        yield
        ms = jnp.dot((t * t).astype(bf16), g32_ref[...], preferred_element_type=f32)
        yield
        t = t * lax.rsqrt(ms + EPS) * g_ref[...]
        for c in range(BRANCH // LANE):
            sl = slice(c * LANE, (c + 1) * LANE)
            o_ref[0, :, sl] = _rope(t[:, sl], roped_ref).astype(bf16)

    def diff_v():
        dv = seg("dv")
        yield
        for hh in range(DIFF_HEADS):
            sl = slice(hh * LANE, (hh + 1) * LANE)
            vd_ref[0, :, sl] = (dv[:, sl] + ones_lane).astype(bf16)

    def plain(name, o_ref):
        z = seg(name)
        yield
        o_ref[0] = z.astype(bf16)

    def hgrn_gate(d, name):
        z = seg(name)
        yield
        lb = lb_ref[d:d + 1, :]
        f = lb + (1.0 - lb) * _sigmoid(z)
        hk_ref[0, :, d * BRANCH:(d + 1) * BRANCH] = (1.0 - f).astype(bf16)
        lf = jnp.log(f)
        hi = lf.astype(bf16)
        hlf_ref[0, :, 2 * d * BRANCH:(2 * d + 1) * BRANCH] = hi
        hlf_ref[0, :, (2 * d + 1) * BRANCH:(2 * d + 2) * BRANCH] = (lf - hi.astype(f32)).astype(bf16)

    _round_robin([mla_q(), hgrn_gate(0, "hff"), mla_kv(), hgrn_gate(1, "hfb"), diff_qk("dq", gdq_ref, qd_ref),
                  plain("fnet", fn_ref), diff_qk("dk", gdk_ref, kd_ref), plain("hq", hq_ref), diff_v(),
                  plain("hi", hv_ref), plain("hg", hg_ref)], IN_PROJ_CHAINS_IN_FLIGHT)


def _in_proj(x, p, rope_m, rope_d, tm):
    B, S, _ = x.shape
    row = lambda w: pl.BlockSpec((1, tm, w), lambda b, i: (b, i, 0))
    tab = pl.BlockSpec((3, tm, LANE), lambda b, i: (0, i, 0))
    out_w = (512, 512, 512, 256, 256, 256, 512, 256, 256, 512, 1024, 256)
    consts = (p["ln_mix"], p["wcat"], p["gqa"], p["wuq"], p["gkva"], p["wukv"], p["gqn"], p["gkn"])
    consts2 = (p["gdq"], p["gdk"], p["g32"])
    in_specs = ([row(D_MODEL)] + [_const_spec(c.shape) for c in consts] + [tab]
                + [_const_spec(c.shape) for c in consts2] + [tab, _const_spec(p["lb"].shape)])
    return pl.pallas_call(
        _in_proj_kernel,
        grid=(B, S // tm),
        in_specs=in_specs,
        out_specs=[row(w) for w in out_w],
        out_shape=[jax.ShapeDtypeStruct((B, S, w), bf16) for w in out_w],
        compiler_params=_cparams(("parallel", "parallel")),
        name="in_proj",
    )(x, *consts, rope_m, *consts2, rope_d, p["lb"])


def _online_softmax_step(s, v, m_ref, acc_ref, idx):
    m_prev = m_ref[idx]
    m_new = jnp.maximum(m_prev, jnp.max(s, axis=1, keepdims=True))
    alpha = jnp.exp2(m_prev - m_new)
    p = jnp.exp2((s - m_new[:, :1]).astype(bf16))
    acc_ref[idx] = alpha * acc_ref[idx] + jnp.dot(p, v, preferred_element_type=f32)
    m_ref[idx] = m_new


_NT = (((1,), (1,)), ((), ()))
_TN = (((0,), (0,)), ((), ()))


def _mla_attn_kernel(q_ref, k_ref, v_ref, o_ref, m_ref, acc_ref):
    j = pl.program_id(2)

    @pl.when(j == 0)
    def _():
        m_ref[...] = jnp.full(m_ref.shape, -jnp.inf, f32)
        acc_ref[...] = jnp.zeros(acc_ref.shape, f32)

    for hh in range(MLA_HEADS):
        sl = slice(hh * LANE, (hh + 1) * LANE)
        s = lax.dot_general(q_ref[0, :, sl], k_ref[0, :, sl], _NT, preferred_element_type=f32)
        _online_softmax_step(s, v_ref[0, :, sl], m_ref, acc_ref, hh)

    @pl.when(j == pl.num_programs(2) - 1)
    def _():
        _mla_finalize(acc_ref, o_ref)


def _mla_finalize(acc_ref, o_ref, transposed=False):
    for hh in range(MLA_HEADS):
        acc = acc_ref[hh]
        if transposed:
            o = (acc[:MLA_V] * (1.0 / acc[MLA_V:MLA_V + 1])).T
        else:
            o = acc[:, :MLA_V] * (1.0 / acc[:, MLA_V:MLA_V + 1])
        o_ref[0, :, hh * MLA_V:(hh + 1) * MLA_V] = o.astype(bf16)


def _diff_finalize(lam_ref, gsub_ref, acc_ref, o_ref, transposed=False):
    lam = lam_ref[0]
    dv = 2 * DIFF_HEAD_DIM
    for hh in range(DIFF_HEADS):
        a1 = acc_ref[2 * hh]
        a2 = acc_ref[2 * hh + 1]
        if transposed:
            o = a1[:dv] * (1.0 / a1[dv:dv + 1]) - a2[:dv] * (lam / a2[dv:dv + 1])
            o = (o * lax.rsqrt(jnp.mean(o * o, axis=0, keepdims=True) + EPS)).T
        else:
            o = a1[:, :dv] * (1.0 / a1[:, dv:dv + 1]) - a2[:, :dv] * (lam / a2[:, dv:dv + 1])
            o = o * lax.rsqrt(jnp.mean(o * o, axis=-1, keepdims=True) + EPS)
        o_ref[0, :, hh * dv:(hh + 1) * dv] = (o * gsub_ref[...]).astype(bf16)


def _diff_attn_kernel(lam_ref, q_ref, k_ref, v_ref, gsub_ref, o_ref, m_ref, acc_ref):
    j = pl.program_id(2)

    @pl.when(j == 0)
    def _():
        m_ref[...] = jnp.full(m_ref.shape, -jnp.inf, f32)
        acc_ref[...] = jnp.zeros(acc_ref.shape, f32)

    q = q_ref[0]
    k = k_ref[0]
    group = lax.broadcasted_iota(jnp.int32, (1, BRANCH), 1) // DIFF_HEAD_DIM
    for g in range(2 * DIFF_HEADS):
        qg = jnp.where(group == g, q, jnp.zeros_like(q))
        s = lax.dot_general(qg, k, _NT, preferred_element_type=f32)
        hh = g // 2
        _online_softmax_step(s, v_ref[0, :, hh * LANE:(hh + 1) * LANE], m_ref, acc_ref, g)

    @pl.when(j == pl.num_programs(2) - 1)
    def _():
        _diff_finalize(lam_ref, gsub_ref, acc_ref, o_ref)


def _round_robin(chains, in_flight):
    pending, active = list(chains), []
    while pending or active:
        while pending and len(active) < in_flight:
            active.append(pending.pop(0))
        for c in list(active):
            if next(c, "done") == "done":
                active.remove(c)


def _bounded_attn_loop(maps, k_ref, v_ref, acc_ref, tk):
    acc_ref[...] = jnp.zeros(acc_ref.shape, f32)

    def kv_block(j, carry):
        def chain(g, r0):
            q, ksl, vsl = maps[g]
            keys = pl.ds(pl.multiple_of(j * tk + r0, ATTN_KEY_SUB), ATTN_KEY_SUB)
            st = lax.dot_general(k_ref[0, keys, ksl], q, _NT, preferred_element_type=f32)
            yield
            pt = jnp.exp2(st).astype(bf16)
            yield
            acc_ref[g] += lax.dot_general(v_ref[0, keys, vsl], pt, _TN, preferred_element_type=f32)

        _round_robin([chain(g, r0) for r0 in range(0, tk, ATTN_KEY_SUB) for g in range(len(maps))],
                     ATTN_CHAINS_IN_FLIGHT)
        return carry

    lax.fori_loop(0, k_ref.shape[1] // tk, kv_block, 0)


def _mla_attn_bounded_kernel(q_ref, k_ref, v_ref, o_ref, acc_ref, *, tk):
    maps = [(q_ref[0, :, hh * LANE:(hh + 1) * LANE], slice(hh * LANE, (hh + 1) * LANE),
             slice(hh * LANE, hh * LANE + VT_ROWS)) for hh in range(MLA_HEADS)]
    _bounded_attn_loop(maps, k_ref, v_ref, acc_ref, tk)
    _mla_finalize(acc_ref, o_ref, transposed=True)


def _diff_attn_bounded_kernel(lam_ref, q_ref, k_ref, v_ref, gsub_ref, o_ref, acc_ref, *, tk):
    q = q_ref[0]
    group = lax.broadcasted_iota(jnp.int32, (1, BRANCH), 1) // DIFF_HEAD_DIM
    maps = [(jnp.where(group == g, q, jnp.zeros_like(q)), slice(0, BRANCH),
             slice((g // 2) * LANE, (g // 2) * LANE + VT_ROWS)) for g in range(2 * DIFF_HEADS)]
    _bounded_attn_loop(maps, k_ref, v_ref, acc_ref, tk)
    _diff_finalize(lam_ref, gsub_ref, acc_ref, o_ref, transposed=True)


def _attn_bounded_call(kernel_fn, n_acc, name, args, in_specs, S, W, B, tq):
    return pl.pallas_call(
        kernel_fn,
        grid=(B, S // tq),
        in_specs=in_specs,
        out_specs=pl.BlockSpec((1, tq, BRANCH), lambda b, i: (b, i, 0)),
        out_shape=jax.ShapeDtypeStruct((B, S, BRANCH), bf16),
        scratch_shapes=[pltpu.VMEM((n_acc, VT_ROWS, tq), f32)],
        compiler_params=_cparams(("parallel", "parallel")),
        name=name,
    )(*args)


def _mla_attn_bounded(q, k, v, tq, tk):
    B, S, W = q.shape
    full = pl.BlockSpec((1, S, W), lambda b, i: (b, 0, 0))
    in_specs = [pl.BlockSpec((1, tq, W), lambda b, i: (b, i, 0)), full, full]
    return _attn_bounded_call(functools.partial(_mla_attn_bounded_kernel, tk=tk), MLA_HEADS, "mla_attn_bounded",
                              (q, k, v), in_specs, S, W, B, tq)


def _diff_attn_bounded(lam, q, k, v, gsub, tq, tk):
    B, S, W = v.shape
    in_specs = [pl.BlockSpec(memory_space=pltpu.SMEM),
                pl.BlockSpec((1, tq, BRANCH), lambda b, i: (b, i, 0)),
                pl.BlockSpec((1, S, BRANCH), lambda b, i: (b, 0, 0)),
                pl.BlockSpec((1, S, W), lambda b, i: (b, 0, 0)),
                pl.BlockSpec((1, 2 * DIFF_HEAD_DIM), lambda b, i: (0, 0))]
    return _attn_bounded_call(functools.partial(_diff_attn_bounded_kernel, tk=tk), 2 * DIFF_HEADS,
                              "diff_attn_bounded", (lam, q, k, v, gsub), in_specs, S, W, B, tq)


def _mla_attn(q, k, v, tq, tk):
    B, S, W = q.shape
    return pl.pallas_call(
        _mla_attn_kernel,
        grid=(B, S // tq, S // tk),
        in_specs=[pl.BlockSpec((1, tq, W), lambda b, i, j: (b, i, 0)),
                  pl.BlockSpec((1, tk, W), lambda b, i, j: (b, j, 0)),
                  pl.BlockSpec((1, tk, W), lambda b, i, j: (b, j, 0))],
        out_specs=pl.BlockSpec((1, tq, BRANCH), lambda b, i, j: (b, i, 0)),
        out_shape=jax.ShapeDtypeStruct((B, S, BRANCH), bf16),
        scratch_shapes=[pltpu.VMEM((MLA_HEADS, tq, LANE), f32), pltpu.VMEM((MLA_HEADS, tq, LANE), f32)],
        compiler_params=_cparams(("parallel", "parallel", "arbitrary")),
        name="mla_attn",
    )(q, k, v)


def _diff_attn(lam, q, k, v, gsub, tq, tk):
    B, S, W = v.shape
    return pl.pallas_call(
        _diff_attn_kernel,
        grid=(B, S // tq, S // tk),
        in_specs=[pl.BlockSpec(memory_space=pltpu.SMEM),
                  pl.BlockSpec((1, tq, BRANCH), lambda b, i, j: (b, i, 0)),
                  pl.BlockSpec((1, tk, BRANCH), lambda b, i, j: (b, j, 0)),
                  pl.BlockSpec((1, tk, W), lambda b, i, j: (b, j, 0)),
                  pl.BlockSpec((1, 2 * DIFF_HEAD_DIM), lambda b, i, j: (0, 0))],
        out_specs=pl.BlockSpec((1, tq, BRANCH), lambda b, i, j: (b, i, 0)),
        out_shape=jax.ShapeDtypeStruct((B, S, BRANCH), bf16),
        scratch_shapes=[pltpu.VMEM((2 * DIFF_HEADS, tq, LANE), f32), pltpu.VMEM((2 * DIFF_HEADS, tq, LANE), f32)],
        compiler_params=_cparams(("parallel", "parallel", "arbitrary")),
        name="diff_attn",
    )(lam, q, k, v, gsub)


def _fnet1_kernel(x_ref, w_ref, a_ref):
    a_ref[0] = jnp.dot(w_ref[...], x_ref[0], preferred_element_type=f32).astype(bf16)


def _fnet2_kernel(a_ref, t_ref, cs_ref, o_ref):
    zs = []
    for jj in range(FNET_K1_BLOCK):
        a = jnp.concatenate([a_ref[0, 0, jj], a_ref[0, 1, jj]], axis=0)
        zs.append(jnp.dot(t_ref[jj], a, preferred_element_type=f32))
    zc = jnp.concatenate([jnp.concatenate([z[:FNET_N2], z[FNET_N2:]], axis=1) for z in zs], axis=0).astype(bf16)
    y = jnp.dot(zc, cs_ref[...], preferred_element_type=f32).astype(bf16)
    for jj in range(FNET_K1_BLOCK):
        o_ref[0, :, jj, :] = y[jj * FNET_N2:(jj + 1) * FNET_N2]


def _fnet_consts(S):
    n1 = S // FNET_N2
    a = np.arange(n1)
    ang1 = 2.0 * np.pi * np.outer(a, a) / n1
    w1 = np.concatenate([np.cos(ang1), -np.sin(ang1)], axis=0)
    n2 = np.arange(FNET_N2)
    phi = 2.0 * np.pi * (n2[None, None, :] * a[:, None, None] / S + n2[None, None, :] * n2[None, :, None] / FNET_N2)
    tr, ti = np.cos(phi), -np.sin(phi)
    t = np.concatenate([np.concatenate([tr, -ti], axis=2), np.concatenate([ti, tr], axis=2)], axis=1)
    c = np.arange(BRANCH)
    same = (c[:, None] // FNET_N2) == (c[None, :] // FNET_N2)
    angc = 2.0 * np.pi * np.outer(c % FNET_N2, c % FNET_N2) / FNET_N2
    norm = 1.0 / math.sqrt(S * FNET_N2)
    cs = np.concatenate([np.where(same, np.cos(angc), 0.0), np.where(same, np.sin(angc), 0.0)], axis=0) * norm
    return (jnp.asarray(w1, f32).astype(bf16), jnp.asarray(t, f32).astype(bf16), jnp.asarray(cs, f32).astype(bf16))


def _fnet(u, consts):
    B, S, W = u.shape
    w1, t, cs = consts
    n1 = S // FNET_N2
    cols = FNET_N2 * W
    tn = min(cols, 4096)
    a = pl.pallas_call(
        _fnet1_kernel,
        grid=(B, cols // tn),
        in_specs=[pl.BlockSpec((1, n1, tn), lambda b, i: (b, 0, i)), _const_spec(w1.shape)],
        out_specs=pl.BlockSpec((1, 2 * n1, tn), lambda b, i: (b, 0, i)),
        out_shape=jax.ShapeDtypeStruct((B, 2 * n1, cols), bf16),
        compiler_params=_cparams(("parallel", "parallel")),
        name="fnet1",
    )(u.reshape(B, n1, cols), w1)
    kb = FNET_K1_BLOCK
    y = pl.pallas_call(
        _fnet2_kernel,
        grid=(B, n1 // kb),
        in_specs=[pl.BlockSpec((1, 2, kb, FNET_N2, W), lambda b, i: (b, 0, i, 0, 0)),
                  pl.BlockSpec((kb, 2 * FNET_N2, 2 * FNET_N2), lambda b, i: (i, 0, 0)),
                  _const_spec(cs.shape)],
        out_specs=pl.BlockSpec((1, FNET_N2, kb, W), lambda b, i: (b, 0, i, 0)),
        out_shape=jax.ShapeDtypeStruct((B, FNET_N2, n1, W), bf16),
        compiler_params=_cparams(("parallel", "parallel")),
        name="fnet2",
    )(a.reshape(B, 2, n1, FNET_N2, W), t, cs)
    return y.reshape(B, S, W)


def _hgrn_consts(C):
    L = int(math.log2(C))
    t = np.arange(C)
    m = np.zeros((2, (L + 2) * C, C), np.float32)
    lvl = np.full((2, C, C), -1, np.int32)
    for d in (0, 1):
        for li in range(L):
            n = C >> li
            half = n // 2
            blk = t // n
            mid = blk * n + half
            upper = (t % n) >= half
            for r in range(C):
                if d == 0:
                    if upper[r]:
                        m[d, li * C + r, mid[r]:r + 1] = 1.0
                    else:
                        m[d, li * C + r, r + 1:mid[r]] = 1.0
                else:
                    if upper[r]:
                        m[d, li * C + r, mid[r]:r] = 1.0
                    else:
                        m[d, li * C + r, r:mid[r]] = 1.0
            same = blk[:, None] == blk[None, :]
            if d == 0:
                msk = same & upper[:, None] & (~upper)[None, :]
            else:
                msk = same & (~upper)[:, None] & upper[None, :]
            lvl[d][msk] = li
        for r in range(C):
            if d == 0:
                m[d, L * C + r, :r + 1] = 1.0
                m[d, (L + 1) * C + r, r + 1:] = 1.0
            else:
                m[d, L * C + r, r:] = 1.0
                m[d, (L + 1) * C + r, :r] = 1.0
    c = np.arange(BRANCH)
    headsum = ((c[:, None] // HGRN_DIM) == (c[None, :] // HGRN_DIM)).astype(np.float32)
    lvl = np.tile(lvl, (1, 1, HGRN_HEADS))
    m = np.tile(m, (1, 1, 2))
    return jnp.asarray(m, f32).astype(bf16), jnp.asarray(lvl), jnp.asarray(headsum, f32).astype(bf16)


def _hgrn_kernel(qf_ref, vf_ref, kf_ref, lff_ref, qb_ref, vb_ref, kb_ref, lfb_ref, m_ref, lvl_ref, hs_ref,
                 of_ref, ob_ref, st_ref, *, C, L):
    @pl.when(pl.program_id(1) == 0)
    def _():
        st_ref[...] = jnp.zeros(st_ref.shape, f32)

    refs = ((qf_ref, vf_ref, kf_ref, lff_ref, of_ref), (qb_ref, vb_ref, kb_ref, lfb_ref, ob_ref))
    head = lax.broadcasted_iota(jnp.int32, (1, BRANCH), 1) // HGRN_DIM
    r = lax.broadcasted_iota(jnp.int32, (BRANCH, BRANCH), 0) // HGRN_DIM
    cidx = lax.broadcasted_iota(jnp.int32, (BRANCH, BRANCH), 1) // HGRN_DIM
    n_chunks = qf_ref.shape[1] // C
    order = (tuple(range(n_chunks)), tuple(reversed(range(n_chunks))))
    chains = [(d, order[d][j]) for j in range(n_chunks) for d in (0, 1)]
    rows = {c: slice(c[1] * C, (c[1] + 1) * C) for c in chains}
    q = {c: refs[c[0]][0][0, rows[c], :].astype(f32) for c in chains}
    v = {c: refs[c[0]][1][0, rows[c], :] for c in chains}
    k = {c: refs[c[0]][2][0, rows[c], :].astype(f32) for c in chains}
    e = {}
    for c in chains:
        lf = refs[c[0]][3][0, rows[c], :]
        lf2 = jnp.concatenate([lf[:, :BRANCH], lf[:, BRANCH:]], axis=0)
        e[c] = jnp.dot(m_ref[c[0]], lf2, preferred_element_type=f32)

    o = {}
    st = [st_ref[0], st_ref[1]]
    for c in chains:
        d = c[0]
        ein = jnp.exp(e[c][L * C:(L + 1) * C])
        o[c] = lax.dot_general((q[c] * ein).astype(bf16), st[d].astype(bf16), _NT, preferred_element_type=f32)
        kout = (k[c] * jnp.exp(e[c][(L + 1) * C:(L + 2) * C])).astype(bf16)
        upd = lax.dot_general(v[c], kout, _TN, preferred_element_type=f32)
        total = ein[C - 1:C] if d == 0 else ein[0:1]
        st[d] = st[d] * total + jnp.where(r == cidx, upd, 0.0)
        o[c] = o[c] + jnp.dot((q[c] * k[c]).astype(bf16), hs_ref[...], preferred_element_type=f32) * v[c].astype(f32)
    st_ref[0] = st[0]
    st_ref[1] = st[1]

    scores = {c: jnp.zeros((C, HGRN_HEADS * C), f32) for c in chains}
    for li in range(L):
        for c in chains:
            ex = jnp.exp(e[c][li * C:(li + 1) * C])
            qt = (q[c] * ex).astype(bf16)
            kt = (k[c] * ex).astype(bf16)
            kstack = jnp.concatenate([jnp.where(head == hh, kt, jnp.zeros_like(kt)) for hh in range(HGRN_HEADS)],
                                     axis=0)
            s = lax.dot_general(qt, kstack, _NT, preferred_element_type=f32)
            scores[c] = jnp.where(lvl_ref[c[0]] == li, s, scores[c])
    for c in chains:
        vstack = jnp.concatenate([jnp.where(head == hh, v[c], jnp.zeros_like(v[c])) for hh in range(HGRN_HEADS)],
                                 axis=0)
        oc = o[c] + jnp.dot(scores[c].astype(bf16), vstack, preferred_element_type=f32)
        refs[c[0]][4][0, rows[c], :] = oc.astype(bf16)


def _hgrn(hq, hv, hk, hlf, consts, C):
    B, S, W = hq.shape
    m, lvl, hs = consts
    L = int(math.log2(C))
    rows = min(S, HGRN_CHUNKS_PER_STEP * C)
    nb = S // rows
    fwd = lambda w, c: pl.BlockSpec((1, rows, w), lambda b, i: (b, i, c))
    bwd = lambda w, c: pl.BlockSpec((1, rows, w), lambda b, i: (b, nb - 1 - i, c))
    return pl.pallas_call(
        functools.partial(_hgrn_kernel, C=C, L=L),
        grid=(B, nb),
        in_specs=[fwd(W, 0), fwd(W, 0), fwd(W, 0), fwd(2 * W, 0), bwd(W, 0), bwd(W, 0), bwd(W, 1), bwd(2 * W, 1),
                  _const_spec(m.shape), _const_spec(lvl.shape), _const_spec(hs.shape)],
        out_specs=[fwd(W, 0), bwd(W, 0)],
        out_shape=[jax.ShapeDtypeStruct((B, S, W), bf16)] * 2,
        scratch_shapes=[pltpu.VMEM((2, BRANCH, BRANCH), f32)],
        compiler_params=_cparams(("parallel", "arbitrary")),
        name="hgrn",
    )(hq, hv, hk, hlf, hq, hv, hk, hlf, m, lvl, hs)


def _merge_ffn_kernel(x_ref, om_ref, of_ref, od_ref, ohf_ref, ohb_ref, hg_ref, ln_ref, wg_ref, wbm_ref, wbf_ref,
                      wbd_ref, wbh_ref, wo_ref, gon_ref, g64_ref, lnf_ref, wgu_ref, wd_ref, o_ref):
    x = x_ref[0]
    h = _rms(x, ln_ref[...]).astype(bf16)
    oh = ohf_ref[0].astype(f32) + ohb_ref[0].astype(f32)
    ms = jnp.dot((oh * oh).astype(bf16), g64_ref[...], preferred_element_type=f32)
    oh = (oh * lax.rsqrt(ms + EPS) * gon_ref[...] * _sigmoid(hg_ref[0].astype(f32))).astype(bf16)
    merged = None
    for n, (o_n, w_ref) in enumerate(((om_ref[0], wbm_ref), (of_ref[0], wbf_ref), (od_ref[0], wbd_ref), (oh, wbh_ref))):
        gate = _sigmoid(jnp.dot(h, wg_ref[:, n * D_MODEL:(n + 1) * D_MODEL], preferred_element_type=f32))
        y = gate * jnp.dot(o_n, w_ref[...], preferred_element_type=f32)
        merged = y if merged is None else merged + y
    x = x + jnp.dot(merged.astype(bf16), wo_ref[...], preferred_element_type=f32)

    h = _rms(x, lnf_ref[...]).astype(bf16)
    acc = x
    for c in range(D_FF // FF_CHUNK):
        g = jnp.dot(h, wgu_ref[:, c * FF_CHUNK:(c + 1) * FF_CHUNK], preferred_element_type=f32)
        u = jnp.dot(h, wgu_ref[:, D_FF + c * FF_CHUNK:D_FF + (c + 1) * FF_CHUNK], preferred_element_type=f32)
        a = (g * _sigmoid(g) * u).astype(bf16)
        acc = acc + jnp.dot(a, wd_ref[c], preferred_element_type=f32)
    o_ref[0] = acc


def _merge_ffn(x, om, of, od, oh, hg, p, tm):
    B, S, _ = x.shape
    row = lambda w: pl.BlockSpec((1, tm, w), lambda b, i: (b, i, 0))
    consts = (p["ln_mix"], p["wgate"], p["wbm"], p["wbf"], p["wbd"], p["wbh"], p["wout"], p["gon"], p["g64"],
              p["ln_ffn"], p["wgu"], p["wd"])
    return pl.pallas_call(
        _merge_ffn_kernel,
        grid=(B, S // tm),
        in_specs=[row(D_MODEL)] + [row(BRANCH)] * 6
                 + [_const_spec(c.shape) for c in consts],
        out_specs=row(D_MODEL),
        out_shape=jax.ShapeDtypeStruct((B, S, D_MODEL), f32),
        compiler_params=_cparams(("parallel", "parallel")),
        name="merge_ffn",
    )(x, om, of, od, oh[0], oh[1], hg, *consts)


def _pad_heads(w, heads, width):
    lead = w.shape[:-1]
    w = w.reshape(lead + (heads, width))
    w = jnp.pad(w, [(0, 0)] * len(lead) + [(0, 0), (0, LANE - width)])
    return w.reshape(lead + (heads * LANE,))


def _rope_tables(S):
    def cs(dim):
        inv = 1.0 / (ROPE_THETA ** (jnp.arange(0, dim, 2, dtype=f32) / dim))
        ang = jnp.arange(S, dtype=f32)[:, None] * inv[None, :]
        return jnp.cos(ang), jnp.sin(ang)

    cm, sm = cs(MLA_ROPE)
    one, zero = jnp.ones((S, MLA_NOPE), f32), jnp.zeros((S, MLA_NOPE), f32)
    z16, z32, o32 = jnp.zeros((S, 16), f32), jnp.zeros((S, 32), f32), jnp.ones((S, 32), f32)
    rope_m = jnp.stack([jnp.concatenate([one, cm, cm, o32], 1),
                        jnp.concatenate([zero, -sm, z16, z32], 1),
                        jnp.concatenate([zero, z16, sm, z32], 1)])
    cd, sd = cs(DIFF_HEAD_DIM)
    rope_d = jnp.stack([jnp.tile(jnp.concatenate([cd, cd], 1), (1, 4)),
                        jnp.tile(jnp.concatenate([-sd, z16], 1), (1, 4)),
                        jnp.tile(jnp.concatenate([z16, sd], 1), (1, 4))])
    return rope_m, rope_d


def _block_mean(width, group):
    c = np.arange(width)
    return jnp.asarray(((c[:, None] // group) == (c[None, :] // group)) / group, f32).astype(bf16)


def _layer_params(l, a, lower_bounds):
    w = a["w_in"][l]
    offs = np.cumsum([0, 384, 128, 32, 256, 256, 256, 256, 256, 256, 256, 256, 256, 4096])
    col = lambda i: w[:, offs[i]:offs[i + 1]]
    wcat = jnp.concatenate([col(0), col(1), jnp.pad(col(2), ((0, 0), (MLA_NOPE, LANE - MLA_QK))), col(3), col(4),
                            col(5), _pad_heads(col(6), DIFF_HEADS, 2 * DIFF_HEAD_DIM), col(7), col(8), col(9),
                            col(10), col(11)], axis=1).astype(bf16)
    wukv = a["mla_w_ukv"][l].reshape(MLA_KV_LORA, MLA_HEADS, MLA_NOPE + MLA_V)
    wukv = jnp.concatenate([_pad_heads(wukv[:, :, :MLA_NOPE].reshape(MLA_KV_LORA, -1), MLA_HEADS, MLA_NOPE),
                            _pad_heads(wukv[:, :, MLA_NOPE:].reshape(MLA_KV_LORA, -1), MLA_HEADS, MLA_V)], axis=1)
    lam_init = 0.8 - 0.6 * math.exp(-0.3 * l)
    lam = (jnp.exp(jnp.sum(a["diff_lq1"][l] * a["diff_lk1"][l])) - jnp.exp(jnp.sum(a["diff_lq2"][l] * a["diff_lk2"][l]))
           + lam_init)
    wb = a["w_branch"][l]
    nc = D_FF // FF_CHUNK
    row = lambda v: v.reshape(1, -1).astype(f32)
    return {
        "ln_mix": row(a["ln_mix"][l]),
        "wcat": wcat,
        "gqa": row(a["mla_g_qa"][l]),
        "wuq": _pad_heads(a["mla_w_uq"][l], MLA_HEADS, MLA_QK).astype(bf16),
        "gkva": row(a["mla_g_kva"][l]),
        "wukv": wukv.astype(bf16),
        "gqn": row(jnp.pad(a["mla_g_qn"][l], (0, LANE - MLA_QK))) * (MLA_QK ** -0.5 * LOG2E),
        "gkn": row(jnp.pad(a["mla_g_kn"][l], (0, LANE - MLA_QK))),
        "gdq": row(jnp.tile(a["diff_g_qn"][l], 2 * DIFF_HEADS)) * (DIFF_HEAD_DIM ** -0.5 * LOG2E),
        "gdk": row(jnp.tile(a["diff_g_kn"][l], 2 * DIFF_HEADS)),
        "g32": _block_mean(BRANCH, DIFF_HEAD_DIM),
        "lb": lower_bounds[:, l].astype(f32),
        "lam": lam.reshape(1).astype(f32),
        "mla_bound": MLA_QK ** 0.5 * LOG2E * jnp.max(jnp.abs(a["mla_g_qn"][l])) * jnp.max(jnp.abs(a["mla_g_kn"][l])),
        "diff_bound": (DIFF_HEAD_DIM ** 0.5 * LOG2E * jnp.max(jnp.abs(a["diff_g_qn"][l]))
                       * jnp.max(jnp.abs(a["diff_g_kn"][l]))),
        "gsub": row(a["diff_g_sub"][l]) * (1.0 - lam_init),
        "wgate": col(12).astype(bf16),
        "wbm": wb[0].astype(bf16),
        "wbf": wb[1].astype(bf16),
        "wbd": wb[2].astype(bf16),
        "wbh": wb[3].astype(bf16),
        "wout": a["w_out"][l].astype(bf16),
        "gon": row(jnp.tile(a["hgrn_g_on"][l], HGRN_HEADS)),
        "g64": _block_mean(BRANCH, HGRN_DIM),
        "ln_ffn": row(a["ln_ffn"][l]),
        "wgu": a["w_gate_up"][l].astype(bf16),
        "wd": a["w_down"][l].reshape(nc, FF_CHUNK, D_MODEL).astype(bf16),
    }


def _tiles(S):
    return {"tm_in": min(256, S),"tq": min(512, S), "tk": min(512, S), "tqb": min(512, S), "tkb": min(2048, S),
            "hgrn_chunk": min(64, S),
            "tm_merge": min(512, S)}


def kernel(x, ln_mix, w_in, mla_g_qa, mla_w_uq, mla_g_kva, mla_w_ukv, mla_g_qn, mla_g_kn, diff_g_qn, diff_g_kn,
           diff_lq1, diff_lk1, diff_lq2, diff_lk2, diff_g_sub, hgrn_lb_logits, hgrn_g_on, w_branch, w_out, ln_ffn,
           w_gate_up, w_down):
    a = dict(ln_mix=ln_mix, w_in=w_in, mla_g_qa=mla_g_qa, mla_w_uq=mla_w_uq, mla_g_kva=mla_g_kva,
             mla_w_ukv=mla_w_ukv, mla_g_qn=mla_g_qn, mla_g_kn=mla_g_kn, diff_g_qn=diff_g_qn, diff_g_kn=diff_g_kn,
             diff_lq1=diff_lq1, diff_lk1=diff_lk1, diff_lq2=diff_lq2, diff_lk2=diff_lk2, diff_g_sub=diff_g_sub,
             hgrn_g_on=hgrn_g_on, w_branch=w_branch, w_out=w_out, ln_ffn=ln_ffn, w_gate_up=w_gate_up, w_down=w_down)
    S = x.shape[1]
    t = _tiles(S)
    rope_m, rope_d = _rope_tables(S)
    lb_p = jax.nn.softmax(hgrn_lb_logits.astype(f32), axis=1)
    lower_bounds = jnp.cumsum(lb_p, axis=1) - lb_p[:, :1]
    fnet_consts = _fnet_consts(S)
    hgrn_consts = _hgrn_consts(t["hgrn_chunk"])
    for l in range(DEPTH):
        p = _layer_params(l, a, lower_bounds)
        qm, km, vm, fn, qd, kd, vd, hq, hv, hk, hlf, hg = _in_proj(x, p, rope_m, rope_d, t["tm_in"])
        om = lax.cond(p["mla_bound"] <= SCORE_BOUND_LOG2,
                      lambda: _mla_attn_bounded(qm, km, vm, t["tqb"], t["tkb"]),
                      lambda: _mla_attn(qm, km, vm, t["tq"], t["tk"]))
        od = lax.cond(p["diff_bound"] <= SCORE_BOUND_LOG2,
                      lambda: _diff_attn_bounded(p["lam"], qd, kd, vd, p["gsub"], t["tqb"], t["tkb"]),
                      lambda: _diff_attn(p["lam"], qd, kd, vd, p["gsub"], t["tq"], t["tk"]))
        of = _fnet(fn, fnet_consts)
        oh = _hgrn(hq, hv, hk, hlf, hgrn_consts, t["hgrn_chunk"])
        x = _merge_ffn(x, om, of, od, oh, hg, p, t["tm_merge"])
    return x
```

```python
import functools
import math

import numpy as np
import jax
import jax.numpy as jnp
from jax import lax
from jax.experimental import pallas as pl
from jax.experimental.pallas import tpu as pltpu

f32 = jnp.float32
bf16 = jnp.bfloat16

D_MODEL = 1024
DEPTH = 2
MLA_HEADS = 4
MLA_Q_LORA = 384
MLA_KV_LORA = 128
MLA_NOPE = 64
MLA_ROPE = 32
MLA_V = 64
MLA_QK = MLA_NOPE + MLA_ROPE
DIFF_HEADS = 4
DIFF_HEAD_DIM = 32
HGRN_HEADS = 4
HGRN_DIM = 64
BRANCH = 256
D_FF = 2816
FF_CHUNK = 256
ROPE_THETA = 10000.0
EPS = 1e-6
LOG2E = 1.4426950408889634
LANE = 128
FNET_N2 = 64
FNET_K1_BLOCK = 8
HGRN_CHUNKS_PER_STEP = 8
IN_PROJ_CHAINS_IN_FLIGHT = 4
ATTN_KEY_SUB = 256
VT_ROWS = LANE
ATTN_CHAINS_IN_FLIGHT = 8
VMEM_LIMIT = 56 * 1024 * 1024
SCORE_BOUND_LOG2 = 60.0

_SEG = {}
_off = 0
for _name, _w in (("cq", 384), ("ckv", 128), ("krope", 128), ("fnet", 256), ("dq", 256), ("dk", 256),
                  ("dv", 512), ("hq", 256), ("hi", 256), ("hff", 256), ("hfb", 256), ("hg", 256)):
    _SEG[_name] = (_off, _off + _w)
    _off += _w
W_CAT = _off


def _cparams(sem):
    return pltpu.CompilerParams(dimension_semantics=sem, vmem_limit_bytes=VMEM_LIMIT)


def _const_spec(shape):
    nd = len(shape)
    return pl.BlockSpec(shape, lambda *_: (0,) * nd, pipeline_mode=pl.Buffered(1))


def _rms(x, g):
    return x * lax.rsqrt(jnp.mean(x * x, axis=-1, keepdims=True) + EPS) * g


def _sigmoid(x):
    return 1.0 / (1.0 + jnp.exp(-x))


def _rope(t, tab_ref):
    return t * tab_ref[0] + pltpu.roll(t, LANE - 16, 1) * tab_ref[1] + pltpu.roll(t, 16, 1) * tab_ref[2]


def _in_proj_kernel(x_ref, ln_ref, wcat_ref, gqa_ref, wuq_ref, gkva_ref, wukv_ref, gqn_ref, gkn_ref,
                    ropem_ref, gdq_ref, gdk_ref, g32_ref, roped_ref, lb_ref,
                    qm_ref, km_ref, vm_ref, fn_ref, qd_ref, kd_ref, vd_ref, hq_ref, hv_ref, hk_ref,
                    hlf_ref, hg_ref):
    h = _rms(x_ref[0], ln_ref[...]).astype(bf16)

    def seg(name):
        a, b = _SEG[name]
        return jnp.dot(h, wcat_ref[:, a:b], preferred_element_type=f32)

    ones_lane = (lax.broadcasted_iota(jnp.int32, (1, LANE), 1) == MLA_V).astype(f32)


    def mla_q():
        cq = seg("cq")
        yield
        q = jnp.dot(_rms(cq, gqa_ref[...]).astype(bf16), wuq_ref[...], preferred_element_type=f32)
        yield
        for hh in range(MLA_HEADS):
            sl = slice(hh * LANE, (hh + 1) * LANE)
            qh = q[:, sl]
            ms = jnp.sum(qh * qh, axis=-1, keepdims=True) * (1.0 / MLA_QK)
            qm_ref[0, :, sl] = _rope(qh * lax.rsqrt(ms + EPS) * gqn_ref[...], ropem_ref).astype(bf16)

    def mla_kv():
        ckv = seg("ckv")
        krope = seg("krope")
        yield
        kv = jnp.dot(_rms(ckv, gkva_ref[...]).astype(bf16), wukv_ref[...], preferred_element_type=f32)
        yield
        for hh in range(MLA_HEADS):
            sl = slice(hh * LANE, (hh + 1) * LANE)
            kh = kv[:, sl] + krope
            ms = jnp.sum(kh * kh, axis=-1, keepdims=True) * (1.0 / MLA_QK)
            km_ref[0, :, sl] = _rope(kh * lax.rsqrt(ms + EPS) * gkn_ref[...], ropem_ref).astype(bf16)
            vsl = slice((MLA_HEADS + hh) * LANE, (MLA_HEADS + hh + 1) * LANE)
            vm_ref[0, :, sl] = (kv[:, vsl] + ones_lane).astype(bf16)

    def diff_qk(name, g_ref, o_ref):
        t = seg(name)
        yield
        ms = jnp.dot((t * t).astype(bf16), g32_ref[...], preferred_element_type=f32)
        yield
        t = t * lax.rsqrt(ms + EPS) * g_ref[...]
        for c in range(BRANCH // LANE):
            sl = slice(c * LANE, (c + 1) * LANE)
            o_ref[0, :, sl] = _rope(t[:, sl], roped_ref).astype(bf16)

    def diff_v():
        dv = seg("dv")
        yield
        for hh in range(DIFF_HEADS):
            sl = slice(hh * LANE, (hh + 1) * LANE)
            vd_ref[0, :, sl] = (dv[:, sl] + ones_lane).astype(bf16)

    def plain(name, o_ref):
        z = seg(name)
        yield
        o_ref[0] = z.astype(bf16)

    def hgrn_gate(d, name):
        z = seg(name)
        yield
        lb = lb_ref[d:d + 1, :]
        f = lb + (1.0 - lb) * _sigmoid(z)
        hk_ref[0, :, d * BRANCH:(d + 1) * BRANCH] = (1.0 - f).astype(bf16)
        lf = jnp.log(f)
        hi = lf.astype(bf16)
        hlf_ref[0, :, 2 * d * BRANCH:(2 * d + 1) * BRANCH] = hi
        hlf_ref[0, :, (2 * d + 1) * BRANCH:(2 * d + 2) * BRANCH] = (lf - hi.astype(f32)).astype(bf16)

    _round_robin([mla_q(), hgrn_gate(0, "hff"), mla_kv(), hgrn_gate(1, "hfb"), diff_qk("dq", gdq_ref, qd_ref),
                  plain("fnet", fn_ref), diff_qk("dk", gdk_ref, kd_ref), plain("hq", hq_ref), diff_v(),
                  plain("hi", hv_ref), plain("hg", hg_ref)], IN_PROJ_CHAINS_IN_FLIGHT)


def _in_proj(x, p, rope_m, rope_d, tm):
    B, S, _ = x.shape
    row = lambda w: pl.BlockSpec((1, tm, w), lambda b, i: (b, i, 0))
    tab = pl.BlockSpec((3, tm, LANE), lambda b, i: (0, i, 0))
    out_w = (512, 512, 512, 256, 256, 256, 512, 256, 256, 512, 1024, 256)
    consts = (p["ln_mix"], p["wcat"], p["gqa"], p["wuq"], p["gkva"], p["wukv"], p["gqn"], p["gkn"])
    consts2 = (p["gdq"], p["gdk"], p["g32"])
    in_specs = ([row(D_MODEL)] + [_const_spec(c.shape) for c in consts] + [tab]
                + [_const_spec(c.shape) for c in consts2] + [tab, _const_spec(p["lb"].shape)])
    return pl.pallas_call(
        _in_proj_kernel,
        grid=(B, S // tm),
        in_specs=in_specs,
        out_specs=[row(w) for w in out_w],
        out_shape=[jax.ShapeDtypeStruct((B, S, w), bf16) for w in out_w],
        compiler_params=_cparams(("parallel", "parallel")),
        name="in_proj",
    )(x, *consts, rope_m, *consts2, rope_d, p["lb"])


def _online_softmax_step(s, v, m_ref, acc_ref, idx):
    m_prev = m_ref[idx]
    m_new = jnp.maximum(m_prev, jnp.max(s, axis=1, keepdims=True))
    alpha = jnp.exp2(m_prev - m_new)
    p = jnp.exp2((s - m_new[:, :1]).astype(bf16))
    acc_ref[idx] = alpha * acc_ref[idx] + jnp.dot(p, v, preferred_element_type=f32)
    m_ref[idx] = m_new


_NT = (((1,), (1,)), ((), ()))
_TN = (((0,), (0,)), ((), ()))


def _mla_attn_kernel(q_ref, k_ref, v_ref, o_ref, m_ref, acc_ref):
    j = pl.program_id(2)

    @pl.when(j == 0)
    def _():
        m_ref[...] = jnp.full(m_ref.shape, -jnp.inf, f32)
        acc_ref[...] = jnp.zeros(acc_ref.shape, f32)

    for hh in range(MLA_HEADS):
        sl = slice(hh * LANE, (hh + 1) * LANE)
        s = lax.dot_general(q_ref[0, :, sl], k_ref[0, :, sl], _NT, preferred_element_type=f32)
        _online_softmax_step(s, v_ref[0, :, sl], m_ref, acc_ref, hh)

    @pl.when(j == pl.num_programs(2) - 1)
    def _():
        _mla_finalize(acc_ref, o_ref)


def _mla_finalize(acc_ref, o_ref, transposed=False):
    for hh in range(MLA_HEADS):
        acc = acc_ref[hh]
        if transposed:
            o = (acc[:MLA_V] * (1.0 / acc[MLA_V:MLA_V + 1])).T
        else:
            o = acc[:, :MLA_V] * (1.0 / acc[:, MLA_V:MLA_V + 1])
        o_ref[0, :, hh * MLA_V:(hh + 1) * MLA_V] = o.astype(bf16)


def _diff_finalize(lam_ref, gsub_ref, acc_ref, o_ref, transposed=False):
    lam = lam_ref[0]
    dv = 2 * DIFF_HEAD_DIM
    for hh in range(DIFF_HEADS):
        a1 = acc_ref[2 * hh]
        a2 = acc_ref[2 * hh + 1]
        if transposed:
            o = a1[:dv] * (1.0 / a1[dv:dv + 1]) - a2[:dv] * (lam / a2[dv:dv + 1])
            o = (o * lax.rsqrt(jnp.mean(o * o, axis=0, keepdims=True) + EPS)).T
        else:
            o = a1[:, :dv] * (1.0 / a1[:, dv:dv + 1]) - a2[:, :dv] * (lam / a2[:, dv:dv + 1])
            o = o * lax.rsqrt(jnp.mean(o * o, axis=-1, keepdims=True) + EPS)
        o_ref[0, :, hh * dv:(hh + 1) * dv] = (o * gsub_ref[...]).astype(bf16)


def _diff_attn_kernel(lam_ref, q_ref, k_ref, v_ref, gsub_ref, o_ref, m_ref, acc_ref):
    j = pl.program_id(2)

    @pl.when(j == 0)
    def _():
        m_ref[...] = jnp.full(m_ref.shape, -jnp.inf, f32)
        acc_ref[...] = jnp.zeros(acc_ref.shape, f32)

    q = q_ref[0]
    k = k_ref[0]
    group = lax.broadcasted_iota(jnp.int32, (1, BRANCH), 1) // DIFF_HEAD_DIM
    for g in range(2 * DIFF_HEADS):
        qg = jnp.where(group == g, q, jnp.zeros_like(q))
        s = lax.dot_general(qg, k, _NT, preferred_element_type=f32)
        hh = g // 2
        _online_softmax_step(s, v_ref[0, :, hh * LANE:(hh + 1) * LANE], m_ref, acc_ref, g)

    @pl.when(j == pl.num_programs(2) - 1)
    def _():
        _diff_finalize(lam_ref, gsub_ref, acc_ref, o_ref)


def _round_robin(chains, in_flight):
    pending, active = list(chains), []
    while pending or active:
        while pending and len(active) < in_flight:
            active.append(pending.pop(0))
        for c in list(active):
            if next(c, "done") == "done":
                active.remove(c)


def _bounded_attn_loop(maps, k_ref, v_ref, acc_ref, tk):
    acc_ref[...] = jnp.zeros(acc_ref.shape, f32)

    def kv_block(j, carry):
        def chain(g, r0):
            q, ksl, vsl = maps[g]
            keys = pl.ds(pl.multiple_of(j * tk + r0, ATTN_KEY_SUB), ATTN_KEY_SUB)
            st = lax.dot_general(k_ref[0, keys, ksl], q, _NT, preferred_element_type=f32)
            yield
            pt = jnp.exp2(st).astype(bf16)
            yield
            acc_ref[g] += lax.dot_general(v_ref[0, keys, vsl], pt, _TN, preferred_element_type=f32)

        _round_robin([chain(g, r0) for r0 in range(0, tk, ATTN_KEY_SUB) for g in range(len(maps))],
                     ATTN_CHAINS_IN_FLIGHT)
        return carry

    lax.fori_loop(0, k_ref.shape[1] // tk, kv_block, 0)


def _mla_attn_bounded_kernel(q_ref, k_ref, v_ref, o_ref, acc_ref, *, tk):
    maps = [(q_ref[0, :, hh * LANE:(hh + 1) * LANE], slice(hh * LANE, (hh + 1) * LANE),
             slice(hh * LANE, hh * LANE + VT_ROWS)) for hh in range(MLA_HEADS)]
    _bounded_attn_loop(maps, k_ref, v_ref, acc_ref, tk)
    _mla_finalize(acc_ref, o_ref, transposed=True)


def _diff_attn_bounded_kernel(lam_ref, q_ref, k_ref, v_ref, gsub_ref, o_ref, acc_ref, *, tk):
    q = q_ref[0]
    group = lax.broadcasted_iota(jnp.int32, (1, BRANCH), 1) // DIFF_HEAD_DIM
    maps = [(jnp.where(group == g, q, jnp.zeros_like(q)), slice(0, BRANCH),
             slice((g // 2) * LANE, (g // 2) * LANE + VT_ROWS)) for g in range(2 * DIFF_HEADS)]
    _bounded_attn_loop(maps, k_ref, v_ref, acc_ref, tk)
    _diff_finalize(lam_ref, gsub_ref, acc_ref, o_ref, transposed=True)


def _attn_bounded_call(kernel_fn, n_acc, name, args, in_specs, S, W, B, tq):
    return pl.pallas_call(
        kernel_fn,
        grid=(B, S // tq),
        in_specs=in_specs,
        out_specs=pl.BlockSpec((1, tq, BRANCH), lambda b, i: (b, i, 0)),
        out_shape=jax.ShapeDtypeStruct((B, S, BRANCH), bf16),
        scratch_shapes=[pltpu.VMEM((n_acc, VT_ROWS, tq), f32)],
        compiler_params=_cparams(("parallel", "parallel")),
        name=name,
    )(*args)


def _mla_attn_bounded(q, k, v, tq, tk):
    B, S, W = q.shape
    full = pl.BlockSpec((1, S, W), lambda b, i: (b, 0, 0))
    in_specs = [pl.BlockSpec((1, tq, W), lambda b, i: (b, i, 0)), full, full]
    return _attn_bounded_call(functools.partial(_mla_attn_bounded_kernel, tk=tk), MLA_HEADS, "mla_attn_bounded",
                              (q, k, v), in_specs, S, W, B, tq)


def _diff_attn_bounded(lam, q, k, v, gsub, tq, tk):
    B, S, W = v.shape
    in_specs = [pl.BlockSpec(memory_space=pltpu.SMEM),
                pl.BlockSpec((1, tq, BRANCH), lambda b, i: (b, i, 0)),
                pl.BlockSpec((1, S, BRANCH), lambda b, i: (b, 0, 0)),
                pl.BlockSpec((1, S, W), lambda b, i: (b, 0, 0)),
                pl.BlockSpec((1, 2 * DIFF_HEAD_DIM), lambda b, i: (0, 0))]
    return _attn_bounded_call(functools.partial(_diff_attn_bounded_kernel, tk=tk), 2 * DIFF_HEADS,
                              "diff_attn_bounded", (lam, q, k, v, gsub), in_specs, S, W, B, tq)


def _mla_attn(q, k, v, tq, tk):
    B, S, W = q.shape
    return pl.pallas_call(
        _mla_attn_kernel,
        grid=(B, S // tq, S // tk),
        in_specs=[pl.BlockSpec((1, tq, W), lambda b, i, j: (b, i, 0)),
                  pl.BlockSpec((1, tk, W), lambda b, i, j: (b, j, 0)),
                  pl.BlockSpec((1, tk, W), lambda b, i, j: (b, j, 0))],
        out_specs=pl.BlockSpec((1, tq, BRANCH), lambda b, i, j: (b, i, 0)),
        out_shape=jax.ShapeDtypeStruct((B, S, BRANCH), bf16),
        scratch_shapes=[pltpu.VMEM((MLA_HEADS, tq, LANE), f32), pltpu.VMEM((MLA_HEADS, tq, LANE), f32)],
        compiler_params=_cparams(("parallel", "parallel", "arbitrary")),
        name="mla_attn",
    )(q, k, v)


def _diff_attn(lam, q, k, v, gsub, tq, tk):
    B, S, W = v.shape
    return pl.pallas_call(
        _diff_attn_kernel,
        grid=(B, S // tq, S // tk),
        in_specs=[pl.BlockSpec(memory_space=pltpu.SMEM),
                  pl.BlockSpec((1, tq, BRANCH), lambda b, i, j: (b, i, 0)),
                  pl.BlockSpec((1, tk, BRANCH), lambda b, i, j: (b, j, 0)),
                  pl.BlockSpec((1, tk, W), lambda b, i, j: (b, j, 0)),
                  pl.BlockSpec((1, 2 * DIFF_HEAD_DIM), lambda b, i, j: (0, 0))],
        out_specs=pl.BlockSpec((1, tq, BRANCH), lambda b, i, j: (b, i, 0)),
        out_shape=jax.ShapeDtypeStruct((B, S, BRANCH), bf16),
        scratch_shapes=[pltpu.VMEM((2 * DIFF_HEADS, tq, LANE), f32), pltpu.VMEM((2 * DIFF_HEADS, tq, LANE), f32)],
        compiler_params=_cparams(("parallel", "parallel", "arbitrary")),
        name="diff_attn",
    )(lam, q, k, v, gsub)


def _fnet1_kernel(x_ref, w_ref, a_ref):
    a_ref[0] = jnp.dot(w_ref[...], x_ref[0], preferred_element_type=f32).astype(bf16)


def _fnet2_kernel(a_ref, t_ref, cs_ref, o_ref):
    zs = []
    for jj in range(FNET_K1_BLOCK):
        a = jnp.concatenate([a_ref[0, 0, jj], a_ref[0, 1, jj]], axis=0)
        zs.append(jnp.dot(t_ref[jj], a, preferred_element_type=f32))
    zc = jnp.concatenate([jnp.concatenate([z[:FNET_N2], z[FNET_N2:]], axis=1) for z in zs], axis=0).astype(bf16)
    y = jnp.dot(zc, cs_ref[...], preferred_element_type=f32).astype(bf16)
    for jj in range(FNET_K1_BLOCK):
        o_ref[0, :, jj, :] = y[jj * FNET_N2:(jj + 1) * FNET_N2]


def _fnet_consts(S):
    n1 = S // FNET_N2
    a = np.arange(n1)
    ang1 = 2.0 * np.pi * np.outer(a, a) / n1
    w1 = np.concatenate([np.cos(ang1), -np.sin(ang1)], axis=0)
    n2 = np.arange(FNET_N2)
    phi = 2.0 * np.pi * (n2[None, None, :] * a[:, None, None] / S + n2[None, None, :] * n2[None, :, None] / FNET_N2)
    tr, ti = np.cos(phi), -np.sin(phi)
    t = np.concatenate([np.concatenate([tr, -ti], axis=2), np.concatenate([ti, tr], axis=2)], axis=1)
    c = np.arange(BRANCH)
    same = (c[:, None] // FNET_N2) == (c[None, :] // FNET_N2)
    angc = 2.0 * np.pi * np.outer(c % FNET_N2, c % FNET_N2) / FNET_N2
    norm = 1.0 / math.sqrt(S * FNET_N2)
    cs = np.concatenate([np.where(same, np.cos(angc), 0.0), np.where(same, np.sin(angc), 0.0)], axis=0) * norm
    return (jnp.asarray(w1, f32).astype(bf16), jnp.asarray(t, f32).astype(bf16), jnp.asarray(cs, f32).astype(bf16))


def _fnet(u, consts):
    B, S, W = u.shape
    w1, t, cs = consts
    n1 = S // FNET_N2
    cols = FNET_N2 * W
    tn = min(cols, 4096)
    a = pl.pallas_call(
        _fnet1_kernel,
        grid=(B, cols // tn),
        in_specs=[pl.BlockSpec((1, n1, tn), lambda b, i: (b, 0, i)), _const_spec(w1.shape)],
        out_specs=pl.BlockSpec((1, 2 * n1, tn), lambda b, i: (b, 0, i)),
        out_shape=jax.ShapeDtypeStruct((B, 2 * n1, cols), bf16),
        compiler_params=_cparams(("parallel", "parallel")),
        name="fnet1",
    )(u.reshape(B, n1, cols), w1)
    kb = FNET_K1_BLOCK
    y = pl.pallas_call(
        _fnet2_kernel,
        grid=(B, n1 // kb),
        in_specs=[pl.BlockSpec((1, 2, kb, FNET_N2, W), lambda b, i: (b, 0, i, 0, 0)),
                  pl.BlockSpec((kb, 2 * FNET_N2, 2 * FNET_N2), lambda b, i: (i, 0, 0)),
                  _const_spec(cs.shape)],
        out_specs=pl.BlockSpec((1, FNET_N2, kb, W), lambda b, i: (b, 0, i, 0)),
        out_shape=jax.ShapeDtypeStruct((B, FNET_N2, n1, W), bf16),
        compiler_params=_cparams(("parallel", "parallel")),
        name="fnet2",
    )(a.reshape(B, 2, n1, FNET_N2, W), t, cs)
    return y.reshape(B, S, W)


def _hgrn_consts(C):
    L = int(math.log2(C))
    t = np.arange(C)
    m = np.zeros((2, (L + 2) * C, C), np.float32)
    lvl = np.full((2, C, C), -1, np.int32)
    for d in (0, 1):
        for li in range(L):
            n = C >> li
            half = n // 2
            blk = t // n
            mid = blk * n + half
            upper = (t % n) >= half
            for r in range(C):
                if d == 0:
                    if upper[r]:
                        m[d, li * C + r, mid[r]:r + 1] = 1.0
                    else:
                        m[d, li * C + r, r + 1:mid[r]] = 1.0
                else:
                    if upper[r]:
                        m[d, li * C + r, mid[r]:r] = 1.0
                    else:
                        m[d, li * C + r, r:mid[r]] = 1.0
            same = blk[:, None] == blk[None, :]
            if d == 0:
                msk = same & upper[:, None] & (~upper)[None, :]
            else:
                msk = same & (~upper)[:, None] & upper[None, :]
            lvl[d][msk] = li
        for r in range(C):
            if d == 0:
                m[d, L * C + r, :r + 1] = 1.0
                m[d, (L + 1) * C + r, r + 1:] = 1.0
            else:
                m[d, L * C + r, r:] = 1.0
                m[d, (L + 1) * C + r, :r] = 1.0
    c = np.arange(BRANCH)
    headsum = ((c[:, None] // HGRN_DIM) == (c[None, :] // HGRN_DIM)).astype(np.float32)
    lvl = np.tile(lvl, (1, 1, HGRN_HEADS))
    m = np.tile(m, (1, 1, 2))
    return jnp.asarray(m, f32).astype(bf16), jnp.asarray(lvl), jnp.asarray(headsum, f32).astype(bf16)


def _hgrn_kernel(qf_ref, vf_ref, kf_ref, lff_ref, qb_ref, vb_ref, kb_ref, lfb_ref, m_ref, lvl_ref, hs_ref,
                 of_ref, ob_ref, st_ref, *, C, L):
    @pl.when(pl.program_id(1) == 0)
    def _():
        st_ref[...] = jnp.zeros(st_ref.shape, f32)

    refs = ((qf_ref, vf_ref, kf_ref, lff_ref, of_ref), (qb_ref, vb_ref, kb_ref, lfb_ref, ob_ref))
    head = lax.broadcasted_iota(jnp.int32, (1, BRANCH), 1) // HGRN_DIM
    r = lax.broadcasted_iota(jnp.int32, (BRANCH, BRANCH), 0) // HGRN_DIM
    cidx = lax.broadcasted_iota(jnp.int32, (BRANCH, BRANCH), 1) // HGRN_DIM
    n_chunks = qf_ref.shape[1] // C
    order = (tuple(range(n_chunks)), tuple(reversed(range(n_chunks))))
    chains = [(d, order[d][j]) for j in range(n_chunks) for d in (0, 1)]
    rows = {c: slice(c[1] * C, (c[1] + 1) * C) for c in chains}
    q = {c: refs[c[0]][0][0, rows[c], :].astype(f32) for c in chains}
    v = {c: refs[c[0]][1][0, rows[c], :] for c in chains}
    k = {c: refs[c[0]][2][0, rows[c], :].astype(f32) for c in chains}
    e = {}
    for c in chains:
        lf = refs[c[0]][3][0, rows[c], :]
        lf2 = jnp.concatenate([lf[:, :BRANCH], lf[:, BRANCH:]], axis=0)
        e[c] = jnp.dot(m_ref[c[0]], lf2, preferred_element_type=f32)

    o = {}
    st = [st_ref[0], st_ref[1]]
    for c in chains:
        d = c[0]
        ein = jnp.exp(e[c][L * C:(L + 1) * C])
        o[c] = lax.dot_general((q[c] * ein).astype(bf16), st[d].astype(bf16), _NT, preferred_element_type=f32)
        kout = (k[c] * jnp.exp(e[c][(L + 1) * C:(L + 2) * C])).astype(bf16)
        upd = lax.dot_general(v[c], kout, _TN, preferred_element_type=f32)
        total = ein[C - 1:C] if d == 0 else ein[0:1]
        st[d] = st[d] * total + jnp.where(r == cidx, upd, 0.0)
        o[c] = o[c] + jnp.dot((q[c] * k[c]).astype(bf16), hs_ref[...], preferred_element_type=f32) * v[c].astype(f32)
    st_ref[0] = st[0]
    st_ref[1] = st[1]

    scores = {c: jnp.zeros((C, HGRN_HEADS * C), f32) for c in chains}
    for li in range(L):
        for c in chains:
            ex = jnp.exp(e[c][li * C:(li + 1) * C])
            qt = (q[c] * ex).astype(bf16)
            kt = (k[c] * ex).astype(bf16)
            kstack = jnp.concatenate([jnp.where(head == hh, kt, jnp.zeros_like(kt)) for hh in range(HGRN_HEADS)],
                                     axis=0)
            s = lax.dot_general(qt, kstack, _NT, preferred_element_type=f32)
            scores[c] = jnp.where(lvl_ref[c[0]] == li, s, scores[c])
    for c in chains:
        vstack = jnp.concatenate([jnp.where(head == hh, v[c], jnp.zeros_like(v[c])) for hh in range(HGRN_HEADS)],
                                 axis=0)
        oc = o[c] + jnp.dot(scores[c].astype(bf16), vstack, preferred_element_type=f32)
        refs[c[0]][4][0, rows[c], :] = oc.astype(bf16)


def _hgrn(hq, hv, hk, hlf, consts, C):
    B, S, W = hq.shape
    m, lvl, hs = consts
    L = int(math.log2(C))
    rows = min(S, HGRN_CHUNKS_PER_STEP * C)
    nb = S // rows
    fwd = lambda w, c: pl.BlockSpec((1, rows, w), lambda b, i: (b, i, c))
    bwd = lambda w, c: pl.BlockSpec((1, rows, w), lambda b, i: (b, nb - 1 - i, c))
    return pl.pallas_call(
        functools.partial(_hgrn_kernel, C=C, L=L),
        grid=(B, nb),
        in_specs=[fwd(W, 0), fwd(W, 0), fwd(W, 0), fwd(2 * W, 0), bwd(W, 0), bwd(W, 0), bwd(W, 1), bwd(2 * W, 1),
                  _const_spec(m.shape), _const_spec(lvl.shape), _const_spec(hs.shape)],
        out_specs=[fwd(W, 0), bwd(W, 0)],
        out_shape=[jax.ShapeDtypeStruct((B, S, W), bf16)] * 2,
        scratch_shapes=[pltpu.VMEM((2, BRANCH, BRANCH), f32)],
        compiler_params=_cparams(("parallel", "arbitrary")),
        name="hgrn",
    )(hq, hv, hk, hlf, hq, hv, hk, hlf, m, lvl, hs)


def _merge_ffn_kernel(x_ref, om_ref, of_ref, od_ref, ohf_ref, ohb_ref, hg_ref, ln_ref, wg_ref, wbm_ref, wbf_ref,
                      wbd_ref, wbh_ref, wo_ref, gon_ref, g64_ref, lnf_ref, wgu_ref, wd_ref, o_ref):
    x = x_ref[0]
    h = _rms(x, ln_ref[...]).astype(bf16)
    oh = ohf_ref[0].astype(f32) + ohb_ref[0].astype(f32)
    ms = jnp.dot((oh * oh).astype(bf16), g64_ref[...], preferred_element_type=f32)
    oh = (oh * lax.rsqrt(ms + EPS) * gon_ref[...] * _sigmoid(hg_ref[0].astype(f32))).astype(bf16)
    merged = None
    for n, (o_n, w_ref) in enumerate(((om_ref[0], wbm_ref), (of_ref[0], wbf_ref), (od_ref[0], wbd_ref), (oh, wbh_ref))):
        gate = _sigmoid(jnp.dot(h, wg_ref[:, n * D_MODEL:(n + 1) * D_MODEL], preferred_element_type=f32))
        y = gate * jnp.dot(o_n, w_ref[...], preferred_element_type=f32)
        merged = y if merged is None else merged + y
    x = x + jnp.dot(merged.astype(bf16), wo_ref[...], preferred_element_type=f32)

    h = _rms(x, lnf_ref[...]).astype(bf16)
    acc = x
    for c in range(D_FF // FF_CHUNK):
        g = jnp.dot(h, wgu_ref[:, c * FF_CHUNK:(c + 1) * FF_CHUNK], preferred_element_type=f32)
        u = jnp.dot(h, wgu_ref[:, D_FF + c * FF_CHUNK:D_FF + (c + 1) * FF_CHUNK], preferred_element_type=f32)
        a = (g * _sigmoid(g) * u).astype(bf16)
        acc = acc + jnp.dot(a, wd_ref[c], preferred_element_type=f32)
    o_ref[0] = acc


def _merge_ffn(x, om, of, od, oh, hg, p, tm):
    B, S, _ = x.shape
    row = lambda w: pl.BlockSpec((1, tm, w), lambda b, i: (b, i, 0))
    consts = (p["ln_mix"], p["wgate"], p["wbm"], p["wbf"], p["wbd"], p["wbh"], p["wout"], p["gon"], p["g64"],
              p["ln_ffn"], p["wgu"], p["wd"])
    return pl.pallas_call(
        _merge_ffn_kernel,
        grid=(B, S // tm),
        in_specs=[row(D_MODEL)] + [row(BRANCH)] * 6
                 + [_const_spec(c.shape) for c in consts],
        out_specs=row(D_MODEL),
        out_shape=jax.ShapeDtypeStruct((B, S, D_MODEL), f32),
        compiler_params=_cparams(("parallel", "parallel")),
        name="merge_ffn",
    )(x, om, of, od, oh[0], oh[1], hg, *consts)


def _pad_heads(w, heads, width):
    lead = w.shape[:-1]
    w = w.reshape(lead + (heads, width))
    w = jnp.pad(w, [(0, 0)] * len(lead) + [(0, 0), (0, LANE - width)])
    return w.reshape(lead + (heads * LANE,))


def _rope_tables(S):
    def cs(dim):
        inv = 1.0 / (ROPE_THETA ** (jnp.arange(0, dim, 2, dtype=f32) / dim))
        ang = jnp.arange(S, dtype=f32)[:, None] * inv[None, :]
        return jnp.cos(ang), jnp.sin(ang)

    cm, sm = cs(MLA_ROPE)
    one, zero = jnp.ones((S, MLA_NOPE), f32), jnp.zeros((S, MLA_NOPE), f32)
    z16, z32, o32 = jnp.zeros((S, 16), f32), jnp.zeros((S, 32), f32), jnp.ones((S, 32), f32)
    rope_m = jnp.stack([jnp.concatenate([one, cm, cm, o32], 1),
                        jnp.concatenate([zero, -sm, z16, z32], 1),
                        jnp.concatenate([zero, z16, sm, z32], 1)])
    cd, sd = cs(DIFF_HEAD_DIM)
    rope_d = jnp.stack([jnp.tile(jnp.concatenate([cd, cd], 1), (1, 4)),
                        jnp.tile(jnp.concatenate([-sd, z16], 1), (1, 4)),
                        jnp.tile(jnp.concatenate([z16, sd], 1), (1, 4))])
    return rope_m, rope_d


def _block_mean(width, group):
    c = np.arange(width)
    return jnp.asarray(((c[:, None] // group) == (c[None, :] // group)) / group, f32).astype(bf16)


def _layer_params(l, a, lower_bounds):
    w = a["w_in"][l]
    offs = np.cumsum([0, 384, 128, 32, 256, 256, 256, 256, 256, 256, 256, 256, 256, 4096])
    col = lambda i: w[:, offs[i]:offs[i + 1]]
    wcat = jnp.concatenate([col(0), col(1), jnp.pad(col(2), ((0, 0), (MLA_NOPE, LANE - MLA_QK))), col(3), col(4),
                            col(5), _pad_heads(col(6), DIFF_HEADS, 2 * DIFF_HEAD_DIM), col(7), col(8), col(9),
                            col(10), col(11)], axis=1).astype(bf16)
    wukv = a["mla_w_ukv"][l].reshape(MLA_KV_LORA, MLA_HEADS, MLA_NOPE + MLA_V)
    wukv = jnp.concatenate([_pad_heads(wukv[:, :, :MLA_NOPE].reshape(MLA_KV_LORA, -1), MLA_HEADS, MLA_NOPE),
                            _pad_heads(wukv[:, :, MLA_NOPE:].reshape(MLA_KV_LORA, -1), MLA_HEADS, MLA_V)], axis=1)
    lam_init = 0.8 - 0.6 * math.exp(-0.3 * l)
    lam = (jnp.exp(jnp.sum(a["diff_lq1"][l] * a["diff_lk1"][l])) - jnp.exp(jnp.sum(a["diff_lq2"][l] * a["diff_lk2"][l]))
           + lam_init)
    wb = a["w_branch"][l]
    nc = D_FF // FF_CHUNK
    row = lambda v: v.reshape(1, -1).astype(f32)
    return {
        "ln_mix": row(a["ln_mix"][l]),
        "wcat": wcat,
        "gqa": row(a["mla_g_qa"][l]),
        "wuq": _pad_heads(a["mla_w_uq"][l], MLA_HEADS, MLA_QK).astype(bf16),
        "gkva": row(a["mla_g_kva"][l]),
        "wukv": wukv.astype(bf16),
        "gqn": row(jnp.pad(a["mla_g_qn"][l], (0, LANE - MLA_QK))) * (MLA_QK ** -0.5 * LOG2E),
        "gkn": row(jnp.pad(a["mla_g_kn"][l], (0, LANE - MLA_QK))),
        "gdq": row(jnp.tile(a["diff_g_qn"][l], 2 * DIFF_HEADS)) * (DIFF_HEAD_DIM ** -0.5 * LOG2E),
        "gdk": row(jnp.tile(a["diff_g_kn"][l], 2 * DIFF_HEADS)),
        "g32": _block_mean(BRANCH, DIFF_HEAD_DIM),
        "lb": lower_bounds[:, l].astype(f32),
        "lam": lam.reshape(1).astype(f32),
        "mla_bound": MLA_QK ** 0.5 * LOG2E * jnp.max(jnp.abs(a["mla_g_qn"][l])) * jnp.max(jnp.abs(a["mla_g_kn"][l])),
        "diff_bound": (DIFF_HEAD_DIM ** 0.5 * LOG2E * jnp.max(jnp.abs(a["diff_g_qn"][l]))
                       * jnp.max(jnp.abs(a["diff_g_kn"][l]))),
        "gsub": row(a["diff_g_sub"][l]) * (1.0 - lam_init),
        "wgate": col(12).astype(bf16),
        "wbm": wb[0].astype(bf16),
        "wbf": wb[1].astype(bf16),
        "wbd": wb[2].astype(bf16),
        "wbh": wb[3].astype(bf16),
        "wout": a["w_out"][l].astype(bf16),
        "gon": row(jnp.tile(a["hgrn_g_on"][l], HGRN_HEADS)),
        "g64": _block_mean(BRANCH, HGRN_DIM),
        "ln_ffn": row(a["ln_ffn"][l]),
        "wgu": a["w_gate_up"][l].astype(bf16),
        "wd": a["w_down"][l].reshape(nc, FF_CHUNK, D_MODEL).astype(bf16),
    }


def _tiles(S):
    return {"tm_in": min(256, S),"tq": min(512, S), "tk": min(512, S), "tqb": min(512, S), "tkb": min(4096, S),
            "hgrn_chunk": min(64, S),
            "tm_merge": min(512, S)}


def kernel(x, ln_mix, w_in, mla_g_qa, mla_w_uq, mla_g_kva, mla_w_ukv, mla_g_qn, mla_g_kn, diff_g_qn, diff_g_kn,
           diff_lq1, diff_lk1, diff_lq2, diff_lk2, diff_g_sub, hgrn_lb_logits, hgrn_g_on, w_branch, w_out, ln_ffn,
           w_gate_up, w_down):
    a = dict(ln_mix=ln_mix, w_in=w_in, mla_g_qa=mla_g_qa, mla_w_uq=mla_w_uq, mla_g_kva=mla_g_kva,
             mla_w_ukv=mla_w_ukv, mla_g_qn=mla_g_qn, mla_g_kn=mla_g_kn, diff_g_qn=diff_g_qn, diff_g_kn=diff_g_kn,
             diff_lq1=diff_lq1, diff_lk1=diff_lk1, diff_lq2=diff_lq2, diff_lk2=diff_lk2, diff_g_sub=diff_g_sub,
             hgrn_g_on=hgrn_g_on, w_branch=w_branch, w_out=w_out, ln_ffn=ln_ffn, w_gate_up=w_gate_up, w_down=w_down)
    S = x.shape[1]
    t = _tiles(S)
    rope_m, rope_d = _rope_tables(S)
    lb_p = jax.nn.softmax(hgrn_lb_logits.astype(f32), axis=1)
    lower_bounds = jnp.cumsum(lb_p, axis=1) - lb_p[:, :1]
    fnet_consts = _fnet_consts(S)
    hgrn_consts = _hgrn_consts(t["hgrn_chunk"])
    for l in range(DEPTH):
        p = _layer_params(l, a, lower_bounds)
        qm, km, vm, fn, qd, kd, vd, hq, hv, hk, hlf, hg = _in_proj(x, p, rope_m, rope_d, t["tm_in"])
        om = lax.cond(p["mla_bound"] <= SCORE_BOUND_LOG2,
                      lambda: _mla_attn_bounded(qm, km, vm, t["tqb"], t["tkb"]),
                      lambda: _mla_attn(qm, km, vm, t["tq"], t["tk"]))
        od = lax.cond(p["diff_bound"] <= SCORE_BOUND_LOG2,
                      lambda: _diff_attn_bounded(p["lam"], qd, kd, vd, p["gsub"], t["tqb"], t["tkb"]),
                      lambda: _diff_attn(p["lam"], qd, kd, vd, p["gsub"], t["tq"], t["tk"]))
        of = _fnet(fn, fnet_consts)
        oh = _hgrn(hq, hv, hk, hlf, hgrn_consts, t["hgrn_chunk"])
        x = _merge_ffn(x, om, of, od, oh, hg, p, t["tm_merge"])
    return x
```

```python
import functools
import math

import numpy as np
import jax
import jax.numpy as jnp
from jax import lax
from jax.experimental import pallas as pl
from jax.experimental.pallas import tpu as pltpu

f32 = jnp.float32
bf16 = jnp.bfloat16

D_MODEL = 1024
DEPTH = 2
MLA_HEADS = 4
MLA_Q_LORA = 384
MLA_KV_LORA = 128
MLA_NOPE = 64
MLA_ROPE = 32
MLA_V = 64
MLA_QK = MLA_NOPE + MLA_ROPE
DIFF_HEADS = 4
DIFF_HEAD_DIM = 32
HGRN_HEADS = 4
HGRN_DIM = 64
BRANCH = 256
D_FF = 2816
FF_CHUNK = 256
ROPE_THETA = 10000.0
EPS = 1e-6
LOG2E = 1.4426950408889634
LANE = 128
FNET_N2 = 64
FNET_K1_BLOCK = 8
HGRN_CHUNKS_PER_STEP = 8
IN_PROJ_CHAINS_IN_FLIGHT = 4
ATTN_KEY_SUB = 256
VT_ROWS = LANE
ATTN_CHAINS_IN_FLIGHT = 8
VMEM_LIMIT = 56 * 1024 * 1024
SCORE_BOUND_LOG2 = 60.0

_SEG = {}
_off = 0
for _name, _w in (("cq", 384), ("ckv", 128), ("krope", 128), ("fnet", 256), ("dq", 256), ("dk", 256),
                  ("dv", 512), ("hq", 256), ("hi", 256), ("hff", 256), ("hfb", 256), ("hg", 256)):
    _SEG[_name] = (_off, _off + _w)
    _off += _w
W_CAT = _off


def _cparams(sem):
    return pltpu.CompilerParams(dimension_semantics=sem, vmem_limit_bytes=VMEM_LIMIT)


def _const_spec(shape):
    nd = len(shape)
    return pl.BlockSpec(shape, lambda *_: (0,) * nd, pipeline_mode=pl.Buffered(1))


def _rms(x, g):
    return x * lax.rsqrt(jnp.mean(x * x, axis=-1, keepdims=True) + EPS) * g


def _sigmoid(x):
    return 1.0 / (1.0 + jnp.exp(-x))


def _rope(t, tab_ref):
    return t * tab_ref[0] + pltpu.roll(t, LANE - 16, 1) * tab_ref[1] + pltpu.roll(t, 16, 1) * tab_ref[2]


def _in_proj_kernel(x_ref, ln_ref, wcat_ref, gqa_ref, wuq_ref, gkva_ref, wukv_ref, gqn_ref, gkn_ref,
                    ropem_ref, gdq_ref, gdk_ref, g32_ref, roped_ref, lb_ref,
                    qm_ref, km_ref, vm_ref, fn_ref, qd_ref, kd_ref, vd_ref, hq_ref, hv_ref, hk_ref,
                    hlf_ref, hg_ref):
    h = _rms(x_ref[0], ln_ref[...]).astype(bf16)

    def seg(name):
        a, b = _SEG[name]
        return jnp.dot(h, wcat_ref[:, a:b], preferred_element_type=f32)

    ones_lane = (lax.broadcasted_iota(jnp.int32, (1, LANE), 1) == MLA_V).astype(f32)


    def mla_q():
        cq = seg("cq")
        yield
        q = jnp.dot(_rms(cq, gqa_ref[...]).astype(bf16), wuq_ref[...], preferred_element_type=f32)
        yield
        for hh in range(MLA_HEADS):
            sl = slice(hh * LANE, (hh + 1) * LANE)
            qh = q[:, sl]
            ms = jnp.sum(qh * qh, axis=-1, keepdims=True) * (1.0 / MLA_QK)
            qm_ref[0, :, sl] = _rope(qh * lax.rsqrt(ms + EPS) * gqn_ref[...], ropem_ref).astype(bf16)

    def mla_kv():
        ckv = seg("ckv")
        krope = seg("krope")
        yield
        kv = jnp.dot(_rms(ckv, gkva_ref[...]).astype(bf16), wukv_ref[...], preferred_element_type=f32)
        yield
        for hh in range(MLA_HEADS):
            sl = slice(hh * LANE, (hh + 1) * LANE)
            kh = kv[:, sl] + krope
            ms = jnp.sum(kh * kh, axis=-1, keepdims=True) * (1.0 / MLA_QK)
            km_ref[0, :, sl] = _rope(kh * lax.rsqrt(ms + EPS) * gkn_ref[...], ropem_ref).astype(bf16)
            vsl = slice((MLA_HEADS + hh) * LANE, (MLA_HEADS + hh + 1) * LANE)
            vm_ref[0, :, sl] = (kv[:, vsl] + ones_lane).astype(bf16)

    def diff_qk(name, g_ref, o_ref):
        t = seg(name)
        yield
        ms = jnp.dot((t * t).astype(bf16), g32_ref[...], preferred_element_type=f32)
        yield
        t = t * lax.rsqrt(ms + EPS) * g_ref[...]
        for c in range(BRANCH // LANE):
            sl = slice(c * LANE, (c + 1) * LANE)
            o_ref[0, :, sl] = _rope(t[:, sl], roped_ref).astype(bf16)

    def diff_v():
        dv = seg("dv")
        yield
        for hh in range(DIFF_HEADS):
            sl = slice(hh * LANE, (hh + 1) * LANE)
            vd_ref[0, :, sl] = (dv[:, sl] + ones_lane).astype(bf16)

    def plain(name, o_ref):
        z = seg(name)
        yield
        o_ref[0] = z.astype(bf16)

    def hgrn_gate(d, name):
        z = seg(name)
        yield
        lb = lb_ref[d:d + 1, :]
        f = lb + (1.0 - lb) * _sigmoid(z)
        hk_ref[0, :, d * BRANCH:(d + 1) * BRANCH] = (1.0 - f).astype(bf16)
        lf = jnp.log(f)
        hi = lf.astype(bf16)
        hlf_ref[0, :, 2 * d * BRANCH:(2 * d + 1) * BRANCH] = hi
        hlf_ref[0, :, (2 * d + 1) * BRANCH:(2 * d + 2) * BRANCH] = (lf - hi.astype(f32)).astype(bf16)

    _round_robin([mla_q(), hgrn_gate(0, "hff"), mla_kv(), hgrn_gate(1, "hfb"), diff_qk("dq", gdq_ref, qd_ref),
                  plain("fnet", fn_ref), diff_qk("dk", gdk_ref, kd_ref), plain("hq", hq_ref), diff_v(),
                  plain("hi", hv_ref), plain("hg", hg_ref)], IN_PROJ_CHAINS_IN_FLIGHT)


def _in_proj(x, p, rope_m, rope_d, tm):
    B, S, _ = x.shape
    row = lambda w: pl.BlockSpec((1, tm, w), lambda b, i: (b, i, 0))
    tab = pl.BlockSpec((3, tm, LANE), lambda b, i: (0, i, 0))
    out_w = (512, 512, 512, 256, 256, 256, 512, 256, 256, 512, 1024, 256)
    consts = (p["ln_mix"], p["wcat"], p["gqa"], p["wuq"], p["gkva"], p["wukv"], p["gqn"], p["gkn"])
    consts2 = (p["gdq"], p["gdk"], p["g32"])
    in_specs = ([row(D_MODEL)] + [_const_spec(c.shape) for c in consts] + [tab]
                + [_const_spec(c.shape) for c in consts2] + [tab, _const_spec(p["lb"].shape)])
    return pl.pallas_call(
        _in_proj_kernel,
        grid=(B, S // tm),
        in_specs=in_specs,
        out_specs=[row(w) for w in out_w],
        out_shape=[jax.ShapeDtypeStruct((B, S, w), bf16) for w in out_w],
        compiler_params=_cparams(("parallel", "parallel")),
        name="in_proj",
    )(x, *consts, rope_m, *consts2, rope_d, p["lb"])


def _online_softmax_step(s, v, m_ref, acc_ref, idx):
    m_prev = m_ref[idx]
    m_new = jnp.maximum(m_prev, jnp.max(s, axis=1, keepdims=True))
    alpha = jnp.exp2(m_prev - m_new)
    p = jnp.exp2((s - m_new[:, :1]).astype(bf16))
    acc_ref[idx] = alpha * acc_ref[idx] + jnp.dot(p, v, preferred_element_type=f32)
    m_ref[idx] = m_new


_NT = (((1,), (1,)), ((), ()))
_TN = (((0,), (0,)), ((), ()))


def _mla_attn_kernel(q_ref, k_ref, v_ref, o_ref, m_ref, acc_ref):
    j = pl.program_id(2)

    @pl.when(j == 0)
    def _():
        m_ref[...] = jnp.full(m_ref.shape, -jnp.inf, f32)
        acc_ref[...] = jnp.zeros(acc_ref.shape, f32)

    for hh in range(MLA_HEADS):
        sl = slice(hh * LANE, (hh + 1) * LANE)
        s = lax.dot_general(q_ref[0, :, sl], k_ref[0, :, sl], _NT, preferred_element_type=f32)
        _online_softmax_step(s, v_ref[0, :, sl], m_ref, acc_ref, hh)

    @pl.when(j == pl.num_programs(2) - 1)
    def _():
        _mla_finalize(acc_ref, o_ref)


def _mla_finalize(acc_ref, o_ref, transposed=False):
    for hh in range(MLA_HEADS):
        acc = acc_ref[hh]
        if transposed:
            o = (acc[:MLA_V] * (1.0 / acc[MLA_V:MLA_V + 1])).T
        else:
            o = acc[:, :MLA_V] * (1.0 / acc[:, MLA_V:MLA_V + 1])
        o_ref[0, :, hh * MLA_V:(hh + 1) * MLA_V] = o.astype(bf16)


def _diff_finalize(lam_ref, gsub_ref, acc_ref, o_ref, transposed=False):
    lam = lam_ref[0]
    dv = 2 * DIFF_HEAD_DIM
    for hh in range(DIFF_HEADS):
        a1 = acc_ref[2 * hh]
        a2 = acc_ref[2 * hh + 1]
        if transposed:
            o = a1[:dv] * (1.0 / a1[dv:dv + 1]) - a2[:dv] * (lam / a2[dv:dv + 1])
            o = (o * lax.rsqrt(jnp.mean(o * o, axis=0, keepdims=True) + EPS)).T
        else:
            o = a1[:, :dv] * (1.0 / a1[:, dv:dv + 1]) - a2[:, :dv] * (lam / a2[:, dv:dv + 1])
            o = o * lax.rsqrt(jnp.mean(o * o, axis=-1, keepdims=True) + EPS)
        o_ref[0, :, hh * dv:(hh + 1) * dv] = (o * gsub_ref[...]).astype(bf16)


def _diff_attn_kernel(lam_ref, q_ref, k_ref, v_ref, gsub_ref, o_ref, m_ref, acc_ref):
    j = pl.program_id(2)

    @pl.when(j == 0)
    def _():
        m_ref[...] = jnp.full(m_ref.shape, -jnp.inf, f32)
        acc_ref[...] = jnp.zeros(acc_ref.shape, f32)

    q = q_ref[0]
    k = k_ref[0]
    group = lax.broadcasted_iota(jnp.int32, (1, BRANCH), 1) // DIFF_HEAD_DIM
    for g in range(2 * DIFF_HEADS):
        qg = jnp.where(group == g, q, jnp.zeros_like(q))
        s = lax.dot_general(qg, k, _NT, preferred_element_type=f32)
        hh = g // 2
        _online_softmax_step(s, v_ref[0, :, hh * LANE:(hh + 1) * LANE], m_ref, acc_ref, g)

    @pl.when(j == pl.num_programs(2) - 1)
    def _():
        _diff_finalize(lam_ref, gsub_ref, acc_ref, o_ref)


def _round_robin(chains, in_flight):
    pending, active = list(chains), []
    while pending or active:
        while pending and len(active) < in_flight:
            active.append(pending.pop(0))
        for c in list(active):
            if next(c, "done") == "done":
                active.remove(c)


def _bounded_attn_loop(maps, k_ref, v_ref, acc_ref, tk):
    acc_ref[...] = jnp.zeros(acc_ref.shape, f32)

    def kv_block(j, carry):
        def chain(g, r0):
            q, ksl, vsl = maps[g]
            keys = pl.ds(pl.multiple_of(j * tk + r0, ATTN_KEY_SUB), ATTN_KEY_SUB)
            st = lax.dot_general(k_ref[0, keys, ksl], q, _NT, preferred_element_type=f32)
            yield
            pt = jnp.exp2(st).astype(bf16)
            yield
            acc_ref[g] += lax.dot_general(v_ref[0, keys, vsl], pt, _TN, preferred_element_type=f32)

        _round_robin([chain(g, r0) for r0 in range(0, tk, ATTN_KEY_SUB) for g in range(len(maps))],
                     ATTN_CHAINS_IN_FLIGHT)
        return carry

    lax.fori_loop(0, k_ref.shape[1] // tk, kv_block, 0)


def _mla_attn_bounded_kernel(q_ref, k_ref, v_ref, o_ref, acc_ref, *, tk):
    maps = [(q_ref[0, :, hh * LANE:(hh + 1) * LANE], slice(hh * LANE, (hh + 1) * LANE),
             slice(hh * LANE, hh * LANE + VT_ROWS)) for hh in range(MLA_HEADS)]
    _bounded_attn_loop(maps, k_ref, v_ref, acc_ref, tk)
    _mla_finalize(acc_ref, o_ref, transposed=True)


def _diff_attn_bounded_kernel(lam_ref, q_ref, k_ref, v_ref, gsub_ref, o_ref, acc_ref, *, tk):
    q = q_ref[0]
    group = lax.broadcasted_iota(jnp.int32, (1, BRANCH), 1) // DIFF_HEAD_DIM
    maps = [(jnp.where(group == g, q, jnp.zeros_like(q)), slice(0, BRANCH),
             slice((g // 2) * LANE, (g // 2) * LANE + VT_ROWS)) for g in range(2 * DIFF_HEADS)]
    _bounded_attn_loop(maps, k_ref, v_ref, acc_ref, tk)
    _diff_finalize(lam_ref, gsub_ref, acc_ref, o_ref, transposed=True)


def _attn_bounded_call(kernel_fn, n_acc, name, args, in_specs, S, W, B, tq):
    return pl.pallas_call(
        kernel_fn,
        grid=(B, S // tq),
        in_specs=in_specs,
        out_specs=pl.BlockSpec((1, tq, BRANCH), lambda b, i: (b, i, 0)),
        out_shape=jax.ShapeDtypeStruct((B, S, BRANCH), bf16),
        scratch_shapes=[pltpu.VMEM((n_acc, VT_ROWS, tq), f32)],
        compiler_params=_cparams(("parallel", "parallel")),
        name=name,
    )(*args)


def _mla_attn_bounded(q, k, v, tq, tk):
    B, S, W = q.shape
    full = pl.BlockSpec((1, S, W), lambda b, i: (b, 0, 0))
    in_specs = [pl.BlockSpec((1, tq, W), lambda b, i: (b, i, 0)), full, full]
    return _attn_bounded_call(functools.partial(_mla_attn_bounded_kernel, tk=tk), MLA_HEADS, "mla_attn_bounded",
                              (q, k, v), in_specs, S, W, B, tq)


def _diff_attn_bounded(lam, q, k, v, gsub, tq, tk):
    B, S, W = v.shape
    in_specs = [pl.BlockSpec(memory_space=pltpu.SMEM),
                pl.BlockSpec((1, tq, BRANCH), lambda b, i: (b, i, 0)),
                pl.BlockSpec((1, S, BRANCH), lambda b, i: (b, 0, 0)),
                pl.BlockSpec((1, S, W), lambda b, i: (b, 0, 0)),
                pl.BlockSpec((1, 2 * DIFF_HEAD_DIM), lambda b, i: (0, 0))]
    return _attn_bounded_call(functools.partial(_diff_attn_bounded_kernel, tk=tk), 2 * DIFF_HEADS,
                              "diff_attn_bounded", (lam, q, k, v, gsub), in_specs, S, W, B, tq)


def _mla_attn(q, k, v, tq, tk):
    B, S, W = q.shape
    return pl.pallas_call(
        _mla_attn_kernel,
        grid=(B, S // tq, S // tk),
        in_specs=[pl.BlockSpec((1, tq, W), lambda b, i, j: (b, i, 0)),
                  pl.BlockSpec((1, tk, W), lambda b, i, j: (b, j, 0)),
                  pl.BlockSpec((1, tk, W), lambda b, i, j: (b, j, 0))],
        out_specs=pl.BlockSpec((1, tq, BRANCH), lambda b, i, j: (b, i, 0)),
        out_shape=jax.ShapeDtypeStruct((B, S, BRANCH), bf16),
        scratch_shapes=[pltpu.VMEM((MLA_HEADS, tq, LANE), f32), pltpu.VMEM((MLA_HEADS, tq, LANE), f32)],
        compiler_params=_cparams(("parallel", "parallel", "arbitrary")),
        name="mla_attn",
    )(q, k, v)


def _diff_attn(lam, q, k, v, gsub, tq, tk):
    B, S, W = v.shape
    return pl.pallas_call(
        _diff_attn_kernel,
        grid=(B, S // tq, S // tk),
        in_specs=[pl.BlockSpec(memory_space=pltpu.SMEM),
                  pl.BlockSpec((1, tq, BRANCH), lambda b, i, j: (b, i, 0)),
                  pl.BlockSpec((1, tk, BRANCH), lambda b, i, j: (b, j, 0)),
                  pl.BlockSpec((1, tk, W), lambda b, i, j: (b, j, 0)),
                  pl.BlockSpec((1, 2 * DIFF_HEAD_DIM), lambda b, i, j: (0, 0))],
        out_specs=pl.BlockSpec((1, tq, BRANCH), lambda b, i, j: (b, i, 0)),
        out_shape=jax.ShapeDtypeStruct((B, S, BRANCH), bf16),
        scratch_shapes=[pltpu.VMEM((2 * DIFF_HEADS, tq, LANE), f32), pltpu.VMEM((2 * DIFF_HEADS, tq, LANE), f32)],
        compiler_params=_cparams(("parallel", "parallel", "arbitrary")),
        name="diff_attn",
    )(lam, q, k, v, gsub)


def _fnet1_kernel(x_ref, w_ref, a_ref):
    a_ref[0] = jnp.dot(w_ref[...], x_ref[0], preferred_element_type=f32).astype(bf16)


def _fnet2_kernel(a_ref, t_ref, cs_ref, o_ref):
    zs = []
    for jj in range(FNET_K1_BLOCK):
        a = jnp.concatenate([a_ref[0, 0, jj], a_ref[0, 1, jj]], axis=0)
        zs.append(jnp.dot(t_ref[jj], a, preferred_element_type=f32))
    zc = jnp.concatenate([jnp.concatenate([z[:FNET_N2], z[FNET_N2:]], axis=1) for z in zs], axis=0).astype(bf16)
    y = jnp.dot(zc, cs_ref[...], preferred_element_type=f32).astype(bf16)
    for jj in range(FNET_K1_BLOCK):
        o_ref[0, :, jj, :] = y[jj * FNET_N2:(jj + 1) * FNET_N2]


def _fnet_consts(S):
    n1 = S // FNET_N2
    a = np.arange(n1)
    ang1 = 2.0 * np.pi * np.outer(a, a) / n1
    w1 = np.concatenate([np.cos(ang1), -np.sin(ang1)], axis=0)
    n2 = np.arange(FNET_N2)
    phi = 2.0 * np.pi * (n2[None, None, :] * a[:, None, None] / S + n2[None, None, :] * n2[None, :, None] / FNET_N2)
    tr, ti = np.cos(phi), -np.sin(phi)
    t = np.concatenate([np.concatenate([tr, -ti], axis=2), np.concatenate([ti, tr], axis=2)], axis=1)
    c = np.arange(BRANCH)
    same = (c[:, None] // FNET_N2) == (c[None, :] // FNET_N2)
    angc = 2.0 * np.pi * np.outer(c % FNET_N2, c % FNET_N2) / FNET_N2
    norm = 1.0 / math.sqrt(S * FNET_N2)
    cs = np.concatenate([np.where(same, np.cos(angc), 0.0), np.where(same, np.sin(angc), 0.0)], axis=0) * norm
    return (jnp.asarray(w1, f32).astype(bf16), jnp.asarray(t, f32).astype(bf16), jnp.asarray(cs, f32).astype(bf16))


def _fnet(u, consts):
    B, S, W = u.shape
    w1, t, cs = consts
    n1 = S // FNET_N2
    cols = FNET_N2 * W
    tn = min(cols, 4096)
    a = pl.pallas_call(
        _fnet1_kernel,
        grid=(B, cols // tn),
        in_specs=[pl.BlockSpec((1, n1, tn), lambda b, i: (b, 0, i)), _const_spec(w1.shape)],
        out_specs=pl.BlockSpec((1, 2 * n1, tn), lambda b, i: (b, 0, i)),
        out_shape=jax.ShapeDtypeStruct((B, 2 * n1, cols), bf16),
        compiler_params=_cparams(("parallel", "parallel")),
        name="fnet1",
    )(u.reshape(B, n1, cols), w1)
    kb = FNET_K1_BLOCK
    y = pl.pallas_call(
        _fnet2_kernel,
        grid=(B, n1 // kb),
        in_specs=[pl.BlockSpec((1, 2, kb, FNET_N2, W), lambda b, i: (b, 0, i, 0, 0)),
                  pl.BlockSpec((kb, 2 * FNET_N2, 2 * FNET_N2), lambda b, i: (i, 0, 0)),
                  _const_spec(cs.shape)],
        out_specs=pl.BlockSpec((1, FNET_N2, kb, W), lambda b, i: (b, 0, i, 0)),
        out_shape=jax.ShapeDtypeStruct((B, FNET_N2, n1, W), bf16),
        compiler_params=_cparams(("parallel", "parallel")),
        name="fnet2",
    )(a.reshape(B, 2, n1, FNET_N2, W), t, cs)
    return y.reshape(B, S, W)


def _hgrn_consts(C):
    L = int(math.log2(C))
    t = np.arange(C)
    m = np.zeros((2, (L + 2) * C, C), np.float32)
    lvl = np.full((2, C, C), -1, np.int32)
    for d in (0, 1):
        for li in range(L):
            n = C >> li
            half = n // 2
            blk = t // n
            mid = blk * n + half
            upper = (t % n) >= half
            for r in range(C):
                if d == 0:
                    if upper[r]:
                        m[d, li * C + r, mid[r]:r + 1] = 1.0
                    else:
                        m[d, li * C + r, r + 1:mid[r]] = 1.0
                else:
                    if upper[r]:
                        m[d, li * C + r, mid[r]:r] = 1.0
                    else:
                        m[d, li * C + r, r:mid[r]] = 1.0
            same = blk[:, None] == blk[None, :]
            if d == 0:
                msk = same & upper[:, None] & (~upper)[None, :]
            else:
                msk = same & (~upper)[:, None] & upper[None, :]
            lvl[d][msk] = li
        for r in range(C):
            if d == 0:
                m[d, L * C + r, :r + 1] = 1.0
                m[d, (L + 1) * C + r, r + 1:] = 1.0
            else:
                m[d, L * C + r, r:] = 1.0
                m[d, (L + 1) * C + r, :r] = 1.0
    c = np.arange(BRANCH)
    headsum = ((c[:, None] // HGRN_DIM) == (c[None, :] // HGRN_DIM)).astype(np.float32)
    lvl = np.tile(lvl, (1, 1, HGRN_HEADS))
    m = np.tile(m, (1, 1, 2))
    return jnp.asarray(m, f32).astype(bf16), jnp.asarray(lvl), jnp.asarray(headsum, f32).astype(bf16)


def _hgrn_kernel(qf_ref, vf_ref, kf_ref, lff_ref, qb_ref, vb_ref, kb_ref, lfb_ref, m_ref, lvl_ref, hs_ref,
                 of_ref, ob_ref, st_ref, *, C, L):
    @pl.when(pl.program_id(1) == 0)
    def _():
        st_ref[...] = jnp.zeros(st_ref.shape, f32)

    refs = ((qf_ref, vf_ref, kf_ref, lff_ref, of_ref), (qb_ref, vb_ref, kb_ref, lfb_ref, ob_ref))
    head = lax.broadcasted_iota(jnp.int32, (1, BRANCH), 1) // HGRN_DIM
    r = lax.broadcasted_iota(jnp.int32, (BRANCH, BRANCH), 0) // HGRN_DIM
    cidx = lax.broadcasted_iota(jnp.int32, (BRANCH, BRANCH), 1) // HGRN_DIM
    n_chunks = qf_ref.shape[1] // C
    order = (tuple(range(n_chunks)), tuple(reversed(range(n_chunks))))
    chains = [(d, order[d][j]) for j in range(n_chunks) for d in (0, 1)]
    rows = {c: slice(c[1] * C, (c[1] + 1) * C) for c in chains}
    q = {c: refs[c[0]][0][0, rows[c], :].astype(f32) for c in chains}
    v = {c: refs[c[0]][1][0, rows[c], :] for c in chains}
    k = {c: refs[c[0]][2][0, rows[c], :].astype(f32) for c in chains}
    e = {}
    for c in chains:
        lf = refs[c[0]][3][0, rows[c], :]
        lf2 = jnp.concatenate([lf[:, :BRANCH], lf[:, BRANCH:]], axis=0)
        e[c] = jnp.dot(m_ref[c[0]], lf2, preferred_element_type=f32)

    o = {}
    st = [st_ref[0], st_ref[1]]
    for c in chains:
        d = c[0]
        ein = jnp.exp(e[c][L * C:(L + 1) * C])
        o[c] = lax.dot_general((q[c] * ein).astype(bf16), st[d].astype(bf16), _NT, preferred_element_type=f32)
        kout = (k[c] * jnp.exp(e[c][(L + 1) * C:(L + 2) * C])).astype(bf16)
        upd = lax.dot_general(v[c], kout, _TN, preferred_element_type=f32)
        total = ein[C - 1:C] if d == 0 else ein[0:1]
        st[d] = st[d] * total + jnp.where(r == cidx, upd, 0.0)
        o[c] = o[c] + jnp.dot((q[c] * k[c]).astype(bf16), hs_ref[...], preferred_element_type=f32) * v[c].astype(f32)
    st_ref[0] = st[0]
    st_ref[1] = st[1]

    scores = {c: jnp.zeros((C, HGRN_HEADS * C), f32) for c in chains}
    for li in range(L):
        for c in chains:
            ex = jnp.exp(e[c][li * C:(li + 1) * C])
            qt = (q[c] * ex).astype(bf16)
            kt = (k[c] * ex).astype(bf16)
            kstack = jnp.concatenate([jnp.where(head == hh, kt, jnp.zeros_like(kt)) for hh in range(HGRN_HEADS)],
                                     axis=0)
            s = lax.dot_general(qt, kstack, _NT, preferred_element_type=f32)
            scores[c] = jnp.where(lvl_ref[c[0]] == li, s, scores[c])
    for c in chains:
        vstack = jnp.concatenate([jnp.where(head == hh, v[c], jnp.zeros_like(v[c])) for hh in range(HGRN_HEADS)],
                                 axis=0)
        oc = o[c] + jnp.dot(scores[c].astype(bf16), vstack, preferred_element_type=f32)
        refs[c[0]][4][0, rows[c], :] = oc.astype(bf16)


def _hgrn(hq, hv, hk, hlf, consts, C):
    B, S, W = hq.shape
    m, lvl, hs = consts
    L = int(math.log2(C))
    rows = min(S, HGRN_CHUNKS_PER_STEP * C)
    nb = S // rows
    fwd = lambda w, c: pl.BlockSpec((1, rows, w), lambda b, i: (b, i, c))
    bwd = lambda w, c: pl.BlockSpec((1, rows, w), lambda b, i: (b, nb - 1 - i, c))
    return pl.pallas_call(
        functools.partial(_hgrn_kernel, C=C, L=L),
        grid=(B, nb),
        in_specs=[fwd(W, 0), fwd(W, 0), fwd(W, 0), fwd(2 * W, 0), bwd(W, 0), bwd(W, 0), bwd(W, 1), bwd(2 * W, 1),
                  _const_spec(m.shape), _const_spec(lvl.shape), _const_spec(hs.shape)],
        out_specs=[fwd(W, 0), bwd(W, 0)],
        out_shape=[jax.ShapeDtypeStruct((B, S, W), bf16)] * 2,
        scratch_shapes=[pltpu.VMEM((2, BRANCH, BRANCH), f32)],
        compiler_params=_cparams(("parallel", "arbitrary")),
        name="hgrn",
    )(hq, hv, hk, hlf, hq, hv, hk, hlf, m, lvl, hs)


def _merge_ffn_kernel(x_ref, om_ref, of_ref, od_ref, ohf_ref, ohb_ref, hg_ref, ln_ref, wg_ref, wbm_ref, wbf_ref,
                      wbd_ref, wbh_ref, wo_ref, gon_ref, g64_ref, lnf_ref, wgu_ref, wd_ref, o_ref):
    x = x_ref[0]
    h = _rms(x, ln_ref[...]).astype(bf16)
    oh = ohf_ref[0].astype(f32) + ohb_ref[0].astype(f32)
    ms = jnp.dot((oh * oh).astype(bf16), g64_ref[...], preferred_element_type=f32)
    oh = (oh * lax.rsqrt(ms + EPS) * gon_ref[...] * _sigmoid(hg_ref[0].astype(f32))).astype(bf16)
    merged = None
    for n, (o_n, w_ref) in enumerate(((om_ref[0], wbm_ref), (of_ref[0], wbf_ref), (od_ref[0], wbd_ref), (oh, wbh_ref))):
        gate = _sigmoid(jnp.dot(h, wg_ref[:, n * D_MODEL:(n + 1) * D_MODEL], preferred_element_type=f32))
        y = gate * jnp.dot(o_n, w_ref[...], preferred_element_type=f32)
        merged = y if merged is None else merged + y
    x = x + jnp.dot(merged.astype(bf16), wo_ref[...], preferred_element_type=f32)

    h = _rms(x, lnf_ref[...]).astype(bf16)
    acc = x
    for c in range(D_FF // FF_CHUNK):
        g = jnp.dot(h, wgu_ref[:, c * FF_CHUNK:(c + 1) * FF_CHUNK], preferred_element_type=f32)
        u = jnp.dot(h, wgu_ref[:, D_FF + c * FF_CHUNK:D_FF + (c + 1) * FF_CHUNK], preferred_element_type=f32)
        a = (g * _sigmoid(g) * u).astype(bf16)
        acc = acc + jnp.dot(a, wd_ref[c], preferred_element_type=f32)
    o_ref[0] = acc


def _merge_ffn(x, om, of, od, oh, hg, p, tm):
    B, S, _ = x.shape
    row = lambda w: pl.BlockSpec((1, tm, w), lambda b, i: (b, i, 0))
    consts = (p["ln_mix"], p["wgate"], p["wbm"], p["wbf"], p["wbd"], p["wbh"], p["wout"], p["gon"], p["g64"],
              p["ln_ffn"], p["wgu"], p["wd"])
    return pl.pallas_call(
        _merge_ffn_kernel,
        grid=(B, S // tm),
        in_specs=[row(D_MODEL)] + [row(BRANCH)] * 6
                 + [_const_spec(c.shape) for c in consts],
        out_specs=row(D_MODEL),
        out_shape=jax.ShapeDtypeStruct((B, S, D_MODEL), f32),
        compiler_params=_cparams(("parallel", "parallel")),
        name="merge_ffn",
    )(x, om, of, od, oh[0], oh[1], hg, *consts)


def _pad_heads(w, heads, width):
    lead = w.shape[:-1]
    w = w.reshape(lead + (heads, width))
    w = jnp.pad(w, [(0, 0)] * len(lead) + [(0, 0), (0, LANE - width)])
    return w.reshape(lead + (heads * LANE,))


def _rope_tables(S):
    def cs(dim):
        inv = 1.0 / (ROPE_THETA ** (jnp.arange(0, dim, 2, dtype=f32) / dim))
        ang = jnp.arange(S, dtype=f32)[:, None] * inv[None, :]
        return jnp.cos(ang), jnp.sin(ang)

    cm, sm = cs(MLA_ROPE)
    one, zero = jnp.ones((S, MLA_NOPE), f32), jnp.zeros((S, MLA_NOPE), f32)
    z16, z32, o32 = jnp.zeros((S, 16), f32), jnp.zeros((S, 32), f32), jnp.ones((S, 32), f32)
    rope_m = jnp.stack([jnp.concatenate([one, cm, cm, o32], 1),
                        jnp.concatenate([zero, -sm, z16, z32], 1),
                        jnp.concatenate([zero, z16, sm, z32], 1)])
    cd, sd = cs(DIFF_HEAD_DIM)
    rope_d = jnp.stack([jnp.tile(jnp.concatenate([cd, cd], 1), (1, 4)),
                        jnp.tile(jnp.concatenate([-sd, z16], 1), (1, 4)),
                        jnp.tile(jnp.concatenate([z16, sd], 1), (1, 4))])
    return rope_m, rope_d


def _block_mean(width, group):
    c = np.arange(width)
    return jnp.asarray(((c[:, None] // group) == (c[None, :] // group)) / group, f32).astype(bf16)


def _layer_params(l, a, lower_bounds):
    w = a["w_in"][l]
    offs = np.cumsum([0, 384, 128, 32, 256, 256, 256, 256, 256, 256, 256, 256, 256, 4096])
    col = lambda i: w[:, offs[i]:offs[i + 1]]
    wcat = jnp.concatenate([col(0), col(1), jnp.pad(col(2), ((0, 0), (MLA_NOPE, LANE - MLA_QK))), col(3), col(4),
                            col(5), _pad_heads(col(6), DIFF_HEADS, 2 * DIFF_HEAD_DIM), col(7), col(8), col(9),
                            col(10), col(11)], axis=1).astype(bf16)
    wukv = a["mla_w_ukv"][l].reshape(MLA_KV_LORA, MLA_HEADS, MLA_NOPE + MLA_V)
    wukv = jnp.concatenate([_pad_heads(wukv[:, :, :MLA_NOPE].reshape(MLA_KV_LORA, -1), MLA_HEADS, MLA_NOPE),
                            _pad_heads(wukv[:, :, MLA_NOPE:].reshape(MLA_KV_LORA, -1), MLA_HEADS, MLA_V)], axis=1)
    lam_init = 0.8 - 0.6 * math.exp(-0.3 * l)
    lam = (jnp.exp(jnp.sum(a["diff_lq1"][l] * a["diff_lk1"][l])) - jnp.exp(jnp.sum(a["diff_lq2"][l] * a["diff_lk2"][l]))
           + lam_init)
    wb = a["w_branch"][l]
    nc = D_FF // FF_CHUNK
    row = lambda v: v.reshape(1, -1).astype(f32)
    return {
        "ln_mix": row(a["ln_mix"][l]),
        "wcat": wcat,
        "gqa": row(a["mla_g_qa"][l]),
        "wuq": _pad_heads(a["mla_w_uq"][l], MLA_HEADS, MLA_QK).astype(bf16),
        "gkva": row(a["mla_g_kva"][l]),
        "wukv": wukv.astype(bf16),
        "gqn": row(jnp.pad(a["mla_g_qn"][l], (0, LANE - MLA_QK))) * (MLA_QK ** -0.5 * LOG2E),
        "gkn": row(jnp.pad(a["mla_g_kn"][l], (0, LANE - MLA_QK))),
        "gdq": row(jnp.tile(a["diff_g_qn"][l], 2 * DIFF_HEADS)) * (DIFF_HEAD_DIM ** -0.5 * LOG2E),
        "gdk": row(jnp.tile(a["diff_g_kn"][l], 2 * DIFF_HEADS)),
        "g32": _block_mean(BRANCH, DIFF_HEAD_DIM),
        "lb": lower_bounds[:, l].astype(f32),
        "lam": lam.reshape(1).astype(f32),
        "mla_bound": MLA_QK ** 0.5 * LOG2E * jnp.max(jnp.abs(a["mla_g_qn"][l])) * jnp.max(jnp.abs(a["mla_g_kn"][l])),
        "diff_bound": (DIFF_HEAD_DIM ** 0.5 * LOG2E * jnp.max(jnp.abs(a["diff_g_qn"][l]))
                       * jnp.max(jnp.abs(a["diff_g_kn"][l]))),
        "gsub": row(a["diff_g_sub"][l]) * (1.0 - lam_init),
        "wgate": col(12).astype(bf16),
        "wbm": wb[0].astype(bf16),
        "wbf": wb[1].astype(bf16),
        "wbd": wb[2].astype(bf16),
        "wbh": wb[3].astype(bf16),
        "wout": a["w_out"][l].astype(bf16),
        "gon": row(jnp.tile(a["hgrn_g_on"][l], HGRN_HEADS)),
        "g64": _block_mean(BRANCH, HGRN_DIM),
        "ln_ffn": row(a["ln_ffn"][l]),
        "wgu": a["w_gate_up"][l].astype(bf16),
        "wd": a["w_down"][l].reshape(nc, FF_CHUNK, D_MODEL).astype(bf16),
    }


def _tiles(S):
    return {"tm_in": min(256, S),"tq": min(512, S), "tk": min(512, S), "tqb": min(512, S), "tkb": min(8192, S),
            "hgrn_chunk": min(64, S),
            "tm_merge": min(512, S)}


def kernel(x, ln_mix, w_in, mla_g_qa, mla_w_uq, mla_g_kva, mla_w_ukv, mla_g_qn, mla_g_kn, diff_g_qn, diff_g_kn,
           diff_lq1, diff_lk1, diff_lq2, diff_lk2, diff_g_sub, hgrn_lb_logits, hgrn_g_on, w_branch, w_out, ln_ffn,
           w_gate_up, w_down):
    a = dict(ln_mix=ln_mix, w_in=w_in, mla_g_qa=mla_g_qa, mla_w_uq=mla_w_uq, mla_g_kva=mla_g_kva,
             mla_w_ukv=mla_w_ukv, mla_g_qn=mla_g_qn, mla_g_kn=mla_g_kn, diff_g_qn=diff_g_qn, diff_g_kn=diff_g_kn,
             diff_lq1=diff_lq1, diff_lk1=diff_lk1, diff_lq2=diff_lq2, diff_lk2=diff_lk2, diff_g_sub=diff_g_sub,
             hgrn_g_on=hgrn_g_on, w_branch=w_branch, w_out=w_out, ln_ffn=ln_ffn, w_gate_up=w_gate_up, w_down=w_down)
    S = x.shape[1]
    t = _tiles(S)
    rope_m, rope_d = _rope_tables(S)
    lb_p = jax.nn.softmax(hgrn_lb_logits.astype(f32), axis=1)
    lower_bounds = jnp.cumsum(lb_p, axis=1) - lb_p[:, :1]
    fnet_consts = _fnet_consts(S)
    hgrn_consts = _hgrn_consts(t["hgrn_chunk"])
    for l in range(DEPTH):
        p = _layer_params(l, a, lower_bounds)
        qm, km, vm, fn, qd, kd, vd, hq, hv, hk, hlf, hg = _in_proj(x, p, rope_m, rope_d, t["tm_in"])
        om = lax.cond(p["mla_bound"] <= SCORE_BOUND_LOG2,
                      lambda: _mla_attn_bounded(qm, km, vm, t["tqb"], t["tkb"]),
                      lambda: _mla_attn(qm, km, vm, t["tq"], t["tk"]))
        od = lax.cond(p["diff_bound"] <= SCORE_BOUND_LOG2,
                      lambda: _diff_attn_bounded(p["lam"], qd, kd, vd, p["gsub"], t["tqb"], t["tkb"]),
                      lambda: _diff_attn(p["lam"], qd, kd, vd, p["gsub"], t["tq"], t["tk"]))
        of = _fnet(fn, fnet_consts)
        oh = _hgrn(hq, hv, hk, hlf, hgrn_consts, t["hgrn_chunk"])
        x = _merge_ffn(x, om, of, od, oh, hg, p, t["tm_merge"])
    return x
```

```python
import functools
import math

import numpy as np
import jax
import jax.numpy as jnp
from jax import lax
from jax.experimental import pallas as pl
from jax.experimental.pallas import tpu as pltpu

f32 = jnp.float32
bf16 = jnp.bfloat16

D_MODEL = 1024
DEPTH = 2
MLA_HEADS = 4
MLA_Q_LORA = 384
MLA_KV_LORA = 128
MLA_NOPE = 64
MLA_ROPE = 32
MLA_V = 64
MLA_QK = MLA_NOPE + MLA_ROPE
DIFF_HEADS = 4
DIFF_HEAD_DIM = 32
HGRN_HEADS = 4
HGRN_DIM = 64
BRANCH = 256
D_FF = 2816
FF_CHUNK = 256
ROPE_THETA = 10000.0
EPS = 1e-6
LOG2E = 1.4426950408889634
LANE = 128
FNET_N2 = 64
FNET_K1_BLOCK = 8
HGRN_CHUNKS_PER_STEP = 8
IN_PROJ_CHAINS_IN_FLIGHT = 4
ATTN_KEY_SUB = 256
VT_ROWS = LANE
ATTN_CHAINS_IN_FLIGHT = 16
VMEM_LIMIT = 56 * 1024 * 1024
SCORE_BOUND_LOG2 = 60.0

_SEG = {}
_off = 0
for _name, _w in (("cq", 384), ("ckv", 128), ("krope", 128), ("fnet", 256), ("dq", 256), ("dk", 256),
                  ("dv", 512), ("hq", 256), ("hi", 256), ("hff", 256), ("hfb", 256), ("hg", 256)):
    _SEG[_name] = (_off, _off + _w)
    _off += _w
W_CAT = _off


def _cparams(sem):
    return pltpu.CompilerParams(dimension_semantics=sem, vmem_limit_bytes=VMEM_LIMIT)


def _const_spec(shape):
    nd = len(shape)
    return pl.BlockSpec(shape, lambda *_: (0,) * nd, pipeline_mode=pl.Buffered(1))


def _rms(x, g):
    return x * lax.rsqrt(jnp.mean(x * x, axis=-1, keepdims=True) + EPS) * g


def _sigmoid(x):
    return 1.0 / (1.0 + jnp.exp(-x))


def _rope(t, tab_ref):
    return t * tab_ref[0] + pltpu.roll(t, LANE - 16, 1) * tab_ref[1] + pltpu.roll(t, 16, 1) * tab_ref[2]


def _in_proj_kernel(x_ref, ln_ref, wcat_ref, gqa_ref, wuq_ref, gkva_ref, wukv_ref, gqn_ref, gkn_ref,
                    ropem_ref, gdq_ref, gdk_ref, g32_ref, roped_ref, lb_ref,
                    qm_ref, km_ref, vm_ref, fn_ref, qd_ref, kd_ref, vd_ref, hq_ref, hv_ref, hk_ref,
                    hlf_ref, hg_ref):
    h = _rms(x_ref[0], ln_ref[...]).astype(bf16)

    def seg(name):
        a, b = _SEG[name]
        return jnp.dot(h, wcat_ref[:, a:b], preferred_element_type=f32)

    ones_lane = (lax.broadcasted_iota(jnp.int32, (1, LANE), 1) == MLA_V).astype(f32)


    def mla_q():
        cq = seg("cq")
        yield
        q = jnp.dot(_rms(cq, gqa_ref[...]).astype(bf16), wuq_ref[...], preferred_element_type=f32)
        yield
        for hh in range(MLA_HEADS):
            sl = slice(hh * LANE, (hh + 1) * LANE)
            qh = q[:, sl]
            ms = jnp.sum(qh * qh, axis=-1, keepdims=True) * (1.0 / MLA_QK)
            qm_ref[0, :, sl] = _rope(qh * lax.rsqrt(ms + EPS) * gqn_ref[...], ropem_ref).astype(bf16)

    def mla_kv():
        ckv = seg("ckv")
        krope = seg("krope")
        yield
        kv = jnp.dot(_rms(ckv, gkva_ref[...]).astype(bf16), wukv_ref[...], preferred_element_type=f32)
        yield
        for hh in range(MLA_HEADS):
            sl = slice(hh * LANE, (hh + 1) * LANE)
            kh = kv[:, sl] + krope
            ms = jnp.sum(kh * kh, axis=-1, keepdims=True) * (1.0 / MLA_QK)
            km_ref[0, :, sl] = _rope(kh * lax.rsqrt(ms + EPS) * gkn_ref[...], ropem_ref).astype(bf16)
            vsl = slice((MLA_HEADS + hh) * LANE, (MLA_HEADS + hh + 1) * LANE)
            vm_ref[0, :, sl] = (kv[:, vsl] + ones_lane).astype(bf16)

    def diff_qk(name, g_ref, o_ref):
        t = seg(name)
        yield
        ms = jnp.dot((t * t).astype(bf16), g32_ref[...], preferred_element_type=f32)
        yield
        t = t * lax.rsqrt(ms + EPS) * g_ref[...]
        for c in range(BRANCH // LANE):
            sl = slice(c * LANE, (c + 1) * LANE)
            o_ref[0, :, sl] = _rope(t[:, sl], roped_ref).astype(bf16)

    def diff_v():
        dv = seg("dv")
        yield
        for hh in range(DIFF_HEADS):
            sl = slice(hh * LANE, (hh + 1) * LANE)
            vd_ref[0, :, sl] = (dv[:, sl] + ones_lane).astype(bf16)

    def plain(name, o_ref):
        z = seg(name)
        yield
        o_ref[0] = z.astype(bf16)

    def hgrn_gate(d, name):
        z = seg(name)
        yield
        lb = lb_ref[d:d + 1, :]
        f = lb + (1.0 - lb) * _sigmoid(z)
        hk_ref[0, :, d * BRANCH:(d + 1) * BRANCH] = (1.0 - f).astype(bf16)
        lf = jnp.log(f)
        hi = lf.astype(bf16)
        hlf_ref[0, :, 2 * d * BRANCH:(2 * d + 1) * BRANCH] = hi
        hlf_ref[0, :, (2 * d + 1) * BRANCH:(2 * d + 2) * BRANCH] = (lf - hi.astype(f32)).astype(bf16)

    _round_robin([mla_q(), hgrn_gate(0, "hff"), mla_kv(), hgrn_gate(1, "hfb"), diff_qk("dq", gdq_ref, qd_ref),
                  plain("fnet", fn_ref), diff_qk("dk", gdk_ref, kd_ref), plain("hq", hq_ref), diff_v(),
                  plain("hi", hv_ref), plain("hg", hg_ref)], IN_PROJ_CHAINS_IN_FLIGHT)


def _in_proj(x, p, rope_m, rope_d, tm):
    B, S, _ = x.shape
    row = lambda w: pl.BlockSpec((1, tm, w), lambda b, i: (b, i, 0))
    tab = pl.BlockSpec((3, tm, LANE), lambda b, i: (0, i, 0))
    out_w = (512, 512, 512, 256, 256, 256, 512, 256, 256, 512, 1024, 256)
    consts = (p["ln_mix"], p["wcat"], p["gqa"], p["wuq"], p["gkva"], p["wukv"], p["gqn"], p["gkn"])
    consts2 = (p["gdq"], p["gdk"], p["g32"])
    in_specs = ([row(D_MODEL)] + [_const_spec(c.shape) for c in consts] + [tab]
                + [_const_spec(c.shape) for c in consts2] + [tab, _const_spec(p["lb"].shape)])
    return pl.pallas_call(
        _in_proj_kernel,
        grid=(B, S // tm),
        in_specs=in_specs,
        out_specs=[row(w) for w in out_w],
        out_shape=[jax.ShapeDtypeStruct((B, S, w), bf16) for w in out_w],
        compiler_params=_cparams(("parallel", "parallel")),
        name="in_proj",
    )(x, *consts, rope_m, *consts2, rope_d, p["lb"])


def _online_softmax_step(s, v, m_ref, acc_ref, idx):
    m_prev = m_ref[idx]
    m_new = jnp.maximum(m_prev, jnp.max(s, axis=1, keepdims=True))
    alpha = jnp.exp2(m_prev - m_new)
    p = jnp.exp2((s - m_new[:, :1]).astype(bf16))
    acc_ref[idx] = alpha * acc_ref[idx] + jnp.dot(p, v, preferred_element_type=f32)
    m_ref[idx] = m_new


_NT = (((1,), (1,)), ((), ()))
_TN = (((0,), (0,)), ((), ()))


def _mla_attn_kernel(q_ref, k_ref, v_ref, o_ref, m_ref, acc_ref):
    j = pl.program_id(2)

    @pl.when(j == 0)
    def _():
        m_ref[...] = jnp.full(m_ref.shape, -jnp.inf, f32)
        acc_ref[...] = jnp.zeros(acc_ref.shape, f32)

    for hh in range(MLA_HEADS):
        sl = slice(hh * LANE, (hh + 1) * LANE)
        s = lax.dot_general(q_ref[0, :, sl], k_ref[0, :, sl], _NT, preferred_element_type=f32)
        _online_softmax_step(s, v_ref[0, :, sl], m_ref, acc_ref, hh)

    @pl.when(j == pl.num_programs(2) - 1)
    def _():
        _mla_finalize(acc_ref, o_ref)


def _mla_finalize(acc_ref, o_ref, transposed=False):
    for hh in range(MLA_HEADS):
        acc = acc_ref[hh]
        if transposed:
            o = (acc[:MLA_V] * (1.0 / acc[MLA_V:MLA_V + 1])).T
        else:
            o = acc[:, :MLA_V] * (1.0 / acc[:, MLA_V:MLA_V + 1])
        o_ref[0, :, hh * MLA_V:(hh + 1) * MLA_V] = o.astype(bf16)


def _diff_finalize(lam_ref, gsub_ref, acc_ref, o_ref, transposed=False):
    lam = lam_ref[0]
    dv = 2 * DIFF_HEAD_DIM
    for hh in range(DIFF_HEADS):
        a1 = acc_ref[2 * hh]
        a2 = acc_ref[2 * hh + 1]
        if transposed:
            o = a1[:dv] * (1.0 / a1[dv:dv + 1]) - a2[:dv] * (lam / a2[dv:dv + 1])
            o = (o * lax.rsqrt(jnp.mean(o * o, axis=0, keepdims=True) + EPS)).T
        else:
            o = a1[:, :dv] * (1.0 / a1[:, dv:dv + 1]) - a2[:, :dv] * (lam / a2[:, dv:dv + 1])
            o = o * lax.rsqrt(jnp.mean(o * o, axis=-1, keepdims=True) + EPS)
        o_ref[0, :, hh * dv:(hh + 1) * dv] = (o * gsub_ref[...]).astype(bf16)


def _diff_attn_kernel(lam_ref, q_ref, k_ref, v_ref, gsub_ref, o_ref, m_ref, acc_ref):
    j = pl.program_id(2)

    @pl.when(j == 0)
    def _():
        m_ref[...] = jnp.full(m_ref.shape, -jnp.inf, f32)
        acc_ref[...] = jnp.zeros(acc_ref.shape, f32)

    q = q_ref[0]
    k = k_ref[0]
    group = lax.broadcasted_iota(jnp.int32, (1, BRANCH), 1) // DIFF_HEAD_DIM
    for g in range(2 * DIFF_HEADS):
        qg = jnp.where(group == g, q, jnp.zeros_like(q))
        s = lax.dot_general(qg, k, _NT, preferred_element_type=f32)
        hh = g // 2
        _online_softmax_step(s, v_ref[0, :, hh * LANE:(hh + 1) * LANE], m_ref, acc_ref, g)

    @pl.when(j == pl.num_programs(2) - 1)
    def _():
        _diff_finalize(lam_ref, gsub_ref, acc_ref, o_ref)


def _round_robin(chains, in_flight):
    pending, active = list(chains), []
    while pending or active:
        while pending and len(active) < in_flight:
            active.append(pending.pop(0))
        for c in list(active):
            if next(c, "done") == "done":
                active.remove(c)


def _bounded_attn_loop(maps, k_ref, v_ref, acc_ref, tk):
    acc_ref[...] = jnp.zeros(acc_ref.shape, f32)

    def kv_block(j, carry):
        def chain(g, r0):
            q, ksl, vsl = maps[g]
            keys = pl.ds(pl.multiple_of(j * tk + r0, ATTN_KEY_SUB), ATTN_KEY_SUB)
            st = lax.dot_general(k_ref[0, keys, ksl], q, _NT, preferred_element_type=f32)
            yield
            pt = jnp.exp2(st).astype(bf16)
            yield
            acc_ref[g] += lax.dot_general(v_ref[0, keys, vsl], pt, _TN, preferred_element_type=f32)

        _round_robin([chain(g, r0) for r0 in range(0, tk, ATTN_KEY_SUB) for g in range(len(maps))],
                     ATTN_CHAINS_IN_FLIGHT)
        return carry

    lax.fori_loop(0, k_ref.shape[1] // tk, kv_block, 0)


def _mla_attn_bounded_kernel(q_ref, k_ref, v_ref, o_ref, acc_ref, *, tk):
    maps = [(q_ref[0, :, hh * LANE:(hh + 1) * LANE], slice(hh * LANE, (hh + 1) * LANE),
             slice(hh * LANE, hh * LANE + VT_ROWS)) for hh in range(MLA_HEADS)]
    _bounded_attn_loop(maps, k_ref, v_ref, acc_ref, tk)
    _mla_finalize(acc_ref, o_ref, transposed=True)


def _diff_attn_bounded_kernel(lam_ref, q_ref, k_ref, v_ref, gsub_ref, o_ref, acc_ref, *, tk):
    q = q_ref[0]
    group = lax.broadcasted_iota(jnp.int32, (1, BRANCH), 1) // DIFF_HEAD_DIM
    maps = [(jnp.where(group == g, q, jnp.zeros_like(q)), slice(0, BRANCH),
             slice((g // 2) * LANE, (g // 2) * LANE + VT_ROWS)) for g in range(2 * DIFF_HEADS)]
    _bounded_attn_loop(maps, k_ref, v_ref, acc_ref, tk)
    _diff_finalize(lam_ref, gsub_ref, acc_ref, o_ref, transposed=True)


def _attn_bounded_call(kernel_fn, n_acc, name, args, in_specs, S, W, B, tq):
    return pl.pallas_call(
        kernel_fn,
        grid=(B, S // tq),
        in_specs=in_specs,
        out_specs=pl.BlockSpec((1, tq, BRANCH), lambda b, i: (b, i, 0)),
        out_shape=jax.ShapeDtypeStruct((B, S, BRANCH), bf16),
        scratch_shapes=[pltpu.VMEM((n_acc, VT_ROWS, tq), f32)],
        compiler_params=_cparams(("parallel", "parallel")),
        name=name,
    )(*args)


def _mla_attn_bounded(q, k, v, tq, tk):
    B, S, W = q.shape
    full = pl.BlockSpec((1, S, W), lambda b, i: (b, 0, 0))
    in_specs = [pl.BlockSpec((1, tq, W), lambda b, i: (b, i, 0)), full, full]
    return _attn_bounded_call(functools.partial(_mla_attn_bounded_kernel, tk=tk), MLA_HEADS, "mla_attn_bounded",
                              (q, k, v), in_specs, S, W, B, tq)


def _diff_attn_bounded(lam, q, k, v, gsub, tq, tk):
    B, S, W = v.shape
    in_specs = [pl.BlockSpec(memory_space=pltpu.SMEM),
                pl.BlockSpec((1, tq, BRANCH), lambda b, i: (b, i, 0)),
                pl.BlockSpec((1, S, BRANCH), lambda b, i: (b, 0, 0)),
                pl.BlockSpec((1, S, W), lambda b, i: (b, 0, 0)),
                pl.BlockSpec((1, 2 * DIFF_HEAD_DIM), lambda b, i: (0, 0))]
    return _attn_bounded_call(functools.partial(_diff_attn_bounded_kernel, tk=tk), 2 * DIFF_HEADS,
                              "diff_attn_bounded", (lam, q, k, v, gsub), in_specs, S, W, B, tq)


def _mla_attn(q, k, v, tq, tk):
    B, S, W = q.shape
    return pl.pallas_call(
        _mla_attn_kernel,
        grid=(B, S // tq, S // tk),
        in_specs=[pl.BlockSpec((1, tq, W), lambda b, i, j: (b, i, 0)),
                  pl.BlockSpec((1, tk, W), lambda b, i, j: (b, j, 0)),
                  pl.BlockSpec((1, tk, W), lambda b, i, j: (b, j, 0))],
        out_specs=pl.BlockSpec((1, tq, BRANCH), lambda b, i, j: (b, i, 0)),
        out_shape=jax.ShapeDtypeStruct((B, S, BRANCH), bf16),
        scratch_shapes=[pltpu.VMEM((MLA_HEADS, tq, LANE), f32), pltpu.VMEM((MLA_HEADS, tq, LANE), f32)],
        compiler_params=_cparams(("parallel", "parallel", "arbitrary")),
        name="mla_attn",
    )(q, k, v)


def _diff_attn(lam, q, k, v, gsub, tq, tk):
    B, S, W = v.shape
    return pl.pallas_call(
        _diff_attn_kernel,
        grid=(B, S // tq, S // tk),
        in_specs=[pl.BlockSpec(memory_space=pltpu.SMEM),
                  pl.BlockSpec((1, tq, BRANCH), lambda b, i, j: (b, i, 0)),
                  pl.BlockSpec((1, tk, BRANCH), lambda b, i, j: (b, j, 0)),
                  pl.BlockSpec((1, tk, W), lambda b, i, j: (b, j, 0)),
                  pl.BlockSpec((1, 2 * DIFF_HEAD_DIM), lambda b, i, j: (0, 0))],
        out_specs=pl.BlockSpec((1, tq, BRANCH), lambda b, i, j: (b, i, 0)),
        out_shape=jax.ShapeDtypeStruct((B, S, BRANCH), bf16),
        scratch_shapes=[pltpu.VMEM((2 * DIFF_HEADS, tq, LANE), f32), pltpu.VMEM((2 * DIFF_HEADS, tq, LANE), f32)],
        compiler_params=_cparams(("parallel", "parallel", "arbitrary")),
        name="diff_attn",
    )(lam, q, k, v, gsub)


def _fnet1_kernel(x_ref, w_ref, a_ref):
    a_ref[0] = jnp.dot(w_ref[...], x_ref[0], preferred_element_type=f32).astype(bf16)


def _fnet2_kernel(a_ref, t_ref, cs_ref, o_ref):
    zs = []
    for jj in range(FNET_K1_BLOCK):
        a = jnp.concatenate([a_ref[0, 0, jj], a_ref[0, 1, jj]], axis=0)
        zs.append(jnp.dot(t_ref[jj], a, preferred_element_type=f32))
    zc = jnp.concatenate([jnp.concatenate([z[:FNET_N2], z[FNET_N2:]], axis=1) for z in zs], axis=0).astype(bf16)
    y = jnp.dot(zc, cs_ref[...], preferred_element_type=f32).astype(bf16)
    for jj in range(FNET_K1_BLOCK):
        o_ref[0, :, jj, :] = y[jj * FNET_N2:(jj + 1) * FNET_N2]


def _fnet_consts(S):
    n1 = S // FNET_N2
    a = np.arange(n1)
    ang1 = 2.0 * np.pi * np.outer(a, a) / n1
    w1 = np.concatenate([np.cos(ang1), -np.sin(ang1)], axis=0)
    n2 = np.arange(FNET_N2)
    phi = 2.0 * np.pi * (n2[None, None, :] * a[:, None, None] / S + n2[None, None, :] * n2[None, :, None] / FNET_N2)
    tr, ti = np.cos(phi), -np.sin(phi)
    t = np.concatenate([np.concatenate([tr, -ti], axis=2), np.concatenate([ti, tr], axis=2)], axis=1)
    c = np.arange(BRANCH)
    same = (c[:, None] // FNET_N2) == (c[None, :] // FNET_N2)
    angc = 2.0 * np.pi * np.outer(c % FNET_N2, c % FNET_N2) / FNET_N2
    norm = 1.0 / math.sqrt(S * FNET_N2)
    cs = np.concatenate([np.where(same, np.cos(angc), 0.0), np.where(same, np.sin(angc), 0.0)], axis=0) * norm
    return (jnp.asarray(w1, f32).astype(bf16), jnp.asarray(t, f32).astype(bf16), jnp.asarray(cs, f32).astype(bf16))


def _fnet(u, consts):
    B, S, W = u.shape
    w1, t, cs = consts
    n1 = S // FNET_N2
    cols = FNET_N2 * W
    tn = min(cols, 4096)
    a = pl.pallas_call(
        _fnet1_kernel,
        grid=(B, cols // tn),
        in_specs=[pl.BlockSpec((1, n1, tn), lambda b, i: (b, 0, i)), _const_spec(w1.shape)],
        out_specs=pl.BlockSpec((1, 2 * n1, tn), lambda b, i: (b, 0, i)),
        out_shape=jax.ShapeDtypeStruct((B, 2 * n1, cols), bf16),
        compiler_params=_cparams(("parallel", "parallel")),
        name="fnet1",
    )(u.reshape(B, n1, cols), w1)
    kb = FNET_K1_BLOCK
    y = pl.pallas_call(
        _fnet2_kernel,
        grid=(B, n1 // kb),
        in_specs=[pl.BlockSpec((1, 2, kb, FNET_N2, W), lambda b, i: (b, 0, i, 0, 0)),
                  pl.BlockSpec((kb, 2 * FNET_N2, 2 * FNET_N2), lambda b, i: (i, 0, 0)),
                  _const_spec(cs.shape)],
        out_specs=pl.BlockSpec((1, FNET_N2, kb, W), lambda b, i: (b, 0, i, 0)),
        out_shape=jax.ShapeDtypeStruct((B, FNET_N2, n1, W), bf16),
        compiler_params=_cparams(("parallel", "parallel")),
        name="fnet2",
    )(a.reshape(B, 2, n1, FNET_N2, W), t, cs)
    return y.reshape(B, S, W)


def _hgrn_consts(C):
    L = int(math.log2(C))
    t = np.arange(C)
    m = np.zeros((2, (L + 2) * C, C), np.float32)
    lvl = np.full((2, C, C), -1, np.int32)
    for d in (0, 1):
        for li in range(L):
            n = C >> li
            half = n // 2
            blk = t // n
            mid = blk * n + half
            upper = (t % n) >= half
            for r in range(C):
                if d == 0:
                    if upper[r]:
                        m[d, li * C + r, mid[r]:r + 1] = 1.0
                    else:
                        m[d, li * C + r, r + 1:mid[r]] = 1.0
                else:
                    if upper[r]:
                        m[d, li * C + r, mid[r]:r] = 1.0
                    else:
                        m[d, li * C + r, r:mid[r]] = 1.0
            same = blk[:, None] == blk[None, :]
            if d == 0:
                msk = same & upper[:, None] & (~upper)[None, :]
            else:
                msk = same & (~upper)[:, None] & upper[None, :]
            lvl[d][msk] = li
        for r in range(C):
            if d == 0:
                m[d, L * C + r, :r + 1] = 1.0
                m[d, (L + 1) * C + r, r + 1:] = 1.0
            else:
                m[d, L * C + r, r:] = 1.0
                m[d, (L + 1) * C + r, :r] = 1.0
    c = np.arange(BRANCH)
    headsum = ((c[:, None] // HGRN_DIM) == (c[None, :] // HGRN_DIM)).astype(np.float32)
    lvl = np.tile(lvl, (1, 1, HGRN_HEADS))
    m = np.tile(m, (1, 1, 2))
    return jnp.asarray(m, f32).astype(bf16), jnp.asarray(lvl), jnp.asarray(headsum, f32).astype(bf16)


def _hgrn_kernel(qf_ref, vf_ref, kf_ref, lff_ref, qb_ref, vb_ref, kb_ref, lfb_ref, m_ref, lvl_ref, hs_ref,
                 of_ref, ob_ref, st_ref, *, C, L):
    @pl.when(pl.program_id(1) == 0)
    def _():
        st_ref[...] = jnp.zeros(st_ref.shape, f32)

    refs = ((qf_ref, vf_ref, kf_ref, lff_ref, of_ref), (qb_ref, vb_ref, kb_ref, lfb_ref, ob_ref))
    head = lax.broadcasted_iota(jnp.int32, (1, BRANCH), 1) // HGRN_DIM
    r = lax.broadcasted_iota(jnp.int32, (BRANCH, BRANCH), 0) // HGRN_DIM
    cidx = lax.broadcasted_iota(jnp.int32, (BRANCH, BRANCH), 1) // HGRN_DIM
    n_chunks = qf_ref.shape[1] // C
    order = (tuple(range(n_chunks)), tuple(reversed(range(n_chunks))))
    chains = [(d, order[d][j]) for j in range(n_chunks) for d in (0, 1)]
    rows = {c: slice(c[1] * C, (c[1] + 1) * C) for c in chains}
    q = {c: refs[c[0]][0][0, rows[c], :].astype(f32) for c in chains}
    v = {c: refs[c[0]][1][0, rows[c], :] for c in chains}
    k = {c: refs[c[0]][2][0, rows[c], :].astype(f32) for c in chains}
    e = {}
    for c in chains:
        lf = refs[c[0]][3][0, rows[c], :]
        lf2 = jnp.concatenate([lf[:, :BRANCH], lf[:, BRANCH:]], axis=0)
        e[c] = jnp.dot(m_ref[c[0]], lf2, preferred_element_type=f32)

    o = {}
    st = [st_ref[0], st_ref[1]]
    for c in chains:
        d = c[0]
        ein = jnp.exp(e[c][L * C:(L + 1) * C])
        o[c] = lax.dot_general((q[c] * ein).astype(bf16), st[d].astype(bf16), _NT, preferred_element_type=f32)
        kout = (k[c] * jnp.exp(e[c][(L + 1) * C:(L + 2) * C])).astype(bf16)
        upd = lax.dot_general(v[c], kout, _TN, preferred_element_type=f32)
        total = ein[C - 1:C] if d == 0 else ein[0:1]
        st[d] = st[d] * total + jnp.where(r == cidx, upd, 0.0)
        o[c] = o[c] + jnp.dot((q[c] * k[c]).astype(bf16), hs_ref[...], preferred_element_type=f32) * v[c].astype(f32)
    st_ref[0] = st[0]
    st_ref[1] = st[1]

    scores = {c: jnp.zeros((C, HGRN_HEADS * C), f32) for c in chains}
    for li in range(L):
        for c in chains:
            ex = jnp.exp(e[c][li * C:(li + 1) * C])
            qt = (q[c] * ex).astype(bf16)
            kt = (k[c] * ex).astype(bf16)
            kstack = jnp.concatenate([jnp.where(head == hh, kt, jnp.zeros_like(kt)) for hh in range(HGRN_HEADS)],
                                     axis=0)
            s = lax.dot_general(qt, kstack, _NT, preferred_element_type=f32)
            scores[c] = jnp.where(lvl_ref[c[0]] == li, s, scores[c])
    for c in chains:
        vstack = jnp.concatenate([jnp.where(head == hh, v[c], jnp.zeros_like(v[c])) for hh in range(HGRN_HEADS)],
                                 axis=0)
        oc = o[c] + jnp.dot(scores[c].astype(bf16), vstack, preferred_element_type=f32)
        refs[c[0]][4][0, rows[c], :] = oc.astype(bf16)


def _hgrn(hq, hv, hk, hlf, consts, C):
    B, S, W = hq.shape
    m, lvl, hs = consts
    L = int(math.log2(C))
    rows = min(S, HGRN_CHUNKS_PER_STEP * C)
    nb = S // rows
    fwd = lambda w, c: pl.BlockSpec((1, rows, w), lambda b, i: (b, i, c))
    bwd = lambda w, c: pl.BlockSpec((1, rows, w), lambda b, i: (b, nb - 1 - i, c))
    return pl.pallas_call(
        functools.partial(_hgrn_kernel, C=C, L=L),
        grid=(B, nb),
        in_specs=[fwd(W, 0), fwd(W, 0), fwd(W, 0), fwd(2 * W, 0), bwd(W, 0), bwd(W, 0), bwd(W, 1), bwd(2 * W, 1),
                  _const_spec(m.shape), _const_spec(lvl.shape), _const_spec(hs.shape)],
        out_specs=[fwd(W, 0), bwd(W, 0)],
        out_shape=[jax.ShapeDtypeStruct((B, S, W), bf16)] * 2,
        scratch_shapes=[pltpu.VMEM((2, BRANCH, BRANCH), f32)],
        compiler_params=_cparams(("parallel", "arbitrary")),
        name="hgrn",
    )(hq, hv, hk, hlf, hq, hv, hk, hlf, m, lvl, hs)


def _merge_ffn_kernel(x_ref, om_ref, of_ref, od_ref, ohf_ref, ohb_ref, hg_ref, ln_ref, wg_ref, wbm_ref, wbf_ref,
                      wbd_ref, wbh_ref, wo_ref, gon_ref, g64_ref, lnf_ref, wgu_ref, wd_ref, o_ref):
    x = x_ref[0]
    h = _rms(x, ln_ref[...]).astype(bf16)
    oh = ohf_ref[0].astype(f32) + ohb_ref[0].astype(f32)
    ms = jnp.dot((oh * oh).astype(bf16), g64_ref[...], preferred_element_type=f32)
    oh = (oh * lax.rsqrt(ms + EPS) * gon_ref[...] * _sigmoid(hg_ref[0].astype(f32))).astype(bf16)
    merged = None
    for n, (o_n, w_ref) in enumerate(((om_ref[0], wbm_ref), (of_ref[0], wbf_ref), (od_ref[0], wbd_ref), (oh, wbh_ref))):
        gate = _sigmoid(jnp.dot(h, wg_ref[:, n * D_MODEL:(n + 1) * D_MODEL], preferred_element_type=f32))
        y = gate * jnp.dot(o_n, w_ref[...], preferred_element_type=f32)
        merged = y if merged is None else merged + y
    x = x + jnp.dot(merged.astype(bf16), wo_ref[...], preferred_element_type=f32)

    h = _rms(x, lnf_ref[...]).astype(bf16)
    acc = x
    for c in range(D_FF // FF_CHUNK):
        g = jnp.dot(h, wgu_ref[:, c * FF_CHUNK:(c + 1) * FF_CHUNK], preferred_element_type=f32)
        u = jnp.dot(h, wgu_ref[:, D_FF + c * FF_CHUNK:D_FF + (c + 1) * FF_CHUNK], preferred_element_type=f32)
        a = (g * _sigmoid(g) * u).astype(bf16)
        acc = acc + jnp.dot(a, wd_ref[c], preferred_element_type=f32)
    o_ref[0] = acc


def _merge_ffn(x, om, of, od, oh, hg, p, tm):
    B, S, _ = x.shape
    row = lambda w: pl.BlockSpec((1, tm, w), lambda b, i: (b, i, 0))
    consts = (p["ln_mix"], p["wgate"], p["wbm"], p["wbf"], p["wbd"], p["wbh"], p["wout"], p["gon"], p["g64"],
              p["ln_ffn"], p["wgu"], p["wd"])
    return pl.pallas_call(
        _merge_ffn_kernel,
        grid=(B, S // tm),
        in_specs=[row(D_MODEL)] + [row(BRANCH)] * 6
                 + [_const_spec(c.shape) for c in consts],
        out_specs=row(D_MODEL),
        out_shape=jax.ShapeDtypeStruct((B, S, D_MODEL), f32),
        compiler_params=_cparams(("parallel", "parallel")),
        name="merge_ffn",
    )(x, om, of, od, oh[0], oh[1], hg, *consts)


def _pad_heads(w, heads, width):
    lead = w.shape[:-1]
    w = w.reshape(lead + (heads, width))
    w = jnp.pad(w, [(0, 0)] * len(lead) + [(0, 0), (0, LANE - width)])
    return w.reshape(lead + (heads * LANE,))


def _rope_tables(S):
    def cs(dim):
        inv = 1.0 / (ROPE_THETA ** (jnp.arange(0, dim, 2, dtype=f32) / dim))
        ang = jnp.arange(S, dtype=f32)[:, None] * inv[None, :]
        return jnp.cos(ang), jnp.sin(ang)

    cm, sm = cs(MLA_ROPE)
    one, zero = jnp.ones((S, MLA_NOPE), f32), jnp.zeros((S, MLA_NOPE), f32)
    z16, z32, o32 = jnp.zeros((S, 16), f32), jnp.zeros((S, 32), f32), jnp.ones((S, 32), f32)
    rope_m = jnp.stack([jnp.concatenate([one, cm, cm, o32], 1),
                        jnp.concatenate([zero, -sm, z16, z32], 1),
                        jnp.concatenate([zero, z16, sm, z32], 1)])
    cd, sd = cs(DIFF_HEAD_DIM)
    rope_d = jnp.stack([jnp.tile(jnp.concatenate([cd, cd], 1), (1, 4)),
                        jnp.tile(jnp.concatenate([-sd, z16], 1), (1, 4)),
                        jnp.tile(jnp.concatenate([z16, sd], 1), (1, 4))])
    return rope_m, rope_d


def _block_mean(width, group):
    c = np.arange(width)
    return jnp.asarray(((c[:, None] // group) == (c[None, :] // group)) / group, f32).astype(bf16)


def _layer_params(l, a, lower_bounds):
    w = a["w_in"][l]
    offs = np.cumsum([0, 384, 128, 32, 256, 256, 256, 256, 256, 256, 256, 256, 256, 4096])
    col = lambda i: w[:, offs[i]:offs[i + 1]]
    wcat = jnp.concatenate([col(0), col(1), jnp.pad(col(2), ((0, 0), (MLA_NOPE, LANE - MLA_QK))), col(3), col(4),
                            col(5), _pad_heads(col(6), DIFF_HEADS, 2 * DIFF_HEAD_DIM), col(7), col(8), col(9),
                            col(10), col(11)], axis=1).astype(bf16)
    wukv = a["mla_w_ukv"][l].reshape(MLA_KV_LORA, MLA_HEADS, MLA_NOPE + MLA_V)
    wukv = jnp.concatenate([_pad_heads(wukv[:, :, :MLA_NOPE].reshape(MLA_KV_LORA, -1), MLA_HEADS, MLA_NOPE),
                            _pad_heads(wukv[:, :, MLA_NOPE:].reshape(MLA_KV_LORA, -1), MLA_HEADS, MLA_V)], axis=1)
    lam_init = 0.8 - 0.6 * math.exp(-0.3 * l)
    lam = (jnp.exp(jnp.sum(a["diff_lq1"][l] * a["diff_lk1"][l])) - jnp.exp(jnp.sum(a["diff_lq2"][l] * a["diff_lk2"][l]))
           + lam_init)
    wb = a["w_branch"][l]
    nc = D_FF // FF_CHUNK
    row = lambda v: v.reshape(1, -1).astype(f32)
    return {
        "ln_mix": row(a["ln_mix"][l]),
        "wcat": wcat,
        "gqa": row(a["mla_g_qa"][l]),
        "wuq": _pad_heads(a["mla_w_uq"][l], MLA_HEADS, MLA_QK).astype(bf16),
        "gkva": row(a["mla_g_kva"][l]),
        "wukv": wukv.astype(bf16),
        "gqn": row(jnp.pad(a["mla_g_qn"][l], (0, LANE - MLA_QK))) * (MLA_QK ** -0.5 * LOG2E),
        "gkn": row(jnp.pad(a["mla_g_kn"][l], (0, LANE - MLA_QK))),
        "gdq": row(jnp.tile(a["diff_g_qn"][l], 2 * DIFF_HEADS)) * (DIFF_HEAD_DIM ** -0.5 * LOG2E),
        "gdk": row(jnp.tile(a["diff_g_kn"][l], 2 * DIFF_HEADS)),
        "g32": _block_mean(BRANCH, DIFF_HEAD_DIM),
        "lb": lower_bounds[:, l].astype(f32),
        "lam": lam.reshape(1).astype(f32),
        "mla_bound": MLA_QK ** 0.5 * LOG2E * jnp.max(jnp.abs(a["mla_g_qn"][l])) * jnp.max(jnp.abs(a["mla_g_kn"][l])),
        "diff_bound": (DIFF_HEAD_DIM ** 0.5 * LOG2E * jnp.max(jnp.abs(a["diff_g_qn"][l]))
                       * jnp.max(jnp.abs(a["diff_g_kn"][l]))),
        "gsub": row(a["diff_g_sub"][l]) * (1.0 - lam_init),
        "wgate": col(12).astype(bf16),
        "wbm": wb[0].astype(bf16),
        "wbf": wb[1].astype(bf16),
        "wbd": wb[2].astype(bf16),
        "wbh": wb[3].astype(bf16),
        "wout": a["w_out"][l].astype(bf16),
        "gon": row(jnp.tile(a["hgrn_g_on"][l], HGRN_HEADS)),
        "g64": _block_mean(BRANCH, HGRN_DIM),
        "ln_ffn": row(a["ln_ffn"][l]),
        "wgu": a["w_gate_up"][l].astype(bf16),
        "wd": a["w_down"][l].reshape(nc, FF_CHUNK, D_MODEL).astype(bf16),
    }


def _tiles(S):
    return {"tm_in": min(256, S),"tq": min(512, S), "tk": min(512, S), "tqb": min(512, S), "tkb": min(8192, S),
            "hgrn_chunk": min(64, S),
            "tm_merge": min(512, S)}


def kernel(x, ln_mix, w_in, mla_g_qa, mla_w_uq, mla_g_kva, mla_w_ukv, mla_g_qn, mla_g_kn, diff_g_qn, diff_g_kn,
           diff_lq1, diff_lk1, diff_lq2, diff_lk2, diff_g_sub, hgrn_lb_logits, hgrn_g_on, w_branch, w_out, ln_ffn,
           w_gate_up, w_down):
    a = dict(ln_mix=ln_mix, w_in=w_in, mla_g_qa=mla_g_qa, mla_w_uq=mla_w_uq, mla_g_kva=mla_g_kva,
             mla_w_ukv=mla_w_ukv, mla_g_qn=mla_g_qn, mla_g_kn=mla_g_kn, diff_g_qn=diff_g_qn, diff_g_kn=diff_g_kn,
             diff_lq1=diff_lq1, diff_lk1=diff_lk1, diff_lq2=diff_lq2, diff_lk2=diff_lk2, diff_g_sub=diff_g_sub,
             hgrn_g_on=hgrn_g_on, w_branch=w_branch, w_out=w_out, ln_ffn=ln_ffn, w_gate_up=w_gate_up, w_down=w_down)
    S = x.shape[1]
    t = _tiles(S)
    rope_m, rope_d = _rope_tables(S)
    lb_p = jax.nn.softmax(hgrn_lb_logits.astype(f32), axis=1)
    lower_bounds = jnp.cumsum(lb_p, axis=1) - lb_p[:, :1]
    fnet_consts = _fnet_consts(S)
    hgrn_consts = _hgrn_consts(t["hgrn_chunk"])
    for l in range(DEPTH):
        p = _layer_params(l, a, lower_bounds)
        qm, km, vm, fn, qd, kd, vd, hq, hv, hk, hlf, hg = _in_proj(x, p, rope_m, rope_d, t["tm_in"])
        om = lax.cond(p["mla_bound"] <= SCORE_BOUND_LOG2,
                      lambda: _mla_attn_bounded(qm, km, vm, t["tqb"], t["tkb"]),
                      lambda: _mla_attn(qm, km, vm, t["tq"], t["tk"]))
        od = lax.cond(p["diff_bound"] <= SCORE_BOUND_LOG2,
                      lambda: _diff_attn_bounded(p["lam"], qd, kd, vd, p["gsub"], t["tqb"], t["tkb"]),
                      lambda: _diff_attn(p["lam"], qd, kd, vd, p["gsub"], t["tq"], t["tk"]))
        of = _fnet(fn, fnet_consts)
        oh = _hgrn(hq, hv, hk, hlf, hgrn_consts, t["hgrn_chunk"])
        x = _merge_ffn(x, om, of, od, oh, hg, p, t["tm_merge"])
    return x
```

```python
import functools
import math

import numpy as np
import jax
import jax.numpy as jnp
from jax import lax
from jax.experimental import pallas as pl
from jax.experimental.pallas import tpu as pltpu

f32 = jnp.float32
bf16 = jnp.bfloat16

D_MODEL = 1024
DEPTH = 2
MLA_HEADS = 4
MLA_Q_LORA = 384
MLA_KV_LORA = 128
MLA_NOPE = 64
MLA_ROPE = 32
MLA_V = 64
MLA_QK = MLA_NOPE + MLA_ROPE
DIFF_HEADS = 4
DIFF_HEAD_DIM = 32
HGRN_HEADS = 4
HGRN_DIM = 64
BRANCH = 256
D_FF = 2816
FF_CHUNK = 256
ROPE_THETA = 10000.0
EPS = 1e-6
LOG2E = 1.4426950408889634
LANE = 128
FNET_N2 = 64
FNET_K1_BLOCK = 8
HGRN_CHUNKS_PER_STEP = 8
IN_PROJ_CHAINS_IN_FLIGHT = 4
ATTN_KEY_SUB = 256
VT_ROWS = LANE
ATTN_CHAINS_IN_FLIGHT = 32
VMEM_LIMIT = 56 * 1024 * 1024
SCORE_BOUND_LOG2 = 60.0

_SEG = {}
_off = 0
for _name, _w in (("cq", 384), ("ckv", 128), ("krope", 128), ("fnet", 256), ("dq", 256), ("dk", 256),
                  ("dv", 512), ("hq", 256), ("hi", 256), ("hff", 256), ("hfb", 256), ("hg", 256)):
    _SEG[_name] = (_off, _off + _w)
    _off += _w
W_CAT = _off


def _cparams(sem):
    return pltpu.CompilerParams(dimension_semantics=sem, vmem_limit_bytes=VMEM_LIMIT)


def _const_spec(shape):
    nd = len(shape)
    return pl.BlockSpec(shape, lambda *_: (0,) * nd, pipeline_mode=pl.Buffered(1))


def _rms(x, g):
    return x * lax.rsqrt(jnp.mean(x * x, axis=-1, keepdims=True) + EPS) * g


def _sigmoid(x):
    return 1.0 / (1.0 + jnp.exp(-x))


def _rope(t, tab_ref):
    return t * tab_ref[0] + pltpu.roll(t, LANE - 16, 1) * tab_ref[1] + pltpu.roll(t, 16, 1) * tab_ref[2]


def _in_proj_kernel(x_ref, ln_ref, wcat_ref, gqa_ref, wuq_ref, gkva_ref, wukv_ref, gqn_ref, gkn_ref,
                    ropem_ref, gdq_ref, gdk_ref, g32_ref, roped_ref, lb_ref,
                    qm_ref, km_ref, vm_ref, fn_ref, qd_ref, kd_ref, vd_ref, hq_ref, hv_ref, hk_ref,
                    hlf_ref, hg_ref):
    h = _rms(x_ref[0], ln_ref[...]).astype(bf16)

    def seg(name):
        a, b = _SEG[name]
        return jnp.dot(h, wcat_ref[:, a:b], preferred_element_type=f32)

    ones_lane = (lax.broadcasted_iota(jnp.int32, (1, LANE), 1) == MLA_V).astype(f32)


    def mla_q():
        cq = seg("cq")
        yield
        q = jnp.dot(_rms(cq, gqa_ref[...]).astype(bf16), wuq_ref[...], preferred_element_type=f32)
        yield
        for hh in range(MLA_HEADS):
            sl = slice(hh * LANE, (hh + 1) * LANE)
            qh = q[:, sl]
            ms = jnp.sum(qh * qh, axis=-1, keepdims=True) * (1.0 / MLA_QK)
            qm_ref[0, :, sl] = _rope(qh * lax.rsqrt(ms + EPS) * gqn_ref[...], ropem_ref).astype(bf16)

    def mla_kv():
        ckv = seg("ckv")
        krope = seg("krope")
        yield
        kv = jnp.dot(_rms(ckv, gkva_ref[...]).astype(bf16), wukv_ref[...], preferred_element_type=f32)
        yield
        for hh in range(MLA_HEADS):
            sl = slice(hh * LANE, (hh + 1) * LANE)
            kh = kv[:, sl] + krope
            ms = jnp.sum(kh * kh, axis=-1, keepdims=True) * (1.0 / MLA_QK)
            km_ref[0, :, sl] = _rope(kh * lax.rsqrt(ms + EPS) * gkn_ref[...], ropem_ref).astype(bf16)
            vsl = slice((MLA_HEADS + hh) * LANE, (MLA_HEADS + hh + 1) * LANE)
            vm_ref[0, :, sl] = (kv[:, vsl] + ones_lane).astype(bf16)

    def diff_qk(name, g_ref, o_ref):
        t = seg(name)
        yield
        ms = jnp.dot((t * t).astype(bf16), g32_ref[...], preferred_element_type=f32)
        yield
        t = t * lax.rsqrt(ms + EPS) * g_ref[...]
        for c in range(BRANCH // LANE):
            sl = slice(c * LANE, (c + 1) * LANE)
            o_ref[0, :, sl] = _rope(t[:, sl], roped_ref).astype(bf16)

    def diff_v():
        dv = seg("dv")
        yield
        for hh in range(DIFF_HEADS):
            sl = slice(hh * LANE, (hh + 1) * LANE)
            vd_ref[0, :, sl] = (dv[:, sl] + ones_lane).astype(bf16)

    def plain(name, o_ref):
        z = seg(name)
        yield
        o_ref[0] = z.astype(bf16)

    def hgrn_gate(d, name):
        z = seg(name)
        yield
        lb = lb_ref[d:d + 1, :]
        f = lb + (1.0 - lb) * _sigmoid(z)
        hk_ref[0, :, d * BRANCH:(d + 1) * BRANCH] = (1.0 - f).astype(bf16)
        lf = jnp.log(f)
        hi = lf.astype(bf16)
        hlf_ref[0, :, 2 * d * BRANCH:(2 * d + 1) * BRANCH] = hi
        hlf_ref[0, :, (2 * d + 1) * BRANCH:(2 * d + 2) * BRANCH] = (lf - hi.astype(f32)).astype(bf16)

    _round_robin([mla_q(), hgrn_gate(0, "hff"), mla_kv(), hgrn_gate(1, "hfb"), diff_qk("dq", gdq_ref, qd_ref),
                  plain("fnet", fn_ref), diff_qk("dk", gdk_ref, kd_ref), plain("hq", hq_ref), diff_v(),
                  plain("hi", hv_ref), plain("hg", hg_ref)], IN_PROJ_CHAINS_IN_FLIGHT)


def _in_proj(x, p, rope_m, rope_d, tm):
    B, S, _ = x.shape
    row = lambda w: pl.BlockSpec((1, tm, w), lambda b, i: (b, i, 0))
    tab = pl.BlockSpec((3, tm, LANE), lambda b, i: (0, i, 0))
    out_w = (512, 512, 512, 256, 256, 256, 512, 256, 256, 512, 1024, 256)
    consts = (p["ln_mix"], p["wcat"], p["gqa"], p["wuq"], p["gkva"], p["wukv"], p["gqn"], p["gkn"])
    consts2 = (p["gdq"], p["gdk"], p["g32"])
    in_specs = ([row(D_MODEL)] + [_const_spec(c.shape) for c in consts] + [tab]
                + [_const_spec(c.shape) for c in consts2] + [tab, _const_spec(p["lb"].shape)])
    return pl.pallas_call(
        _in_proj_kernel,
        grid=(B, S // tm),
        in_specs=in_specs,
        out_specs=[row(w) for w in out_w],
        out_shape=[jax.ShapeDtypeStruct((B, S, w), bf16) for w in out_w],
        compiler_params=_cparams(("parallel", "parallel")),
        name="in_proj",
    )(x, *consts, rope_m, *consts2, rope_d, p["lb"])


def _online_softmax_step(s, v, m_ref, acc_ref, idx):
    m_prev = m_ref[idx]
    m_new = jnp.maximum(m_prev, jnp.max(s, axis=1, keepdims=True))
    alpha = jnp.exp2(m_prev - m_new)
    p = jnp.exp2((s - m_new[:, :1]).astype(bf16))
    acc_ref[idx] = alpha * acc_ref[idx] + jnp.dot(p, v, preferred_element_type=f32)
    m_ref[idx] = m_new


_NT = (((1,), (1,)), ((), ()))
_TN = (((0,), (0,)), ((), ()))


def _mla_attn_kernel(q_ref, k_ref, v_ref, o_ref, m_ref, acc_ref):
    j = pl.program_id(2)

    @pl.when(j == 0)
    def _():
        m_ref[...] = jnp.full(m_ref.shape, -jnp.inf, f32)
        acc_ref[...] = jnp.zeros(acc_ref.shape, f32)

    for hh in range(MLA_HEADS):
        sl = slice(hh * LANE, (hh + 1) * LANE)
        s = lax.dot_general(q_ref[0, :, sl], k_ref[0, :, sl], _NT, preferred_element_type=f32)
        _online_softmax_step(s, v_ref[0, :, sl], m_ref, acc_ref, hh)

    @pl.when(j == pl.num_programs(2) - 1)
    def _():
        _mla_finalize(acc_ref, o_ref)


def _mla_finalize(acc_ref, o_ref, transposed=False):
    for hh in range(MLA_HEADS):
        acc = acc_ref[hh]
        if transposed:
            o = (acc[:MLA_V] * (1.0 / acc[MLA_V:MLA_V + 1])).T
        else:
            o = acc[:, :MLA_V] * (1.0 / acc[:, MLA_V:MLA_V + 1])
        o_ref[0, :, hh * MLA_V:(hh + 1) * MLA_V] = o.astype(bf16)


def _diff_finalize(lam_ref, gsub_ref, acc_ref, o_ref, transposed=False):
    lam = lam_ref[0]
    dv = 2 * DIFF_HEAD_DIM
    for hh in range(DIFF_HEADS):
        a1 = acc_ref[2 * hh]
        a2 = acc_ref[2 * hh + 1]
        if transposed:
            o = a1[:dv] * (1.0 / a1[dv:dv + 1]) - a2[:dv] * (lam / a2[dv:dv + 1])
            o = (o * lax.rsqrt(jnp.mean(o * o, axis=0, keepdims=True) + EPS)).T
        else:
            o = a1[:, :dv] * (1.0 / a1[:, dv:dv + 1]) - a2[:, :dv] * (lam / a2[:, dv:dv + 1])
            o = o * lax.rsqrt(jnp.mean(o * o, axis=-1, keepdims=True) + EPS)
        o_ref[0, :, hh * dv:(hh + 1) * dv] = (o * gsub_ref[...]).astype(bf16)


def _diff_attn_kernel(lam_ref, q_ref, k_ref, v_ref, gsub_ref, o_ref, m_ref, acc_ref):
    j = pl.program_id(2)

    @pl.when(j == 0)
    def _():
        m_ref[...] = jnp.full(m_ref.shape, -jnp.inf, f32)
        acc_ref[...] = jnp.zeros(acc_ref.shape, f32)

    q = q_ref[0]
    k = k_ref[0]
    group = lax.broadcasted_iota(jnp.int32, (1, BRANCH), 1) // DIFF_HEAD_DIM
    for g in range(2 * DIFF_HEADS):
        qg = jnp.where(group == g, q, jnp.zeros_like(q))
        s = lax.dot_general(qg, k, _NT, preferred_element_type=f32)
        hh = g // 2
        _online_softmax_step(s, v_ref[0, :, hh * LANE:(hh + 1) * LANE], m_ref, acc_ref, g)

    @pl.when(j == pl.num_programs(2) - 1)
    def _():
        _diff_finalize(lam_ref, gsub_ref, acc_ref, o_ref)


def _round_robin(chains, in_flight):
    pending, active = list(chains), []
    while pending or active:
        while pending and len(active) < in_flight:
            active.append(pending.pop(0))
        for c in list(active):
            if next(c, "done") == "done":
                active.remove(c)


def _bounded_attn_loop(maps, k_ref, v_ref, acc_ref, tk):
    acc_ref[...] = jnp.zeros(acc_ref.shape, f32)

    def kv_block(j, carry):
        def chain(g, r0):
            q, ksl, vsl = maps[g]
            keys = pl.ds(pl.multiple_of(j * tk + r0, ATTN_KEY_SUB), ATTN_KEY_SUB)
            st = lax.dot_general(k_ref[0, keys, ksl], q, _NT, preferred_element_type=f32)
            yield
            pt = jnp.exp2(st).astype(bf16)
            yield
            acc_ref[g] += lax.dot_general(v_ref[0, keys, vsl], pt, _TN, preferred_element_type=f32)

        _round_robin([chain(g, r0) for r0 in range(0, tk, ATTN_KEY_SUB) for g in range(len(maps))],
                     ATTN_CHAINS_IN_FLIGHT)
        return carry

    lax.fori_loop(0, k_ref.shape[1] // tk, kv_block, 0)


def _mla_attn_bounded_kernel(q_ref, k_ref, v_ref, o_ref, acc_ref, *, tk):
    maps = [(q_ref[0, :, hh * LANE:(hh + 1) * LANE], slice(hh * LANE, (hh + 1) * LANE),
             slice(hh * LANE, hh * LANE + VT_ROWS)) for hh in range(MLA_HEADS)]
    _bounded_attn_loop(maps, k_ref, v_ref, acc_ref, tk)
    _mla_finalize(acc_ref, o_ref, transposed=True)


def _diff_attn_bounded_kernel(lam_ref, q_ref, k_ref, v_ref, gsub_ref, o_ref, acc_ref, *, tk):
    q = q_ref[0]
    group = lax.broadcasted_iota(jnp.int32, (1, BRANCH), 1) // DIFF_HEAD_DIM
    maps = [(jnp.where(group == g, q, jnp.zeros_like(q)), slice(0, BRANCH),
             slice((g // 2) * LANE, (g // 2) * LANE + VT_ROWS)) for g in range(2 * DIFF_HEADS)]
    _bounded_attn_loop(maps, k_ref, v_ref, acc_ref, tk)
    _diff_finalize(lam_ref, gsub_ref, acc_ref, o_ref, transposed=True)


def _attn_bounded_call(kernel_fn, n_acc, name, args, in_specs, S, W, B, tq):
    return pl.pallas_call(
        kernel_fn,
        grid=(B, S // tq),
        in_specs=in_specs,
        out_specs=pl.BlockSpec((1, tq, BRANCH), lambda b, i: (b, i, 0)),
        out_shape=jax.ShapeDtypeStruct((B, S, BRANCH), bf16),
        scratch_shapes=[pltpu.VMEM((n_acc, VT_ROWS, tq), f32)],
        compiler_params=_cparams(("parallel", "parallel")),
        name=name,
    )(*args)


def _mla_attn_bounded(q, k, v, tq, tk):
    B, S, W = q.shape
    full = pl.BlockSpec((1, S, W), lambda b, i: (b, 0, 0))
    in_specs = [pl.BlockSpec((1, tq, W), lambda b, i: (b, i, 0)), full, full]
    return _attn_bounded_call(functools.partial(_mla_attn_bounded_kernel, tk=tk), MLA_HEADS, "mla_attn_bounded",
                              (q, k, v), in_specs, S, W, B, tq)


def _diff_attn_bounded(lam, q, k, v, gsub, tq, tk):
    B, S, W = v.shape
    in_specs = [pl.BlockSpec(memory_space=pltpu.SMEM),
                pl.BlockSpec((1, tq, BRANCH), lambda b, i: (b, i, 0)),
                pl.BlockSpec((1, S, BRANCH), lambda b, i: (b, 0, 0)),
                pl.BlockSpec((1, S, W), lambda b, i: (b, 0, 0)),
                pl.BlockSpec((1, 2 * DIFF_HEAD_DIM), lambda b, i: (0, 0))]
    return _attn_bounded_call(functools.partial(_diff_attn_bounded_kernel, tk=tk), 2 * DIFF_HEADS,
                              "diff_attn_bounded", (lam, q, k, v, gsub), in_specs, S, W, B, tq)


def _mla_attn(q, k, v, tq, tk):
    B, S, W = q.shape
    return pl.pallas_call(
        _mla_attn_kernel,
        grid=(B, S // tq, S // tk),
        in_specs=[pl.BlockSpec((1, tq, W), lambda b, i, j: (b, i, 0)),
                  pl.BlockSpec((1, tk, W), lambda b, i, j: (b, j, 0)),
                  pl.BlockSpec((1, tk, W), lambda b, i, j: (b, j, 0))],
        out_specs=pl.BlockSpec((1, tq, BRANCH), lambda b, i, j: (b, i, 0)),
        out_shape=jax.ShapeDtypeStruct((B, S, BRANCH), bf16),
        scratch_shapes=[pltpu.VMEM((MLA_HEADS, tq, LANE), f32), pltpu.VMEM((MLA_HEADS, tq, LANE), f32)],
        compiler_params=_cparams(("parallel", "parallel", "arbitrary")),
        name="mla_attn",
    )(q, k, v)


def _diff_attn(lam, q, k, v, gsub, tq, tk):
    B, S, W = v.shape
    return pl.pallas_call(
        _diff_attn_kernel,
        grid=(B, S // tq, S // tk),
        in_specs=[pl.BlockSpec(memory_space=pltpu.SMEM),
                  pl.BlockSpec((1, tq, BRANCH), lambda b, i, j: (b, i, 0)),
                  pl.BlockSpec((1, tk, BRANCH), lambda b, i, j: (b, j, 0)),
                  pl.BlockSpec((1, tk, W), lambda b, i, j: (b, j, 0)),
                  pl.BlockSpec((1, 2 * DIFF_HEAD_DIM), lambda b, i, j: (0, 0))],
        out_specs=pl.BlockSpec((1, tq, BRANCH), lambda b, i, j: (b, i, 0)),
        out_shape=jax.ShapeDtypeStruct((B, S, BRANCH), bf16),
        scratch_shapes=[pltpu.VMEM((2 * DIFF_HEADS, tq, LANE), f32), pltpu.VMEM((2 * DIFF_HEADS, tq, LANE), f32)],
        compiler_params=_cparams(("parallel", "parallel", "arbitrary")),
        name="diff_attn",
    )(lam, q, k, v, gsub)


def _fnet1_kernel(x_ref, w_ref, a_ref):
    a_ref[0] = jnp.dot(w_ref[...], x_ref[0], preferred_element_type=f32).astype(bf16)


def _fnet2_kernel(a_ref, t_ref, cs_ref, o_ref):
    zs = []
    for jj in range(FNET_K1_BLOCK):
        a = jnp.concatenate([a_ref[0, 0, jj], a_ref[0, 1, jj]], axis=0)
        zs.append(jnp.dot(t_ref[jj], a, preferred_element_type=f32))
    zc = jnp.concatenate([jnp.concatenate([z[:FNET_N2], z[FNET_N2:]], axis=1) for z in zs], axis=0).astype(bf16)
    y = jnp.dot(zc, cs_ref[...], preferred_element_type=f32).astype(bf16)
    for jj in range(FNET_K1_BLOCK):
        o_ref[0, :, jj, :] = y[jj * FNET_N2:(jj + 1) * FNET_N2]


def _fnet_consts(S):
    n1 = S // FNET_N2
    a = np.arange(n1)
    ang1 = 2.0 * np.pi * np.outer(a, a) / n1
    w1 = np.concatenate([np.cos(ang1), -np.sin(ang1)], axis=0)
    n2 = np.arange(FNET_N2)
    phi = 2.0 * np.pi * (n2[None, None, :] * a[:, None, None] / S + n2[None, None, :] * n2[None, :, None] / FNET_N2)
    tr, ti = np.cos(phi), -np.sin(phi)
    t = np.concatenate([np.concatenate([tr, -ti], axis=2), np.concatenate([ti, tr], axis=2)], axis=1)
    c = np.arange(BRANCH)
    same = (c[:, None] // FNET_N2) == (c[None, :] // FNET_N2)
    angc = 2.0 * np.pi * np.outer(c % FNET_N2, c % FNET_N2) / FNET_N2
    norm = 1.0 / math.sqrt(S * FNET_N2)
    cs = np.concatenate([np.where(same, np.cos(angc), 0.0), np.where(same, np.sin(angc), 0.0)], axis=0) * norm
    return (jnp.asarray(w1, f32).astype(bf16), jnp.asarray(t, f32).astype(bf16), jnp.asarray(cs, f32).astype(bf16))


def _fnet(u, consts):
    B, S, W = u.shape
    w1, t, cs = consts
    n1 = S // FNET_N2
    cols = FNET_N2 * W
    tn = min(cols, 4096)
    a = pl.pallas_call(
        _fnet1_kernel,
        grid=(B, cols // tn),
        in_specs=[pl.BlockSpec((1, n1, tn), lambda b, i: (b, 0, i)), _const_spec(w1.shape)],
        out_specs=pl.BlockSpec((1, 2 * n1, tn), lambda b, i: (b, 0, i)),
        out_shape=jax.ShapeDtypeStruct((B, 2 * n1, cols), bf16),
        compiler_params=_cparams(("parallel", "parallel")),
        name="fnet1",
    )(u.reshape(B, n1, cols), w1)
    kb = FNET_K1_BLOCK
    y = pl.pallas_call(
        _fnet2_kernel,
        grid=(B, n1 // kb),
        in_specs=[pl.BlockSpec((1, 2, kb, FNET_N2, W), lambda b, i: (b, 0, i, 0, 0)),
                  pl.BlockSpec((kb, 2 * FNET_N2, 2 * FNET_N2), lambda b, i: (i, 0, 0)),
                  _const_spec(cs.shape)],
        out_specs=pl.BlockSpec((1, FNET_N2, kb, W), lambda b, i: (b, 0, i, 0)),
        out_shape=jax.ShapeDtypeStruct((B, FNET_N2, n1, W), bf16),
        compiler_params=_cparams(("parallel", "parallel")),
        name="fnet2",
    )(a.reshape(B, 2, n1, FNET_N2, W), t, cs)
    return y.reshape(B, S, W)


def _hgrn_consts(C):
    L = int(math.log2(C))
    t = np.arange(C)
    m = np.zeros((2, (L + 2) * C, C), np.float32)
    lvl = np.full((2, C, C), -1, np.int32)
    for d in (0, 1):
        for li in range(L):
            n = C >> li
            half = n // 2
            blk = t // n
            mid = blk * n + half
            upper = (t % n) >= half
            for r in range(C):
                if d == 0:
                    if upper[r]:
                        m[d, li * C + r, mid[r]:r + 1] = 1.0
                    else:
                        m[d, li * C + r, r + 1:mid[r]] = 1.0
                else:
                    if upper[r]:
                        m[d, li * C + r, mid[r]:r] = 1.0
                    else:
                        m[d, li * C + r, r:mid[r]] = 1.0
            same = blk[:, None] == blk[None, :]
            if d == 0:
                msk = same & upper[:, None] & (~upper)[None, :]
            else:
                msk = same & (~upper)[:, None] & upper[None, :]
            lvl[d][msk] = li
        for r in range(C):
            if d == 0:
                m[d, L * C + r, :r + 1] = 1.0
                m[d, (L + 1) * C + r, r + 1:] = 1.0
            else:
                m[d, L * C + r, r:] = 1.0
                m[d, (L + 1) * C + r, :r] = 1.0
    c = np.arange(BRANCH)
    headsum = ((c[:, None] // HGRN_DIM) == (c[None, :] // HGRN_DIM)).astype(np.float32)
    lvl = np.tile(lvl, (1, 1, HGRN_HEADS))
    m = np.tile(m, (1, 1, 2))
    return jnp.asarray(m, f32).astype(bf16), jnp.asarray(lvl), jnp.asarray(headsum, f32).astype(bf16)


def _hgrn_kernel(qf_ref, vf_ref, kf_ref, lff_ref, qb_ref, vb_ref, kb_ref, lfb_ref, m_ref, lvl_ref, hs_ref,
                 of_ref, ob_ref, st_ref, *, C, L):
    @pl.when(pl.program_id(1) == 0)
    def _():
        st_ref[...] = jnp.zeros(st_ref.shape, f32)

    refs = ((qf_ref, vf_ref, kf_ref, lff_ref, of_ref), (qb_ref, vb_ref, kb_ref, lfb_ref, ob_ref))
    head = lax.broadcasted_iota(jnp.int32, (1, BRANCH), 1) // HGRN_DIM
    r = lax.broadcasted_iota(jnp.int32, (BRANCH, BRANCH), 0) // HGRN_DIM
    cidx = lax.broadcasted_iota(jnp.int32, (BRANCH, BRANCH), 1) // HGRN_DIM
    n_chunks = qf_ref.shape[1] // C
    order = (tuple(range(n_chunks)), tuple(reversed(range(n_chunks))))
    chains = [(d, order[d][j]) for j in range(n_chunks) for d in (0, 1)]
    rows = {c: slice(c[1] * C, (c[1] + 1) * C) for c in chains}
    q = {c: refs[c[0]][0][0, rows[c], :].astype(f32) for c in chains}
    v = {c: refs[c[0]][1][0, rows[c], :] for c in chains}
    k = {c: refs[c[0]][2][0, rows[c], :].astype(f32) for c in chains}
    e = {}
    for c in chains:
        lf = refs[c[0]][3][0, rows[c], :]
        lf2 = jnp.concatenate([lf[:, :BRANCH], lf[:, BRANCH:]], axis=0)
        e[c] = jnp.dot(m_ref[c[0]], lf2, preferred_element_type=f32)

    o = {}
    st = [st_ref[0], st_ref[1]]
    for c in chains:
        d = c[0]
        ein = jnp.exp(e[c][L * C:(L + 1) * C])
        o[c] = lax.dot_general((q[c] * ein).astype(bf16), st[d].astype(bf16), _NT, preferred_element_type=f32)
        kout = (k[c] * jnp.exp(e[c][(L + 1) * C:(L + 2) * C])).astype(bf16)
        upd = lax.dot_general(v[c], kout, _TN, preferred_element_type=f32)
        total = ein[C - 1:C] if d == 0 else ein[0:1]
        st[d] = st[d] * total + jnp.where(r == cidx, upd, 0.0)
        o[c] = o[c] + jnp.dot((q[c] * k[c]).astype(bf16), hs_ref[...], preferred_element_type=f32) * v[c].astype(f32)
    st_ref[0] = st[0]
    st_ref[1] = st[1]

    scores = {c: jnp.zeros((C, HGRN_HEADS * C), f32) for c in chains}
    for li in range(L):
        for c in chains:
            ex = jnp.exp(e[c][li * C:(li + 1) * C])
            qt = (q[c] * ex).astype(bf16)
            kt = (k[c] * ex).astype(bf16)
            kstack = jnp.concatenate([jnp.where(head == hh, kt, jnp.zeros_like(kt)) for hh in range(HGRN_HEADS)],
                                     axis=0)
            s = lax.dot_general(qt, kstack, _NT, preferred_element_type=f32)
            scores[c] = jnp.where(lvl_ref[c[0]] == li, s, scores[c])
    for c in chains:
        vstack = jnp.concatenate([jnp.where(head == hh, v[c], jnp.zeros_like(v[c])) for hh in range(HGRN_HEADS)],
                                 axis=0)
        oc = o[c] + jnp.dot(scores[c].astype(bf16), vstack, preferred_element_type=f32)
        refs[c[0]][4][0, rows[c], :] = oc.astype(bf16)


def _hgrn(hq, hv, hk, hlf, consts, C):
    B, S, W = hq.shape
    m, lvl, hs = consts
    L = int(math.log2(C))
    rows = min(S, HGRN_CHUNKS_PER_STEP * C)
    nb = S // rows
    fwd = lambda w, c: pl.BlockSpec((1, rows, w), lambda b, i: (b, i, c))
    bwd = lambda w, c: pl.BlockSpec((1, rows, w), lambda b, i: (b, nb - 1 - i, c))
    return pl.pallas_call(
        functools.partial(_hgrn_kernel, C=C, L=L),
        grid=(B, nb),
        in_specs=[fwd(W, 0), fwd(W, 0), fwd(W, 0), fwd(2 * W, 0), bwd(W, 0), bwd(W, 0), bwd(W, 1), bwd(2 * W, 1),
                  _const_spec(m.shape), _const_spec(lvl.shape), _const_spec(hs.shape)],
        out_specs=[fwd(W, 0), bwd(W, 0)],
        out_shape=[jax.ShapeDtypeStruct((B, S, W), bf16)] * 2,
        scratch_shapes=[pltpu.VMEM((2, BRANCH, BRANCH), f32)],
        compiler_params=_cparams(("parallel", "arbitrary")),
        name="hgrn",
    )(hq, hv, hk, hlf, hq, hv, hk, hlf, m, lvl, hs)


def _merge_ffn_kernel(x_ref, om_ref, of_ref, od_ref, ohf_ref, ohb_ref, hg_ref, ln_ref, wg_ref, wbm_ref, wbf_ref,
                      wbd_ref, wbh_ref, wo_ref, gon_ref, g64_ref, lnf_ref, wgu_ref, wd_ref, o_ref):
    x = x_ref[0]
    h = _rms(x, ln_ref[...]).astype(bf16)
    oh = ohf_ref[0].astype(f32) + ohb_ref[0].astype(f32)
    ms = jnp.dot((oh * oh).astype(bf16), g64_ref[...], preferred_element_type=f32)
    oh = (oh * lax.rsqrt(ms + EPS) * gon_ref[...] * _sigmoid(hg_ref[0].astype(f32))).astype(bf16)
    merged = None
    for n, (o_n, w_ref) in enumerate(((om_ref[0], wbm_ref), (of_ref[0], wbf_ref), (od_ref[0], wbd_ref), (oh, wbh_ref))):
        gate = _sigmoid(jnp.dot(h, wg_ref[:, n * D_MODEL:(n + 1) * D_MODEL], preferred_element_type=f32))
        y = gate * jnp.dot(o_n, w_ref[...], preferred_element_type=f32)
        merged = y if merged is None else merged + y
    x = x + jnp.dot(merged.astype(bf16), wo_ref[...], preferred_element_type=f32)

    h = _rms(x, lnf_ref[...]).astype(bf16)
    acc = x
    for c in range(D_FF // FF_CHUNK):
        g = jnp.dot(h, wgu_ref[:, c * FF_CHUNK:(c + 1) * FF_CHUNK], preferred_element_type=f32)
        u = jnp.dot(h, wgu_ref[:, D_FF + c * FF_CHUNK:D_FF + (c + 1) * FF_CHUNK], preferred_element_type=f32)
        a = (g * _sigmoid(g) * u).astype(bf16)
        acc = acc + jnp.dot(a, wd_ref[c], preferred_element_type=f32)
    o_ref[0] = acc


def _merge_ffn(x, om, of, od, oh, hg, p, tm):
    B, S, _ = x.shape
    row = lambda w: pl.BlockSpec((1, tm, w), lambda b, i: (b, i, 0))
    consts = (p["ln_mix"], p["wgate"], p["wbm"], p["wbf"], p["wbd"], p["wbh"], p["wout"], p["gon"], p["g64"],
              p["ln_ffn"], p["wgu"], p["wd"])
    return pl.pallas_call(
        _merge_ffn_kernel,
        grid=(B, S // tm),
        in_specs=[row(D_MODEL)] + [row(BRANCH)] * 6
                 + [_const_spec(c.shape) for c in consts],
        out_specs=row(D_MODEL),
        out_shape=jax.ShapeDtypeStruct((B, S, D_MODEL), f32),
        compiler_params=_cparams(("parallel", "parallel")),
        name="merge_ffn",
    )(x, om, of, od, oh[0], oh[1], hg, *consts)


def _pad_heads(w, heads, width):
    lead = w.shape[:-1]
    w = w.reshape(lead + (heads, width))
    w = jnp.pad(w, [(0, 0)] * len(lead) + [(0, 0), (0, LANE - width)])
    return w.reshape(lead + (heads * LANE,))


def _rope_tables(S):
    def cs(dim):
        inv = 1.0 / (ROPE_THETA ** (jnp.arange(0, dim, 2, dtype=f32) / dim))
        ang = jnp.arange(S, dtype=f32)[:, None] * inv[None, :]
        return jnp.cos(ang), jnp.sin(ang)

    cm, sm = cs(MLA_ROPE)
    one, zero = jnp.ones((S, MLA_NOPE), f32), jnp.zeros((S, MLA_NOPE), f32)
    z16, z32, o32 = jnp.zeros((S, 16), f32), jnp.zeros((S, 32), f32), jnp.ones((S, 32), f32)
    rope_m = jnp.stack([jnp.concatenate([one, cm, cm, o32], 1),
                        jnp.concatenate([zero, -sm, z16, z32], 1),
                        jnp.concatenate([zero, z16, sm, z32], 1)])
    cd, sd = cs(DIFF_HEAD_DIM)
    rope_d = jnp.stack([jnp.tile(jnp.concatenate([cd, cd], 1), (1, 4)),
                        jnp.tile(jnp.concatenate([-sd, z16], 1), (1, 4)),
                        jnp.tile(jnp.concatenate([z16, sd], 1), (1, 4))])
    return rope_m, rope_d


def _block_mean(width, group):
    c = np.arange(width)
    return jnp.asarray(((c[:, None] // group) == (c[None, :] // group)) / group, f32).astype(bf16)


def _layer_params(l, a, lower_bounds):
    w = a["w_in"][l]
    offs = np.cumsum([0, 384, 128, 32, 256, 256, 256, 256, 256, 256, 256, 256, 256, 4096])
    col = lambda i: w[:, offs[i]:offs[i + 1]]
    wcat = jnp.concatenate([col(0), col(1), jnp.pad(col(2), ((0, 0), (MLA_NOPE, LANE - MLA_QK))), col(3), col(4),
                            col(5), _pad_heads(col(6), DIFF_HEADS, 2 * DIFF_HEAD_DIM), col(7), col(8), col(9),
                            col(10), col(11)], axis=1).astype(bf16)
    wukv = a["mla_w_ukv"][l].reshape(MLA_KV_LORA, MLA_HEADS, MLA_NOPE + MLA_V)
    wukv = jnp.concatenate([_pad_heads(wukv[:, :, :MLA_NOPE].reshape(MLA_KV_LORA, -1), MLA_HEADS, MLA_NOPE),
                            _pad_heads(wukv[:, :, MLA_NOPE:].reshape(MLA_KV_LORA, -1), MLA_HEADS, MLA_V)], axis=1)
    lam_init = 0.8 - 0.6 * math.exp(-0.3 * l)
    lam = (jnp.exp(jnp.sum(a["diff_lq1"][l] * a["diff_lk1"][l])) - jnp.exp(jnp.sum(a["diff_lq2"][l] * a["diff_lk2"][l]))
           + lam_init)
    wb = a["w_branch"][l]
    nc = D_FF // FF_CHUNK
    row = lambda v: v.reshape(1, -1).astype(f32)
    return {
        "ln_mix": row(a["ln_mix"][l]),
        "wcat": wcat,
        "gqa": row(a["mla_g_qa"][l]),
        "wuq": _pad_heads(a["mla_w_uq"][l], MLA_HEADS, MLA_QK).astype(bf16),
        "gkva": row(a["mla_g_kva"][l]),
        "wukv": wukv.astype(bf16),
        "gqn": row(jnp.pad(a["mla_g_qn"][l], (0, LANE - MLA_QK))) * (MLA_QK ** -0.5 * LOG2E),
        "gkn": row(jnp.pad(a["mla_g_kn"][l], (0, LANE - MLA_QK))),
        "gdq": row(jnp.tile(a["diff_g_qn"][l], 2 * DIFF_HEADS)) * (DIFF_HEAD_DIM ** -0.5 * LOG2E),
        "gdk": row(jnp.tile(a["diff_g_kn"][l], 2 * DIFF_HEADS)),
        "g32": _block_mean(BRANCH, DIFF_HEAD_DIM),
        "lb": lower_bounds[:, l].astype(f32),
        "lam": lam.reshape(1).astype(f32),
        "mla_bound": MLA_QK ** 0.5 * LOG2E * jnp.max(jnp.abs(a["mla_g_qn"][l])) * jnp.max(jnp.abs(a["mla_g_kn"][l])),
        "diff_bound": (DIFF_HEAD_DIM ** 0.5 * LOG2E * jnp.max(jnp.abs(a["diff_g_qn"][l]))
                       * jnp.max(jnp.abs(a["diff_g_kn"][l]))),
        "gsub": row(a["diff_g_sub"][l]) * (1.0 - lam_init),
        "wgate": col(12).astype(bf16),
        "wbm": wb[0].astype(bf16),
        "wbf": wb[1].astype(bf16),
        "wbd": wb[2].astype(bf16),
        "wbh": wb[3].astype(bf16),
        "wout": a["w_out"][l].astype(bf16),
        "gon": row(jnp.tile(a["hgrn_g_on"][l], HGRN_HEADS)),
        "g64": _block_mean(BRANCH, HGRN_DIM),
        "ln_ffn": row(a["ln_ffn"][l]),
        "wgu": a["w_gate_up"][l].astype(bf16),
        "wd": a["w_down"][l].reshape(nc, FF_CHUNK, D_MODEL).astype(bf16),
    }


def _tiles(S):
    return {"tm_in": min(256, S),"tq": min(512, S), "tk": min(512, S), "tqb": min(512, S), "tkb": min(8192, S),
            "hgrn_chunk": min(64, S),
            "tm_merge": min(512, S)}


def kernel(x, ln_mix, w_in, mla_g_qa, mla_w_uq, mla_g_kva, mla_w_ukv, mla_g_qn, mla_g_kn, diff_g_qn, diff_g_kn,
           diff_lq1, diff_lk1, diff_lq2, diff_lk2, diff_g_sub, hgrn_lb_logits, hgrn_g_on, w_branch, w_out, ln_ffn,
           w_gate_up, w_down):
    a = dict(ln_mix=ln_mix, w_in=w_in, mla_g_qa=mla_g_qa, mla_w_uq=mla_w_uq, mla_g_kva=mla_g_kva,
             mla_w_ukv=mla_w_ukv, mla_g_qn=mla_g_qn, mla_g_kn=mla_g_kn, diff_g_qn=diff_g_qn, diff_g_kn=diff_g_kn,
             diff_lq1=diff_lq1, diff_lk1=diff_lk1, diff_lq2=diff_lq2, diff_lk2=diff_lk2, diff_g_sub=diff_g_sub,
             hgrn_g_on=hgrn_g_on, w_branch=w_branch, w_out=w_out, ln_ffn=ln_ffn, w_gate_up=w_gate_up, w_down=w_down)
    S = x.shape[1]
    t = _tiles(S)
    rope_m, rope_d = _rope_tables(S)
    lb_p = jax.nn.softmax(hgrn_lb_logits.astype(f32), axis=1)
    lower_bounds = jnp.cumsum(lb_p, axis=1) - lb_p[:, :1]
    fnet_consts = _fnet_consts(S)
    hgrn_consts = _hgrn_consts(t["hgrn_chunk"])
    for l in range(DEPTH):
        p = _layer_params(l, a, lower_bounds)
        qm, km, vm, fn, qd, kd, vd, hq, hv, hk, hlf, hg = _in_proj(x, p, rope_m, rope_d, t["tm_in"])
        om = lax.cond(p["mla_bound"] <= SCORE_BOUND_LOG2,
                      lambda: _mla_attn_bounded(qm, km, vm, t["tqb"], t["tkb"]),
                      lambda: _mla_attn(qm, km, vm, t["tq"], t["tk"]))
        od = lax.cond(p["diff_bound"] <= SCORE_BOUND_LOG2,
                      lambda: _diff_attn_bounded(p["lam"], qd, kd, vd, p["gsub"], t["tqb"], t["tkb"]),
                      lambda: _diff_attn(p["lam"], qd, kd, vd, p["gsub"], t["tq"], t["tk"]))
        of = _fnet(fn, fnet_consts)
        oh = _hgrn(hq, hv, hk, hlf, hgrn_consts, t["hgrn_chunk"])
        x = _merge_ffn(x, om, of, od, oh, hg, p, t["tm_merge"])
    return x
```
